```python
import math
import jax, jax.numpy as jnp
from jax import lax
import numpy as np

D_MODEL = 2048
BATCH = 8
SEQ = 4096
DEPTH = 4

N_A = DEPTH // 2
N_B = DEPTH - N_A
N_META = 16
CONV_WIDTH = 3
HEAD_DIM = 128
N_HEADS = D_MODEL // HEAD_DIM
D_FF = ((8 * D_MODEL // 3 + 255) // 256) * 256
BLOCK = 128
PAD = BLOCK - N_META
EPS = 1e-6
NEG = -1e30

kernel_name = "yoco_shortconv_forgetting_attn_meta"


def rms_norm(x, g):
    xf = x.astype(jnp.float32)
    y = xf * lax.rsqrt(jnp.mean(xf * xf, axis=-1, keepdims=True) + EPS)
    return (y * g.astype(jnp.float32)).astype(x.dtype)


def short_conv_mixer(xn, w_in, conv_w, w_out):
    L = xn.shape[1]
    b_gate, c_gate, h = jnp.split(xn @ w_in, 3, axis=-1)
    u = jnp.pad(c_gate * h, ((0, 0), (CONV_WIDTH - 1, 0), (0, 0)))
    conv = sum(u[:, j:j + L, :] * conv_w[j] for j in range(CONV_WIDTH))
    return (b_gate * conv) @ w_out


def swiglu(xn, w_gu, w_down):
    g, u = jnp.split(xn @ w_gu, 2, axis=-1)
    return (jax.nn.silu(g) * u) @ w_down


def shared_kv(h, kv_norm, w_kv, k_norm, w_f, b_f):
    Bsz, L, _ = h.shape
    xn = rms_norm(h, kv_norm)
    k, v = jnp.split(xn @ w_kv, 2, axis=-1)
    k = rms_norm(k.reshape(Bsz, L, N_HEADS, HEAD_DIM), k_norm)
    v = v.reshape(Bsz, L, N_HEADS, HEAD_DIM)
    log_f = jax.nn.log_sigmoid((xn @ w_f + b_f).astype(jnp.float32))
    pad4 = ((0, 0), (PAD, 0), (0, 0), (0, 0))
    k = jnp.pad(k, pad4)
    v = jnp.pad(v, pad4)
    c = jnp.cumsum(jnp.pad(log_f, ((0, 0), (PAD, 0), (0, 0))), axis=1)
    return k, v, jnp.transpose(c, (0, 2, 1))


def forgetting_attention(xn, w_q, q_norm, w_o, k, v, c):
    Bsz, L, _ = xn.shape
    Lp = k.shape[1]
    q = rms_norm((xn @ w_q).reshape(Bsz, L, N_HEADS, HEAD_DIM), q_norm)
    q = jnp.pad(q, ((0, 0), (PAD, 0), (0, 0), (0, 0)))
    scale = 1.0 / math.sqrt(HEAD_DIM)
    kpos = jnp.arange(Lp)

    def block(i):
        start = i * BLOCK
        qb = lax.dynamic_slice_in_dim(q, start, BLOCK, axis=1)
        cq = lax.dynamic_slice_in_dim(c, start, BLOCK, axis=2)
        s = jnp.einsum('bqhd,bkhd->bhqk', qb, k).astype(jnp.float32) * scale
        s = s + (cq[..., :, None] - c[..., None, :])
        qpos = start + jnp.arange(BLOCK)
        mask = (kpos[None, :] <= qpos[:, None]) & (kpos[None, :] >= PAD)
        s = jnp.where(mask, s, NEG)
        p = jax.nn.softmax(s, axis=-1).astype(v.dtype)
        return jnp.einsum('bhqk,bkhd->bqhd', p, v)

    o = lax.map(block, jnp.arange(Lp // BLOCK))
    o = jnp.transpose(o, (1, 0, 2, 3, 4)).reshape(Bsz, Lp, N_HEADS * HEAD_DIM)[:, PAD:]
    return o @ w_o


def _fwd_setup_inputs(seed: int = 0) -> dict:
    key = jax.random.key(seed)
    ks = jax.random.split(key, 24)
    D, F, H, Dh = D_MODEL, D_FF, N_HEADS, HEAD_DIM
    nrm = lambda k, shape, s: jax.random.normal(k, shape, jnp.float32) * s
    gain = lambda k, shape: 1.0 + nrm(k, shape, 0.02)
    out_s = 0.5 / math.sqrt(DEPTH)
    return {
        "x": nrm(ks[0], (BATCH, SEQ, D), 1.0),
        "meta": nrm(ks[1], (N_META, D), 1.0),
        "a_norm": gain(ks[2], (N_A, D)),
        "a_w_in": nrm(ks[3], (N_A, D, 3 * D), D ** -0.5),
        "a_conv": nrm(ks[4], (N_A, CONV_WIDTH, D), CONV_WIDTH ** -0.5),
        "a_w_out": nrm(ks[5], (N_A, D, D), D ** -0.5 * out_s),
        "kv_norm": gain(ks[6], (D,)),
        "w_kv": nrm(ks[7], (D, 2 * H * Dh), D ** -0.5),
        "k_norm": gain(ks[8], (Dh,)),
        "w_f": nrm(ks[9], (D, H), D ** -0.5),
        "b_f": 3.0 + nrm(ks[10], (H,), 0.5),
        "b_norm": gain(ks[11], (N_B, D)),
        "b_w_q": nrm(ks[12], (N_B, D, H * Dh), D ** -0.5),
        "b_q_norm": gain(ks[13], (N_B, Dh)),
        "b_w_o": nrm(ks[14], (N_B, H * Dh, D), D ** -0.5 * out_s),
        "ffn_norm": gain(ks[15], (DEPTH, D)),
        "ffn_w_gu": nrm(ks[16], (DEPTH, D, 2 * F), D ** -0.5),
        "ffn_w_down": nrm(ks[17], (DEPTH, F, D), F ** -0.5 * out_s),
    }


def _fwd_reference(x, meta, a_norm, a_w_in, a_conv, a_w_out, kv_norm, w_kv, k_norm, w_f, b_f,
              b_norm, b_w_q, b_q_norm, b_w_o, ffn_norm, ffn_w_gu, ffn_w_down):
    Bsz = x.shape[0]
    meta_b = jnp.broadcast_to(meta.astype(x.dtype)[None], (Bsz, N_META, D_MODEL))
    h = jnp.concatenate([meta_b, x], axis=1)
    k = v = c = None
    for layer in range(DEPTH):
        if layer < N_A:
            h = h + short_conv_mixer(rms_norm(h, a_norm[layer]), a_w_in[layer],
                                     a_conv[layer], a_w_out[layer])
        else:
            if layer == N_A:
                k, v, c = shared_kv(h, kv_norm, w_kv, k_norm, w_f, b_f)
            j = layer - N_A
            h = h + forgetting_attention(rms_norm(h, b_norm[j]), b_w_q[j], b_q_norm[j],
                                         b_w_o[j], k, v, c)
        h = h + swiglu(rms_norm(h, ffn_norm[layer]), ffn_w_gu[layer], ffn_w_down[layer])
    return h[:, N_META:, :]


import jax as _jax
import jax.numpy as _jnp

TWIN_FORMAT = 'train_step'
FWD_PARAMS = ['x', 'meta', 'a_norm', 'a_w_in', 'a_conv', 'a_w_out', 'kv_norm', 'w_kv', 'k_norm', 'w_f', 'b_f', 'b_norm', 'b_w_q', 'b_q_norm', 'b_w_o', 'ffn_norm', 'ffn_w_gu', 'ffn_w_down']
TWIN_WEIGHTS = ['meta', 'a_norm', 'a_w_in', 'a_conv', 'a_w_out', 'kv_norm', 'w_kv', 'k_norm', 'w_f', 'b_f', 'b_norm', 'b_w_q', 'b_q_norm', 'b_w_o', 'ffn_norm', 'ffn_w_gu', 'ffn_w_down']
TWIN_DIFF_INPUT = 'x'
TWIN_INPUTS = ['x', 'meta', 'a_norm', 'a_w_in', 'a_conv', 'a_w_out', 'kv_norm', 'w_kv', 'k_norm', 'w_f', 'b_f', 'b_norm', 'b_w_q', 'b_q_norm', 'b_w_o', 'ffn_norm', 'ffn_w_gu', 'ffn_w_down', 'loss_target', 'm_meta', 'm_a_norm', 'm_a_w_in', 'm_a_conv', 'm_a_w_out', 'm_kv_norm', 'm_w_kv', 'm_k_norm', 'm_w_f', 'm_b_f', 'm_b_norm', 'm_b_w_q', 'm_b_q_norm', 'm_b_w_o', 'm_ffn_norm', 'm_ffn_w_gu', 'm_ffn_w_down', 'v_meta', 'v_a_norm', 'v_a_w_in', 'v_a_conv', 'v_a_w_out', 'v_kv_norm', 'v_w_kv', 'v_k_norm', 'v_w_f', 'v_b_f', 'v_b_norm', 'v_b_w_q', 'v_b_q_norm', 'v_b_w_o', 'v_ffn_norm', 'v_ffn_w_gu', 'v_ffn_w_down']
TWIN_OUTPUTS = ['loss', 'grad_x', 'grad_meta', 'grad_a_norm', 'grad_a_w_in', 'grad_a_conv', 'grad_a_w_out', 'grad_kv_norm', 'grad_w_kv', 'grad_k_norm', 'grad_w_f', 'grad_b_f', 'grad_b_norm', 'grad_b_w_q', 'grad_b_q_norm', 'grad_b_w_o', 'grad_ffn_norm', 'grad_ffn_w_gu', 'grad_ffn_w_down', 'delta_meta', 'delta_a_norm', 'delta_a_w_in', 'delta_a_conv', 'delta_a_w_out', 'delta_kv_norm', 'delta_w_kv', 'delta_k_norm', 'delta_w_f', 'delta_b_f', 'delta_b_norm', 'delta_b_w_q', 'delta_b_q_norm', 'delta_b_w_o', 'delta_ffn_norm', 'delta_ffn_w_gu', 'delta_ffn_w_down', 'new_m_meta', 'new_m_a_norm', 'new_m_a_w_in', 'new_m_a_conv', 'new_m_a_w_out', 'new_m_kv_norm', 'new_m_w_kv', 'new_m_k_norm', 'new_m_w_f', 'new_m_b_f', 'new_m_b_norm', 'new_m_b_w_q', 'new_m_b_q_norm', 'new_m_b_w_o', 'new_m_ffn_norm', 'new_m_ffn_w_gu', 'new_m_ffn_w_down', 'new_v_meta', 'new_v_a_norm', 'new_v_a_w_in', 'new_v_a_conv', 'new_v_a_w_out', 'new_v_kv_norm', 'new_v_w_kv', 'new_v_k_norm', 'new_v_w_f', 'new_v_b_f', 'new_v_b_norm', 'new_v_b_w_q', 'new_v_b_q_norm', 'new_v_b_w_o', 'new_v_ffn_norm', 'new_v_ffn_w_gu', 'new_v_ffn_w_down']
TWIN_LEAF_KINDS = {'loss': 'loss', 'grad_x': 'grad_x', 'grad_meta': 'grad_w', 'grad_a_norm': 'grad_w', 'grad_a_w_in': 'grad_w', 'grad_a_conv': 'grad_w', 'grad_a_w_out': 'grad_w', 'grad_kv_norm': 'grad_w', 'grad_w_kv': 'grad_w', 'grad_k_norm': 'grad_w', 'grad_w_f': 'grad_w', 'grad_b_f': 'grad_w', 'grad_b_norm': 'grad_w', 'grad_b_w_q': 'grad_w', 'grad_b_q_norm': 'grad_w', 'grad_b_w_o': 'grad_w', 'grad_ffn_norm': 'grad_w', 'grad_ffn_w_gu': 'grad_w', 'grad_ffn_w_down': 'grad_w', 'delta_meta': 'delta_w', 'delta_a_norm': 'delta_w', 'delta_a_w_in': 'delta_w', 'delta_a_conv': 'delta_w', 'delta_a_w_out': 'delta_w', 'delta_kv_norm': 'delta_w', 'delta_w_kv': 'delta_w', 'delta_k_norm': 'delta_w', 'delta_w_f': 'delta_w', 'delta_b_f': 'delta_w', 'delta_b_norm': 'delta_w', 'delta_b_w_q': 'delta_w', 'delta_b_q_norm': 'delta_w', 'delta_b_w_o': 'delta_w', 'delta_ffn_norm': 'delta_w', 'delta_ffn_w_gu': 'delta_w', 'delta_ffn_w_down': 'delta_w', 'new_m_meta': 'new_m', 'new_m_a_norm': 'new_m', 'new_m_a_w_in': 'new_m', 'new_m_a_conv': 'new_m', 'new_m_a_w_out': 'new_m', 'new_m_kv_norm': 'new_m', 'new_m_w_kv': 'new_m', 'new_m_k_norm': 'new_m', 'new_m_w_f': 'new_m', 'new_m_b_f': 'new_m', 'new_m_b_norm': 'new_m', 'new_m_b_w_q': 'new_m', 'new_m_b_q_norm': 'new_m', 'new_m_b_w_o': 'new_m', 'new_m_ffn_norm': 'new_m', 'new_m_ffn_w_gu': 'new_m', 'new_m_ffn_w_down': 'new_m', 'new_v_meta': 'new_v', 'new_v_a_norm': 'new_v', 'new_v_a_w_in': 'new_v', 'new_v_a_conv': 'new_v', 'new_v_a_w_out': 'new_v', 'new_v_kv_norm': 'new_v', 'new_v_w_kv': 'new_v', 'new_v_k_norm': 'new_v', 'new_v_w_f': 'new_v', 'new_v_b_f': 'new_v', 'new_v_b_norm': 'new_v', 'new_v_b_w_q': 'new_v', 'new_v_b_q_norm': 'new_v', 'new_v_b_w_o': 'new_v', 'new_v_ffn_norm': 'new_v', 'new_v_ffn_w_gu': 'new_v', 'new_v_ffn_w_down': 'new_v'}


def _forward(args):
    return _fwd_reference(*[args[k] for k in FWD_PARAMS])


def _output_shape():
    def fwd():
        inp = _fwd_setup_inputs(0)
        return _fwd_reference(*[inp[k] for k in FWD_PARAMS])
    out = _jax.eval_shape(fwd)
    return out.shape, out.dtype

N_MICROBATCH = 1
ADAM_LR = 0.001
ADAM_B1 = 0.9
ADAM_B2 = 0.999
ADAM_EPS = 1e-08
ADAM_WD = 0.01
ADAM_STEP = 10
PER_EXAMPLE_BATCH_AXIS = {'x': 0, 'loss_target': 0}
SHARED_INPUTS = []
_WEIGHT_DTYPES = {'meta': _jnp.float32, 'a_norm': _jnp.float32, 'a_w_in': _jnp.float32, 'a_conv': _jnp.float32, 'a_w_out': _jnp.float32, 'kv_norm': _jnp.float32, 'w_kv': _jnp.float32, 'k_norm': _jnp.float32, 'w_f': _jnp.float32, 'b_f': _jnp.float32, 'b_norm': _jnp.float32, 'b_w_q': _jnp.float32, 'b_q_norm': _jnp.float32, 'b_w_o': _jnp.float32, 'ffn_norm': _jnp.float32, 'ffn_w_gu': _jnp.float32, 'ffn_w_down': _jnp.float32}
MOMENT_SCALE = {'meta': 2.041228e-03, 'a_norm': 3.039859e+00, 'a_w_in': 4.725116e-02, 'a_conv': 5.711834e-01, 'a_w_out': 1.721395e-01, 'kv_norm': 2.093257e-01, 'w_kv': 1.408633e-02, 'k_norm': 1.401839e+00, 'w_f': 1.425546e-01, 'b_f': 7.927261e+00, 'b_norm': 8.857209e-03, 'b_w_q': 8.730178e-03, 'b_q_norm': 6.984764e-01, 'b_w_o': 4.398385e-02, 'ffn_norm': 7.681457e-01, 'ffn_w_gu': 1.441388e-02, 'ffn_w_down': 1.010422e-01}


def _to_microbatches(a, axis):
    t = _jnp.moveaxis(a, axis, 0)
    t = t.reshape((N_MICROBATCH, t.shape[0] // N_MICROBATCH) + t.shape[1:])
    return _jnp.moveaxis(t, 1, axis + 1)


def setup_inputs(seed: int = 0) -> dict:
    inp = _fwd_setup_inputs(seed)
    key = _jax.random.fold_in(_jax.random.key(seed), 7919)
    shape, _ = _output_shape()
    out = dict(inp)
    out["loss_target"] = _jax.random.normal(_jax.random.fold_in(key, 0), shape, _jnp.float32)
    for i, name in enumerate(TWIN_WEIGHTS):
        w = inp[name].astype(_jnp.float32)
        if MOMENT_SCALE is None:
            s = _jnp.sqrt(_jnp.mean(_jnp.square(w)) + 1e-30)
        else:
            s = MOMENT_SCALE[name]
        km, kv = _jax.random.split(_jax.random.fold_in(key, i + 1))
        out[name] = w
        out["m_" + name] = s * _jax.random.normal(km, w.shape, _jnp.float32)
        out["v_" + name] = (s * s) * _jax.random.uniform(kv, w.shape, _jnp.float32, 0.5, 1.5)
    if N_MICROBATCH > 1:
        for name, axis in PER_EXAMPLE_BATCH_AXIS.items():
            out[name] = _to_microbatches(out[name], axis)
    return {'x': out['x'], 'meta': out['meta'], 'a_norm': out['a_norm'], 'a_w_in': out['a_w_in'], 'a_conv': out['a_conv'], 'a_w_out': out['a_w_out'], 'kv_norm': out['kv_norm'], 'w_kv': out['w_kv'], 'k_norm': out['k_norm'], 'w_f': out['w_f'], 'b_f': out['b_f'], 'b_norm': out['b_norm'], 'b_w_q': out['b_w_q'], 'b_q_norm': out['b_q_norm'], 'b_w_o': out['b_w_o'], 'ffn_norm': out['ffn_norm'], 'ffn_w_gu': out['ffn_w_gu'], 'ffn_w_down': out['ffn_w_down'], 'loss_target': out['loss_target'], 'm_meta': out['m_meta'], 'm_a_norm': out['m_a_norm'], 'm_a_w_in': out['m_a_w_in'], 'm_a_conv': out['m_a_conv'], 'm_a_w_out': out['m_a_w_out'], 'm_kv_norm': out['m_kv_norm'], 'm_w_kv': out['m_w_kv'], 'm_k_norm': out['m_k_norm'], 'm_w_f': out['m_w_f'], 'm_b_f': out['m_b_f'], 'm_b_norm': out['m_b_norm'], 'm_b_w_q': out['m_b_w_q'], 'm_b_q_norm': out['m_b_q_norm'], 'm_b_w_o': out['m_b_w_o'], 'm_ffn_norm': out['m_ffn_norm'], 'm_ffn_w_gu': out['m_ffn_w_gu'], 'm_ffn_w_down': out['m_ffn_w_down'], 'v_meta': out['v_meta'], 'v_a_norm': out['v_a_norm'], 'v_a_w_in': out['v_a_w_in'], 'v_a_conv': out['v_a_conv'], 'v_a_w_out': out['v_a_w_out'], 'v_kv_norm': out['v_kv_norm'], 'v_w_kv': out['v_w_kv'], 'v_k_norm': out['v_k_norm'], 'v_w_f': out['v_w_f'], 'v_b_f': out['v_b_f'], 'v_b_norm': out['v_b_norm'], 'v_b_w_q': out['v_b_w_q'], 'v_b_q_norm': out['v_b_q_norm'], 'v_b_w_o': out['v_b_w_o'], 'v_ffn_norm': out['v_ffn_norm'], 'v_ffn_w_gu': out['v_ffn_w_gu'], 'v_ffn_w_down': out['v_ffn_w_down']}


def _loss(weights, diff, rest, loss_target):
    with _jax.named_scope("forward"):
        args = {**rest, TWIN_DIFF_INPUT: diff, **{k: w.astype(_WEIGHT_DTYPES[k]) for k, w in weights.items()}}
        y = _forward(args)
    with _jax.named_scope("loss_head"):
        err = _jnp.square(y.astype(_jnp.float32) - loss_target)
        return 0.5 * _jnp.sum(_jnp.mean(err, axis=-1)) if err.ndim else 0.5 * err


def _adamw(w, g, m, v):
    m = ADAM_B1 * m + (1.0 - ADAM_B1) * g
    v = ADAM_B2 * v + (1.0 - ADAM_B2) * _jnp.square(g)
    m_hat = m / (1.0 - ADAM_B1 ** ADAM_STEP)
    v_hat = v / (1.0 - ADAM_B2 ** ADAM_STEP)
    delta = -ADAM_LR * (m_hat / (_jnp.sqrt(v_hat) + ADAM_EPS) + ADAM_WD * w)
    return delta, m, v


def reference(x, meta, a_norm, a_w_in, a_conv, a_w_out, kv_norm, w_kv, k_norm, w_f, b_f, b_norm, b_w_q, b_q_norm, b_w_o, ffn_norm, ffn_w_gu, ffn_w_down, loss_target, m_meta, m_a_norm, m_a_w_in, m_a_conv, m_a_w_out, m_kv_norm, m_w_kv, m_k_norm, m_w_f, m_b_f, m_b_norm, m_b_w_q, m_b_q_norm, m_b_w_o, m_ffn_norm, m_ffn_w_gu, m_ffn_w_down, v_meta, v_a_norm, v_a_w_in, v_a_conv, v_a_w_out, v_kv_norm, v_w_kv, v_k_norm, v_w_f, v_b_f, v_b_norm, v_b_w_q, v_b_q_norm, v_b_w_o, v_ffn_norm, v_ffn_w_gu, v_ffn_w_down):
    given = dict(x=x, meta=meta, a_norm=a_norm, a_w_in=a_w_in, a_conv=a_conv, a_w_out=a_w_out, kv_norm=kv_norm, w_kv=w_kv, k_norm=k_norm, w_f=w_f, b_f=b_f, b_norm=b_norm, b_w_q=b_w_q, b_q_norm=b_q_norm, b_w_o=b_w_o, ffn_norm=ffn_norm, ffn_w_gu=ffn_w_gu, ffn_w_down=ffn_w_down, loss_target=loss_target, m_meta=m_meta, m_a_norm=m_a_norm, m_a_w_in=m_a_w_in, m_a_conv=m_a_conv, m_a_w_out=m_a_w_out, m_kv_norm=m_kv_norm, m_w_kv=m_w_kv, m_k_norm=m_k_norm, m_w_f=m_w_f, m_b_f=m_b_f, m_b_norm=m_b_norm, m_b_w_q=m_b_w_q, m_b_q_norm=m_b_q_norm, m_b_w_o=m_b_w_o, m_ffn_norm=m_ffn_norm, m_ffn_w_gu=m_ffn_w_gu, m_ffn_w_down=m_ffn_w_down, v_meta=v_meta, v_a_norm=v_a_norm, v_a_w_in=v_a_w_in, v_a_conv=v_a_conv, v_a_w_out=v_a_w_out, v_kv_norm=v_kv_norm, v_w_kv=v_w_kv, v_k_norm=v_k_norm, v_w_f=v_w_f, v_b_f=v_b_f, v_b_norm=v_b_norm, v_b_w_q=v_b_w_q, v_b_q_norm=v_b_q_norm, v_b_w_o=v_b_w_o, v_ffn_norm=v_ffn_norm, v_ffn_w_gu=v_ffn_w_gu, v_ffn_w_down=v_ffn_w_down)
    weights = {n: given[n] for n in TWIN_WEIGHTS}
    shared = {n: given[n] for n in SHARED_INPUTS}
    per_example = {n: given[n] for n in ['x']}
    grad_fn = _jax.value_and_grad(_loss, argnums=(0, 1))

    def one_microbatch(ex, loss_target):
        ex = dict(ex)
        diff = ex.pop(TWIN_DIFF_INPUT)
        return grad_fn(weights, diff, {**shared, **ex}, loss_target)

    if N_MICROBATCH == 1:
        loss, (grad_w, grad_x) = one_microbatch(per_example, given["loss_target"])
    else:
        def body(carry, xs):
            loss_sum, grad_sum = carry
            l_k, (gw_k, gx_k) = one_microbatch(xs[0], xs[1])
            with _jax.named_scope("update"):
                return (loss_sum + l_k, _jax.tree.map(_jnp.add, grad_sum, gw_k)), gx_k

        init = (_jnp.zeros((), _jnp.float32), _jax.tree.map(_jnp.zeros_like, weights))
        (loss, grad_w), grad_x = _jax.lax.scan(body, init, (per_example, given["loss_target"]))
    with _jax.named_scope("update"):
        delta_w, new_m, new_v = {}, {}, {}
        for n in TWIN_WEIGHTS:
            delta_w[n], new_m[n], new_v[n] = _adamw(weights[n], grad_w[n], given["m_" + n], given["v_" + n])
    return (loss, grad_x, *[grad_w[n] for n in TWIN_WEIGHTS], *[delta_w[n] for n in TWIN_WEIGHTS],
            *[new_m[n] for n in TWIN_WEIGHTS], *[new_v[n] for n in TWIN_WEIGHTS])
```

```python
import functools
import math

import jax
import jax.numpy as jnp
from jax import lax
from jax.experimental import pallas as pl
from jax.experimental.pallas import tpu as pltpu

F32 = jnp.float32
BF16 = jnp.bfloat16
HEAD_DIM = 128
BLOCK = 128
LANES = 128
EPS = 1e-6
NEG = -1e30
ADAM_LR, ADAM_B1, ADAM_B2, ADAM_EPS, ADAM_WD, ADAM_STEP = 0.001, 0.9, 0.999, 1e-08, 0.01, 10
VMEM_LIMIT = 56 * 1024 * 1024
TILE_BUDGET = 40 * 1024 * 1024
MESH = pl.DeviceIdType.MESH
N_CHIPS = 4


def _tile(n, target, align):
    best = None
    for d in range(align, min(n, target) + 1, align):
        if n % d == 0:
            best = d
    return best if best is not None else n


def _cp(sem):
    return pltpu.CompilerParams(dimension_semantics=sem, vmem_limit_bytes=VMEM_LIMIT)


def _matmul(a, b, *, mode, out_dtype, name, res=None, out_parts=1):
    a_parts = a.shape[0] if a.ndim == 3 else 1
    b_parts = b.shape[0] if b.ndim == 3 else 1
    if mode == "nn":
        M, Kp = a.shape[-2:]
        K, N = Kp * a_parts, b.shape[1]
    elif mode == "nt":
        M, Kp = a.shape[-2:]
        K, N = Kp * a_parts, b.shape[0]
    else:
        K, M = a.shape
        Kp = K
        N = b.shape[-1] * b_parts
    Np = N // max(b_parts, out_parts)
    tm = _tile(M, 1024, LANES) if mode == "tn" else _tile(M, 1056, 16)
    tn = _tile(Np, 1536, LANES)
    tk = _tile(Kp, 1056, 16) if mode == "tn" else _tile(Kp, 2048, LANES)
    ab, bb, ob = a.dtype.itemsize, b.dtype.itemsize, jnp.dtype(out_dtype).itemsize

    def vmem(tm_):
        blocks = 2 * (tm_ * tk * ab + tk * tn * bb + tm_ * tn * ob + (tm_ * tn * 4 if res is not None else 0))
        temps = tm_ * tn * 8 + (tm_ * tk * 2 if ab == 4 else 0) + (tk * tn * 2 if bb == 4 else 0)
        return blocks + temps

    while vmem(tm) > TILE_BUDGET and tm > 256:
        tm = _tile(M, tm // 2, LANES if mode == "tn" else 16)
    ni, nj, nk = M // tm, N // tn, K // tk
    nkp, njp = Kp // tk, Np // tn

    if mode == "tn":
        a_spec = pl.BlockSpec((tk, tm), lambda i, j, k: (k, i))
    elif a_parts > 1:
        a_spec = pl.BlockSpec((None, tm, tk), lambda i, j, k: (k // nkp, i, k % nkp))
    else:
        a_spec = pl.BlockSpec((tm, tk), lambda i, j, k: (i, k))
    if mode == "nn":
        b_spec = pl.BlockSpec((tk, tn), lambda i, j, k: (k, j))
    elif mode == "nt":
        b_spec = pl.BlockSpec((tn, tk), lambda i, j, k: (j, k))
    elif b_parts > 1:
        b_spec = pl.BlockSpec((None, tk, tn), lambda i, j, k: (j // njp, k, j % njp))
    else:
        b_spec = pl.BlockSpec((tk, tn), lambda i, j, k: (k, j))
    in_specs = [a_spec, b_spec]
    operands = [a, b]
    if res is not None:
        in_specs.append(pl.BlockSpec((tm, tn), lambda i, j, k: (i, j)))
        operands.append(res)
    if out_parts > 1:
        out_spec = pl.BlockSpec((None, tm, tn), lambda i, j, k: (j // njp, i, j % njp))
        out_shape = jax.ShapeDtypeStruct((out_parts, M, Np), out_dtype)
    else:
        out_spec = pl.BlockSpec((tm, tn), lambda i, j, k: (i, j))
        out_shape = jax.ShapeDtypeStruct((M, N), out_dtype)
    dims = {"nn": (((1,), (0,)), ((), ())), "nt": (((1,), (1,)), ((), ())), "tn": (((0,), (0,)), ((), ()))}[mode]
    has_res = res is not None

    def body(*refs):
        a_ref, b_ref = refs[0], refs[1]
        res_ref = refs[2] if has_res else None
        o_ref = refs[3] if has_res else refs[2]
        d = lax.dot_general(a_ref[...].astype(BF16), b_ref[...].astype(BF16), dims, preferred_element_type=F32)
        if nk == 1:
            if has_res:
                d = d + res_ref[...]
            o_ref[...] = d.astype(out_dtype)
        else:
            acc_ref = refs[-1]
            k = pl.program_id(2)

            @pl.when(k == 0)
            def _():
                acc_ref[...] = d

            @pl.when(k > 0)
            def _():
                acc_ref[...] += d

            @pl.when(k == nk - 1)
            def _():
                r = acc_ref[...]
                if has_res:
                    r = r + res_ref[...]
                o_ref[...] = r.astype(out_dtype)

    return pl.pallas_call(
        body, name=name, grid=(ni, nj, nk), in_specs=in_specs, out_specs=out_spec, out_shape=out_shape,
        scratch_shapes=[pltpu.VMEM((tm, tn), F32)] if nk > 1 else [],
        compiler_params=_cp(("parallel", "parallel", "arbitrary")),
    )(*operands)


def _cast_layer(w3, layer, name):
    _, R, C = w3.shape
    tr = _tile(R, max(16, (4 * 1024 * 1024) // (C * 4)), 16)

    def body(w_ref, o_ref):
        o_ref[...] = w_ref[...].astype(BF16)

    return pl.pallas_call(
        body, name=name, grid=(R // tr,),
        in_specs=[pl.BlockSpec((None, tr, C), lambda i: (layer, i, 0))],
        out_specs=pl.BlockSpec((tr, C), lambda i: (i, 0)),
        out_shape=jax.ShapeDtypeStruct((R, C), BF16), compiler_params=_cp(("parallel",)),
    )(w3)


def _rms_fwd(h, g, name):
    T, D = h.shape
    tr = _tile(T, 528, 16)

    def body(h_ref, g_ref, o_ref):
        x = h_ref[...]
        r = lax.rsqrt(jnp.mean(x * x, axis=-1, keepdims=True) + EPS)
        o_ref[...] = (x * r * g_ref[...]).astype(BF16)

    return pl.pallas_call(
        body, name=name, grid=(T // tr,),
        in_specs=[pl.BlockSpec((tr, D), lambda i: (i, 0)), pl.BlockSpec((1, D), lambda i: (0, 0))],
        out_specs=pl.BlockSpec((tr, D), lambda i: (i, 0)),
        out_shape=jax.ShapeDtypeStruct((T, D), BF16), compiler_params=_cp(("parallel",)),
    )(h, g)


def _rms_bwd(h, g, dxn, dh, name):
    T, D = h.shape
    tr = _tile(T, 264, 8)

    def body(h_ref, g_ref, dxn_ref, dh_ref, o_ref, dg_ref):
        x = h_ref[...]
        r = lax.rsqrt(jnp.mean(x * x, axis=-1, keepdims=True) + EPS)
        xh = x * r
        dy = dxn_ref[...]
        dxh = dy * g_ref[...]
        dx = r * (dxh - xh * jnp.mean(dxh * xh, axis=-1, keepdims=True))
        o_ref[...] = dh_ref[...] + dx
        part = jnp.sum(dy * xh, axis=0, keepdims=True)

        @pl.when(pl.program_id(0) == 0)
        def _():
            dg_ref[...] = part

        @pl.when(pl.program_id(0) > 0)
        def _():
            dg_ref[...] += part

    row = pl.BlockSpec((tr, D), lambda i: (i, 0))
    vec = pl.BlockSpec((1, D), lambda i: (0, 0))
    return pl.pallas_call(
        body, name=name, grid=(T // tr,), in_specs=[row, vec, row, row], out_specs=[row, vec],
        out_shape=[jax.ShapeDtypeStruct((T, D), F32), jax.ShapeDtypeStruct((1, D), F32)],
        compiler_params=_cp(("arbitrary",)),
    )(h, g, dxn, dh)


def _shift_down(u, n, rows):
    return jnp.where(rows >= n, pltpu.roll(u, n, 0), 0.0)


def _shift_up(u, n, rows, total):
    return jnp.where(rows < total - n, pltpu.roll(u, total - n, 0), 0.0)


def _gate_fwd(z, conv_w, name):
    _, T, D = z.shape
    tc = LANES

    def body(b_ref, c_ref, h_ref, w_ref, y_ref):
        rows = lax.broadcasted_iota(jnp.int32, (T, tc), 0)
        u = c_ref[...] * h_ref[...]
        w0, w1, w2 = w_ref[0:1, :], w_ref[1:2, :], w_ref[2:3, :]
        conv = u * w2 + _shift_down(u, 1, rows) * w1 + _shift_down(u, 2, rows) * w0
        y_ref[...] = (b_ref[...] * conv).astype(BF16)

    part = lambda p: pl.BlockSpec((None, T, tc), lambda j, p=p: (p, 0, j))
    return pl.pallas_call(
        body, name=name, grid=(D // tc,),
        in_specs=[part(0), part(1), part(2), pl.BlockSpec((3, tc), lambda j: (0, j))],
        out_specs=pl.BlockSpec((T, tc), lambda j: (0, j)),
        out_shape=jax.ShapeDtypeStruct((T, D), BF16), compiler_params=_cp(("parallel",)),
    )(z, z, z, conv_w)


def _gate_bwd(z, conv_w, dy, name):
    _, T, D = z.shape
    tc = LANES

    def body(b_ref, c_ref, h_ref, w_ref, dy_ref, dz_ref, dw_ref):
        rows = lax.broadcasted_iota(jnp.int32, (T, tc), 0)
        cg, hh = c_ref[...], h_ref[...]
        u = cg * hh
        w0, w1, w2 = w_ref[0:1, :], w_ref[1:2, :], w_ref[2:3, :]
        s1, s2 = _shift_down(u, 1, rows), _shift_down(u, 2, rows)
        g = dy_ref[...]
        dz_ref[0] = (g * (u * w2 + s1 * w1 + s2 * w0)).astype(BF16)
        dconv = g * b_ref[...]
        dw_ref[0:1, :] = jnp.sum(dconv * s2, axis=0, keepdims=True)
        dw_ref[1:2, :] = jnp.sum(dconv * s1, axis=0, keepdims=True)
        dw_ref[2:3, :] = jnp.sum(dconv * u, axis=0, keepdims=True)
        du = dconv * w2 + _shift_up(dconv, 1, rows, T) * w1 + _shift_up(dconv, 2, rows, T) * w0
        dz_ref[1] = (du * hh).astype(BF16)
        dz_ref[2] = (du * cg).astype(BF16)

    part = lambda p: pl.BlockSpec((None, T, tc), lambda j, p=p: (p, 0, j))
    return pl.pallas_call(
        body, name=name, grid=(D // tc,),
        in_specs=[part(0), part(1), part(2), pl.BlockSpec((3, tc), lambda j: (0, j)),
                  pl.BlockSpec((T, tc), lambda j: (0, j))],
        out_specs=[pl.BlockSpec((3, T, tc), lambda j: (0, 0, j)), pl.BlockSpec((3, tc), lambda j: (0, j))],
        out_shape=[jax.ShapeDtypeStruct((3, T, D), BF16), jax.ShapeDtypeStruct((3, D), F32)],
        compiler_params=_cp(("parallel",)),
    )(z, z, z, conv_w, dy)


def _swiglu_fwd(z, name):
    _, T, Fd = z.shape
    tr, tc = _tile(T, 528, 16), _tile(Fd, 512, LANES)

    def body(g_ref, u_ref, o_ref):
        g = g_ref[...]
        o_ref[...] = (g * jax.nn.sigmoid(g) * u_ref[...]).astype(BF16)

    part = lambda p: pl.BlockSpec((None, tr, tc), lambda i, j, p=p: (p, i, j))
    return pl.pallas_call(
        body, name=name, grid=(T // tr, Fd // tc), in_specs=[part(0), part(1)],
        out_specs=pl.BlockSpec((tr, tc), lambda i, j: (i, j)),
        out_shape=jax.ShapeDtypeStruct((T, Fd), BF16), compiler_params=_cp(("parallel", "parallel")),
    )(z, z)


def _swiglu_bwd(z, da, name):
    _, T, Fd = z.shape
    tr, tc = _tile(T, 528, 16), _tile(Fd, 512, LANES)

    def body(g_ref, u_ref, da_ref, dz_ref):
        g, d = g_ref[...], da_ref[...]
        s = jax.nn.sigmoid(g)
        dz_ref[0] = (d * u_ref[...] * (s * (1.0 + g * (1.0 - s)))).astype(BF16)
        dz_ref[1] = (d * (g * s)).astype(BF16)

    part = lambda p: pl.BlockSpec((None, tr, tc), lambda i, j, p=p: (p, i, j))
    return pl.pallas_call(
        body, name=name, grid=(T // tr, Fd // tc),
        in_specs=[part(0), part(1), pl.BlockSpec((tr, tc), lambda i, j: (i, j))],
        out_specs=pl.BlockSpec((2, tr, tc), lambda i, j: (0, i, j)),
        out_shape=jax.ShapeDtypeStruct((2, T, Fd), BF16), compiler_params=_cp(("parallel", "parallel")),
    )(z, z, da)


def _headnorm_fwd(z, part, g, name):
    _, T, D = z.shape
    tr = _tile(T, 1056, 16)

    def body(z_ref, g_ref, o_ref):
        x = z_ref[...]
        r = lax.rsqrt(jnp.mean(x * x, axis=-1, keepdims=True) + EPS)
        o_ref[...] = (x * r * g_ref[...]).astype(BF16)

    return pl.pallas_call(
        body, name=name, grid=(T // tr, D // HEAD_DIM),
        in_specs=[pl.BlockSpec((None, tr, HEAD_DIM), lambda i, h: (part, i, h)),
                  pl.BlockSpec((1, HEAD_DIM), lambda i, h: (0, 0))],
        out_specs=pl.BlockSpec((tr, HEAD_DIM), lambda i, h: (i, h)),
        out_shape=jax.ShapeDtypeStruct((T, D), BF16), compiler_params=_cp(("parallel", "parallel")),
    )(z, g)


def _headnorm_bwd(z, part, g, dy, name):
    _, T, D = z.shape
    tr = _tile(T, 1056, 16)

    def body(z_ref, g_ref, dy_ref, dz_ref, dg_ref):
        x = z_ref[...]
        r = lax.rsqrt(jnp.mean(x * x, axis=-1, keepdims=True) + EPS)
        xh = x * r
        dy_ = dy_ref[...]
        dxh = dy_ * g_ref[...]
        dz_ref[...] = (r * (dxh - xh * jnp.mean(dxh * xh, axis=-1, keepdims=True))).astype(BF16)
        partial = jnp.sum(dy_ * xh, axis=0, keepdims=True)
        first = (pl.program_id(0) == 0) & (pl.program_id(1) == 0)

        @pl.when(first)
        def _():
            dg_ref[...] = partial

        @pl.when(jnp.logical_not(first))
        def _():
            dg_ref[...] += partial

    blk = pl.BlockSpec((tr, HEAD_DIM), lambda i, h: (i, h))
    vec = pl.BlockSpec((1, HEAD_DIM), lambda i, h: (0, 0))
    return pl.pallas_call(
        body, name=name, grid=(T // tr, D // HEAD_DIM),
        in_specs=[pl.BlockSpec((None, tr, HEAD_DIM), lambda i, h: (part, i, h)), vec, blk],
        out_specs=[blk, vec],
        out_shape=[jax.ShapeDtypeStruct((T, D), BF16), jax.ShapeDtypeStruct((1, HEAD_DIM), F32)],
        compiler_params=_cp(("arbitrary", "arbitrary")),
    )(z, g, dy)


def _cast_part(z, part, name):
    _, T, D = z.shape
    tr = _tile(T, 528, 16)

    def body(z_ref, o_ref):
        o_ref[...] = z_ref[...].astype(BF16)

    return pl.pallas_call(
        body, name=name, grid=(T // tr,),
        in_specs=[pl.BlockSpec((None, tr, D), lambda i: (part, i, 0))],
        out_specs=pl.BlockSpec((tr, D), lambda i: (i, 0)),
        out_shape=jax.ShapeDtypeStruct((T, D), BF16), compiler_params=_cp(("parallel",)),
    )(z)


def _split3(x):
    a = x.astype(BF16)
    r = x - a.astype(F32)
    b = r.astype(BF16)
    c = (r - b.astype(F32)).astype(BF16)
    return a, b, c


def _tri_matmul(tri, x):
    a, b, c = _split3(x)
    dot = lambda v: jnp.dot(tri, v, preferred_element_type=F32)
    return (dot(c) + dot(b)) + dot(a)


def _logf_cumsum(pre, bias, pad, name):
    T = pre.shape[0]
    nb = T // BLOCK

    def body(p_ref, b_ref, c_ref, carry):
        i = pl.program_id(0)

        @pl.when(i == 0)
        def _():
            carry[...] = jnp.zeros_like(carry)

        x = p_ref[...] + b_ref[...]
        lf = jnp.minimum(x, 0.0) - jnp.log(1.0 + jnp.exp(-jnp.abs(x)))
        rows = i * BLOCK + lax.broadcasted_iota(jnp.int32, (BLOCK, LANES), 0)
        lf = jnp.where(rows >= pad, lf, 0.0)
        r = lax.broadcasted_iota(jnp.int32, (BLOCK, BLOCK), 0)
        c = lax.broadcasted_iota(jnp.int32, (BLOCK, BLOCK), 1)
        tri = jnp.where(c <= r, 1.0, 0.0).astype(BF16)
        c_ref[...] = _tri_matmul(tri, lf) + carry[...]
        carry[...] = c_ref[BLOCK - 1:BLOCK, :]

    return pl.pallas_call(
        body, name=name, grid=(nb,),
        in_specs=[pl.BlockSpec((BLOCK, LANES), lambda i: (i, 0)), pl.BlockSpec((1, LANES), lambda i: (0, 0))],
        out_specs=pl.BlockSpec((BLOCK, LANES), lambda i: (i, 0)),
        out_shape=jax.ShapeDtypeStruct((T, LANES), F32),
        scratch_shapes=[pltpu.VMEM((1, LANES), F32)], compiler_params=_cp(("arbitrary",)),
    )(pre, bias)


def _logf_bwd(pre, bias, dc, pad, name):
    T = pre.shape[0]
    nb = T // BLOCK

    def body(p_ref, b_ref, dc_ref, dp_ref, db_ref, carry, dlf_ref):
        i = pl.program_id(0)

        @pl.when(i == 0)
        def _():
            carry[...] = jnp.zeros_like(carry)

        r = lax.broadcasted_iota(jnp.int32, (BLOCK, BLOCK), 0)
        c = lax.broadcasted_iota(jnp.int32, (BLOCK, BLOCK), 1)
        tri = jnp.where(c >= r, 1.0, 0.0).astype(BF16)
        dlf_ref[...] = _tri_matmul(tri, dc_ref[...]) + carry[...]
        carry[...] = dlf_ref[0:1, :]
        dlf = dlf_ref[...]
        x = p_ref[...] + b_ref[...]
        rows = (nb - 1 - i) * BLOCK + lax.broadcasted_iota(jnp.int32, (BLOCK, LANES), 0)
        dpre = jnp.where(rows >= pad, dlf * jax.nn.sigmoid(-x), 0.0)
        dp_ref[...] = dpre
        partial = jnp.sum(dpre, axis=0, keepdims=True)

        @pl.when(i == 0)
        def _():
            db_ref[...] = partial

        @pl.when(i > 0)
        def _():
            db_ref[...] += partial

    rev = pl.BlockSpec((BLOCK, LANES), lambda i: (nb - 1 - i, 0))
    vec = pl.BlockSpec((1, LANES), lambda i: (0, 0))
    return pl.pallas_call(
        body, name=name, grid=(nb,), in_specs=[rev, vec, rev], out_specs=[rev, vec],
        out_shape=[jax.ShapeDtypeStruct((T, LANES), F32), jax.ShapeDtypeStruct((1, LANES), F32)],
        scratch_shapes=[pltpu.VMEM((1, LANES), F32), pltpu.VMEM((BLOCK, LANES), F32)],
        compiler_params=_cp(("arbitrary",)),
    )(pre, bias, dc)


def _loss_head(h, target, first, name):
    T, D = h.shape
    tr = BLOCK
    skip = first // tr

    def body(h_ref, t_ref, dh_ref, loss_ref):
        i = pl.program_id(0)

        @pl.when(i == 0)
        def _():
            loss_ref[...] = jnp.zeros_like(loss_ref)

        @pl.when(i < skip)
        def _():
            dh_ref[...] = jnp.zeros_like(dh_ref)

        @pl.when(i >= skip)
        def _():
            err = h_ref[...] - t_ref[...]
            dh_ref[...] = err * (1.0 / D)
            loss_ref[...] += jnp.sum(err * err) * (0.5 / D)

    row = pl.BlockSpec((tr, D), lambda i: (i, 0))
    return pl.pallas_call(
        body, name=name, grid=(T // tr,),
        in_specs=[row, pl.BlockSpec((tr, D), lambda i: (jnp.maximum(i - skip, 0), 0))],
        out_specs=[row, pl.BlockSpec((8, LANES), lambda i: (0, 0))],
        out_shape=[jax.ShapeDtypeStruct((T, D), F32), jax.ShapeDtypeStruct((8, LANES), F32)],
        compiler_params=_cp(("arbitrary",)),
    )(h, target)


def _adamw(w, g, m, v, name):
    R, C = w.shape
    tr = _tile(R, max(8, TILE_BUDGET // (C * 4 * 7 * 3)), 8)
    bc1, bc2 = 1.0 - ADAM_B1 ** ADAM_STEP, 1.0 - ADAM_B2 ** ADAM_STEP

    def body(w_ref, g_ref, m_ref, v_ref, d_ref, mo_ref, vo_ref):
        g_ = g_ref[...]
        m_ = ADAM_B1 * m_ref[...] + (1.0 - ADAM_B1) * g_
        v_ = ADAM_B2 * v_ref[...] + (1.0 - ADAM_B2) * (g_ * g_)
        d_ref[...] = -ADAM_LR * ((m_ / bc1) / (jnp.sqrt(v_ / bc2) + ADAM_EPS) + ADAM_WD * w_ref[...])
        mo_ref[...] = m_
        vo_ref[...] = v_

    blk = pl.BlockSpec((tr, C), lambda i: (i, 0))
    sds = jax.ShapeDtypeStruct((R, C), F32)
    return pl.pallas_call(
        body, name=name, grid=(R // tr,), in_specs=[blk] * 4, out_specs=[blk] * 3, out_shape=[sds] * 3,
        compiler_params=_cp(("parallel",)),
    )(w, g, m, v)


def _pick_head(c_blk, h):
    lane = lax.broadcasted_iota(jnp.int32, c_blk.shape, 1)
    return jnp.sum(jnp.where(lane == h, c_blk, 0.0), axis=1, keepdims=True)


def _attn_fwd(q, k, v, c, ct, *, blk, pad, name):
    T, D = q.shape
    H, nq = D // HEAD_DIM, T // blk
    scale = 1.0 / math.sqrt(HEAD_DIM)

    def body(q_ref, k_ref, v_ref, c_ref, ct_ref, o_ref, lse_ref):
        h, i = pl.program_id(0), pl.program_id(1)
        qb = q_ref[...]
        cq = _pick_head(c_ref[...], h)
        qpos = i * blk + lax.broadcasted_iota(jnp.int32, (blk, blk), 0)
        kio = lax.broadcasted_iota(jnp.int32, (blk, blk), 1)

        def step(j, carry):
            m, l, acc = carry
            off = pl.multiple_of(j * blk, blk)
            kb = k_ref[pl.ds(off, blk), :]
            vb = v_ref[pl.ds(off, blk), :]
            s = lax.dot_general(qb, kb, (((1,), (1,)), ((), ())), preferred_element_type=F32) * scale
            s = s + (cq - ct_ref[j])
            kpos = j * blk + kio
            s = jnp.where((kpos <= qpos) & (kpos >= pad), s, NEG)
            m_new = jnp.maximum(m, jnp.max(s, axis=1, keepdims=True))
            p = jnp.exp(s - m_new)
            alpha = jnp.exp(m - m_new)
            l = alpha * l + jnp.sum(p, axis=1, keepdims=True)
            acc = alpha * acc + jnp.dot(p.astype(BF16), vb, preferred_element_type=F32)
            return m_new, l, acc

        init = (jnp.full((blk, 1), NEG, F32), jnp.zeros((blk, 1), F32), jnp.zeros((blk, HEAD_DIM), F32))
        m, l, acc = lax.fori_loop(0, i + 1, step, init)
        rowpos = i * blk + lax.broadcasted_iota(jnp.int32, (blk, 1), 0)
        o_ref[...] = jnp.where(rowpos >= pad, acc / l, 0.0)
        lse_ref[...] = jnp.broadcast_to(m + jnp.log(l), (blk, LANES))

    return pl.pallas_call(
        body, name=name, grid=(H, nq),
        in_specs=[pl.BlockSpec((blk, HEAD_DIM), lambda h, i: (i, h)),
                  pl.BlockSpec((T, HEAD_DIM), lambda h, i: (0, h)),
                  pl.BlockSpec((T, HEAD_DIM), lambda h, i: (0, h)),
                  pl.BlockSpec((blk, LANES), lambda h, i: (i, 0)),
                  pl.BlockSpec((None, nq, 1, blk), lambda h, i: (h, 0, 0, 0))],
        out_specs=[pl.BlockSpec((blk, HEAD_DIM), lambda h, i: (i, h)),
                   pl.BlockSpec((None, blk, LANES), lambda h, i: (h, i, 0))],
        out_shape=[jax.ShapeDtypeStruct((T, D), F32), jax.ShapeDtypeStruct((H, T, LANES), F32)],
        compiler_params=_cp(("parallel", "arbitrary")),
    )(q, k, v, c, ct)


def _attn_bwd(q, k, v, o, do, lse, c, ct, prev, *, blk, pad, name):
    T, D = q.shape
    H, nq = D // HEAD_DIM, T // blk
    scale = 1.0 / math.sqrt(HEAD_DIM)
    has_prev = prev is not None

    def body(*refs):
        q_ref, k_ref, v_ref, o_ref, do_ref, lse_ref, c_ref, ct_ref = refs[:8]
        pdk_ref, pdv_ref, pdc_ref = refs[8:11] if has_prev else (None, None, None)
        dq_ref, dk_ref, dv_ref, dct_ref = refs[-4:]
        h, j = pl.program_id(0), pl.program_id(1)

        @pl.when(j == 0)
        def _():
            dq_ref[...] = jnp.zeros_like(dq_ref)

        kb, vb = k_ref[...], v_ref[...]
        ck = ct_ref[...]
        kpos = j * blk + lax.broadcasted_iota(jnp.int32, (blk, blk), 1)
        qio = lax.broadcasted_iota(jnp.int32, (blk, blk), 0)

        def step(i, carry):
            dk, dv, dck = carry
            off = pl.multiple_of(i * blk, blk)
            qb = q_ref[pl.ds(off, blk), :]
            dob = do_ref[pl.ds(off, blk), :]
            ob = o_ref[pl.ds(off, blk), :]
            lse_i = lse_ref[pl.ds(off, blk), :][:, 0:1]
            cq = _pick_head(c_ref[pl.ds(off, blk), :], h)
            delta = jnp.sum(dob.astype(F32) * ob, axis=1, keepdims=True)
            s = lax.dot_general(qb, kb, (((1,), (1,)), ((), ())), preferred_element_type=F32) * scale
            s = s + (cq - ck)
            qpos = i * blk + qio
            p = jnp.where((kpos <= qpos) & (kpos >= pad), jnp.exp(s - lse_i), 0.0)
            dp = lax.dot_general(dob, vb, (((1,), (1,)), ((), ())), preferred_element_type=F32)
            ds = p * (dp - delta)
            pb, dsb = p.astype(BF16), ds.astype(BF16)
            dv = dv + lax.dot_general(pb, dob, (((0,), (0,)), ((), ())), preferred_element_type=F32)
            dk = dk + lax.dot_general(dsb, qb, (((0,), (0,)), ((), ())), preferred_element_type=F32)
            dck = dck - jnp.sum(ds, axis=0, keepdims=True)
            dq_ref[pl.ds(off, blk), :] += jnp.dot(dsb, kb, preferred_element_type=F32) * scale
            return dk, dv, dck

        init = (jnp.zeros((blk, HEAD_DIM), F32), jnp.zeros((blk, HEAD_DIM), F32), jnp.zeros((1, blk), F32))
        dk, dv, dck = lax.fori_loop(j, nq, step, init)
        dk = dk * scale
        if has_prev:
            dk, dv, dck = dk + pdk_ref[...], dv + pdv_ref[...], dck + pdc_ref[...]
        dk_ref[...] = dk
        dv_ref[...] = dv
        dct_ref[...] = dck

    col = pl.BlockSpec((T, HEAD_DIM), lambda h, j: (0, h))
    kblk = pl.BlockSpec((blk, HEAD_DIM), lambda h, j: (j, h))
    ctb = pl.BlockSpec((None, None, 1, blk), lambda h, j: (h, j, 0, 0))
    in_specs = [col, kblk, kblk, col, col,
                pl.BlockSpec((None, T, LANES), lambda h, j: (h, 0, 0)),
                pl.BlockSpec((T, LANES), lambda h, j: (0, 0)), ctb]
    operands = [q, k, v, o, do, lse, c, ct]
    if has_prev:
        in_specs += [kblk, kblk, ctb]
        operands += list(prev)
    return pl.pallas_call(
        body, name=name, grid=(H, nq), in_specs=in_specs, out_specs=[col, kblk, kblk, ctb],
        out_shape=[jax.ShapeDtypeStruct((T, D), F32), jax.ShapeDtypeStruct((T, D), F32),
                   jax.ShapeDtypeStruct((T, D), F32), jax.ShapeDtypeStruct((H, nq, 1, blk), F32)],
        compiler_params=_cp(("parallel", "arbitrary")),
    )(*operands)


HBM_SPEC = pl.BlockSpec(memory_space=pltpu.HBM)


def _place():
    x, y, c = lax.axis_index("x"), lax.axis_index("y"), lax.axis_index("c")
    chips = [(1 - x, y), (x, 1 - y), (1 - x, 1 - y)]
    return x, y, c, chips


def _remote(src, dst, send_sems, recv_sems, k, to):
    return pltpu.make_async_remote_copy(src_ref=src, dst_ref=dst, send_sem=send_sems.at[k],
                                        recv_sem=recv_sems.at[k], device_id=to, device_id_type=MESH)


def _all_gather(shards, specs, name):
    n = len(shards)

    def full_shape(s, axis):
        return (s.shape[0] * N_CHIPS, s.shape[1]) if axis == 0 else (s.shape[0], s.shape[1] * N_CHIPS)

    def body(*refs):
        srcs, outs = refs[:n], refs[n:2 * n]
        send_sems, recv_sems, local_sems = refs[2 * n:]
        x, y, c, chips = _place()
        sibling = (x, y, 1 - c)

        def region(t, chip, half):
            rs, cs = shards[t].shape
            q = 2 * chip[0] + chip[1]
            axis, split = specs[t]
            nrow = rs // 2 if half is not None else rs
            r0 = 0 if half is None else half * nrow
            if axis == 0:
                return outs[t].at[pl.ds(q * rs + r0, nrow), :]
            return outs[t].at[pl.ds(r0, nrow), pl.ds(pl.multiple_of(q * cs, cs), cs)]

        def piece(t, half):
            rs = shards[t].shape[0]
            if half is None:
                return srcs[t]
            return srcs[t].at[pl.ds(half * (rs // 2), rs // 2), :]

        local = [pltpu.make_async_copy(srcs[t], region(t, (x, y), None), local_sems.at[t]) for t in range(n)]
        for cp in local:
            cp.start()
        sends = []
        for t in range(n):
            half = c if specs[t][1] else None
            for j, chip in enumerate(chips):
                cp = _remote(piece(t, half), region(t, (x, y), half), send_sems, recv_sems, 6 * t + j, (*chip, c))
                cp.start()
                sends.append(cp)
        for t in range(n):
            half = c if specs[t][1] else None
            for j, chip in enumerate(chips):
                landed = region(t, chip, half)
                _remote(landed, landed, send_sems, recv_sems, 6 * t + j, (*chip, c)).wait_recv()
                if specs[t][1]:
                    cp = _remote(landed, landed, send_sems, recv_sems, 6 * t + 3 + j, sibling)
                    cp.start()
                    sends.append(cp)
        for t in range(n):
            if specs[t][1]:
                for j, chip in enumerate(chips):
                    got = region(t, chip, 1 - c)
                    _remote(got, got, send_sems, recv_sems, 6 * t + 3 + j, sibling).wait_recv()
        for cp in sends:
            cp.wait_send()
        for cp in local:
            cp.wait()

    return pl.pallas_call(
        body, name=name, in_specs=[HBM_SPEC] * n, out_specs=[HBM_SPEC] * n,
        out_shape=[jax.ShapeDtypeStruct(full_shape(s, specs[t][0]), s.dtype) for t, s in enumerate(shards)],
        scratch_shapes=[pltpu.SemaphoreType.DMA((6 * n,)), pltpu.SemaphoreType.DMA((6 * n,)),
                        pltpu.SemaphoreType.DMA((n,))],
        compiler_params=pltpu.CompilerParams(has_side_effects=True),
    )(*shards)


def _grad_view(g, axis):
    R, C = g.shape
    nq = N_CHIPS if axis == 0 else 1
    return g.reshape(nq, 2, R // (2 * nq), C)


def _swap_halves(views, name):
    n = len(views)

    def body(*refs):
        srcs, outs, send_sems, recv_sems = refs[:n], refs[n:2 * n], refs[2 * n], refs[2 * n + 1]
        x, y, c, _ = _place()
        cps = [_remote(srcs[t].at[:, 1 - c], outs[t], send_sems, recv_sems, t, (x, y, 1 - c)) for t in range(n)]
        for cp in cps:
            cp.start()
        for cp in cps:
            cp.wait()

    return pl.pallas_call(
        body, name=name, in_specs=[HBM_SPEC] * n, out_specs=[HBM_SPEC] * n,
        out_shape=[jax.ShapeDtypeStruct((v.shape[0],) + v.shape[2:], v.dtype) for v in views],
        scratch_shapes=[pltpu.SemaphoreType.DMA((n,)), pltpu.SemaphoreType.DMA((n,))],
        compiler_params=pltpu.CompilerParams(has_side_effects=True),
    )(*views)


def _add_half(view, got, c_idx, name):
    nq, _, Rh, C = view.shape
    tr = _tile(Rh, max(16, (2 * 1024 * 1024) // (C * 2)), 16)

    def body(c_ref, a_ref, b_ref, o_ref):
        o_ref[...] = (a_ref[...].astype(F32) + b_ref[...].astype(F32)).astype(BF16)

    grid_spec = pltpu.PrefetchScalarGridSpec(
        num_scalar_prefetch=1, grid=(nq, Rh // tr),
        in_specs=[pl.BlockSpec((None, None, tr, C), lambda q, i, c_ref: (q, c_ref[0], i, 0)),
                  pl.BlockSpec((None, tr, C), lambda q, i, c_ref: (q, i, 0))],
        out_specs=pl.BlockSpec((None, tr, C), lambda q, i, c_ref: (q, i, 0)))
    return pl.pallas_call(
        body, name=name, grid_spec=grid_spec, out_shape=jax.ShapeDtypeStruct((nq, Rh, C), BF16),
        compiler_params=_cp(("parallel", "parallel")),
    )(c_idx, view, got)


def _scatter_chips(sums, axes, name):
    n = len(sums)

    def shard_cols(t):
        return sums[t].shape[2] if axes[t] == 0 else sums[t].shape[2] // N_CHIPS

    def body(*refs):
        srcs, outs, send_sems, recv_sems = refs[:n], refs[n:2 * n], refs[2 * n], refs[2 * n + 1]
        x, y, c, chips = _place()
        cps = []
        for t in range(n):
            for j, chip in enumerate(chips):
                q = 2 * chip[0] + chip[1]
                if axes[t] == 0:
                    src = srcs[t].at[q]
                else:
                    cs = shard_cols(t)
                    src = srcs[t].at[0, :, pl.ds(pl.multiple_of(q * cs, cs), cs)]
                cps.append(_remote(src, outs[t].at[j], send_sems, recv_sems, 3 * t + j, (*chip, c)))
        for cp in cps:
            cp.start()
        for cp in cps:
            cp.wait()

    return pl.pallas_call(
        body, name=name, in_specs=[HBM_SPEC] * n, out_specs=[HBM_SPEC] * n,
        out_shape=[jax.ShapeDtypeStruct((3, s.shape[1], shard_cols(t)), s.dtype) for t, s in enumerate(sums)],
        scratch_shapes=[pltpu.SemaphoreType.DMA((3 * n,)), pltpu.SemaphoreType.DMA((3 * n,))],
        compiler_params=pltpu.CompilerParams(has_side_effects=True),
    )(*sums)


def _sum_chips(own, axis, got, q_idx, name):
    _, Rh, cc = got.shape
    tr = _tile(Rh, max(16, (1024 * 1024) // (cc * 2)), 16)

    def body(q_ref, a_ref, b0, b1, b2, o_ref):
        f = lambda r: r[...].astype(F32)
        o_ref[...] = ((f(a_ref) + f(b0)) + f(b1)) + f(b2)

    if axis == 0:
        own_spec = pl.BlockSpec((None, tr, cc), lambda i, q_ref: (q_ref[0], i, 0))
    else:
        own_spec = pl.BlockSpec((None, tr, cc), lambda i, q_ref: (0, i, q_ref[0]))
    slot = lambda j: pl.BlockSpec((None, tr, cc), lambda i, q_ref, j=j: (j, i, 0))
    grid_spec = pltpu.PrefetchScalarGridSpec(
        num_scalar_prefetch=1, grid=(Rh // tr,), in_specs=[own_spec, slot(0), slot(1), slot(2)],
        out_specs=pl.BlockSpec((tr, cc), lambda i, q_ref: (i, 0)))
    return pl.pallas_call(
        body, name=name, grid_spec=grid_spec, out_shape=jax.ShapeDtypeStruct((Rh, cc), F32),
        compiler_params=_cp(("parallel",)),
    )(q_idx, own, got, got, got)


def _join_halves(halves, name):
    n = len(halves)

    def body(*refs):
        srcs, outs = refs[:n], refs[n:2 * n]
        send_sems, recv_sems, local_sems = refs[2 * n:]
        x, y, c, _ = _place()
        local = [pltpu.make_async_copy(srcs[t], outs[t].at[c], local_sems.at[t]) for t in range(n)]
        cps = [_remote(srcs[t], outs[t].at[c], send_sems, recv_sems, t, (x, y, 1 - c)) for t in range(n)]
        for cp in local + cps:
            cp.start()
        for t in range(n):
            cps[t].wait_send()
            _remote(srcs[t], outs[t].at[1 - c], send_sems, recv_sems, t, (x, y, 1 - c)).wait_recv()
        for cp in local:
            cp.wait()

    return pl.pallas_call(
        body, name=name, in_specs=[HBM_SPEC] * n, out_specs=[HBM_SPEC] * n,
        out_shape=[jax.ShapeDtypeStruct((2,) + h.shape, h.dtype) for h in halves],
        scratch_shapes=[pltpu.SemaphoreType.DMA((n,)), pltpu.SemaphoreType.DMA((n,)), pltpu.SemaphoreType.DMA((n,))],
        compiler_params=pltpu.CompilerParams(has_side_effects=True),
    )(*halves)


def _reduce_scatter(grads, axes, c_idx, q_idx, tag):
    views = [_grad_view(g, a) for g, a in zip(grads, axes)]
    got = _swap_halves(views, f"rs_swap_{tag}")
    sums = [_add_half(v, p, c_idx, f"rs_add_{tag}_{t}") for t, (v, p) in enumerate(zip(views, got))]
    recv = _scatter_chips(sums, axes, f"rs_chips_{tag}")
    halves = [_sum_chips(s, a, r, q_idx, f"rs_sum_{tag}_{t}") for t, (s, a, r) in enumerate(zip(sums, axes, recv))]
    joined = _join_halves(halves, f"rs_join_{tag}")
    return [j.reshape(2 * j.shape[1], j.shape[2]) for j in joined]


def _all_reduce_small(buf, name):
    R, C = buf.shape

    def body(b_ref, o_ref, slots, send_sems, recv_sems):
        x, y, c, _ = _place()
        me = 4 * x + 2 * y + c
        cps = []
        for k in range(1, 8):
            to = (x ^ (k >> 2), y ^ ((k >> 1) & 1), c ^ (k & 1))
            cps.append(pltpu.make_async_remote_copy(
                src_ref=b_ref, dst_ref=slots.at[me], send_sem=send_sems.at[k - 1], recv_sem=recv_sems.at[me],
                device_id=to, device_id_type=MESH))
        for cp in cps:
            cp.start()
        slots[me] = b_ref[...]
        for k in range(1, 8):
            src = me ^ k
            pltpu.make_async_remote_copy(
                src_ref=b_ref, dst_ref=slots.at[src], send_sem=send_sems.at[k - 1], recv_sem=recv_sems.at[src],
                device_id=(x, y, c), device_id_type=MESH).wait_recv()
        for cp in cps:
            cp.wait_send()
        total = slots[0]
        for d in range(1, 8):
            total = total + slots[d]
        o_ref[...] = total

    vm = pl.BlockSpec(memory_space=pltpu.VMEM)
    return pl.pallas_call(
        body, name=name, in_specs=[vm], out_specs=vm, out_shape=jax.ShapeDtypeStruct((R, C), F32),
        scratch_shapes=[pltpu.VMEM((8, R, C), F32), pltpu.SemaphoreType.DMA((7,)), pltpu.SemaphoreType.DMA((8,))],
        compiler_params=pltpu.CompilerParams(has_side_effects=True, vmem_limit_bytes=VMEM_LIMIT),
    )(buf)


def kernel(x, meta, a_norm, a_w_in, a_conv, a_w_out, kv_norm, w_kv, k_norm, w_f, b_f, b_norm, b_w_q, b_q_norm, b_w_o, ffn_norm, ffn_w_gu, ffn_w_down, loss_target, m_meta, m_a_norm, m_a_w_in, m_a_conv, m_a_w_out, m_kv_norm, m_w_kv, m_k_norm, m_w_f, m_b_f, m_b_norm, m_b_w_q, m_b_q_norm, m_b_w_o, m_ffn_norm, m_ffn_w_gu, m_ffn_w_down, v_meta, v_a_norm, v_a_w_in, v_a_conv, v_a_w_out, v_kv_norm, v_w_kv, v_k_norm, v_w_f, v_b_f, v_b_norm, v_b_w_q, v_b_q_norm, v_b_w_o, v_ffn_norm, v_ffn_w_gu, v_ffn_w_down):
    SEQ, D = x.shape[1], x.shape[2]
    n_meta = meta.shape[0]
    pad = BLOCK - n_meta
    first = pad + n_meta
    T = first + SEQ
    H = D // HEAD_DIM
    Ds = D // N_CHIPS
    n_a, n_b, depth = a_w_in.shape[0], b_w_q.shape[0], ffn_norm.shape[0]
    blk = _tile(T, 384, BLOCK)
    cx, cy, cc = lax.axis_index("x"), lax.axis_index("y"), lax.axis_index("c")
    q_me = 2 * cx + cy
    c_idx = jnp.reshape(cc, (1,)).astype(jnp.int32)
    q_idx = jnp.reshape(q_me, (1,)).astype(jnp.int32)

    n_col = n_meta + n_a + 3 * n_a
    col_pack = jnp.concatenate([meta, a_norm, a_conv.reshape(3 * n_a, Ds)], axis=0)
    col_full, w_f_full = _all_gather([col_pack, w_f], [(1, False), (0, False)], "ag_small")
    meta_f = col_full[:n_meta]
    a_norm_f = col_full[n_meta:n_meta + n_a]
    a_conv_f = col_full[n_meta + n_a:n_col].reshape(n_a, 3, D)
    w_fp = jnp.pad(w_f_full, ((0, 0), (0, LANES - H))).astype(BF16)
    b_fp = jnp.pad(b_f, (0, LANES - H)).reshape(1, LANES)

    def gathered(tag, items):
        shards = [_cast_layer(w if w.ndim == 3 else w[None], l, f"cast_{tag}_{i}") for i, (w, l, _) in enumerate(items)]
        return _all_gather(shards, [(ax, True) for _, _, ax in items], f"ag_{tag}")

    wa = [gathered(f"a{l}", [(a_w_in, l, 1), (a_w_out, l, 0), (ffn_w_gu, l, 1), (ffn_w_down, l, 0)]) for l in range(n_a)]
    (w_kv_b,) = gathered("kv", [(w_kv, 0, 1)])
    wb = [gathered(f"b{j}", [(b_w_q, j, 0), (b_w_o, j, 0), (ffn_w_gu, n_a + j, 1), (ffn_w_down, n_a + j, 0)])
          for j in range(n_b)]

    h = jnp.concatenate([jnp.zeros((pad, D), F32), meta_f, x[0]], axis=0)
    saved = []

    def ffn_fwd(h, layer, w_gu, w_down):
        xn = _rms_fwd(h, ffn_norm[layer:layer + 1], f"ffn_norm_{layer}")
        z = _matmul(xn, w_gu, mode="nn", out_dtype=F32, name=f"ffn_gu_{layer}", out_parts=2)
        act = _swiglu_fwd(z, f"swiglu_{layer}")
        return _matmul(act, w_down, mode="nn", out_dtype=F32, name=f"ffn_down_{layer}", res=h), (h, xn, z, act)

    for l in range(n_a):
        w_in, w_out, w_gu, w_down = wa[l]
        xn = _rms_fwd(h, a_norm_f[l:l + 1], f"a_norm_{l}")
        z = _matmul(xn, w_in, mode="nn", out_dtype=F32, name=f"a_in_{l}", out_parts=3)
        y = _gate_fwd(z, a_conv_f[l], f"a_gate_{l}")
        h2 = _matmul(y, w_out, mode="nn", out_dtype=F32, name=f"a_out_{l}", res=h)
        h3, ffn_saved = ffn_fwd(h2, l, w_gu, w_down)
        saved.append((h, xn, z, y, ffn_saved))
        h = h3

    h_kv = h
    xkv = _rms_fwd(h, kv_norm.reshape(1, D), "kv_norm")
    kvz = _matmul(xkv, w_kv_b, mode="nn", out_dtype=F32, name="kv_proj", out_parts=2)
    k_n = _headnorm_fwd(kvz, 0, k_norm.reshape(1, HEAD_DIM), "k_headnorm")
    v_b = _cast_part(kvz, 1, "v_cast")
    pre = _matmul(xkv, w_fp, mode="nn", out_dtype=F32, name="f_proj")
    c_cum = _logf_cumsum(pre, b_fp, pad, "logf_cumsum")
    c_t = c_cum[:, :H].T.reshape(H, T // blk, 1, blk)

    for j in range(n_b):
        layer = n_a + j
        w_q, w_o, w_gu, w_down = wb[j]
        xn = _rms_fwd(h, b_norm[j:j + 1], f"b_norm_{j}")
        qz = _matmul(xn, w_q, mode="nn", out_dtype=F32, name=f"b_q_{j}")[None]
        q_n = _headnorm_fwd(qz, 0, b_q_norm[j:j + 1], f"q_headnorm_{j}")
        o, lse = _attn_fwd(q_n, k_n, v_b, c_cum, c_t, blk=blk, pad=pad, name=f"attn_fwd_{j}")
        h2 = _matmul(o, w_o, mode="nn", out_dtype=F32, name=f"b_o_{j}", res=h)
        h3, ffn_saved = ffn_fwd(h2, layer, w_gu, w_down)
        saved.append((h, xn, qz, q_n, o, lse, ffn_saved))
        h = h3

    dh, loss_blk = _loss_head(h, loss_target[0], first, "loss_head")
    loss = lax.psum(loss_blk[0, 0], ("x", "y", "c"))

    big = {}
    small = {}

    def ffn_bwd(dh, layer, w_gu, w_down, ffn_saved):
        h_in, xn, z, act = ffn_saved
        da = _matmul(dh, w_down, mode="nt", out_dtype=F32, name=f"ffn_down_dx_{layer}")
        g_down = _matmul(act, dh, mode="tn", out_dtype=BF16, name=f"ffn_down_dw_{layer}")
        dz = _swiglu_bwd(z, da, f"swiglu_bwd_{layer}")
        dxn = _matmul(dz, w_gu, mode="nt", out_dtype=F32, name=f"ffn_gu_dx_{layer}")
        g_gu = _matmul(xn, dz, mode="tn", out_dtype=BF16, name=f"ffn_gu_dw_{layer}")
        dh, dg = _rms_bwd(h_in, ffn_norm[layer:layer + 1], dxn, dh, f"ffn_norm_bwd_{layer}")
        return dh, dg, g_gu, g_down

    d_ffn_norm, d_b_norm, d_q_norm, d_a_norm, d_a_conv = {}, {}, {}, {}, {}
    g_ffn_gu, g_ffn_down, g_b_q, g_b_o, g_a_in, g_a_out = {}, {}, {}, {}, {}, {}
    kv_prev = None
    for j in reversed(range(n_b)):
        layer = n_a + j
        w_q, w_o, w_gu, w_down = wb[j]
        h_in, xn, qz, q_n, o, lse, ffn_saved = saved[layer]
        dh, d_ffn_norm[layer], g_gu, g_down = ffn_bwd(dh, layer, w_gu, w_down, ffn_saved)
        do = _matmul(dh, w_o, mode="nt", out_dtype=BF16, name=f"b_o_dx_{j}")
        g_o = _matmul(o, dh, mode="tn", out_dtype=BF16, name=f"b_o_dw_{j}")
        dq, dk, dv, dct = _attn_bwd(q_n, k_n, v_b, o, do, lse, c_cum, c_t, kv_prev, blk=blk, pad=pad,
                                    name=f"attn_bwd_{j}")
        kv_prev = (dk, dv, dct)
        dqz, d_q_norm[j] = _headnorm_bwd(qz, 0, b_q_norm[j:j + 1], dq, f"q_headnorm_bwd_{j}")
        dxn = _matmul(dqz, w_q, mode="nt", out_dtype=F32, name=f"b_q_dx_{j}")
        g_q = _matmul(xn, dqz, mode="tn", out_dtype=BF16, name=f"b_q_dw_{j}")
        dh, d_b_norm[j] = _rms_bwd(h_in, b_norm[j:j + 1], dxn, dh, f"b_norm_bwd_{j}")
        g_b_q[j], g_b_o[j], g_ffn_gu[layer], g_ffn_down[layer] = _reduce_scatter(
            [g_q, g_o, g_gu, g_down], [0, 0, 1, 0], c_idx, q_idx, f"b{j}")

    dk, dv, dct = kv_prev
    dkz, d_k_norm = _headnorm_bwd(kvz, 0, k_norm.reshape(1, HEAD_DIM), dk, "k_headnorm_bwd")
    dvz = _cast_part(dv[None], 0, "dv_cast")
    dkv = jnp.stack([dkz, dvz])
    dc = jnp.pad(dct.reshape(H, T).T, ((0, 0), (0, LANES - H)))
    dpre, d_b_f = _logf_bwd(pre, b_fp, dc, pad, "logf_bwd")
    dxkv = _matmul(dkv, w_kv_b, mode="nt", out_dtype=F32, name="kv_proj_dx")
    dxkv = _matmul(dpre, w_fp, mode="nt", out_dtype=F32, name="f_proj_dx", res=dxkv)
    g_kv = _matmul(xkv, dkv, mode="tn", out_dtype=BF16, name="kv_proj_dw")
    d_w_f = _matmul(xkv, dpre, mode="tn", out_dtype=F32, name="f_proj_dw")
    dh, d_kv_norm = _rms_bwd(h_kv, kv_norm.reshape(1, D), dxkv, dh, "kv_norm_bwd")
    (g_w_kv,) = _reduce_scatter([g_kv], [1], c_idx, q_idx, "kv")

    for l in reversed(range(n_a)):
        w_in, w_out, w_gu, w_down = wa[l]
        h_in, xn, z, y, ffn_saved = saved[l]
        dh, d_ffn_norm[l], g_gu, g_down = ffn_bwd(dh, l, w_gu, w_down, ffn_saved)
        dy = _matmul(dh, w_out, mode="nt", out_dtype=F32, name=f"a_out_dx_{l}")
        g_out = _matmul(y, dh, mode="tn", out_dtype=BF16, name=f"a_out_dw_{l}")
        dz, d_a_conv[l] = _gate_bwd(z, a_conv_f[l], dy, f"a_gate_bwd_{l}")
        dxn = _matmul(dz, w_in, mode="nt", out_dtype=F32, name=f"a_in_dx_{l}")
        g_in = _matmul(xn, dz, mode="tn", out_dtype=BF16, name=f"a_in_dw_{l}")
        dh, d_a_norm[l] = _rms_bwd(h_in, a_norm_f[l:l + 1], dxn, dh, f"a_norm_bwd_{l}")
        g_a_in[l], g_a_out[l], g_ffn_gu[l], g_ffn_down[l] = _reduce_scatter(
            [g_in, g_out, g_gu, g_down], [1, 0, 1, 0], c_idx, q_idx, f"a{l}")

    grad_x = dh[first:][None]

    widen = lambda v: jnp.pad(v, ((0, 0), (0, D - v.shape[1])))
    pieces = [
        dh[pad:first],
        jnp.concatenate([d_a_norm[l] for l in range(n_a)], axis=0),
        jnp.concatenate([d_a_conv[l] for l in range(n_a)], axis=0),
        d_kv_norm,
        widen(d_k_norm),
        d_w_f[:, :H].T,
        widen(d_b_f),
        jnp.concatenate([d_b_norm[j] for j in range(n_b)], axis=0),
        widen(jnp.concatenate([d_q_norm[j] for j in range(n_b)], axis=0)),
        jnp.concatenate([d_ffn_norm[l] for l in range(depth)], axis=0),
    ]
    rows = [p.shape[0] for p in pieces]
    total_rows = sum(rows)
    pack = jnp.concatenate(pieces + [jnp.zeros((-total_rows % 8, D), F32)], axis=0)
    red = _all_reduce_small(pack, "ar_small")
    offs = [sum(rows[:i]) for i in range(len(rows))]
    take = lambda i: red[offs[i]:offs[i] + rows[i]]
    mine = lambda a: lax.dynamic_slice_in_dim(a, q_me * Ds, Ds, axis=1)
    grads = {
        "meta": mine(take(0)),
        "a_norm": mine(take(1)),
        "a_w_in": jnp.stack([g_a_in[l] for l in range(n_a)]),
        "a_conv": mine(take(2)).reshape(n_a, 3, Ds),
        "a_w_out": jnp.stack([g_a_out[l] for l in range(n_a)]),
        "kv_norm": take(3).reshape(D),
        "w_kv": g_w_kv,
        "k_norm": take(4)[0, :HEAD_DIM],
        "w_f": mine(take(5)).T,
        "b_f": take(6)[0, :H],
        "b_norm": take(7),
        "b_w_q": jnp.stack([g_b_q[j] for j in range(n_b)]),
        "b_q_norm": take(8)[:, :HEAD_DIM],
        "b_w_o": jnp.stack([g_b_o[j] for j in range(n_b)]),
        "ffn_norm": take(9),
        "ffn_w_gu": jnp.stack([g_ffn_gu[l] for l in range(depth)]),
        "ffn_w_down": jnp.stack([g_ffn_down[l] for l in range(depth)]),
    }
    weights = dict(meta=meta, a_norm=a_norm, a_w_in=a_w_in, a_conv=a_conv, a_w_out=a_w_out, kv_norm=kv_norm, w_kv=w_kv,
                   k_norm=k_norm, w_f=w_f, b_f=b_f, b_norm=b_norm, b_w_q=b_w_q, b_q_norm=b_q_norm, b_w_o=b_w_o,
                   ffn_norm=ffn_norm, ffn_w_gu=ffn_w_gu, ffn_w_down=ffn_w_down)
    m_in = dict(meta=m_meta, a_norm=m_a_norm, a_w_in=m_a_w_in, a_conv=m_a_conv, a_w_out=m_a_w_out, kv_norm=m_kv_norm,
                w_kv=m_w_kv, k_norm=m_k_norm, w_f=m_w_f, b_f=m_b_f, b_norm=m_b_norm, b_w_q=m_b_w_q,
                b_q_norm=m_b_q_norm, b_w_o=m_b_w_o, ffn_norm=m_ffn_norm, ffn_w_gu=m_ffn_w_gu, ffn_w_down=m_ffn_w_down)
    v_in = dict(meta=v_meta, a_norm=v_a_norm, a_w_in=v_a_w_in, a_conv=v_a_conv, a_w_out=v_a_w_out, kv_norm=v_kv_norm,
                w_kv=v_w_kv, k_norm=v_k_norm, w_f=v_w_f, b_f=v_b_f, b_norm=v_b_norm, b_w_q=v_b_w_q,
                b_q_norm=v_b_q_norm, b_w_o=v_b_w_o, ffn_norm=v_ffn_norm, ffn_w_gu=v_ffn_w_gu, ffn_w_down=v_ffn_w_down)

    deltas, new_m, new_v = {}, {}, {}
    for name, w in weights.items():
        shape = w.shape
        two_d = (1, shape[0]) if w.ndim == 1 else (math.prod(shape[:-1]), shape[-1])
        r2 = lambda a: a.reshape(two_d)
        d_, m_, v_ = _adamw(r2(w), r2(grads[name]), r2(m_in[name]), r2(v_in[name]), f"adamw_{name}")
        deltas[name], new_m[name], new_v[name] = d_.reshape(shape), m_.reshape(shape), v_.reshape(shape)
        grads[name] = grads[name].reshape(shape)

    names = list(weights)
    return (loss, grad_x, *[grads[n] for n in names], *[deltas[n] for n in names],
            *[new_m[n] for n in names], *[new_v[n] for n in names])
```

```python
import functools
import math

import jax
import jax.numpy as jnp
from jax import lax
from jax.experimental import pallas as pl
from jax.experimental.pallas import tpu as pltpu

F32 = jnp.float32
BF16 = jnp.bfloat16
HEAD_DIM = 128
BLOCK = 128
LANES = 128
EPS = 1e-6
NEG = -1e30
ADAM_LR, ADAM_B1, ADAM_B2, ADAM_EPS, ADAM_WD, ADAM_STEP = 0.001, 0.9, 0.999, 1e-08, 0.01, 10
VMEM_LIMIT = 56 * 1024 * 1024
TILE_BUDGET = 40 * 1024 * 1024
MESH = pl.DeviceIdType.MESH
N_CHIPS = 4


def _tile(n, target, align):
    best = None
    for d in range(align, min(n, target) + 1, align):
        if n % d == 0:
            best = d
    return best if best is not None else n


def _cp(sem):
    return pltpu.CompilerParams(dimension_semantics=sem, vmem_limit_bytes=VMEM_LIMIT)


def _matmul(a, b, *, mode, out_dtype, name, res=None, out_parts=1, dep=None):
    a_parts = a.shape[0] if a.ndim == 3 else 1
    b_parts = b.shape[0] if b.ndim == 3 else 1
    if mode == "nn":
        M, Kp = a.shape[-2:]
        K, N = Kp * a_parts, b.shape[1]
    elif mode == "nt":
        M, Kp = a.shape[-2:]
        K, N = Kp * a_parts, b.shape[0]
    else:
        K, M = a.shape
        Kp = K
        N = b.shape[-1] * b_parts
    Np = N // max(b_parts, out_parts)
    tm = _tile(M, 1024, LANES) if mode == "tn" else _tile(M, 1056, 16)
    tn = _tile(Np, 1536, LANES)
    tk = _tile(Kp, 1056, 16) if mode == "tn" else _tile(Kp, 2048, LANES)
    ab, bb, ob = a.dtype.itemsize, b.dtype.itemsize, jnp.dtype(out_dtype).itemsize

    def vmem(tm_):
        blocks = 2 * (tm_ * tk * ab + tk * tn * bb + tm_ * tn * ob + (tm_ * tn * 4 if res is not None else 0))
        temps = tm_ * tn * 8 + (tm_ * tk * 2 if ab == 4 else 0) + (tk * tn * 2 if bb == 4 else 0)
        return blocks + temps

    while vmem(tm) > TILE_BUDGET and tm > 256:
        tm = _tile(M, tm // 2, LANES if mode == "tn" else 16)
    ni, nj, nk = M // tm, N // tn, K // tk
    nkp, njp = Kp // tk, Np // tn

    if mode == "tn":
        a_spec = pl.BlockSpec((tk, tm), lambda i, j, k: (k, i))
    elif a_parts > 1:
        a_spec = pl.BlockSpec((None, tm, tk), lambda i, j, k: (k // nkp, i, k % nkp))
    else:
        a_spec = pl.BlockSpec((tm, tk), lambda i, j, k: (i, k))
    if mode == "nn":
        b_spec = pl.BlockSpec((tk, tn), lambda i, j, k: (k, j))
    elif mode == "nt":
        b_spec = pl.BlockSpec((tn, tk), lambda i, j, k: (j, k))
    elif b_parts > 1:
        b_spec = pl.BlockSpec((None, tk, tn), lambda i, j, k: (j // njp, k, j % njp))
    else:
        b_spec = pl.BlockSpec((tk, tn), lambda i, j, k: (k, j))
    in_specs = [a_spec, b_spec]
    operands = [a, b]
    if res is not None:
        in_specs.append(pl.BlockSpec((tm, tn), lambda i, j, k: (i, j)))
        operands.append(res)
    if dep is not None:
        in_specs.append(pl.BlockSpec((8, LANES), lambda i, j, k: (0, 0)))
        operands.append(dep)
    n_in = len(operands)
    if out_parts > 1:
        out_spec = pl.BlockSpec((None, tm, tn), lambda i, j, k: (j // njp, i, j % njp))
        out_shape = jax.ShapeDtypeStruct((out_parts, M, Np), out_dtype)
    else:
        out_spec = pl.BlockSpec((tm, tn), lambda i, j, k: (i, j))
        out_shape = jax.ShapeDtypeStruct((M, N), out_dtype)
    dims = {"nn": (((1,), (0,)), ((), ())), "nt": (((1,), (1,)), ((), ())), "tn": (((0,), (0,)), ((), ()))}[mode]
    has_res = res is not None

    def body(*refs):
        a_ref, b_ref = refs[0], refs[1]
        res_ref = refs[2] if has_res else None
        o_ref = refs[n_in]
        d = lax.dot_general(a_ref[...].astype(BF16), b_ref[...].astype(BF16), dims, preferred_element_type=F32)
        if nk == 1:
            if has_res:
                d = d + res_ref[...]
            o_ref[...] = d.astype(out_dtype)
        else:
            acc_ref = refs[-1]
            k = pl.program_id(2)

            @pl.when(k == 0)
            def _():
                acc_ref[...] = d

            @pl.when(k > 0)
            def _():
                acc_ref[...] += d

            @pl.when(k == nk - 1)
            def _():
                r = acc_ref[...]
                if has_res:
                    r = r + res_ref[...]
                o_ref[...] = r.astype(out_dtype)

    return pl.pallas_call(
        body, name=name, grid=(ni, nj, nk), in_specs=in_specs, out_specs=out_spec, out_shape=out_shape,
        scratch_shapes=[pltpu.VMEM((tm, tn), F32)] if nk > 1 else [],
        compiler_params=_cp(("parallel", "parallel", "arbitrary")),
    )(*operands)


def _cast_into_full(w3, layer, axis, q_idx, name):
    _, R, C = w3.shape
    tr = _tile(R, max(16, (4 * 1024 * 1024) // (C * 4)), 16)
    nb = R // tr

    def body(q_ref, w_ref, o_ref):
        o_ref[...] = w_ref[...].astype(BF16)

    if axis == 0:
        out_spec = pl.BlockSpec((tr, C), lambda i, q_ref: (q_ref[0] * nb + i, 0))
        full = (N_CHIPS * R, C)
    else:
        out_spec = pl.BlockSpec((tr, C), lambda i, q_ref: (i, q_ref[0]))
        full = (R, N_CHIPS * C)
    grid_spec = pltpu.PrefetchScalarGridSpec(
        num_scalar_prefetch=1, grid=(nb,),
        in_specs=[pl.BlockSpec((None, tr, C), lambda i, q_ref: (layer, i, 0))], out_specs=out_spec)
    return pl.pallas_call(
        body, name=name, grid_spec=grid_spec, out_shape=jax.ShapeDtypeStruct(full, BF16),
        compiler_params=_cp(("parallel",)),
    )(q_idx, w3)


def _rms_fwd(h, g, name, dep=None):
    T, D = h.shape
    tr = _tile(T, 528, 16)

    def body(h_ref, g_ref, *rest):
        o_ref = rest[-1]
        x = h_ref[...]
        r = lax.rsqrt(jnp.mean(x * x, axis=-1, keepdims=True) + EPS)
        o_ref[...] = (x * r * g_ref[...]).astype(BF16)

    in_specs = [pl.BlockSpec((tr, D), lambda i: (i, 0)), pl.BlockSpec((1, D), lambda i: (0, 0))]
    operands = [h, g]
    if dep is not None:
        in_specs.append(pl.BlockSpec((8, LANES), lambda i: (0, 0)))
        operands.append(dep)
    return pl.pallas_call(
        body, name=name, grid=(T // tr,), in_specs=in_specs,
        out_specs=pl.BlockSpec((tr, D), lambda i: (i, 0)),
        out_shape=jax.ShapeDtypeStruct((T, D), BF16), compiler_params=_cp(("parallel",)),
    )(*operands)


def _rms_bwd(h, g, dxn, dh, name):
    T, D = h.shape
    tr = _tile(T, 264, 8)

    def body(h_ref, g_ref, dxn_ref, dh_ref, o_ref, dg_ref):
        x = h_ref[...]
        r = lax.rsqrt(jnp.mean(x * x, axis=-1, keepdims=True) + EPS)
        xh = x * r
        dy = dxn_ref[...]
        dxh = dy * g_ref[...]
        dx = r * (dxh - xh * jnp.mean(dxh * xh, axis=-1, keepdims=True))
        o_ref[...] = dh_ref[...] + dx
        part = jnp.sum(dy * xh, axis=0, keepdims=True)

        @pl.when(pl.program_id(0) == 0)
        def _():
            dg_ref[...] = part

        @pl.when(pl.program_id(0) > 0)
        def _():
            dg_ref[...] += part

    row = pl.BlockSpec((tr, D), lambda i: (i, 0))
    vec = pl.BlockSpec((1, D), lambda i: (0, 0))
    return pl.pallas_call(
        body, name=name, grid=(T // tr,), in_specs=[row, vec, row, row], out_specs=[row, vec],
        out_shape=[jax.ShapeDtypeStruct((T, D), F32), jax.ShapeDtypeStruct((1, D), F32)],
        compiler_params=_cp(("arbitrary",)),
    )(h, g, dxn, dh)


def _shift_down(u, n, rows):
    return jnp.where(rows >= n, pltpu.roll(u, n, 0), 0.0)


def _shift_up(u, n, rows, total):
    return jnp.where(rows < total - n, pltpu.roll(u, total - n, 0), 0.0)


def _gate_fwd(z, conv_w, name):
    _, T, D = z.shape
    tc = LANES

    def body(b_ref, c_ref, h_ref, w_ref, y_ref):
        rows = lax.broadcasted_iota(jnp.int32, (T, tc), 0)
        u = c_ref[...] * h_ref[...]
        w0, w1, w2 = w_ref[0:1, :], w_ref[1:2, :], w_ref[2:3, :]
        conv = u * w2 + _shift_down(u, 1, rows) * w1 + _shift_down(u, 2, rows) * w0
        y_ref[...] = (b_ref[...] * conv).astype(BF16)

    part = lambda p: pl.BlockSpec((None, T, tc), lambda j, p=p: (p, 0, j))
    return pl.pallas_call(
        body, name=name, grid=(D // tc,),
        in_specs=[part(0), part(1), part(2), pl.BlockSpec((3, tc), lambda j: (0, j))],
        out_specs=pl.BlockSpec((T, tc), lambda j: (0, j)),
        out_shape=jax.ShapeDtypeStruct((T, D), BF16), compiler_params=_cp(("parallel",)),
    )(z, z, z, conv_w)


def _gate_bwd(z, conv_w, dy, name):
    _, T, D = z.shape
    tc = LANES

    def body(b_ref, c_ref, h_ref, w_ref, dy_ref, dz_ref, dw_ref):
        rows = lax.broadcasted_iota(jnp.int32, (T, tc), 0)
        cg, hh = c_ref[...], h_ref[...]
        u = cg * hh
        w0, w1, w2 = w_ref[0:1, :], w_ref[1:2, :], w_ref[2:3, :]
        s1, s2 = _shift_down(u, 1, rows), _shift_down(u, 2, rows)
        g = dy_ref[...]
        dz_ref[0] = (g * (u * w2 + s1 * w1 + s2 * w0)).astype(BF16)
        dconv = g * b_ref[...]
        dw_ref[0:1, :] = jnp.sum(dconv * s2, axis=0, keepdims=True)
        dw_ref[1:2, :] = jnp.sum(dconv * s1, axis=0, keepdims=True)
        dw_ref[2:3, :] = jnp.sum(dconv * u, axis=0, keepdims=True)
        du = dconv * w2 + _shift_up(dconv, 1, rows, T) * w1 + _shift_up(dconv, 2, rows, T) * w0
        dz_ref[1] = (du * hh).astype(BF16)
        dz_ref[2] = (du * cg).astype(BF16)

    part = lambda p: pl.BlockSpec((None, T, tc), lambda j, p=p: (p, 0, j))
    return pl.pallas_call(
        body, name=name, grid=(D // tc,),
        in_specs=[part(0), part(1), part(2), pl.BlockSpec((3, tc), lambda j: (0, j)),
                  pl.BlockSpec((T, tc), lambda j: (0, j))],
        out_specs=[pl.BlockSpec((3, T, tc), lambda j: (0, 0, j)), pl.BlockSpec((3, tc), lambda j: (0, j))],
        out_shape=[jax.ShapeDtypeStruct((3, T, D), BF16), jax.ShapeDtypeStruct((3, D), F32)],
        compiler_params=_cp(("parallel",)),
    )(z, z, z, conv_w, dy)


def _swiglu_fwd(z, name):
    _, T, Fd = z.shape
    tr, tc = _tile(T, 528, 16), _tile(Fd, 512, LANES)

    def body(g_ref, u_ref, o_ref):
        g = g_ref[...]
        o_ref[...] = (g * jax.nn.sigmoid(g) * u_ref[...]).astype(BF16)

    part = lambda p: pl.BlockSpec((None, tr, tc), lambda i, j, p=p: (p, i, j))
    return pl.pallas_call(
        body, name=name, grid=(T // tr, Fd // tc), in_specs=[part(0), part(1)],
        out_specs=pl.BlockSpec((tr, tc), lambda i, j: (i, j)),
        out_shape=jax.ShapeDtypeStruct((T, Fd), BF16), compiler_params=_cp(("parallel", "parallel")),
    )(z, z)


def _swiglu_bwd(z, da, name):
    _, T, Fd = z.shape
    tr, tc = _tile(T, 528, 16), _tile(Fd, 512, LANES)

    def body(g_ref, u_ref, da_ref, dz_ref):
        g, d = g_ref[...], da_ref[...]
        s = jax.nn.sigmoid(g)
        dz_ref[0] = (d * u_ref[...] * (s * (1.0 + g * (1.0 - s)))).astype(BF16)
        dz_ref[1] = (d * (g * s)).astype(BF16)

    part = lambda p: pl.BlockSpec((None, tr, tc), lambda i, j, p=p: (p, i, j))
    return pl.pallas_call(
        body, name=name, grid=(T // tr, Fd // tc),
        in_specs=[part(0), part(1), pl.BlockSpec((tr, tc), lambda i, j: (i, j))],
        out_specs=pl.BlockSpec((2, tr, tc), lambda i, j: (0, i, j)),
        out_shape=jax.ShapeDtypeStruct((2, T, Fd), BF16), compiler_params=_cp(("parallel", "parallel")),
    )(z, z, da)


def _headnorm_fwd(z, part, g, name):
    _, T, D = z.shape
    tr = _tile(T, 1056, 16)

    def body(z_ref, g_ref, o_ref):
        x = z_ref[...]
        r = lax.rsqrt(jnp.mean(x * x, axis=-1, keepdims=True) + EPS)
        o_ref[...] = (x * r * g_ref[...]).astype(BF16)

    return pl.pallas_call(
        body, name=name, grid=(T // tr, D // HEAD_DIM),
        in_specs=[pl.BlockSpec((None, tr, HEAD_DIM), lambda i, h: (part, i, h)),
                  pl.BlockSpec((1, HEAD_DIM), lambda i, h: (0, 0))],
        out_specs=pl.BlockSpec((tr, HEAD_DIM), lambda i, h: (i, h)),
        out_shape=jax.ShapeDtypeStruct((T, D), BF16), compiler_params=_cp(("parallel", "parallel")),
    )(z, g)


def _headnorm_bwd(z, part, g, dy, name):
    _, T, D = z.shape
    tr = _tile(T, 1056, 16)

    def body(z_ref, g_ref, dy_ref, dz_ref, dg_ref):
        x = z_ref[...]
        r = lax.rsqrt(jnp.mean(x * x, axis=-1, keepdims=True) + EPS)
        xh = x * r
        dy_ = dy_ref[...]
        dxh = dy_ * g_ref[...]
        dz_ref[...] = (r * (dxh - xh * jnp.mean(dxh * xh, axis=-1, keepdims=True))).astype(BF16)
        partial = jnp.sum(dy_ * xh, axis=0, keepdims=True)
        first = (pl.program_id(0) == 0) & (pl.program_id(1) == 0)

        @pl.when(first)
        def _():
            dg_ref[...] = partial

        @pl.when(jnp.logical_not(first))
        def _():
            dg_ref[...] += partial

    blk = pl.BlockSpec((tr, HEAD_DIM), lambda i, h: (i, h))
    vec = pl.BlockSpec((1, HEAD_DIM), lambda i, h: (0, 0))
    return pl.pallas_call(
        body, name=name, grid=(T // tr, D // HEAD_DIM),
        in_specs=[pl.BlockSpec((None, tr, HEAD_DIM), lambda i, h: (part, i, h)), vec, blk],
        out_specs=[blk, vec],
        out_shape=[jax.ShapeDtypeStruct((T, D), BF16), jax.ShapeDtypeStruct((1, HEAD_DIM), F32)],
        compiler_params=_cp(("arbitrary", "arbitrary")),
    )(z, g, dy)


def _cast_part(z, part, name):
    _, T, D = z.shape
    tr = _tile(T, 528, 16)

    def body(z_ref, o_ref):
        o_ref[...] = z_ref[...].astype(BF16)

    return pl.pallas_call(
        body, name=name, grid=(T // tr,),
        in_specs=[pl.BlockSpec((None, tr, D), lambda i: (part, i, 0))],
        out_specs=pl.BlockSpec((tr, D), lambda i: (i, 0)),
        out_shape=jax.ShapeDtypeStruct((T, D), BF16), compiler_params=_cp(("parallel",)),
    )(z)


def _split3(x):
    a = x.astype(BF16)
    r = x - a.astype(F32)
    b = r.astype(BF16)
    c = (r - b.astype(F32)).astype(BF16)
    return a, b, c


def _tri_matmul(tri, x):
    a, b, c = _split3(x)
    dot = lambda v: jnp.dot(tri, v, preferred_element_type=F32)
    return (dot(c) + dot(b)) + dot(a)


def _logf_cumsum(pre, bias, pad, name):
    T = pre.shape[0]
    nb = T // BLOCK

    def body(p_ref, b_ref, c_ref, carry):
        i = pl.program_id(0)

        @pl.when(i == 0)
        def _():
            carry[...] = jnp.zeros_like(carry)

        x = p_ref[...] + b_ref[...]
        lf = jnp.minimum(x, 0.0) - jnp.log(1.0 + jnp.exp(-jnp.abs(x)))
        rows = i * BLOCK + lax.broadcasted_iota(jnp.int32, (BLOCK, LANES), 0)
        lf = jnp.where(rows >= pad, lf, 0.0)
        r = lax.broadcasted_iota(jnp.int32, (BLOCK, BLOCK), 0)
        c = lax.broadcasted_iota(jnp.int32, (BLOCK, BLOCK), 1)
        tri = jnp.where(c <= r, 1.0, 0.0).astype(BF16)
        c_ref[...] = _tri_matmul(tri, lf) + carry[...]
        carry[...] = c_ref[BLOCK - 1:BLOCK, :]

    return pl.pallas_call(
        body, name=name, grid=(nb,),
        in_specs=[pl.BlockSpec((BLOCK, LANES), lambda i: (i, 0)), pl.BlockSpec((1, LANES), lambda i: (0, 0))],
        out_specs=pl.BlockSpec((BLOCK, LANES), lambda i: (i, 0)),
        out_shape=jax.ShapeDtypeStruct((T, LANES), F32),
        scratch_shapes=[pltpu.VMEM((1, LANES), F32)], compiler_params=_cp(("arbitrary",)),
    )(pre, bias)


def _logf_bwd(pre, bias, dc, pad, name):
    T = pre.shape[0]
    nb = T // BLOCK

    def body(p_ref, b_ref, dc_ref, dp_ref, db_ref, carry, dlf_ref):
        i = pl.program_id(0)

        @pl.when(i == 0)
        def _():
            carry[...] = jnp.zeros_like(carry)

        r = lax.broadcasted_iota(jnp.int32, (BLOCK, BLOCK), 0)
        c = lax.broadcasted_iota(jnp.int32, (BLOCK, BLOCK), 1)
        tri = jnp.where(c >= r, 1.0, 0.0).astype(BF16)
        dlf_ref[...] = _tri_matmul(tri, dc_ref[...]) + carry[...]
        carry[...] = dlf_ref[0:1, :]
        dlf = dlf_ref[...]
        x = p_ref[...] + b_ref[...]
        rows = (nb - 1 - i) * BLOCK + lax.broadcasted_iota(jnp.int32, (BLOCK, LANES), 0)
        dpre = jnp.where(rows >= pad, dlf * jax.nn.sigmoid(-x), 0.0)
        dp_ref[...] = dpre
        partial = jnp.sum(dpre, axis=0, keepdims=True)

        @pl.when(i == 0)
        def _():
            db_ref[...] = partial

        @pl.when(i > 0)
        def _():
            db_ref[...] += partial

    rev = pl.BlockSpec((BLOCK, LANES), lambda i: (nb - 1 - i, 0))
    vec = pl.BlockSpec((1, LANES), lambda i: (0, 0))
    return pl.pallas_call(
        body, name=name, grid=(nb,), in_specs=[rev, vec, rev], out_specs=[rev, vec],
        out_shape=[jax.ShapeDtypeStruct((T, LANES), F32), jax.ShapeDtypeStruct((1, LANES), F32)],
        scratch_shapes=[pltpu.VMEM((1, LANES), F32), pltpu.VMEM((BLOCK, LANES), F32)],
        compiler_params=_cp(("arbitrary",)),
    )(pre, bias, dc)


def _loss_head(h, target, first, name):
    T, D = h.shape
    tr = BLOCK
    skip = first // tr

    def body(h_ref, t_ref, dh_ref, loss_ref):
        i = pl.program_id(0)

        @pl.when(i == 0)
        def _():
            loss_ref[...] = jnp.zeros_like(loss_ref)

        @pl.when(i < skip)
        def _():
            dh_ref[...] = jnp.zeros_like(dh_ref)

        @pl.when(i >= skip)
        def _():
            err = h_ref[...] - t_ref[...]
            dh_ref[...] = err * (1.0 / D)
            loss_ref[...] += jnp.sum(err * err) * (0.5 / D)

    row = pl.BlockSpec((tr, D), lambda i: (i, 0))
    return pl.pallas_call(
        body, name=name, grid=(T // tr,),
        in_specs=[row, pl.BlockSpec((tr, D), lambda i: (jnp.maximum(i - skip, 0), 0))],
        out_specs=[row, pl.BlockSpec((8, LANES), lambda i: (0, 0))],
        out_shape=[jax.ShapeDtypeStruct((T, D), F32), jax.ShapeDtypeStruct((8, LANES), F32)],
        compiler_params=_cp(("arbitrary",)),
    )(h, target)


def _adamw(w, g, m, v, name):
    R, C = w.shape
    tr = _tile(R, max(8, TILE_BUDGET // (C * 4 * 7 * 3)), 8)
    bc1, bc2 = 1.0 - ADAM_B1 ** ADAM_STEP, 1.0 - ADAM_B2 ** ADAM_STEP

    def body(w_ref, g_ref, m_ref, v_ref, d_ref, mo_ref, vo_ref):
        g_ = g_ref[...]
        m_ = ADAM_B1 * m_ref[...] + (1.0 - ADAM_B1) * g_
        v_ = ADAM_B2 * v_ref[...] + (1.0 - ADAM_B2) * (g_ * g_)
        d_ref[...] = -ADAM_LR * ((m_ / bc1) / (jnp.sqrt(v_ / bc2) + ADAM_EPS) + ADAM_WD * w_ref[...])
        mo_ref[...] = m_
        vo_ref[...] = v_

    blk = pl.BlockSpec((tr, C), lambda i: (i, 0))
    sds = jax.ShapeDtypeStruct((R, C), F32)
    return pl.pallas_call(
        body, name=name, grid=(R // tr,), in_specs=[blk] * 4, out_specs=[blk] * 3, out_shape=[sds] * 3,
        compiler_params=_cp(("parallel",)),
    )(w, g, m, v)


def _pick_head(c_blk, h):
    lane = lax.broadcasted_iota(jnp.int32, c_blk.shape, 1)
    return jnp.sum(jnp.where(lane == h, c_blk, 0.0), axis=1, keepdims=True)


def _attn_fwd(q, k, v, c, ct, *, blk, pad, name):
    T, D = q.shape
    H, nq = D // HEAD_DIM, T // blk
    scale = 1.0 / math.sqrt(HEAD_DIM)

    def body(q_ref, k_ref, v_ref, c_ref, ct_ref, o_ref, lse_ref):
        h, i = pl.program_id(0), pl.program_id(1)
        qb = q_ref[...]
        cq = _pick_head(c_ref[...], h)
        qpos = i * blk + lax.broadcasted_iota(jnp.int32, (blk, blk), 0)
        kio = lax.broadcasted_iota(jnp.int32, (blk, blk), 1)

        def step(j, carry):
            m, l, acc = carry
            off = pl.multiple_of(j * blk, blk)
            kb = k_ref[pl.ds(off, blk), :]
            vb = v_ref[pl.ds(off, blk), :]
            s = lax.dot_general(qb, kb, (((1,), (1,)), ((), ())), preferred_element_type=F32) * scale
            s = s + (cq - ct_ref[j])
            kpos = j * blk + kio
            s = jnp.where((kpos <= qpos) & (kpos >= pad), s, NEG)
            m_new = jnp.maximum(m, jnp.max(s, axis=1, keepdims=True))
            p = jnp.exp(s - m_new)
            alpha = jnp.exp(m - m_new)
            l = alpha * l + jnp.sum(p, axis=1, keepdims=True)
            acc = alpha * acc + jnp.dot(p.astype(BF16), vb, preferred_element_type=F32)
            return m_new, l, acc

        init = (jnp.full((blk, 1), NEG, F32), jnp.zeros((blk, 1), F32), jnp.zeros((blk, HEAD_DIM), F32))
        m, l, acc = lax.fori_loop(0, i + 1, step, init)
        rowpos = i * blk + lax.broadcasted_iota(jnp.int32, (blk, 1), 0)
        o_ref[...] = jnp.where(rowpos >= pad, acc / l, 0.0)
        lse_ref[...] = jnp.broadcast_to(m + jnp.log(l), (blk, LANES))

    return pl.pallas_call(
        body, name=name, grid=(H, nq),
        in_specs=[pl.BlockSpec((blk, HEAD_DIM), lambda h, i: (i, h)),
                  pl.BlockSpec((T, HEAD_DIM), lambda h, i: (0, h)),
                  pl.BlockSpec((T, HEAD_DIM), lambda h, i: (0, h)),
                  pl.BlockSpec((blk, LANES), lambda h, i: (i, 0)),
                  pl.BlockSpec((None, nq, 1, blk), lambda h, i: (h, 0, 0, 0))],
        out_specs=[pl.BlockSpec((blk, HEAD_DIM), lambda h, i: (i, h)),
                   pl.BlockSpec((None, blk, LANES), lambda h, i: (h, i, 0))],
        out_shape=[jax.ShapeDtypeStruct((T, D), F32), jax.ShapeDtypeStruct((H, T, LANES), F32)],
        compiler_params=_cp(("parallel", "arbitrary")),
    )(q, k, v, c, ct)


def _attn_bwd(q, k, v, o, do, lse, c, ct, prev, *, blk, pad, name):
    T, D = q.shape
    H, nq = D // HEAD_DIM, T // blk
    scale = 1.0 / math.sqrt(HEAD_DIM)
    has_prev = prev is not None

    def body(*refs):
        q_ref, k_ref, v_ref, o_ref, do_ref, lse_ref, c_ref, ct_ref = refs[:8]
        pdk_ref, pdv_ref, pdc_ref = refs[8:11] if has_prev else (None, None, None)
        dq_ref, dk_ref, dv_ref, dct_ref = refs[-4:]
        h, j = pl.program_id(0), pl.program_id(1)

        @pl.when(j == 0)
        def _():
            dq_ref[...] = jnp.zeros_like(dq_ref)

        kb, vb = k_ref[...], v_ref[...]
        ck = ct_ref[...]
        kpos = j * blk + lax.broadcasted_iota(jnp.int32, (blk, blk), 1)
        qio = lax.broadcasted_iota(jnp.int32, (blk, blk), 0)

        def step(i, carry):
            dk, dv, dck = carry
            off = pl.multiple_of(i * blk, blk)
            qb = q_ref[pl.ds(off, blk), :]
            dob = do_ref[pl.ds(off, blk), :]
            ob = o_ref[pl.ds(off, blk), :]
            lse_i = lse_ref[pl.ds(off, blk), :][:, 0:1]
            cq = _pick_head(c_ref[pl.ds(off, blk), :], h)
            delta = jnp.sum(dob.astype(F32) * ob, axis=1, keepdims=True)
            s = lax.dot_general(qb, kb, (((1,), (1,)), ((), ())), preferred_element_type=F32) * scale
            s = s + (cq - ck)
            qpos = i * blk + qio
            p = jnp.where((kpos <= qpos) & (kpos >= pad), jnp.exp(s - lse_i), 0.0)
            dp = lax.dot_general(dob, vb, (((1,), (1,)), ((), ())), preferred_element_type=F32)
            ds = p * (dp - delta)
            pb, dsb = p.astype(BF16), ds.astype(BF16)
            dv = dv + lax.dot_general(pb, dob, (((0,), (0,)), ((), ())), preferred_element_type=F32)
            dk = dk + lax.dot_general(dsb, qb, (((0,), (0,)), ((), ())), preferred_element_type=F32)
            dck = dck - jnp.sum(ds, axis=0, keepdims=True)
            dq_ref[pl.ds(off, blk), :] += jnp.dot(dsb, kb, preferred_element_type=F32) * scale
            return dk, dv, dck

        init = (jnp.zeros((blk, HEAD_DIM), F32), jnp.zeros((blk, HEAD_DIM), F32), jnp.zeros((1, blk), F32))
        dk, dv, dck = lax.fori_loop(j, nq, step, init)
        dk = dk * scale
        if has_prev:
            dk, dv, dck = dk + pdk_ref[...], dv + pdv_ref[...], dck + pdc_ref[...]
        dk_ref[...] = dk
        dv_ref[...] = dv
        dct_ref[...] = dck

    col = pl.BlockSpec((T, HEAD_DIM), lambda h, j: (0, h))
    kblk = pl.BlockSpec((blk, HEAD_DIM), lambda h, j: (j, h))
    ctb = pl.BlockSpec((None, None, 1, blk), lambda h, j: (h, j, 0, 0))
    in_specs = [col, kblk, kblk, col, col,
                pl.BlockSpec((None, T, LANES), lambda h, j: (h, 0, 0)),
                pl.BlockSpec((T, LANES), lambda h, j: (0, 0)), ctb]
    operands = [q, k, v, o, do, lse, c, ct]
    if has_prev:
        in_specs += [kblk, kblk, ctb]
        operands += list(prev)
    return pl.pallas_call(
        body, name=name, grid=(H, nq), in_specs=in_specs, out_specs=[col, kblk, kblk, ctb],
        out_shape=[jax.ShapeDtypeStruct((T, D), F32), jax.ShapeDtypeStruct((T, D), F32),
                   jax.ShapeDtypeStruct((T, D), F32), jax.ShapeDtypeStruct((H, nq, 1, blk), F32)],
        compiler_params=_cp(("parallel", "arbitrary")),
    )(*operands)


HBM_SPEC = pl.BlockSpec(memory_space=pltpu.HBM)


def _place():
    x, y, c = lax.axis_index("x"), lax.axis_index("y"), lax.axis_index("c")
    chips = [(1 - x, y), (x, 1 - y), (1 - x, 1 - y)]
    return x, y, c, chips


def _remote(src, dst, send_sems, recv_sems, k, to):
    return pltpu.make_async_remote_copy(src_ref=src, dst_ref=dst, send_sem=send_sems.at[k],
                                        recv_sem=recv_sems.at[k], device_id=to, device_id_type=MESH)


def _all_gather(shards, specs, name):
    n = len(shards)

    def full_shape(s, axis):
        return (s.shape[0] * N_CHIPS, s.shape[1]) if axis == 0 else (s.shape[0], s.shape[1] * N_CHIPS)

    def body(*refs):
        srcs, outs = refs[:n], refs[n:2 * n]
        send_sems, recv_sems, local_sems = refs[2 * n:]
        x, y, c, chips = _place()
        sibling = (x, y, 1 - c)

        def region(t, chip, half):
            rs, cs = shards[t].shape
            q = 2 * chip[0] + chip[1]
            axis, split = specs[t]
            nrow = rs // 2 if half is not None else rs
            r0 = 0 if half is None else half * nrow
            if axis == 0:
                return outs[t].at[pl.ds(q * rs + r0, nrow), :]
            return outs[t].at[pl.ds(r0, nrow), pl.ds(pl.multiple_of(q * cs, cs), cs)]

        def piece(t, half):
            rs = shards[t].shape[0]
            if half is None:
                return srcs[t]
            return srcs[t].at[pl.ds(half * (rs // 2), rs // 2), :]

        local = [pltpu.make_async_copy(srcs[t], region(t, (x, y), None), local_sems.at[t]) for t in range(n)]
        for cp in local:
            cp.start()
        sends = []
        for t in range(n):
            half = c if specs[t][1] else None
            for j, chip in enumerate(chips):
                cp = _remote(piece(t, half), region(t, (x, y), half), send_sems, recv_sems, 6 * t + j, (*chip, c))
                cp.start()
                sends.append(cp)
        for t in range(n):
            half = c if specs[t][1] else None
            for j, chip in enumerate(chips):
                landed = region(t, chip, half)
                _remote(landed, landed, send_sems, recv_sems, 6 * t + j, (*chip, c)).wait_recv()
                if specs[t][1]:
                    cp = _remote(landed, landed, send_sems, recv_sems, 6 * t + 3 + j, sibling)
                    cp.start()
                    sends.append(cp)
        for t in range(n):
            if specs[t][1]:
                for j, chip in enumerate(chips):
                    got = region(t, chip, 1 - c)
                    _remote(got, got, send_sems, recv_sems, 6 * t + 3 + j, sibling).wait_recv()
        for cp in sends:
            cp.wait_send()
        for cp in local:
            cp.wait()

    return pl.pallas_call(
        body, name=name, in_specs=[HBM_SPEC] * n, out_specs=[HBM_SPEC] * n,
        out_shape=[jax.ShapeDtypeStruct(full_shape(s, specs[t][0]), s.dtype) for t, s in enumerate(shards)],
        scratch_shapes=[pltpu.SemaphoreType.DMA((6 * n,)), pltpu.SemaphoreType.DMA((6 * n,)),
                        pltpu.SemaphoreType.DMA((n,))],
        compiler_params=pltpu.CompilerParams(has_side_effects=True),
    )(*shards)


def _grad_view(g, axis):
    R, C = g.shape
    nq = N_CHIPS if axis == 0 else 1
    return g.reshape(nq, 2, R // (2 * nq), C)


def _swap_halves(views, name):
    n = len(views)

    def body(*refs):
        srcs, outs, send_sems, recv_sems = refs[:n], refs[n:2 * n], refs[2 * n], refs[2 * n + 1]
        x, y, c, _ = _place()
        cps = [_remote(srcs[t].at[:, 1 - c], outs[t], send_sems, recv_sems, t, (x, y, 1 - c)) for t in range(n)]
        for cp in cps:
            cp.start()
        for cp in cps:
            cp.wait()

    return pl.pallas_call(
        body, name=name, in_specs=[HBM_SPEC] * n, out_specs=[HBM_SPEC] * n,
        out_shape=[jax.ShapeDtypeStruct((v.shape[0],) + v.shape[2:], v.dtype) for v in views],
        scratch_shapes=[pltpu.SemaphoreType.DMA((n,)), pltpu.SemaphoreType.DMA((n,))],
        compiler_params=pltpu.CompilerParams(has_side_effects=True),
    )(*views)


def _add_half(view, got, c_idx, name):
    nq, _, Rh, C = view.shape
    tr = _tile(Rh, max(16, (2 * 1024 * 1024) // (C * 2)), 16)

    def body(c_ref, a_ref, b_ref, o_ref):
        o_ref[...] = (a_ref[...].astype(F32) + b_ref[...].astype(F32)).astype(BF16)

    grid_spec = pltpu.PrefetchScalarGridSpec(
        num_scalar_prefetch=1, grid=(nq, Rh // tr),
        in_specs=[pl.BlockSpec((None, None, tr, C), lambda q, i, c_ref: (q, c_ref[0], i, 0)),
                  pl.BlockSpec((None, tr, C), lambda q, i, c_ref: (q, i, 0))],
        out_specs=pl.BlockSpec((None, tr, C), lambda q, i, c_ref: (q, i, 0)))
    return pl.pallas_call(
        body, name=name, grid_spec=grid_spec, out_shape=jax.ShapeDtypeStruct((nq, Rh, C), BF16),
        compiler_params=_cp(("parallel", "parallel")),
    )(c_idx, view, got)


SEM_SPEC = pl.BlockSpec(memory_space=pltpu.SEMAPHORE)
ANY_SPEC = pl.BlockSpec(memory_space=pl.ANY)
TOKEN_SPEC = pl.BlockSpec(memory_space=pltpu.VMEM)
TOKEN = jax.ShapeDtypeStruct((8, LANES), F32)
SPLIT_COPY = pltpu.CompilerParams(has_side_effects=pltpu.SideEffectType.DATAFLOW_SIDE_EFFECTING)


def _region(ref, axis, chip, half):
    q = 2 * chip[0] + chip[1]
    if axis == 0:
        rs = ref.shape[0] // N_CHIPS
        return ref.at[pl.ds(q * rs + half * (rs // 2), rs // 2), :]
    rh, cs = ref.shape[0] // 2, ref.shape[1] // N_CHIPS
    return ref.at[pl.ds(half * rh, rh), pl.ds(pl.multiple_of(q * cs, cs), cs)]


def _gather_start(fulls, axes, after, name):
    n = len(fulls)

    def body(*refs):
        ins = refs[:n]
        token = refs[-1]
        send_sems, recv_sems = refs[n + 1], refs[n + 2]
        x, y, c, chips = _place()
        for t in range(n):
            mine = _region(ins[t], axes[t], (x, y), c)
            for j, chip in enumerate(chips):
                _remote(mine, mine, send_sems, recv_sems, 3 * t + j, (*chip, c)).start()
        token[...] = jnp.zeros_like(token)

    sems = pltpu.SemaphoreType.DMA((3 * n,))
    outs = pl.pallas_call(
        body, name=name, in_specs=[HBM_SPEC] * n + [ANY_SPEC],
        out_specs=[SEM_SPEC, SEM_SPEC] + [HBM_SPEC] * n + [TOKEN_SPEC],
        out_shape=[sems, sems] + [jax.ShapeDtypeStruct(f.shape, f.dtype) for f in fulls] + [TOKEN],
        input_output_aliases={t: 2 + t for t in range(n)}, compiler_params=SPLIT_COPY,
    )(*fulls, after)
    return outs[0], outs[1], list(outs[2:2 + n]), outs[-1]


def _gather_wait(send_sems, recv_sems, fulls, axes, after, name):
    n = len(fulls)

    def body(*refs):
        ins = refs[:n]
        send_sems, recv_sems = refs[n], refs[n + 1]
        x, y, c, chips = _place()
        for t in range(n):
            mine = _region(ins[t], axes[t], (x, y), c)
            for j, chip in enumerate(chips):
                _remote(mine, mine, send_sems, recv_sems, 3 * t + j, (*chip, c)).wait_send()
                theirs = _region(ins[t], axes[t], chip, c)
                _remote(theirs, theirs, send_sems, recv_sems, 3 * t + j, (*chip, c)).wait_recv()

    outs = pl.pallas_call(
        body, name=name, in_specs=[HBM_SPEC] * n + [SEM_SPEC, SEM_SPEC, ANY_SPEC], out_specs=[HBM_SPEC] * n,
        out_shape=[jax.ShapeDtypeStruct(f.shape, f.dtype) for f in fulls],
        input_output_aliases={t: t for t in range(n)}, compiler_params=SPLIT_COPY,
    )(*fulls, send_sems, recv_sems, after)
    return list(outs)


def _gather_forward(fulls, axes, name):
    n = len(fulls)

    def body(*refs):
        outs = refs[n:2 * n]
        send_sems, recv_sems = refs[2 * n], refs[2 * n + 1]
        x, y, c, chips = _place()
        sibling = (x, y, 1 - c)
        cps = []
        for t in range(n):
            for j, chip in enumerate(chips):
                landed = _region(outs[t], axes[t], chip, c)
                cps.append(_remote(landed, landed, send_sems, recv_sems, 3 * t + j, sibling))
        for cp in cps:
            cp.start()
        for t in range(n):
            for j, chip in enumerate(chips):
                got = _region(outs[t], axes[t], chip, 1 - c)
                _remote(got, got, send_sems, recv_sems, 3 * t + j, sibling).wait_recv()
        for cp in cps:
            cp.wait_send()

    outs = pl.pallas_call(
        body, name=name, in_specs=[HBM_SPEC] * n, out_specs=[HBM_SPEC] * n,
        out_shape=[jax.ShapeDtypeStruct(f.shape, f.dtype) for f in fulls],
        scratch_shapes=[pltpu.SemaphoreType.DMA((3 * n,)), pltpu.SemaphoreType.DMA((3 * n,))],
        input_output_aliases={t: t for t in range(n)},
        compiler_params=pltpu.CompilerParams(has_side_effects=True),
    )(*fulls)
    return list(outs)


def _shard_cols(s, axis):
    return s.shape[2] if axis == 0 else s.shape[2] // N_CHIPS


def _piece(ref, axis, chip):
    q = 2 * chip[0] + chip[1]
    if axis == 0:
        return ref.at[q]
    cs = ref.shape[2] // N_CHIPS
    return ref.at[0, :, pl.ds(pl.multiple_of(q * cs, cs), cs)]


def _scatter_start(sums, axes, after, name):
    n = len(sums)

    def body(*refs):
        ins = refs[:n]
        send_sems, recv_sems = refs[n + 1], refs[n + 2]
        lands = refs[2 * n + 3:3 * n + 3]
        token = refs[-1]
        x, y, c, chips = _place()
        for t in range(n):
            for j, chip in enumerate(chips):
                _remote(_piece(ins[t], axes[t], chip), lands[t].at[j], send_sems, recv_sems, 3 * t + j, (*chip, c)).start()
        token[...] = jnp.zeros_like(token)

    sems = pltpu.SemaphoreType.DMA((3 * n,))
    land_shapes = [jax.ShapeDtypeStruct((3, s.shape[1], _shard_cols(s, a)), s.dtype) for s, a in zip(sums, axes)]
    outs = pl.pallas_call(
        body, name=name, in_specs=[HBM_SPEC] * n + [ANY_SPEC],
        out_specs=[SEM_SPEC, SEM_SPEC] + [HBM_SPEC] * (2 * n) + [TOKEN_SPEC],
        out_shape=[sems, sems] + [jax.ShapeDtypeStruct(s.shape, s.dtype) for s in sums] + land_shapes + [TOKEN],
        input_output_aliases={t: 2 + t for t in range(n)}, compiler_params=SPLIT_COPY,
    )(*sums, after)
    return outs[0], outs[1], list(outs[2:2 + n]), list(outs[2 + n:2 + 2 * n]), outs[-1]


def _scatter_wait(send_sems, recv_sems, sums, lands, axes, after, name):
    n = len(sums)

    def body(*refs):
        ins, lnd = refs[:n], refs[n:2 * n]
        send_sems, recv_sems = refs[2 * n], refs[2 * n + 1]
        x, y, c, chips = _place()
        for t in range(n):
            for j, chip in enumerate(chips):
                cp = _remote(_piece(ins[t], axes[t], chip), lnd[t].at[j], send_sems, recv_sems, 3 * t + j, (*chip, c))
                cp.wait_send()
                cp.wait_recv()

    outs = pl.pallas_call(
        body, name=name, in_specs=[HBM_SPEC] * (2 * n) + [SEM_SPEC, SEM_SPEC, ANY_SPEC], out_specs=[HBM_SPEC] * (2 * n),
        out_shape=[jax.ShapeDtypeStruct(s.shape, s.dtype) for s in sums + lands],
        input_output_aliases={t: t for t in range(2 * n)}, compiler_params=SPLIT_COPY,
    )(*sums, *lands, send_sems, recv_sems, after)
    return list(outs[:n]), list(outs[n:])


def _sum_chips(own, axis, got, q_idx, c_idx, name):
    _, Rh, cc = got.shape
    tr = _tile(Rh, max(16, (1024 * 1024) // (cc * 2)), 16)

    def body(q_ref, c_ref, a_ref, b0, b1, b2, o_ref):
        f = lambda r: r[...].astype(F32)
        o_ref[...] = ((f(a_ref) + f(b0)) + f(b1)) + f(b2)

    if axis == 0:
        own_spec = pl.BlockSpec((None, tr, cc), lambda i, q, c: (q[0], i, 0))
    else:
        own_spec = pl.BlockSpec((None, tr, cc), lambda i, q, c: (0, i, q[0]))
    slot = lambda j: pl.BlockSpec((None, tr, cc), lambda i, q, c, j=j: (j, i, 0))
    grid_spec = pltpu.PrefetchScalarGridSpec(
        num_scalar_prefetch=2, grid=(Rh // tr,), in_specs=[own_spec, slot(0), slot(1), slot(2)],
        out_specs=pl.BlockSpec((None, tr, cc), lambda i, q, c: (c[0], i, 0)))
    return pl.pallas_call(
        body, name=name, grid_spec=grid_spec, out_shape=jax.ShapeDtypeStruct((2, Rh, cc), F32),
        compiler_params=_cp(("parallel",)),
    )(q_idx, c_idx, own, got, got, got)


def _join_halves(pairs, name):
    n = len(pairs)

    def body(*refs):
        outs = refs[n:2 * n]
        send_sems, recv_sems = refs[2 * n], refs[2 * n + 1]
        x, y, c, _ = _place()
        cps = [_remote(outs[t].at[c], outs[t].at[c], send_sems, recv_sems, t, (x, y, 1 - c)) for t in range(n)]
        for cp in cps:
            cp.start()
        for t in range(n):
            cps[t].wait_send()
            _remote(outs[t].at[1 - c], outs[t].at[1 - c], send_sems, recv_sems, t, (x, y, 1 - c)).wait_recv()

    outs = pl.pallas_call(
        body, name=name, in_specs=[HBM_SPEC] * n, out_specs=[HBM_SPEC] * n,
        out_shape=[jax.ShapeDtypeStruct(p.shape, p.dtype) for p in pairs],
        scratch_shapes=[pltpu.SemaphoreType.DMA((n,)), pltpu.SemaphoreType.DMA((n,))],
        input_output_aliases={t: t for t in range(n)},
        compiler_params=pltpu.CompilerParams(has_side_effects=True),
    )(*pairs)
    return list(outs)


def _reduce_begin(grads, axes, c_idx, after, tag):
    views = [_grad_view(g, a) for g, a in zip(grads, axes)]
    got = _swap_halves(views, f"rs_swap_{tag}")
    sums = [_add_half(v, p, c_idx, f"rs_add_{tag}_{t}") for t, (v, p) in enumerate(zip(views, got))]
    send_sems, recv_sems, sums, lands, token = _scatter_start(sums, axes, after, f"rs_chips_start_{tag}")
    return (send_sems, recv_sems, sums, lands, axes, tag), token


def _reduce_finish(pending, q_idx, c_idx, after):
    send_sems, recv_sems, sums, lands, axes, tag = pending
    sums, lands = _scatter_wait(send_sems, recv_sems, sums, lands, axes, after, f"rs_chips_wait_{tag}")
    pairs = [_sum_chips(s, a, r, q_idx, c_idx, f"rs_sum_{tag}_{t}")
             for t, (s, a, r) in enumerate(zip(sums, axes, lands))]
    joined = _join_halves(pairs, f"rs_join_{tag}")
    return [j.reshape(2 * j.shape[1], j.shape[2]) for j in joined]


def _all_reduce_small(buf, name):
    R, C = buf.shape

    def body(b_ref, o_ref, slots, send_sems, recv_sems):
        x, y, c, _ = _place()
        me = 4 * x + 2 * y + c
        cps = []
        for k in range(1, 8):
            to = (x ^ (k >> 2), y ^ ((k >> 1) & 1), c ^ (k & 1))
            cps.append(pltpu.make_async_remote_copy(
                src_ref=b_ref, dst_ref=slots.at[me], send_sem=send_sems.at[k - 1], recv_sem=recv_sems.at[me],
                device_id=to, device_id_type=MESH))
        for cp in cps:
            cp.start()
        slots[me] = b_ref[...]
        for k in range(1, 8):
            src = me ^ k
            pltpu.make_async_remote_copy(
                src_ref=b_ref, dst_ref=slots.at[src], send_sem=send_sems.at[k - 1], recv_sem=recv_sems.at[src],
                device_id=(x, y, c), device_id_type=MESH).wait_recv()
        for cp in cps:
            cp.wait_send()
        total = slots[0]
        for d in range(1, 8):
            total = total + slots[d]
        o_ref[...] = total

    vm = pl.BlockSpec(memory_space=pltpu.VMEM)
    return pl.pallas_call(
        body, name=name, in_specs=[vm], out_specs=vm, out_shape=jax.ShapeDtypeStruct((R, C), F32),
        scratch_shapes=[pltpu.VMEM((8, R, C), F32), pltpu.SemaphoreType.DMA((7,)), pltpu.SemaphoreType.DMA((8,))],
        compiler_params=pltpu.CompilerParams(has_side_effects=True, vmem_limit_bytes=VMEM_LIMIT),
    )(buf)


def kernel(x, meta, a_norm, a_w_in, a_conv, a_w_out, kv_norm, w_kv, k_norm, w_f, b_f, b_norm, b_w_q, b_q_norm, b_w_o, ffn_norm, ffn_w_gu, ffn_w_down, loss_target, m_meta, m_a_norm, m_a_w_in, m_a_conv, m_a_w_out, m_kv_norm, m_w_kv, m_k_norm, m_w_f, m_b_f, m_b_norm, m_b_w_q, m_b_q_norm, m_b_w_o, m_ffn_norm, m_ffn_w_gu, m_ffn_w_down, v_meta, v_a_norm, v_a_w_in, v_a_conv, v_a_w_out, v_kv_norm, v_w_kv, v_k_norm, v_w_f, v_b_f, v_b_norm, v_b_w_q, v_b_q_norm, v_b_w_o, v_ffn_norm, v_ffn_w_gu, v_ffn_w_down):
    SEQ, D = x.shape[1], x.shape[2]
    n_meta = meta.shape[0]
    pad = BLOCK - n_meta
    first = pad + n_meta
    T = first + SEQ
    H = D // HEAD_DIM
    Ds = D // N_CHIPS
    n_a, n_b, depth = a_w_in.shape[0], b_w_q.shape[0], ffn_norm.shape[0]
    blk = _tile(T, 384, BLOCK)
    cx, cy, cc = lax.axis_index("x"), lax.axis_index("y"), lax.axis_index("c")
    q_me = 2 * cx + cy
    c_idx = jnp.reshape(cc, (1,)).astype(jnp.int32)
    q_idx = jnp.reshape(q_me, (1,)).astype(jnp.int32)
    rows8 = lambda v: jnp.pad(v, ((0, -v.shape[0] % 8), (0, 0)))

    col_parts = [meta, a_norm, a_conv.reshape(3 * n_a, Ds)]
    col_pack = jnp.concatenate([rows8(p) for p in col_parts], axis=0)
    col_full, w_f_full = _all_gather([col_pack, w_f], [(1, False), (0, False)], "ag_small")
    col_offs = [sum(rows8(p).shape[0] for p in col_parts[:i]) for i in range(3)]
    meta_f = col_full[:n_meta]
    a_norm_f = col_full[col_offs[1]:col_offs[1] + n_a]
    a_conv_f = col_full[col_offs[2]:col_offs[2] + 3 * n_a].reshape(n_a, 3, D)
    w_fp = jnp.pad(w_f_full, ((0, 0), (0, LANES - H))).astype(BF16)
    b_fp = jnp.pad(b_f, (0, LANES - H)).reshape(1, LANES)

    stages = [[(a_w_in, l, 1), (a_w_out, l, 0), (ffn_w_gu, l, 1), (ffn_w_down, l, 0)] for l in range(n_a)]
    stages += [[(b_w_q, j, 0), (b_w_o, j, 0), (ffn_w_gu, n_a + j, 1), (ffn_w_down, n_a + j, 0)] for j in range(n_b)]
    stages[n_a].append((w_kv[None], 0, 1))

    def gather_begin(k, after):
        axes = [ax for _, _, ax in stages[k]]
        fulls = [_cast_into_full(w, l, ax, q_idx, f"cast_{k}_{i}") for i, (w, l, ax) in enumerate(stages[k])]
        send_sems, recv_sems, fulls, token = _gather_start(fulls, axes, after, f"ag_start_{k}")
        return (send_sems, recv_sems, fulls, axes, k), token

    def gather_end(handle, after):
        send_sems, recv_sems, fulls, axes, k = handle
        fulls = _gather_wait(send_sems, recv_sems, fulls, axes, after, f"ag_wait_{k}")
        return _gather_forward(fulls, axes, f"ag_forward_{k}")

    handle, _ = gather_begin(0, c_idx)
    stage_w = gather_end(handle, c_idx)

    def next_stage(k, w_now):
        return gather_begin(k + 1, w_now[0]) if k + 1 < len(stages) else (None, None)

    h = jnp.concatenate([jnp.zeros((pad, D), F32), meta_f, x[0]], axis=0)
    saved = []

    def ffn_fwd(h, layer, w_gu, w_down):
        xn = _rms_fwd(h, ffn_norm[layer:layer + 1], f"ffn_norm_{layer}")
        z = _matmul(xn, w_gu, mode="nn", out_dtype=F32, name=f"ffn_gu_{layer}", out_parts=2)
        act = _swiglu_fwd(z, f"swiglu_{layer}")
        return _matmul(act, w_down, mode="nn", out_dtype=F32, name=f"ffn_down_{layer}", res=h), (h, xn, z, act)

    wa, wb = [], []
    for l in range(n_a):
        w_in, w_out, w_gu, w_down = stage_w[:4]
        wa.append(stage_w[:4])
        handle, token = next_stage(l, stage_w)
        xn = _rms_fwd(h, a_norm_f[l:l + 1], f"a_norm_{l}", dep=token)
        z = _matmul(xn, w_in, mode="nn", out_dtype=F32, name=f"a_in_{l}", out_parts=3)
        y = _gate_fwd(z, a_conv_f[l], f"a_gate_{l}")
        h2 = _matmul(y, w_out, mode="nn", out_dtype=F32, name=f"a_out_{l}", res=h)
        h3, ffn_saved = ffn_fwd(h2, l, w_gu, w_down)
        saved.append((h, xn, z, y, ffn_saved))
        h = h3
        stage_w = gather_end(handle, h)

    h_kv = h
    w_kv_b = stage_w[4]
    handle, token = next_stage(n_a, stage_w)
    xkv = _rms_fwd(h, kv_norm.reshape(1, D), "kv_norm", dep=token)
    kvz = _matmul(xkv, w_kv_b, mode="nn", out_dtype=F32, name="kv_proj", out_parts=2)
    k_n = _headnorm_fwd(kvz, 0, k_norm.reshape(1, HEAD_DIM), "k_headnorm")
    v_b = _cast_part(kvz, 1, "v_cast")
    pre = _matmul(xkv, w_fp, mode="nn", out_dtype=F32, name="f_proj")
    c_cum = _logf_cumsum(pre, b_fp, pad, "logf_cumsum")
    c_t = c_cum[:, :H].T.reshape(H, T // blk, 1, blk)

    for j in range(n_b):
        layer = n_a + j
        w_q, w_o, w_gu, w_down = stage_w[:4]
        wb.append(stage_w[:4])
        if j > 0:
            handle, token = next_stage(layer, stage_w)
        xn = _rms_fwd(h, b_norm[j:j + 1], f"b_norm_{j}", dep=token if j > 0 else None)
        qz = _matmul(xn, w_q, mode="nn", out_dtype=F32, name=f"b_q_{j}")[None]
        q_n = _headnorm_fwd(qz, 0, b_q_norm[j:j + 1], f"q_headnorm_{j}")
        o, lse = _attn_fwd(q_n, k_n, v_b, c_cum, c_t, blk=blk, pad=pad, name=f"attn_fwd_{j}")
        h2 = _matmul(o, w_o, mode="nn", out_dtype=F32, name=f"b_o_{j}", res=h)
        h3, ffn_saved = ffn_fwd(h2, layer, w_gu, w_down)
        saved.append((h, xn, qz, q_n, o, lse, ffn_saved))
        h = h3
        if handle is not None:
            stage_w = gather_end(handle, h)

    dh, loss_blk = _loss_head(h, loss_target[0], first, "loss_head")
    loss = lax.psum(loss_blk[0, 0], ("x", "y", "c"))

    shards = {}
    in_flight = [None]

    def reduce_later(names, grads, axes, tag, done):
        after = c_idx
        if in_flight[0] is not None:
            prev_names, pending = in_flight[0]
            got = _reduce_finish(pending, q_idx, c_idx, done)
            shards.update(zip(prev_names, got))
            after = got[0]
        pending, token = _reduce_begin(grads, axes, c_idx, after, tag)
        in_flight[0] = (names, pending)
        return token

    def ffn_bwd(dh, layer, w_gu, w_down, ffn_saved, dep):
        h_in, xn, z, act = ffn_saved
        da = _matmul(dh, w_down, mode="nt", out_dtype=F32, name=f"ffn_down_dx_{layer}", dep=dep)
        g_down = _matmul(act, dh, mode="tn", out_dtype=BF16, name=f"ffn_down_dw_{layer}")
        dz = _swiglu_bwd(z, da, f"swiglu_bwd_{layer}")
        dxn = _matmul(dz, w_gu, mode="nt", out_dtype=F32, name=f"ffn_gu_dx_{layer}")
        g_gu = _matmul(xn, dz, mode="tn", out_dtype=BF16, name=f"ffn_gu_dw_{layer}")
        dh, dg = _rms_bwd(h_in, ffn_norm[layer:layer + 1], dxn, dh, f"ffn_norm_bwd_{layer}")
        token = reduce_later([("ffn_w_gu", layer), ("ffn_w_down", layer)], [g_gu, g_down], [1, 0], f"f{layer}", dh)
        return dh, dg, token

    d_ffn_norm, d_b_norm, d_q_norm, d_a_norm, d_a_conv = {}, {}, {}, {}, {}
    kv_prev = None
    token = None
    for j in reversed(range(n_b)):
        layer = n_a + j
        w_q, w_o, w_gu, w_down = wb[j]
        h_in, xn, qz, q_n, o, lse, ffn_saved = saved[layer]
        dh, d_ffn_norm[layer], token = ffn_bwd(dh, layer, w_gu, w_down, ffn_saved, token)
        do = _matmul(dh, w_o, mode="nt", out_dtype=BF16, name=f"b_o_dx_{j}", dep=token)
        g_o = _matmul(o, dh, mode="tn", out_dtype=BF16, name=f"b_o_dw_{j}")
        dq, dk, dv, dct = _attn_bwd(q_n, k_n, v_b, o, do, lse, c_cum, c_t, kv_prev, blk=blk, pad=pad,
                                    name=f"attn_bwd_{j}")
        kv_prev = (dk, dv, dct)
        dqz, d_q_norm[j] = _headnorm_bwd(qz, 0, b_q_norm[j:j + 1], dq, f"q_headnorm_bwd_{j}")
        dxn = _matmul(dqz, w_q, mode="nt", out_dtype=F32, name=f"b_q_dx_{j}")
        g_q = _matmul(xn, dqz, mode="tn", out_dtype=BF16, name=f"b_q_dw_{j}")
        dh, d_b_norm[j] = _rms_bwd(h_in, b_norm[j:j + 1], dxn, dh, f"b_norm_bwd_{j}")
        if j > 0:
            token = reduce_later([("b_w_q", j), ("b_w_o", j)], [g_q, g_o], [0, 0], f"b{j}", dh)

    dk, dv, dct = kv_prev
    dkz, d_k_norm = _headnorm_bwd(kvz, 0, k_norm.reshape(1, HEAD_DIM), dk, "k_headnorm_bwd")
    dvz = _cast_part(dv[None], 0, "dv_cast")
    dkv = jnp.stack([dkz, dvz])
    dc = jnp.pad(dct.reshape(H, T).T, ((0, 0), (0, LANES - H)))
    dpre, d_b_f = _logf_bwd(pre, b_fp, dc, pad, "logf_bwd")
    dxkv = _matmul(dkv, w_kv_b, mode="nt", out_dtype=F32, name="kv_proj_dx")
    dxkv = _matmul(dpre, w_fp, mode="nt", out_dtype=F32, name="f_proj_dx", res=dxkv)
    g_kv = _matmul(xkv, dkv, mode="tn", out_dtype=BF16, name="kv_proj_dw")
    d_w_f = _matmul(xkv, dpre, mode="tn", out_dtype=F32, name="f_proj_dw")
    dh, d_kv_norm = _rms_bwd(h_kv, kv_norm.reshape(1, D), dxkv, dh, "kv_norm_bwd")
    token = reduce_later([("b_w_q", 0), ("b_w_o", 0), ("w_kv", 0)], [g_q, g_o, g_kv], [0, 0, 1], "b0", dh)

    for l in reversed(range(n_a)):
        w_in, w_out, w_gu, w_down = wa[l]
        h_in, xn, z, y, ffn_saved = saved[l]
        dh, d_ffn_norm[l], token = ffn_bwd(dh, l, w_gu, w_down, ffn_saved, token)
        dy = _matmul(dh, w_out, mode="nt", out_dtype=F32, name=f"a_out_dx_{l}", dep=token)
        g_out = _matmul(y, dh, mode="tn", out_dtype=BF16, name=f"a_out_dw_{l}")
        dz, d_a_conv[l] = _gate_bwd(z, a_conv_f[l], dy, f"a_gate_bwd_{l}")
        dxn = _matmul(dz, w_in, mode="nt", out_dtype=F32, name=f"a_in_dx_{l}")
        g_in = _matmul(xn, dz, mode="tn", out_dtype=BF16, name=f"a_in_dw_{l}")
        dh, d_a_norm[l] = _rms_bwd(h_in, a_norm_f[l:l + 1], dxn, dh, f"a_norm_bwd_{l}")
        token = reduce_later([("a_w_in", l), ("a_w_out", l)], [g_in, g_out], [1, 0], f"a{l}", dh)

    prev_names, pending = in_flight[0]
    shards.update(zip(prev_names, _reduce_finish(pending, q_idx, c_idx, dh)))
    grad_x = dh[first:][None]

    widen = lambda v: jnp.pad(v, ((0, 0), (0, D - v.shape[1])))
    groups = [
        [dh[pad:first]],
        [d_a_norm[l] for l in range(n_a)],
        [d_a_conv[l] for l in range(n_a)],
        [d_kv_norm],
        [widen(d_k_norm)],
        [d_w_f[:, :H].T],
        [widen(d_b_f)],
        [d_b_norm[j] for j in range(n_b)],
        [widen(d_q_norm[j]) for j in range(n_b)],
        [d_ffn_norm[l] for l in range(depth)],
    ]
    pack = jnp.concatenate([rows8(p) for g in groups for p in g], axis=0)
    red = _all_reduce_small(pack, "ar_small")
    taken, off = [], 0
    for g in groups:
        r, rp = g[0].shape[0], rows8(g[0]).shape[0]
        taken.append(red[off:off + len(g) * rp].reshape(len(g), rp, D)[:, :r].reshape(len(g) * r, D))
        off += len(g) * rp
    take = lambda i: taken[i]
    mine = lambda a: lax.dynamic_slice_in_dim(a, q_me * Ds, Ds, axis=1)
    layers_of = lambda name, n: jnp.stack([shards[(name, l)] for l in range(n)])
    grads = {
        "meta": mine(take(0)),
        "a_norm": mine(take(1)),
        "a_w_in": layers_of("a_w_in", n_a),
        "a_conv": mine(take(2)).reshape(n_a, 3, Ds),
        "a_w_out": layers_of("a_w_out", n_a),
        "kv_norm": take(3).reshape(D),
        "w_kv": shards[("w_kv", 0)],
        "k_norm": take(4)[0, :HEAD_DIM],
        "w_f": mine(take(5)).T,
        "b_f": take(6)[0, :H],
        "b_norm": take(7),
        "b_w_q": layers_of("b_w_q", n_b),
        "b_q_norm": take(8)[:, :HEAD_DIM],
        "b_w_o": layers_of("b_w_o", n_b),
        "ffn_norm": take(9),
        "ffn_w_gu": layers_of("ffn_w_gu", depth),
        "ffn_w_down": layers_of("ffn_w_down", depth),
    }
    weights = dict(meta=meta, a_norm=a_norm, a_w_in=a_w_in, a_conv=a_conv, a_w_out=a_w_out, kv_norm=kv_norm, w_kv=w_kv,
                   k_norm=k_norm, w_f=w_f, b_f=b_f, b_norm=b_norm, b_w_q=b_w_q, b_q_norm=b_q_norm, b_w_o=b_w_o,
                   ffn_norm=ffn_norm, ffn_w_gu=ffn_w_gu, ffn_w_down=ffn_w_down)
    m_in = dict(meta=m_meta, a_norm=m_a_norm, a_w_in=m_a_w_in, a_conv=m_a_conv, a_w_out=m_a_w_out, kv_norm=m_kv_norm,
                w_kv=m_w_kv, k_norm=m_k_norm, w_f=m_w_f, b_f=m_b_f, b_norm=m_b_norm, b_w_q=m_b_w_q,
                b_q_norm=m_b_q_norm, b_w_o=m_b_w_o, ffn_norm=m_ffn_norm, ffn_w_gu=m_ffn_w_gu, ffn_w_down=m_ffn_w_down)
    v_in = dict(meta=v_meta, a_norm=v_a_norm, a_w_in=v_a_w_in, a_conv=v_a_conv, a_w_out=v_a_w_out, kv_norm=v_kv_norm,
                w_kv=v_w_kv, k_norm=v_k_norm, w_f=v_w_f, b_f=v_b_f, b_norm=v_b_norm, b_w_q=v_b_w_q,
                b_q_norm=v_b_q_norm, b_w_o=v_b_w_o, ffn_norm=v_ffn_norm, ffn_w_gu=v_ffn_w_gu, ffn_w_down=v_ffn_w_down)

    deltas, new_m, new_v = {}, {}, {}
    for name, w in weights.items():
        shape = w.shape
        two_d = (1, shape[0]) if w.ndim == 1 else (math.prod(shape[:-1]), shape[-1])
        r2 = lambda a: a.reshape(two_d)
        d_, m_, v_ = _adamw(r2(w), r2(grads[name]), r2(m_in[name]), r2(v_in[name]), f"adamw_{name}")
        deltas[name], new_m[name], new_v[name] = d_.reshape(shape), m_.reshape(shape), v_.reshape(shape)
        grads[name] = grads[name].reshape(shape)

    names = list(weights)
    return (loss, grad_x, *[grads[n] for n in names], *[deltas[n] for n in names],
            *[new_m[n] for n in names], *[new_v[n] for n in names])
```

```python
import functools
import math

import jax
import jax.numpy as jnp
from jax import lax
from jax.experimental import pallas as pl
from jax.experimental.pallas import tpu as pltpu

F32 = jnp.float32
BF16 = jnp.bfloat16
HEAD_DIM = 128
BLOCK = 128
LANES = 128
EPS = 1e-6
NEG = -1e30
ADAM_LR, ADAM_B1, ADAM_B2, ADAM_EPS, ADAM_WD, ADAM_STEP = 0.001, 0.9, 0.999, 1e-08, 0.01, 10
VMEM_LIMIT = 56 * 1024 * 1024
TILE_BUDGET = 40 * 1024 * 1024
MESH = pl.DeviceIdType.MESH
N_CHIPS = 4


def _tile(n, target, align):
    best = None
    for d in range(align, min(n, target) + 1, align):
        if n % d == 0:
            best = d
    return best if best is not None else n


def _cp(sem):
    return pltpu.CompilerParams(dimension_semantics=sem, vmem_limit_bytes=VMEM_LIMIT)


def _matmul(a, b, *, mode, out_dtype, name, res=None, out_parts=1, dep=None):
    a_parts = a.shape[0] if a.ndim == 3 else 1
    b_parts = b.shape[0] if b.ndim == 3 else 1
    if mode == "nn":
        M, Kp = a.shape[-2:]
        K, N = Kp * a_parts, b.shape[1]
    elif mode == "nt":
        M, Kp = a.shape[-2:]
        K, N = Kp * a_parts, b.shape[0]
    else:
        K, M = a.shape
        Kp = K
        N = b.shape[-1] * b_parts
    Np = N // max(b_parts, out_parts)
    tm = _tile(M, 1024, LANES) if mode == "tn" else _tile(M, 1056, 16)
    tn = _tile(Np, 1536, LANES)
    tk = _tile(Kp, 1056, 16) if mode == "tn" else _tile(Kp, 2048, LANES)
    ab, bb, ob = a.dtype.itemsize, b.dtype.itemsize, jnp.dtype(out_dtype).itemsize

    def vmem(tm_):
        blocks = 2 * (tm_ * tk * ab + tk * tn * bb + tm_ * tn * ob + (tm_ * tn * 4 if res is not None else 0))
        temps = tm_ * tn * 8 + (tm_ * tk * 2 if ab == 4 else 0) + (tk * tn * 2 if bb == 4 else 0)
        return blocks + temps

    while vmem(tm) > TILE_BUDGET and tm > 256:
        tm = _tile(M, tm // 2, LANES if mode == "tn" else 16)
    ni, nj, nk = M // tm, N // tn, K // tk
    nkp, njp = Kp // tk, Np // tn

    if mode == "tn":
        a_spec = pl.BlockSpec((tk, tm), lambda i, j, k: (k, i))
    elif a_parts > 1:
        a_spec = pl.BlockSpec((None, tm, tk), lambda i, j, k: (k // nkp, i, k % nkp))
    else:
        a_spec = pl.BlockSpec((tm, tk), lambda i, j, k: (i, k))
    if mode == "nn":
        b_spec = pl.BlockSpec((tk, tn), lambda i, j, k: (k, j))
    elif mode == "nt":
        b_spec = pl.BlockSpec((tn, tk), lambda i, j, k: (j, k))
    elif b_parts > 1:
        b_spec = pl.BlockSpec((None, tk, tn), lambda i, j, k: (j // njp, k, j % njp))
    else:
        b_spec = pl.BlockSpec((tk, tn), lambda i, j, k: (k, j))
    in_specs = [a_spec, b_spec]
    operands = [a, b]
    if res is not None:
        in_specs.append(pl.BlockSpec((tm, tn), lambda i, j, k: (i, j)))
        operands.append(res)
    if dep is not None:
        in_specs.append(pl.BlockSpec((8, LANES), lambda i, j, k: (0, 0)))
        operands.append(dep)
    n_in = len(operands)
    if out_parts > 1:
        out_spec = pl.BlockSpec((None, tm, tn), lambda i, j, k: (j // njp, i, j % njp))
        out_shape = jax.ShapeDtypeStruct((out_parts, M, Np), out_dtype)
    else:
        out_spec = pl.BlockSpec((tm, tn), lambda i, j, k: (i, j))
        out_shape = jax.ShapeDtypeStruct((M, N), out_dtype)
    dims = {"nn": (((1,), (0,)), ((), ())), "nt": (((1,), (1,)), ((), ())), "tn": (((0,), (0,)), ((), ()))}[mode]
    has_res = res is not None

    def body(*refs):
        a_ref, b_ref = refs[0], refs[1]
        res_ref = refs[2] if has_res else None
        o_ref = refs[n_in]
        d = lax.dot_general(a_ref[...].astype(BF16), b_ref[...].astype(BF16), dims, preferred_element_type=F32)
        if nk == 1:
            if has_res:
                d = d + res_ref[...]
            o_ref[...] = d.astype(out_dtype)
        else:
            acc_ref = refs[-1]
            k = pl.program_id(2)

            @pl.when(k == 0)
            def _():
                acc_ref[...] = d

            @pl.when(k > 0)
            def _():
                acc_ref[...] += d

            @pl.when(k == nk - 1)
            def _():
                r = acc_ref[...]
                if has_res:
                    r = r + res_ref[...]
                o_ref[...] = r.astype(out_dtype)

    return pl.pallas_call(
        body, name=name, grid=(ni, nj, nk), in_specs=in_specs, out_specs=out_spec, out_shape=out_shape,
        scratch_shapes=[pltpu.VMEM((tm, tn), F32)] if nk > 1 else [],
        compiler_params=_cp(("parallel", "parallel", "arbitrary")),
    )(*operands)


def _cast_into_full(w3, layer, axis, q_idx, name):
    _, R, C = w3.shape
    tr = _tile(R, max(16, (4 * 1024 * 1024) // (C * 4)), 16)
    nb = R // tr

    def body(q_ref, w_ref, o_ref):
        o_ref[...] = w_ref[...].astype(BF16)

    if axis == 0:
        out_spec = pl.BlockSpec((tr, C), lambda i, q_ref: (q_ref[0] * nb + i, 0))
        full = (N_CHIPS * R, C)
    else:
        out_spec = pl.BlockSpec((tr, C), lambda i, q_ref: (i, q_ref[0]))
        full = (R, N_CHIPS * C)
    grid_spec = pltpu.PrefetchScalarGridSpec(
        num_scalar_prefetch=1, grid=(nb,),
        in_specs=[pl.BlockSpec((None, tr, C), lambda i, q_ref: (layer, i, 0))], out_specs=out_spec)
    return pl.pallas_call(
        body, name=name, grid_spec=grid_spec, out_shape=jax.ShapeDtypeStruct(full, BF16),
        compiler_params=_cp(("parallel",)),
    )(q_idx, w3)


def _rms_fwd(h, g, name, dep=None):
    T, D = h.shape
    tr = _tile(T, 528, 16)

    def body(h_ref, g_ref, *rest):
        o_ref = rest[-1]
        x = h_ref[...]
        r = lax.rsqrt(jnp.mean(x * x, axis=-1, keepdims=True) + EPS)
        o_ref[...] = (x * r * g_ref[...]).astype(BF16)

    in_specs = [pl.BlockSpec((tr, D), lambda i: (i, 0)), pl.BlockSpec((1, D), lambda i: (0, 0))]
    operands = [h, g]
    if dep is not None:
        in_specs.append(pl.BlockSpec((8, LANES), lambda i: (0, 0)))
        operands.append(dep)
    return pl.pallas_call(
        body, name=name, grid=(T // tr,), in_specs=in_specs,
        out_specs=pl.BlockSpec((tr, D), lambda i: (i, 0)),
        out_shape=jax.ShapeDtypeStruct((T, D), BF16), compiler_params=_cp(("parallel",)),
    )(*operands)


def _rms_bwd(h, g, dxn, dh, name):
    T, D = h.shape
    tr = _tile(T, 264, 8)

    def body(h_ref, g_ref, dxn_ref, dh_ref, o_ref, dg_ref):
        x = h_ref[...]
        r = lax.rsqrt(jnp.mean(x * x, axis=-1, keepdims=True) + EPS)
        xh = x * r
        dy = dxn_ref[...]
        dxh = dy * g_ref[...]
        dx = r * (dxh - xh * jnp.mean(dxh * xh, axis=-1, keepdims=True))
        o_ref[...] = dh_ref[...] + dx
        part = jnp.sum(dy * xh, axis=0, keepdims=True)

        @pl.when(pl.program_id(0) == 0)
        def _():
            dg_ref[...] = part

        @pl.when(pl.program_id(0) > 0)
        def _():
            dg_ref[...] += part

    row = pl.BlockSpec((tr, D), lambda i: (i, 0))
    vec = pl.BlockSpec((1, D), lambda i: (0, 0))
    return pl.pallas_call(
        body, name=name, grid=(T // tr,), in_specs=[row, vec, row, row], out_specs=[row, vec],
        out_shape=[jax.ShapeDtypeStruct((T, D), F32), jax.ShapeDtypeStruct((1, D), F32)],
        compiler_params=_cp(("arbitrary",)),
    )(h, g, dxn, dh)


def _shift_down(u, n, rows):
    return jnp.where(rows >= n, pltpu.roll(u, n, 0), 0.0)


def _shift_up(u, n, rows, total):
    return jnp.where(rows < total - n, pltpu.roll(u, total - n, 0), 0.0)


def _gate_fwd(z, conv_w, name):
    _, T, D = z.shape
    tc = LANES

    def body(b_ref, c_ref, h_ref, w_ref, y_ref):
        rows = lax.broadcasted_iota(jnp.int32, (T, tc), 0)
        u = c_ref[...].astype(F32) * h_ref[...].astype(F32)
        w0, w1, w2 = w_ref[0:1, :], w_ref[1:2, :], w_ref[2:3, :]
        conv = u * w2 + _shift_down(u, 1, rows) * w1 + _shift_down(u, 2, rows) * w0
        y_ref[...] = (b_ref[...].astype(F32) * conv).astype(BF16)

    part = lambda p: pl.BlockSpec((None, T, tc), lambda j, p=p: (p, 0, j))
    return pl.pallas_call(
        body, name=name, grid=(D // tc,),
        in_specs=[part(0), part(1), part(2), pl.BlockSpec((3, tc), lambda j: (0, j))],
        out_specs=pl.BlockSpec((T, tc), lambda j: (0, j)),
        out_shape=jax.ShapeDtypeStruct((T, D), BF16), compiler_params=_cp(("parallel",)),
    )(z, z, z, conv_w)


def _gate_bwd(z, conv_w, dy, name):
    _, T, D = z.shape
    tc = LANES

    def body(b_ref, c_ref, h_ref, w_ref, dy_ref, dz_ref, dw_ref):
        rows = lax.broadcasted_iota(jnp.int32, (T, tc), 0)
        cg, hh = c_ref[...].astype(F32), h_ref[...].astype(F32)
        u = cg * hh
        w0, w1, w2 = w_ref[0:1, :], w_ref[1:2, :], w_ref[2:3, :]
        s1, s2 = _shift_down(u, 1, rows), _shift_down(u, 2, rows)
        g = dy_ref[...]
        dz_ref[0] = (g * (u * w2 + s1 * w1 + s2 * w0)).astype(BF16)
        dconv = g * b_ref[...].astype(F32)
        dw_ref[0:1, :] = jnp.sum(dconv * s2, axis=0, keepdims=True)
        dw_ref[1:2, :] = jnp.sum(dconv * s1, axis=0, keepdims=True)
        dw_ref[2:3, :] = jnp.sum(dconv * u, axis=0, keepdims=True)
        du = dconv * w2 + _shift_up(dconv, 1, rows, T) * w1 + _shift_up(dconv, 2, rows, T) * w0
        dz_ref[1] = (du * hh).astype(BF16)
        dz_ref[2] = (du * cg).astype(BF16)

    part = lambda p: pl.BlockSpec((None, T, tc), lambda j, p=p: (p, 0, j))
    return pl.pallas_call(
        body, name=name, grid=(D // tc,),
        in_specs=[part(0), part(1), part(2), pl.BlockSpec((3, tc), lambda j: (0, j)),
                  pl.BlockSpec((T, tc), lambda j: (0, j))],
        out_specs=[pl.BlockSpec((3, T, tc), lambda j: (0, 0, j)), pl.BlockSpec((3, tc), lambda j: (0, j))],
        out_shape=[jax.ShapeDtypeStruct((3, T, D), BF16), jax.ShapeDtypeStruct((3, D), F32)],
        compiler_params=_cp(("parallel",)),
    )(z, z, z, conv_w, dy)


def _swiglu_fwd(z, name):
    _, T, Fd = z.shape
    tr, tc = _tile(T, 528, 16), _tile(Fd, 512, LANES)

    def body(g_ref, u_ref, o_ref):
        g = g_ref[...].astype(F32)
        o_ref[...] = (g * jax.nn.sigmoid(g) * u_ref[...].astype(F32)).astype(BF16)

    part = lambda p: pl.BlockSpec((None, tr, tc), lambda i, j, p=p: (p, i, j))
    return pl.pallas_call(
        body, name=name, grid=(T // tr, Fd // tc), in_specs=[part(0), part(1)],
        out_specs=pl.BlockSpec((tr, tc), lambda i, j: (i, j)),
        out_shape=jax.ShapeDtypeStruct((T, Fd), BF16), compiler_params=_cp(("parallel", "parallel")),
    )(z, z)


def _swiglu_bwd(z, da, name):
    _, T, Fd = z.shape
    tr, tc = _tile(T, 528, 16), _tile(Fd, 512, LANES)

    def body(g_ref, u_ref, da_ref, dz_ref):
        g, d = g_ref[...].astype(F32), da_ref[...].astype(F32)
        s = jax.nn.sigmoid(g)
        dz_ref[0] = (d * u_ref[...].astype(F32) * (s * (1.0 + g * (1.0 - s)))).astype(BF16)
        dz_ref[1] = (d * (g * s)).astype(BF16)

    part = lambda p: pl.BlockSpec((None, tr, tc), lambda i, j, p=p: (p, i, j))
    return pl.pallas_call(
        body, name=name, grid=(T // tr, Fd // tc),
        in_specs=[part(0), part(1), pl.BlockSpec((tr, tc), lambda i, j: (i, j))],
        out_specs=pl.BlockSpec((2, tr, tc), lambda i, j: (0, i, j)),
        out_shape=jax.ShapeDtypeStruct((2, T, Fd), BF16), compiler_params=_cp(("parallel", "parallel")),
    )(z, z, da)


def _headnorm_fwd(z, part, g, name):
    _, T, D = z.shape
    tr = _tile(T, 1056, 16)

    def body(z_ref, g_ref, o_ref):
        x = z_ref[...]
        r = lax.rsqrt(jnp.mean(x * x, axis=-1, keepdims=True) + EPS)
        o_ref[...] = (x * r * g_ref[...]).astype(BF16)

    return pl.pallas_call(
        body, name=name, grid=(T // tr, D // HEAD_DIM),
        in_specs=[pl.BlockSpec((None, tr, HEAD_DIM), lambda i, h: (part, i, h)),
                  pl.BlockSpec((1, HEAD_DIM), lambda i, h: (0, 0))],
        out_specs=pl.BlockSpec((tr, HEAD_DIM), lambda i, h: (i, h)),
        out_shape=jax.ShapeDtypeStruct((T, D), BF16), compiler_params=_cp(("parallel", "parallel")),
    )(z, g)


def _headnorm_bwd(z, part, g, dy, name):
    _, T, D = z.shape
    tr = _tile(T, 1056, 16)

    def body(z_ref, g_ref, dy_ref, dz_ref, dg_ref):
        x = z_ref[...]
        r = lax.rsqrt(jnp.mean(x * x, axis=-1, keepdims=True) + EPS)
        xh = x * r
        dy_ = dy_ref[...]
        dxh = dy_ * g_ref[...]
        dz_ref[...] = (r * (dxh - xh * jnp.mean(dxh * xh, axis=-1, keepdims=True))).astype(BF16)
        partial = jnp.sum(dy_ * xh, axis=0, keepdims=True)
        first = (pl.program_id(0) == 0) & (pl.program_id(1) == 0)

        @pl.when(first)
        def _():
            dg_ref[...] = partial

        @pl.when(jnp.logical_not(first))
        def _():
            dg_ref[...] += partial

    blk = pl.BlockSpec((tr, HEAD_DIM), lambda i, h: (i, h))
    vec = pl.BlockSpec((1, HEAD_DIM), lambda i, h: (0, 0))
    return pl.pallas_call(
        body, name=name, grid=(T // tr, D // HEAD_DIM),
        in_specs=[pl.BlockSpec((None, tr, HEAD_DIM), lambda i, h: (part, i, h)), vec, blk],
        out_specs=[blk, vec],
        out_shape=[jax.ShapeDtypeStruct((T, D), BF16), jax.ShapeDtypeStruct((1, HEAD_DIM), F32)],
        compiler_params=_cp(("arbitrary", "arbitrary")),
    )(z, g, dy)


def _cast_part(z, part, name):
    _, T, D = z.shape
    tr = _tile(T, 528, 16)

    def body(z_ref, o_ref):
        o_ref[...] = z_ref[...].astype(BF16)

    return pl.pallas_call(
        body, name=name, grid=(T // tr,),
        in_specs=[pl.BlockSpec((None, tr, D), lambda i: (part, i, 0))],
        out_specs=pl.BlockSpec((tr, D), lambda i: (i, 0)),
        out_shape=jax.ShapeDtypeStruct((T, D), BF16), compiler_params=_cp(("parallel",)),
    )(z)


def _split3(x):
    a = x.astype(BF16)
    r = x - a.astype(F32)
    b = r.astype(BF16)
    c = (r - b.astype(F32)).astype(BF16)
    return a, b, c


def _tri_matmul(tri, x):
    a, b, c = _split3(x)
    dot = lambda v: jnp.dot(tri, v, preferred_element_type=F32)
    return (dot(c) + dot(b)) + dot(a)


def _logf_cumsum(pre, bias, pad, name):
    T = pre.shape[0]
    nb = T // BLOCK

    def body(p_ref, b_ref, c_ref, carry):
        i = pl.program_id(0)

        @pl.when(i == 0)
        def _():
            carry[...] = jnp.zeros_like(carry)

        x = p_ref[...] + b_ref[...]
        lf = jnp.minimum(x, 0.0) - jnp.log(1.0 + jnp.exp(-jnp.abs(x)))
        rows = i * BLOCK + lax.broadcasted_iota(jnp.int32, (BLOCK, LANES), 0)
        lf = jnp.where(rows >= pad, lf, 0.0)
        r = lax.broadcasted_iota(jnp.int32, (BLOCK, BLOCK), 0)
        c = lax.broadcasted_iota(jnp.int32, (BLOCK, BLOCK), 1)
        tri = jnp.where(c <= r, 1.0, 0.0).astype(BF16)
        c_ref[...] = _tri_matmul(tri, lf) + carry[...]
        carry[...] = c_ref[BLOCK - 1:BLOCK, :]

    return pl.pallas_call(
        body, name=name, grid=(nb,),
        in_specs=[pl.BlockSpec((BLOCK, LANES), lambda i: (i, 0)), pl.BlockSpec((1, LANES), lambda i: (0, 0))],
        out_specs=pl.BlockSpec((BLOCK, LANES), lambda i: (i, 0)),
        out_shape=jax.ShapeDtypeStruct((T, LANES), F32),
        scratch_shapes=[pltpu.VMEM((1, LANES), F32)], compiler_params=_cp(("arbitrary",)),
    )(pre, bias)


def _logf_bwd(pre, bias, dc, pad, name):
    T = pre.shape[0]
    nb = T // BLOCK

    def body(p_ref, b_ref, dc_ref, dp_ref, db_ref, carry, dlf_ref):
        i = pl.program_id(0)

        @pl.when(i == 0)
        def _():
            carry[...] = jnp.zeros_like(carry)

        r = lax.broadcasted_iota(jnp.int32, (BLOCK, BLOCK), 0)
        c = lax.broadcasted_iota(jnp.int32, (BLOCK, BLOCK), 1)
        tri = jnp.where(c >= r, 1.0, 0.0).astype(BF16)
        dlf_ref[...] = _tri_matmul(tri, dc_ref[...]) + carry[...]
        carry[...] = dlf_ref[0:1, :]
        dlf = dlf_ref[...]
        x = p_ref[...] + b_ref[...]
        rows = (nb - 1 - i) * BLOCK + lax.broadcasted_iota(jnp.int32, (BLOCK, LANES), 0)
        dpre = jnp.where(rows >= pad, dlf * jax.nn.sigmoid(-x), 0.0)
        dp_ref[...] = dpre
        partial = jnp.sum(dpre, axis=0, keepdims=True)

        @pl.when(i == 0)
        def _():
            db_ref[...] = partial

        @pl.when(i > 0)
        def _():
            db_ref[...] += partial

    rev = pl.BlockSpec((BLOCK, LANES), lambda i: (nb - 1 - i, 0))
    vec = pl.BlockSpec((1, LANES), lambda i: (0, 0))
    return pl.pallas_call(
        body, name=name, grid=(nb,), in_specs=[rev, vec, rev], out_specs=[rev, vec],
        out_shape=[jax.ShapeDtypeStruct((T, LANES), F32), jax.ShapeDtypeStruct((1, LANES), F32)],
        scratch_shapes=[pltpu.VMEM((1, LANES), F32), pltpu.VMEM((BLOCK, LANES), F32)],
        compiler_params=_cp(("arbitrary",)),
    )(pre, bias, dc)


def _loss_head(h, target, first, name):
    T, D = h.shape
    tr = BLOCK
    skip = first // tr

    def body(h_ref, t_ref, dh_ref, loss_ref):
        i = pl.program_id(0)

        @pl.when(i == 0)
        def _():
            loss_ref[...] = jnp.zeros_like(loss_ref)

        @pl.when(i < skip)
        def _():
            dh_ref[...] = jnp.zeros_like(dh_ref)

        @pl.when(i >= skip)
        def _():
            err = h_ref[...] - t_ref[...]
            dh_ref[...] = err * (1.0 / D)
            loss_ref[...] += jnp.sum(err * err) * (0.5 / D)

    row = pl.BlockSpec((tr, D), lambda i: (i, 0))
    return pl.pallas_call(
        body, name=name, grid=(T // tr,),
        in_specs=[row, pl.BlockSpec((tr, D), lambda i: (jnp.maximum(i - skip, 0), 0))],
        out_specs=[row, pl.BlockSpec((8, LANES), lambda i: (0, 0))],
        out_shape=[jax.ShapeDtypeStruct((T, D), F32), jax.ShapeDtypeStruct((8, LANES), F32)],
        compiler_params=_cp(("arbitrary",)),
    )(h, target)


def _adamw(w, g, m, v, name):
    R, C = w.shape
    tr = _tile(R, max(8, TILE_BUDGET // (C * 4 * 7 * 3)), 8)
    bc1, bc2 = 1.0 - ADAM_B1 ** ADAM_STEP, 1.0 - ADAM_B2 ** ADAM_STEP

    def body(w_ref, g_ref, m_ref, v_ref, d_ref, mo_ref, vo_ref):
        g_ = g_ref[...]
        m_ = ADAM_B1 * m_ref[...] + (1.0 - ADAM_B1) * g_
        v_ = ADAM_B2 * v_ref[...] + (1.0 - ADAM_B2) * (g_ * g_)
        d_ref[...] = -ADAM_LR * ((m_ / bc1) / (jnp.sqrt(v_ / bc2) + ADAM_EPS) + ADAM_WD * w_ref[...])
        mo_ref[...] = m_
        vo_ref[...] = v_

    blk = pl.BlockSpec((tr, C), lambda i: (i, 0))
    sds = jax.ShapeDtypeStruct((R, C), F32)
    return pl.pallas_call(
        body, name=name, grid=(R // tr,), in_specs=[blk] * 4, out_specs=[blk] * 3, out_shape=[sds] * 3,
        compiler_params=_cp(("parallel",)),
    )(w, g, m, v)


ROW_CHUNK = 32


def _pick_head(c_blk, h):
    lane = lax.broadcasted_iota(jnp.int32, c_blk.shape, 1)
    return jnp.sum(jnp.where(lane == h, c_blk, 0.0), axis=1, keepdims=True)


def _attn_fwd(q, k, v, c, ct, *, blk, pad, name):
    T, D = q.shape
    H, nq = D // HEAD_DIM, T // blk
    scale = 1.0 / math.sqrt(HEAD_DIM)

    rc = _tile(blk, ROW_CHUNK, 16)

    def body(q_ref, k_ref, v_ref, c_ref, ct_ref, o_ref, lse_ref, s_ref, p_ref, m_ref, l_ref, acc_ref, cq_ref):
        h, i = pl.program_id(0), pl.program_id(1)
        qb = q_ref[...]
        cq_ref[...] = _pick_head(c_ref[...], h)
        m_ref[...] = jnp.full((blk, 1), NEG, F32)
        l_ref[...] = jnp.zeros((blk, 1), F32)
        acc_ref[...] = jnp.zeros((blk, HEAD_DIM), F32)

        def step(j, _):
            off = pl.multiple_of(j * blk, blk)
            s_ref[...] = lax.dot_general(qb, k_ref[pl.ds(off, blk), :], (((1,), (1,)), ((), ())),
                                         preferred_element_type=F32)
            ck = ct_ref[j]

            def softmax_rows(masked):
                def chunk(r, _):
                    r0 = pl.multiple_of(r * rc, rc)
                    rows = pl.ds(r0, rc)
                    s = s_ref[rows, :] * scale + (cq_ref[rows, :] - ck)
                    if masked:
                        qpos = i * blk + r0 + lax.broadcasted_iota(jnp.int32, (rc, blk), 0)
                        kpos = j * blk + lax.broadcasted_iota(jnp.int32, (rc, blk), 1)
                        s = jnp.where((kpos <= qpos) & (kpos >= pad), s, NEG)
                    m_old = m_ref[rows, :]
                    m_new = jnp.maximum(m_old, jnp.max(s, axis=1, keepdims=True))
                    p = jnp.exp(s - m_new)
                    alpha = jnp.exp(m_old - m_new)
                    l_ref[rows, :] = alpha * l_ref[rows, :] + jnp.sum(p, axis=1, keepdims=True)
                    m_ref[rows, :] = m_new
                    acc_ref[rows, :] = alpha * acc_ref[rows, :]
                    p_ref[rows, :] = p.astype(BF16)
                    return 0
                return chunk

            edge = (j == i) | (j == 0)

            @pl.when(edge)
            def _():
                lax.fori_loop(0, blk // rc, softmax_rows(True), 0)

            @pl.when(jnp.logical_not(edge))
            def _():
                lax.fori_loop(0, blk // rc, softmax_rows(False), 0)

            acc_ref[...] += jnp.dot(p_ref[...], v_ref[pl.ds(off, blk), :], preferred_element_type=F32)
            return 0

        lax.fori_loop(0, i + 1, step, 0)
        rowpos = i * blk + lax.broadcasted_iota(jnp.int32, (blk, 1), 0)
        l = l_ref[...]
        o_ref[...] = jnp.where(rowpos >= pad, acc_ref[...] / l, 0.0)
        lse_ref[...] = jnp.broadcast_to(m_ref[...] + jnp.log(l), (blk, LANES))

    scratch = [pltpu.VMEM((blk, blk), F32), pltpu.VMEM((blk, blk), BF16), pltpu.VMEM((blk, 1), F32),
               pltpu.VMEM((blk, 1), F32), pltpu.VMEM((blk, HEAD_DIM), F32), pltpu.VMEM((blk, 1), F32)]
    return pl.pallas_call(
        body, name=name, grid=(H, nq), scratch_shapes=scratch,
        in_specs=[pl.BlockSpec((blk, HEAD_DIM), lambda h, i: (i, h)),
                  pl.BlockSpec((T, HEAD_DIM), lambda h, i: (0, h)),
                  pl.BlockSpec((T, HEAD_DIM), lambda h, i: (0, h)),
                  pl.BlockSpec((blk, LANES), lambda h, i: (i, 0)),
                  pl.BlockSpec((None, nq, 1, blk), lambda h, i: (h, 0, 0, 0))],
        out_specs=[pl.BlockSpec((blk, HEAD_DIM), lambda h, i: (i, h)),
                   pl.BlockSpec((None, blk, LANES), lambda h, i: (h, i, 0))],
        out_shape=[jax.ShapeDtypeStruct((T, D), F32), jax.ShapeDtypeStruct((H, T, LANES), F32)],
        compiler_params=_cp(("parallel", "arbitrary")),
    )(q, k, v, c, ct)


def _attn_bwd(q, k, v, o, do, lse, c, ct, prev, *, blk, pad, name):
    T, D = q.shape
    H, nq = D // HEAD_DIM, T // blk
    scale = 1.0 / math.sqrt(HEAD_DIM)
    has_prev = prev is not None

    rc = _tile(blk, ROW_CHUNK, 16)
    n_in = 11 if has_prev else 8
    nt_dims = (((1,), (1,)), ((), ()))
    tn_dims = (((0,), (0,)), ((), ()))

    def body(*refs):
        q_ref, k_ref, v_ref, o_ref, do_ref, lse_ref, c_ref, ct_ref = refs[:8]
        pdk_ref, pdv_ref, pdc_ref = refs[8:11] if has_prev else (None, None, None)
        dq_ref, dk_ref, dv_ref, dct_ref = refs[n_in:n_in + 4]
        s_ref, dp_ref, p_ref, ds_ref, dka_ref, dva_ref, dc8_ref = refs[n_in + 4:]
        h, j = pl.program_id(0), pl.program_id(1)

        @pl.when(j == 0)
        def _():
            dq_ref[...] = jnp.zeros_like(dq_ref)

        kb, vb = k_ref[...], v_ref[...]
        ck = ct_ref[...]
        dka_ref[...] = jnp.zeros_like(dka_ref)
        dva_ref[...] = jnp.zeros_like(dva_ref)
        dc8_ref[...] = jnp.zeros_like(dc8_ref)

        def step(i, _):
            off = pl.multiple_of(i * blk, blk)
            qb = q_ref[pl.ds(off, blk), :]
            dob = do_ref[pl.ds(off, blk), :]
            s_ref[...] = lax.dot_general(qb, kb, nt_dims, preferred_element_type=F32)
            dp_ref[...] = lax.dot_general(dob, vb, nt_dims, preferred_element_type=F32)

            def score_grad_rows(masked):
                def chunk(r, _):
                    r0 = pl.multiple_of(r * rc, rc)
                    rows = pl.ds(r0, rc)
                    grows = pl.ds(pl.multiple_of(off + r0, rc), rc)
                    cq = _pick_head(c_ref[grows, :], h)
                    lse_r = lse_ref[grows, :][:, 0:1]
                    delta = jnp.sum(do_ref[grows, :].astype(F32) * o_ref[grows, :], axis=1, keepdims=True)
                    p = jnp.exp(s_ref[rows, :] * scale + (cq - ck) - lse_r)
                    if masked:
                        qpos = off + r0 + lax.broadcasted_iota(jnp.int32, (rc, blk), 0)
                        kpos = j * blk + lax.broadcasted_iota(jnp.int32, (rc, blk), 1)
                        p = jnp.where((kpos <= qpos) & (kpos >= pad), p, 0.0)
                    ds = p * (dp_ref[rows, :] - delta)
                    p_ref[rows, :] = p.astype(BF16)
                    ds_ref[rows, :] = ds.astype(BF16)
                    part = ds[0:8, :]
                    for a in range(1, rc // 8):
                        part = part + ds[8 * a:8 * a + 8, :]
                    dc8_ref[...] += part
                    return 0
                return chunk

            edge = (i == j) | (j == 0)

            @pl.when(edge)
            def _():
                lax.fori_loop(0, blk // rc, score_grad_rows(True), 0)

            @pl.when(jnp.logical_not(edge))
            def _():
                lax.fori_loop(0, blk // rc, score_grad_rows(False), 0)

            dsb = ds_ref[...]
            dva_ref[...] += lax.dot_general(p_ref[...], dob, tn_dims, preferred_element_type=F32)
            dka_ref[...] += lax.dot_general(dsb, qb, tn_dims, preferred_element_type=F32)
            dq_ref[pl.ds(off, blk), :] += jnp.dot(dsb, kb, preferred_element_type=F32) * scale
            return 0

        lax.fori_loop(j, nq, step, 0)
        dk = dka_ref[...] * scale
        dv = dva_ref[...]
        dck = -jnp.sum(dc8_ref[...], axis=0, keepdims=True)
        if has_prev:
            dk, dv, dck = dk + pdk_ref[...], dv + pdv_ref[...], dck + pdc_ref[...]
        dk_ref[...] = dk
        dv_ref[...] = dv
        dct_ref[...] = dck

    col = pl.BlockSpec((T, HEAD_DIM), lambda h, j: (0, h))
    kblk = pl.BlockSpec((blk, HEAD_DIM), lambda h, j: (j, h))
    ctb = pl.BlockSpec((None, None, 1, blk), lambda h, j: (h, j, 0, 0))
    in_specs = [col, kblk, kblk, col, col,
                pl.BlockSpec((None, T, LANES), lambda h, j: (h, 0, 0)),
                pl.BlockSpec((T, LANES), lambda h, j: (0, 0)), ctb]
    operands = [q, k, v, o, do, lse, c, ct]
    if has_prev:
        in_specs += [kblk, kblk, ctb]
        operands += list(prev)
    scratch = [pltpu.VMEM((blk, blk), F32), pltpu.VMEM((blk, blk), F32), pltpu.VMEM((blk, blk), BF16),
               pltpu.VMEM((blk, blk), BF16), pltpu.VMEM((blk, HEAD_DIM), F32), pltpu.VMEM((blk, HEAD_DIM), F32),
               pltpu.VMEM((8, blk), F32)]
    return pl.pallas_call(
        body, name=name, grid=(H, nq), in_specs=in_specs, out_specs=[col, kblk, kblk, ctb], scratch_shapes=scratch,
        out_shape=[jax.ShapeDtypeStruct((T, D), F32), jax.ShapeDtypeStruct((T, D), F32),
                   jax.ShapeDtypeStruct((T, D), F32), jax.ShapeDtypeStruct((H, nq, 1, blk), F32)],
        compiler_params=_cp(("parallel", "arbitrary")),
    )(*operands)


HBM_SPEC = pl.BlockSpec(memory_space=pltpu.HBM)


def _place():
    x, y, c = lax.axis_index("x"), lax.axis_index("y"), lax.axis_index("c")
    chips = [(1 - x, y), (x, 1 - y), (1 - x, 1 - y)]
    return x, y, c, chips


def _remote(src, dst, send_sems, recv_sems, k, to):
    return pltpu.make_async_remote_copy(src_ref=src, dst_ref=dst, send_sem=send_sems.at[k],
                                        recv_sem=recv_sems.at[k], device_id=to, device_id_type=MESH)


def _all_gather(shards, specs, name):
    n = len(shards)

    def full_shape(s, axis):
        return (s.shape[0] * N_CHIPS, s.shape[1]) if axis == 0 else (s.shape[0], s.shape[1] * N_CHIPS)

    def body(*refs):
        srcs, outs = refs[:n], refs[n:2 * n]
        send_sems, recv_sems, local_sems = refs[2 * n:]
        x, y, c, chips = _place()
        sibling = (x, y, 1 - c)

        def region(t, chip, half):
            rs, cs = shards[t].shape
            q = 2 * chip[0] + chip[1]
            axis, split = specs[t]
            nrow = rs // 2 if half is not None else rs
            r0 = 0 if half is None else half * nrow
            if axis == 0:
                return outs[t].at[pl.ds(q * rs + r0, nrow), :]
            return outs[t].at[pl.ds(r0, nrow), pl.ds(pl.multiple_of(q * cs, cs), cs)]

        def piece(t, half):
            rs = shards[t].shape[0]
            if half is None:
                return srcs[t]
            return srcs[t].at[pl.ds(half * (rs // 2), rs // 2), :]

        local = [pltpu.make_async_copy(srcs[t], region(t, (x, y), None), local_sems.at[t]) for t in range(n)]
        for cp in local:
            cp.start()
        sends = []
        for t in range(n):
            half = c if specs[t][1] else None
            for j, chip in enumerate(chips):
                cp = _remote(piece(t, half), region(t, (x, y), half), send_sems, recv_sems, 6 * t + j, (*chip, c))
                cp.start()
                sends.append(cp)
        for t in range(n):
            half = c if specs[t][1] else None
            for j, chip in enumerate(chips):
                landed = region(t, chip, half)
                _remote(landed, landed, send_sems, recv_sems, 6 * t + j, (*chip, c)).wait_recv()
                if specs[t][1]:
                    cp = _remote(landed, landed, send_sems, recv_sems, 6 * t + 3 + j, sibling)
                    cp.start()
                    sends.append(cp)
        for t in range(n):
            if specs[t][1]:
                for j, chip in enumerate(chips):
                    got = region(t, chip, 1 - c)
                    _remote(got, got, send_sems, recv_sems, 6 * t + 3 + j, sibling).wait_recv()
        for cp in sends:
            cp.wait_send()
        for cp in local:
            cp.wait()

    return pl.pallas_call(
        body, name=name, in_specs=[HBM_SPEC] * n, out_specs=[HBM_SPEC] * n,
        out_shape=[jax.ShapeDtypeStruct(full_shape(s, specs[t][0]), s.dtype) for t, s in enumerate(shards)],
        scratch_shapes=[pltpu.SemaphoreType.DMA((6 * n,)), pltpu.SemaphoreType.DMA((6 * n,)),
                        pltpu.SemaphoreType.DMA((n,))],
        compiler_params=pltpu.CompilerParams(has_side_effects=True),
    )(*shards)


def _grad_view(g, axis):
    R, C = g.shape
    nq = N_CHIPS if axis == 0 else 1
    return g.reshape(nq, 2, R // (2 * nq), C)


def _swap_halves(views, name):
    n = len(views)

    def body(*refs):
        srcs, outs, send_sems, recv_sems = refs[:n], refs[n:2 * n], refs[2 * n], refs[2 * n + 1]
        x, y, c, _ = _place()
        cps = [_remote(srcs[t].at[:, 1 - c], outs[t], send_sems, recv_sems, t, (x, y, 1 - c)) for t in range(n)]
        for cp in cps:
            cp.start()
        for cp in cps:
            cp.wait()

    return pl.pallas_call(
        body, name=name, in_specs=[HBM_SPEC] * n, out_specs=[HBM_SPEC] * n,
        out_shape=[jax.ShapeDtypeStruct((v.shape[0],) + v.shape[2:], v.dtype) for v in views],
        scratch_shapes=[pltpu.SemaphoreType.DMA((n,)), pltpu.SemaphoreType.DMA((n,))],
        compiler_params=pltpu.CompilerParams(has_side_effects=True),
    )(*views)


def _add_half(view, got, c_idx, name):
    nq, _, Rh, C = view.shape
    tr = _tile(Rh, max(16, (2 * 1024 * 1024) // (C * 2)), 16)

    def body(c_ref, a_ref, b_ref, o_ref):
        o_ref[...] = (a_ref[...].astype(F32) + b_ref[...].astype(F32)).astype(BF16)

    grid_spec = pltpu.PrefetchScalarGridSpec(
        num_scalar_prefetch=1, grid=(nq, Rh // tr),
        in_specs=[pl.BlockSpec((None, None, tr, C), lambda q, i, c_ref: (q, c_ref[0], i, 0)),
                  pl.BlockSpec((None, tr, C), lambda q, i, c_ref: (q, i, 0))],
        out_specs=pl.BlockSpec((None, tr, C), lambda q, i, c_ref: (q, i, 0)))
    return pl.pallas_call(
        body, name=name, grid_spec=grid_spec, out_shape=jax.ShapeDtypeStruct((nq, Rh, C), BF16),
        compiler_params=_cp(("parallel", "parallel")),
    )(c_idx, view, got)


SEM_SPEC = pl.BlockSpec(memory_space=pltpu.SEMAPHORE)
ANY_SPEC = pl.BlockSpec(memory_space=pl.ANY)
TOKEN_SPEC = pl.BlockSpec(memory_space=pltpu.VMEM)
TOKEN = jax.ShapeDtypeStruct((8, LANES), F32)
SPLIT_COPY = pltpu.CompilerParams(has_side_effects=pltpu.SideEffectType.DATAFLOW_SIDE_EFFECTING)


def _region(ref, axis, chip, half):
    q = 2 * chip[0] + chip[1]
    if axis == 0:
        rs = ref.shape[0] // N_CHIPS
        return ref.at[pl.ds(q * rs + half * (rs // 2), rs // 2), :]
    rh, cs = ref.shape[0] // 2, ref.shape[1] // N_CHIPS
    return ref.at[pl.ds(half * rh, rh), pl.ds(pl.multiple_of(q * cs, cs), cs)]


def _gather_start(fulls, axes, after, name):
    n = len(fulls)

    def body(*refs):
        ins = refs[:n]
        token = refs[-1]
        send_sems, recv_sems = refs[n + 1], refs[n + 2]
        x, y, c, chips = _place()
        for t in range(n):
            mine = _region(ins[t], axes[t], (x, y), c)
            for j, chip in enumerate(chips):
                _remote(mine, mine, send_sems, recv_sems, 3 * t + j, (*chip, c)).start()
        token[...] = jnp.zeros_like(token)

    sems = pltpu.SemaphoreType.DMA((3 * n,))
    outs = pl.pallas_call(
        body, name=name, in_specs=[HBM_SPEC] * n + [ANY_SPEC],
        out_specs=[SEM_SPEC, SEM_SPEC] + [HBM_SPEC] * n + [TOKEN_SPEC],
        out_shape=[sems, sems] + [jax.ShapeDtypeStruct(f.shape, f.dtype) for f in fulls] + [TOKEN],
        input_output_aliases={t: 2 + t for t in range(n)}, compiler_params=SPLIT_COPY,
    )(*fulls, after)
    return outs[0], outs[1], list(outs[2:2 + n]), outs[-1]


def _gather_wait(send_sems, recv_sems, fulls, axes, after, name):
    n = len(fulls)

    def body(*refs):
        ins = refs[:n]
        send_sems, recv_sems = refs[n], refs[n + 1]
        x, y, c, chips = _place()
        for t in range(n):
            mine = _region(ins[t], axes[t], (x, y), c)
            for j, chip in enumerate(chips):
                _remote(mine, mine, send_sems, recv_sems, 3 * t + j, (*chip, c)).wait_send()
                theirs = _region(ins[t], axes[t], chip, c)
                _remote(theirs, theirs, send_sems, recv_sems, 3 * t + j, (*chip, c)).wait_recv()

    outs = pl.pallas_call(
        body, name=name, in_specs=[HBM_SPEC] * n + [SEM_SPEC, SEM_SPEC, ANY_SPEC], out_specs=[HBM_SPEC] * n,
        out_shape=[jax.ShapeDtypeStruct(f.shape, f.dtype) for f in fulls],
        input_output_aliases={t: t for t in range(n)}, compiler_params=SPLIT_COPY,
    )(*fulls, send_sems, recv_sems, after)
    return list(outs)


def _gather_forward(fulls, axes, name):
    n = len(fulls)

    def body(*refs):
        outs = refs[n:2 * n]
        send_sems, recv_sems = refs[2 * n], refs[2 * n + 1]
        x, y, c, chips = _place()
        sibling = (x, y, 1 - c)
        cps = []
        for t in range(n):
            for j, chip in enumerate(chips):
                landed = _region(outs[t], axes[t], chip, c)
                cps.append(_remote(landed, landed, send_sems, recv_sems, 3 * t + j, sibling))
        for cp in cps:
            cp.start()
        for t in range(n):
            for j, chip in enumerate(chips):
                got = _region(outs[t], axes[t], chip, 1 - c)
                _remote(got, got, send_sems, recv_sems, 3 * t + j, sibling).wait_recv()
        for cp in cps:
            cp.wait_send()

    outs = pl.pallas_call(
        body, name=name, in_specs=[HBM_SPEC] * n, out_specs=[HBM_SPEC] * n,
        out_shape=[jax.ShapeDtypeStruct(f.shape, f.dtype) for f in fulls],
        scratch_shapes=[pltpu.SemaphoreType.DMA((3 * n,)), pltpu.SemaphoreType.DMA((3 * n,))],
        input_output_aliases={t: t for t in range(n)},
        compiler_params=pltpu.CompilerParams(has_side_effects=True),
    )(*fulls)
    return list(outs)


def _shard_cols(s, axis):
    return s.shape[2] if axis == 0 else s.shape[2] // N_CHIPS


def _piece(ref, axis, chip):
    q = 2 * chip[0] + chip[1]
    if axis == 0:
        return ref.at[q]
    cs = ref.shape[2] // N_CHIPS
    return ref.at[0, :, pl.ds(pl.multiple_of(q * cs, cs), cs)]


def _scatter_start(sums, axes, after, name):
    n = len(sums)

    def body(*refs):
        ins = refs[:n]
        send_sems, recv_sems = refs[n + 1], refs[n + 2]
        lands = refs[2 * n + 3:3 * n + 3]
        token = refs[-1]
        x, y, c, chips = _place()
        for t in range(n):
            for j, chip in enumerate(chips):
                _remote(_piece(ins[t], axes[t], chip), lands[t].at[j], send_sems, recv_sems, 3 * t + j, (*chip, c)).start()
        token[...] = jnp.zeros_like(token)

    sems = pltpu.SemaphoreType.DMA((3 * n,))
    land_shapes = [jax.ShapeDtypeStruct((3, s.shape[1], _shard_cols(s, a)), s.dtype) for s, a in zip(sums, axes)]
    outs = pl.pallas_call(
        body, name=name, in_specs=[HBM_SPEC] * n + [ANY_SPEC],
        out_specs=[SEM_SPEC, SEM_SPEC] + [HBM_SPEC] * (2 * n) + [TOKEN_SPEC],
        out_shape=[sems, sems] + [jax.ShapeDtypeStruct(s.shape, s.dtype) for s in sums] + land_shapes + [TOKEN],
        input_output_aliases={t: 2 + t for t in range(n)}, compiler_params=SPLIT_COPY,
    )(*sums, after)
    return outs[0], outs[1], list(outs[2:2 + n]), list(outs[2 + n:2 + 2 * n]), outs[-1]


def _scatter_wait(send_sems, recv_sems, sums, lands, axes, after, name):
    n = len(sums)

    def body(*refs):
        ins, lnd = refs[:n], refs[n:2 * n]
        send_sems, recv_sems = refs[2 * n], refs[2 * n + 1]
        x, y, c, chips = _place()
        for t in range(n):
            for j, chip in enumerate(chips):
                cp = _remote(_piece(ins[t], axes[t], chip), lnd[t].at[j], send_sems, recv_sems, 3 * t + j, (*chip, c))
                cp.wait_send()
                cp.wait_recv()

    outs = pl.pallas_call(
        body, name=name, in_specs=[HBM_SPEC] * (2 * n) + [SEM_SPEC, SEM_SPEC, ANY_SPEC], out_specs=[HBM_SPEC] * (2 * n),
        out_shape=[jax.ShapeDtypeStruct(s.shape, s.dtype) for s in sums + lands],
        input_output_aliases={t: t for t in range(2 * n)}, compiler_params=SPLIT_COPY,
    )(*sums, *lands, send_sems, recv_sems, after)
    return list(outs[:n]), list(outs[n:])


def _sum_chips(own, axis, got, q_idx, c_idx, name):
    _, Rh, cc = got.shape
    tr = _tile(Rh, max(16, (1024 * 1024) // (cc * 2)), 16)

    def body(q_ref, c_ref, a_ref, b0, b1, b2, o_ref):
        f = lambda r: r[...].astype(F32)
        o_ref[...] = ((f(a_ref) + f(b0)) + f(b1)) + f(b2)

    if axis == 0:
        own_spec = pl.BlockSpec((None, tr, cc), lambda i, q, c: (q[0], i, 0))
    else:
        own_spec = pl.BlockSpec((None, tr, cc), lambda i, q, c: (0, i, q[0]))
    slot = lambda j: pl.BlockSpec((None, tr, cc), lambda i, q, c, j=j: (j, i, 0))
    grid_spec = pltpu.PrefetchScalarGridSpec(
        num_scalar_prefetch=2, grid=(Rh // tr,), in_specs=[own_spec, slot(0), slot(1), slot(2)],
        out_specs=pl.BlockSpec((None, tr, cc), lambda i, q, c: (c[0], i, 0)))
    return pl.pallas_call(
        body, name=name, grid_spec=grid_spec, out_shape=jax.ShapeDtypeStruct((2, Rh, cc), F32),
        compiler_params=_cp(("parallel",)),
    )(q_idx, c_idx, own, got, got, got)


def _join_halves(pairs, name):
    n = len(pairs)

    def body(*refs):
        outs = refs[n:2 * n]
        send_sems, recv_sems = refs[2 * n], refs[2 * n + 1]
        x, y, c, _ = _place()
        cps = [_remote(outs[t].at[c], outs[t].at[c], send_sems, recv_sems, t, (x, y, 1 - c)) for t in range(n)]
        for cp in cps:
            cp.start()
        for t in range(n):
            cps[t].wait_send()
            _remote(outs[t].at[1 - c], outs[t].at[1 - c], send_sems, recv_sems, t, (x, y, 1 - c)).wait_recv()

    outs = pl.pallas_call(
        body, name=name, in_specs=[HBM_SPEC] * n, out_specs=[HBM_SPEC] * n,
        out_shape=[jax.ShapeDtypeStruct(p.shape, p.dtype) for p in pairs],
        scratch_shapes=[pltpu.SemaphoreType.DMA((n,)), pltpu.SemaphoreType.DMA((n,))],
        input_output_aliases={t: t for t in range(n)},
        compiler_params=pltpu.CompilerParams(has_side_effects=True),
    )(*pairs)
    return list(outs)


def _reduce_begin(grads, axes, c_idx, after, tag):
    views = [_grad_view(g, a) for g, a in zip(grads, axes)]
    got = _swap_halves(views, f"rs_swap_{tag}")
    sums = [_add_half(v, p, c_idx, f"rs_add_{tag}_{t}") for t, (v, p) in enumerate(zip(views, got))]
    send_sems, recv_sems, sums, lands, token = _scatter_start(sums, axes, after, f"rs_chips_start_{tag}")
    return (send_sems, recv_sems, sums, lands, axes, tag), token


def _reduce_finish(pending, q_idx, c_idx, after):
    send_sems, recv_sems, sums, lands, axes, tag = pending
    sums, lands = _scatter_wait(send_sems, recv_sems, sums, lands, axes, after, f"rs_chips_wait_{tag}")
    pairs = [_sum_chips(s, a, r, q_idx, c_idx, f"rs_sum_{tag}_{t}")
             for t, (s, a, r) in enumerate(zip(sums, axes, lands))]
    joined = _join_halves(pairs, f"rs_join_{tag}")
    return [j.reshape(2 * j.shape[1], j.shape[2]) for j in joined]


def _all_reduce_small(buf, name):
    R, C = buf.shape

    def body(b_ref, o_ref, slots, send_sems, recv_sems):
        x, y, c, _ = _place()
        me = 4 * x + 2 * y + c
        cps = []
        for k in range(1, 8):
            to = (x ^ (k >> 2), y ^ ((k >> 1) & 1), c ^ (k & 1))
            cps.append(pltpu.make_async_remote_copy(
                src_ref=b_ref, dst_ref=slots.at[me], send_sem=send_sems.at[k - 1], recv_sem=recv_sems.at[me],
                device_id=to, device_id_type=MESH))
        for cp in cps:
            cp.start()
        slots[me] = b_ref[...]
        for k in range(1, 8):
            src = me ^ k
            pltpu.make_async_remote_copy(
                src_ref=b_ref, dst_ref=slots.at[src], send_sem=send_sems.at[k - 1], recv_sem=recv_sems.at[src],
                device_id=(x, y, c), device_id_type=MESH).wait_recv()
        for cp in cps:
            cp.wait_send()
        total = slots[0]
        for d in range(1, 8):
            total = total + slots[d]
        o_ref[...] = total

    vm = pl.BlockSpec(memory_space=pltpu.VMEM)
    return pl.pallas_call(
        body, name=name, in_specs=[vm], out_specs=vm, out_shape=jax.ShapeDtypeStruct((R, C), F32),
        scratch_shapes=[pltpu.VMEM((8, R, C), F32), pltpu.SemaphoreType.DMA((7,)), pltpu.SemaphoreType.DMA((8,))],
        compiler_params=pltpu.CompilerParams(has_side_effects=True, vmem_limit_bytes=VMEM_LIMIT),
    )(buf)


def kernel(x, meta, a_norm, a_w_in, a_conv, a_w_out, kv_norm, w_kv, k_norm, w_f, b_f, b_norm, b_w_q, b_q_norm, b_w_o, ffn_norm, ffn_w_gu, ffn_w_down, loss_target, m_meta, m_a_norm, m_a_w_in, m_a_conv, m_a_w_out, m_kv_norm, m_w_kv, m_k_norm, m_w_f, m_b_f, m_b_norm, m_b_w_q, m_b_q_norm, m_b_w_o, m_ffn_norm, m_ffn_w_gu, m_ffn_w_down, v_meta, v_a_norm, v_a_w_in, v_a_conv, v_a_w_out, v_kv_norm, v_w_kv, v_k_norm, v_w_f, v_b_f, v_b_norm, v_b_w_q, v_b_q_norm, v_b_w_o, v_ffn_norm, v_ffn_w_gu, v_ffn_w_down):
    SEQ, D = x.shape[1], x.shape[2]
    n_meta = meta.shape[0]
    pad = BLOCK - n_meta
    first = pad + n_meta
    T = first + SEQ
    H = D // HEAD_DIM
    Ds = D // N_CHIPS
    n_a, n_b, depth = a_w_in.shape[0], b_w_q.shape[0], ffn_norm.shape[0]
    blk = _tile(T, 384, BLOCK)
    cx, cy, cc = lax.axis_index("x"), lax.axis_index("y"), lax.axis_index("c")
    q_me = 2 * cx + cy
    c_idx = jnp.reshape(cc, (1,)).astype(jnp.int32)
    q_idx = jnp.reshape(q_me, (1,)).astype(jnp.int32)
    rows8 = lambda v: jnp.pad(v, ((0, -v.shape[0] % 8), (0, 0)))

    col_parts = [meta, a_norm, a_conv.reshape(3 * n_a, Ds)]
    col_pack = jnp.concatenate([rows8(p) for p in col_parts], axis=0)
    col_full, w_f_full = _all_gather([col_pack, w_f], [(1, False), (0, False)], "ag_small")
    col_offs = [sum(rows8(p).shape[0] for p in col_parts[:i]) for i in range(3)]
    meta_f = col_full[:n_meta]
    a_norm_f = col_full[col_offs[1]:col_offs[1] + n_a]
    a_conv_f = col_full[col_offs[2]:col_offs[2] + 3 * n_a].reshape(n_a, 3, D)
    w_fp = jnp.pad(w_f_full, ((0, 0), (0, LANES - H))).astype(BF16)
    b_fp = jnp.pad(b_f, (0, LANES - H)).reshape(1, LANES)

    stages = []
    for l in range(n_a):
        stages += [[(a_w_in, l, 1), (a_w_out, l, 0)], [(ffn_w_gu, l, 1), (ffn_w_down, l, 0)]]
    for j in range(n_b):
        stages += [[(b_w_q, j, 0), (b_w_o, j, 0)], [(ffn_w_gu, n_a + j, 1), (ffn_w_down, n_a + j, 0)]]
    stages[2 * n_a].append((w_kv[None], 0, 1))

    def gather_begin(k, after):
        axes = [ax for _, _, ax in stages[k]]
        fulls = [_cast_into_full(w, l, ax, q_idx, f"cast_{k}_{i}") for i, (w, l, ax) in enumerate(stages[k])]
        send_sems, recv_sems, fulls, token = _gather_start(fulls, axes, after, f"ag_start_{k}")
        return (send_sems, recv_sems, fulls, axes, k), token

    def gather_end(handle, after):
        send_sems, recv_sems, fulls, axes, k = handle
        fulls = _gather_wait(send_sems, recv_sems, fulls, axes, after, f"ag_wait_{k}")
        return _gather_forward(fulls, axes, f"ag_forward_{k}")

    handle, _ = gather_begin(0, col_full)
    arrived = [gather_end(handle, col_full)]
    in_flight_gather = [None]

    def enter_segment():
        w = arrived[-1]
        token = None
        if len(arrived) < len(stages):
            in_flight_gather[0], token = gather_begin(len(arrived), w[0])
        return w, token

    def leave_segment(h_out):
        if in_flight_gather[0] is not None:
            arrived.append(gather_end(in_flight_gather[0], h_out))
            in_flight_gather[0] = None

    h = jnp.concatenate([jnp.zeros((pad, D), F32), meta_f, x[0]], axis=0)
    saved = []

    def ffn_fwd(h, layer):
        (w_gu, w_down), token = enter_segment()
        xn = _rms_fwd(h, ffn_norm[layer:layer + 1], f"ffn_norm_{layer}", dep=token)
        z = _matmul(xn, w_gu, mode="nn", out_dtype=BF16, name=f"ffn_gu_{layer}", out_parts=2)
        act = _swiglu_fwd(z, f"swiglu_{layer}")
        out = _matmul(act, w_down, mode="nn", out_dtype=F32, name=f"ffn_down_{layer}", res=h)
        leave_segment(out)
        return out, (h, xn, z, act), (w_gu, w_down)

    wa, wb = [], []
    for l in range(n_a):
        (w_in, w_out), token = enter_segment()
        xn = _rms_fwd(h, a_norm_f[l:l + 1], f"a_norm_{l}", dep=token)
        z = _matmul(xn, w_in, mode="nn", out_dtype=BF16, name=f"a_in_{l}", out_parts=3)
        y = _gate_fwd(z, a_conv_f[l], f"a_gate_{l}")
        h2 = _matmul(y, w_out, mode="nn", out_dtype=F32, name=f"a_out_{l}", res=h)
        leave_segment(h2)
        h3, ffn_saved, w_ffn = ffn_fwd(h2, l)
        saved.append((h, xn, z, y, ffn_saved))
        wa.append((w_in, w_out) + w_ffn)
        h = h3

    for j in range(n_b):
        layer = n_a + j
        w_mix, token = enter_segment()
        w_q, w_o = w_mix[:2]
        if j == 0:
            h_kv, w_kv_b = h, w_mix[2]
            xkv = _rms_fwd(h, kv_norm.reshape(1, D), "kv_norm", dep=token)
            token = None
            kvz = _matmul(xkv, w_kv_b, mode="nn", out_dtype=F32, name="kv_proj", out_parts=2)
            k_n = _headnorm_fwd(kvz, 0, k_norm.reshape(1, HEAD_DIM), "k_headnorm")
            v_b = _cast_part(kvz, 1, "v_cast")
            pre = _matmul(xkv, w_fp, mode="nn", out_dtype=F32, name="f_proj")
            c_cum = _logf_cumsum(pre, b_fp, pad, "logf_cumsum")
            c_t = c_cum[:, :H].T.reshape(H, T // blk, 1, blk)
        xn = _rms_fwd(h, b_norm[j:j + 1], f"b_norm_{j}", dep=token)
        qz = _matmul(xn, w_q, mode="nn", out_dtype=F32, name=f"b_q_{j}")[None]
        q_n = _headnorm_fwd(qz, 0, b_q_norm[j:j + 1], f"q_headnorm_{j}")
        o, lse = _attn_fwd(q_n, k_n, v_b, c_cum, c_t, blk=blk, pad=pad, name=f"attn_fwd_{j}")
        h2 = _matmul(o, w_o, mode="nn", out_dtype=F32, name=f"b_o_{j}", res=h)
        leave_segment(h2)
        h3, ffn_saved, w_ffn = ffn_fwd(h2, layer)
        saved.append((h, xn, qz, q_n, o, lse, ffn_saved))
        wb.append((w_q, w_o) + w_ffn)
        h = h3

    dh, loss_blk = _loss_head(h, loss_target[0], first, "loss_head")
    loss = lax.psum(loss_blk[0, 0], ("x", "y", "c"))

    shards = {}
    in_flight = [None]

    def reduce_later(names, grads, axes, tag, done):
        after = c_idx
        if in_flight[0] is not None:
            prev_names, pending = in_flight[0]
            got = _reduce_finish(pending, q_idx, c_idx, done)
            shards.update(zip(prev_names, got))
            after = got[0]
        pending, token = _reduce_begin(grads, axes, c_idx, after, tag)
        in_flight[0] = (names, pending)
        return token

    def ffn_bwd(dh, layer, w_gu, w_down, ffn_saved, dep):
        h_in, xn, z, act = ffn_saved
        da = _matmul(dh, w_down, mode="nt", out_dtype=BF16, name=f"ffn_down_dx_{layer}", dep=dep)
        g_down = _matmul(act, dh, mode="tn", out_dtype=BF16, name=f"ffn_down_dw_{layer}")
        dz = _swiglu_bwd(z, da, f"swiglu_bwd_{layer}")
        dxn = _matmul(dz, w_gu, mode="nt", out_dtype=F32, name=f"ffn_gu_dx_{layer}")
        g_gu = _matmul(xn, dz, mode="tn", out_dtype=BF16, name=f"ffn_gu_dw_{layer}")
        dh, dg = _rms_bwd(h_in, ffn_norm[layer:layer + 1], dxn, dh, f"ffn_norm_bwd_{layer}")
        token = reduce_later([("ffn_w_gu", layer), ("ffn_w_down", layer)], [g_gu, g_down], [1, 0], f"f{layer}", dh)
        return dh, dg, token

    d_ffn_norm, d_b_norm, d_q_norm, d_a_norm, d_a_conv = {}, {}, {}, {}, {}
    kv_prev = None
    token = None
    for j in reversed(range(n_b)):
        layer = n_a + j
        w_q, w_o, w_gu, w_down = wb[j]
        h_in, xn, qz, q_n, o, lse, ffn_saved = saved[layer]
        dh, d_ffn_norm[layer], token = ffn_bwd(dh, layer, w_gu, w_down, ffn_saved, token)
        do = _matmul(dh, w_o, mode="nt", out_dtype=BF16, name=f"b_o_dx_{j}", dep=token)
        g_o = _matmul(o, dh, mode="tn", out_dtype=BF16, name=f"b_o_dw_{j}")
        dq, dk, dv, dct = _attn_bwd(q_n, k_n, v_b, o, do, lse, c_cum, c_t, kv_prev, blk=blk, pad=pad,
                                    name=f"attn_bwd_{j}")
        kv_prev = (dk, dv, dct)
        dqz, d_q_norm[j] = _headnorm_bwd(qz, 0, b_q_norm[j:j + 1], dq, f"q_headnorm_bwd_{j}")
        dxn = _matmul(dqz, w_q, mode="nt", out_dtype=F32, name=f"b_q_dx_{j}")
        g_q = _matmul(xn, dqz, mode="tn", out_dtype=BF16, name=f"b_q_dw_{j}")
        dh, d_b_norm[j] = _rms_bwd(h_in, b_norm[j:j + 1], dxn, dh, f"b_norm_bwd_{j}")
        if j > 0:
            token = reduce_later([("b_w_q", j), ("b_w_o", j)], [g_q, g_o], [0, 0], f"b{j}", dh)

    dk, dv, dct = kv_prev
    dkz, d_k_norm = _headnorm_bwd(kvz, 0, k_norm.reshape(1, HEAD_DIM), dk, "k_headnorm_bwd")
    dvz = _cast_part(dv[None], 0, "dv_cast")
    dkv = jnp.stack([dkz, dvz])
    dc = jnp.pad(dct.reshape(H, T).T, ((0, 0), (0, LANES - H)))
    dpre, d_b_f = _logf_bwd(pre, b_fp, dc, pad, "logf_bwd")
    dxkv = _matmul(dkv, w_kv_b, mode="nt", out_dtype=F32, name="kv_proj_dx")
    dxkv = _matmul(dpre, w_fp, mode="nt", out_dtype=F32, name="f_proj_dx", res=dxkv)
    g_kv = _matmul(xkv, dkv, mode="tn", out_dtype=BF16, name="kv_proj_dw")
    d_w_f = _matmul(xkv, dpre, mode="tn", out_dtype=F32, name="f_proj_dw")
    dh, d_kv_norm = _rms_bwd(h_kv, kv_norm.reshape(1, D), dxkv, dh, "kv_norm_bwd")
    token = reduce_later([("b_w_q", 0), ("b_w_o", 0), ("w_kv", 0)], [g_q, g_o, g_kv], [0, 0, 1], "b0", dh)

    for l in reversed(range(n_a)):
        w_in, w_out, w_gu, w_down = wa[l]
        h_in, xn, z, y, ffn_saved = saved[l]
        dh, d_ffn_norm[l], token = ffn_bwd(dh, l, w_gu, w_down, ffn_saved, token)
        dy = _matmul(dh, w_out, mode="nt", out_dtype=F32, name=f"a_out_dx_{l}", dep=token)
        g_out = _matmul(y, dh, mode="tn", out_dtype=BF16, name=f"a_out_dw_{l}")
        dz, d_a_conv[l] = _gate_bwd(z, a_conv_f[l], dy, f"a_gate_bwd_{l}")
        dxn = _matmul(dz, w_in, mode="nt", out_dtype=F32, name=f"a_in_dx_{l}")
        g_in = _matmul(xn, dz, mode="tn", out_dtype=BF16, name=f"a_in_dw_{l}")
        dh, d_a_norm[l] = _rms_bwd(h_in, a_norm_f[l:l + 1], dxn, dh, f"a_norm_bwd_{l}")
        token = reduce_later([("a_w_in", l), ("a_w_out", l)], [g_in, g_out], [1, 0], f"a{l}", dh)

    prev_names, pending = in_flight[0]
    shards.update(zip(prev_names, _reduce_finish(pending, q_idx, c_idx, dh)))
    grad_x = dh[first:][None]

    widen = lambda v: jnp.pad(v, ((0, 0), (0, D - v.shape[1])))
    groups = [
        [dh[pad:first]],
        [d_a_norm[l] for l in range(n_a)],
        [d_a_conv[l] for l in range(n_a)],
        [d_kv_norm],
        [widen(d_k_norm)],
        [d_w_f[:, :H].T],
        [widen(d_b_f)],
        [d_b_norm[j] for j in range(n_b)],
        [widen(d_q_norm[j]) for j in range(n_b)],
        [d_ffn_norm[l] for l in range(depth)],
    ]
    pack = jnp.concatenate([rows8(p) for g in groups for p in g], axis=0)
    red = _all_reduce_small(pack, "ar_small")
    taken, off = [], 0
    for g in groups:
        r, rp = g[0].shape[0], rows8(g[0]).shape[0]
        taken.append(red[off:off + len(g) * rp].reshape(len(g), rp, D)[:, :r].reshape(len(g) * r, D))
        off += len(g) * rp
    take = lambda i: taken[i]
    mine = lambda a: lax.dynamic_slice_in_dim(a, q_me * Ds, Ds, axis=1)
    layers_of = lambda name, n: jnp.stack([shards[(name, l)] for l in range(n)])
    grads = {
        "meta": mine(take(0)),
        "a_norm": mine(take(1)),
        "a_w_in": layers_of("a_w_in", n_a),
        "a_conv": mine(take(2)).reshape(n_a, 3, Ds),
        "a_w_out": layers_of("a_w_out", n_a),
        "kv_norm": take(3).reshape(D),
        "w_kv": shards[("w_kv", 0)],
        "k_norm": take(4)[0, :HEAD_DIM],
        "w_f": mine(take(5)).T,
        "b_f": take(6)[0, :H],
        "b_norm": take(7),
        "b_w_q": layers_of("b_w_q", n_b),
        "b_q_norm": take(8)[:, :HEAD_DIM],
        "b_w_o": layers_of("b_w_o", n_b),
        "ffn_norm": take(9),
        "ffn_w_gu": layers_of("ffn_w_gu", depth),
        "ffn_w_down": layers_of("ffn_w_down", depth),
    }
    weights = dict(meta=meta, a_norm=a_norm, a_w_in=a_w_in, a_conv=a_conv, a_w_out=a_w_out, kv_norm=kv_norm, w_kv=w_kv,
                   k_norm=k_norm, w_f=w_f, b_f=b_f, b_norm=b_norm, b_w_q=b_w_q, b_q_norm=b_q_norm, b_w_o=b_w_o,
                   ffn_norm=ffn_norm, ffn_w_gu=ffn_w_gu, ffn_w_down=ffn_w_down)
    m_in = dict(meta=m_meta, a_norm=m_a_norm, a_w_in=m_a_w_in, a_conv=m_a_conv, a_w_out=m_a_w_out, kv_norm=m_kv_norm,
                w_kv=m_w_kv, k_norm=m_k_norm, w_f=m_w_f, b_f=m_b_f, b_norm=m_b_norm, b_w_q=m_b_w_q,
                b_q_norm=m_b_q_norm, b_w_o=m_b_w_o, ffn_norm=m_ffn_norm, ffn_w_gu=m_ffn_w_gu, ffn_w_down=m_ffn_w_down)
    v_in = dict(meta=v_meta, a_norm=v_a_norm, a_w_in=v_a_w_in, a_conv=v_a_conv, a_w_out=v_a_w_out, kv_norm=v_kv_norm,
                w_kv=v_w_kv, k_norm=v_k_norm, w_f=v_w_f, b_f=v_b_f, b_norm=v_b_norm, b_w_q=v_b_w_q,
                b_q_norm=v_b_q_norm, b_w_o=v_b_w_o, ffn_norm=v_ffn_norm, ffn_w_gu=v_ffn_w_gu, ffn_w_down=v_ffn_w_down)

    deltas, new_m, new_v = {}, {}, {}
    for name, w in weights.items():
        shape = w.shape
        two_d = (1, shape[0]) if w.ndim == 1 else (math.prod(shape[:-1]), shape[-1])
        r2 = lambda a: a.reshape(two_d)
        d_, m_, v_ = _adamw(r2(w), r2(grads[name]), r2(m_in[name]), r2(v_in[name]), f"adamw_{name}")
        deltas[name], new_m[name], new_v[name] = d_.reshape(shape), m_.reshape(shape), v_.reshape(shape)
        grads[name] = grads[name].reshape(shape)

    names = list(weights)
    return (loss, grad_x, *[grads[n] for n in names], *[deltas[n] for n in names],
            *[new_m[n] for n in names], *[new_v[n] for n in names])
```

```python
import functools
import math

import jax
import jax.numpy as jnp
from jax import lax
from jax.experimental import pallas as pl
from jax.experimental.pallas import tpu as pltpu

F32 = jnp.float32
BF16 = jnp.bfloat16
HEAD_DIM = 128
BLOCK = 128
LANES = 128
EPS = 1e-6
NEG = -1e30
ADAM_LR, ADAM_B1, ADAM_B2, ADAM_EPS, ADAM_WD, ADAM_STEP = 0.001, 0.9, 0.999, 1e-08, 0.01, 10
VMEM_LIMIT = 56 * 1024 * 1024
TILE_BUDGET = 40 * 1024 * 1024
MESH = pl.DeviceIdType.MESH
N_CHIPS = 4


def _tile(n, target, align):
    best = None
    for d in range(align, min(n, target) + 1, align):
        if n % d == 0:
            best = d
    return best if best is not None else n


def _cp(sem):
    return pltpu.CompilerParams(dimension_semantics=sem, vmem_limit_bytes=VMEM_LIMIT)


def _matmul(a, b, *, mode, out_dtype, name, res=None, out_parts=1, dep=None):
    a_parts = a.shape[0] if a.ndim == 3 else 1
    b_parts = b.shape[0] if b.ndim == 3 else 1
    if mode == "nn":
        M, Kp = a.shape[-2:]
        K, N = Kp * a_parts, b.shape[1]
    elif mode == "nt":
        M, Kp = a.shape[-2:]
        K, N = Kp * a_parts, b.shape[0]
    else:
        K, M = a.shape
        Kp = K
        N = b.shape[-1] * b_parts
    Np = N // max(b_parts, out_parts)
    tm = _tile(M, 1024, LANES) if mode == "tn" else _tile(M, 1056, 16)
    tn = _tile(Np, 1536, LANES)
    tk = _tile(Kp, 1056, 16) if mode == "tn" else _tile(Kp, 2048, LANES)
    ab, bb, ob = a.dtype.itemsize, b.dtype.itemsize, jnp.dtype(out_dtype).itemsize

    def vmem(tm_):
        blocks = 2 * (tm_ * tk * ab + tk * tn * bb + tm_ * tn * ob + (tm_ * tn * 4 if res is not None else 0))
        temps = tm_ * tn * 8 + (tm_ * tk * 2 if ab == 4 else 0) + (tk * tn * 2 if bb == 4 else 0)
        return blocks + temps

    while vmem(tm) > TILE_BUDGET and tm > 256:
        tm = _tile(M, tm // 2, LANES if mode == "tn" else 16)
    ni, nj, nk = M // tm, N // tn, K // tk
    nkp, njp = Kp // tk, Np // tn

    if mode == "tn":
        a_spec = pl.BlockSpec((tk, tm), lambda i, j, k: (k, i))
    elif a_parts > 1:
        a_spec = pl.BlockSpec((None, tm, tk), lambda i, j, k: (k // nkp, i, k % nkp))
    else:
        a_spec = pl.BlockSpec((tm, tk), lambda i, j, k: (i, k))
    if mode == "nn":
        b_spec = pl.BlockSpec((tk, tn), lambda i, j, k: (k, j))
    elif mode == "nt":
        b_spec = pl.BlockSpec((tn, tk), lambda i, j, k: (j, k))
    elif b_parts > 1:
        b_spec = pl.BlockSpec((None, tk, tn), lambda i, j, k: (j // njp, k, j % njp))
    else:
        b_spec = pl.BlockSpec((tk, tn), lambda i, j, k: (k, j))
    in_specs = [a_spec, b_spec]
    operands = [a, b]
    if res is not None:
        in_specs.append(pl.BlockSpec((tm, tn), lambda i, j, k: (i, j)))
        operands.append(res)
    if dep is not None:
        in_specs.append(pl.BlockSpec((8, LANES), lambda i, j, k: (0, 0)))
        operands.append(dep)
    n_in = len(operands)
    if out_parts > 1:
        out_spec = pl.BlockSpec((None, tm, tn), lambda i, j, k: (j // njp, i, j % njp))
        out_shape = jax.ShapeDtypeStruct((out_parts, M, Np), out_dtype)
    else:
        out_spec = pl.BlockSpec((tm, tn), lambda i, j, k: (i, j))
        out_shape = jax.ShapeDtypeStruct((M, N), out_dtype)
    dims = {"nn": (((1,), (0,)), ((), ())), "nt": (((1,), (1,)), ((), ())), "tn": (((0,), (0,)), ((), ()))}[mode]
    has_res = res is not None

    def body(*refs):
        a_ref, b_ref = refs[0], refs[1]
        res_ref = refs[2] if has_res else None
        o_ref = refs[n_in]
        d = lax.dot_general(a_ref[...].astype(BF16), b_ref[...].astype(BF16), dims, preferred_element_type=F32)
        if nk == 1:
            if has_res:
                d = d + res_ref[...]
            o_ref[...] = d.astype(out_dtype)
        else:
            acc_ref = refs[-1]
            k = pl.program_id(2)

            @pl.when(k == 0)
            def _():
                acc_ref[...] = d

            @pl.when(k > 0)
            def _():
                acc_ref[...] += d

            @pl.when(k == nk - 1)
            def _():
                r = acc_ref[...]
                if has_res:
                    r = r + res_ref[...]
                o_ref[...] = r.astype(out_dtype)

    return pl.pallas_call(
        body, name=name, grid=(ni, nj, nk), in_specs=in_specs, out_specs=out_spec, out_shape=out_shape,
        scratch_shapes=[pltpu.VMEM((tm, tn), F32)] if nk > 1 else [],
        compiler_params=_cp(("parallel", "parallel", "arbitrary")),
    )(*operands)


def _cast_into_full(w3, layer, axis, q_idx, name):
    _, R, C = w3.shape
    tr = _tile(R, max(16, (4 * 1024 * 1024) // (C * 4)), 16)
    nb = R // tr

    def body(q_ref, w_ref, o_ref):
        o_ref[...] = w_ref[...].astype(BF16)

    if axis == 0:
        out_spec = pl.BlockSpec((tr, C), lambda i, q_ref: (q_ref[0] * nb + i, 0))
        full = (N_CHIPS * R, C)
    else:
        out_spec = pl.BlockSpec((tr, C), lambda i, q_ref: (i, q_ref[0]))
        full = (R, N_CHIPS * C)
    grid_spec = pltpu.PrefetchScalarGridSpec(
        num_scalar_prefetch=1, grid=(nb,),
        in_specs=[pl.BlockSpec((None, tr, C), lambda i, q_ref: (layer, i, 0))], out_specs=out_spec)
    return pl.pallas_call(
        body, name=name, grid_spec=grid_spec, out_shape=jax.ShapeDtypeStruct(full, BF16),
        compiler_params=_cp(("parallel",)),
    )(q_idx, w3)


def _rms_fwd(h, g, name, dep=None):
    T, D = h.shape
    tr = _tile(T, 528, 16)

    def body(h_ref, g_ref, *rest):
        o_ref = rest[-1]
        x = h_ref[...]
        r = lax.rsqrt(jnp.mean(x * x, axis=-1, keepdims=True) + EPS)
        o_ref[...] = (x * r * g_ref[...]).astype(BF16)

    in_specs = [pl.BlockSpec((tr, D), lambda i: (i, 0)), pl.BlockSpec((1, D), lambda i: (0, 0))]
    operands = [h, g]
    if dep is not None:
        in_specs.append(pl.BlockSpec((8, LANES), lambda i: (0, 0)))
        operands.append(dep)
    return pl.pallas_call(
        body, name=name, grid=(T // tr,), in_specs=in_specs,
        out_specs=pl.BlockSpec((tr, D), lambda i: (i, 0)),
        out_shape=jax.ShapeDtypeStruct((T, D), BF16), compiler_params=_cp(("parallel",)),
    )(*operands)


def _rms_bwd(h, g, dxn, dh, name):
    T, D = h.shape
    tr = _tile(T, 264, 8)

    def body(h_ref, g_ref, dxn_ref, dh_ref, o_ref, dg_ref):
        x = h_ref[...]
        r = lax.rsqrt(jnp.mean(x * x, axis=-1, keepdims=True) + EPS)
        xh = x * r
        dy = dxn_ref[...]
        dxh = dy * g_ref[...]
        dx = r * (dxh - xh * jnp.mean(dxh * xh, axis=-1, keepdims=True))
        o_ref[...] = dh_ref[...] + dx
        part = jnp.sum(dy * xh, axis=0, keepdims=True)

        @pl.when(pl.program_id(0) == 0)
        def _():
            dg_ref[...] = part

        @pl.when(pl.program_id(0) > 0)
        def _():
            dg_ref[...] += part

    row = pl.BlockSpec((tr, D), lambda i: (i, 0))
    vec = pl.BlockSpec((1, D), lambda i: (0, 0))
    return pl.pallas_call(
        body, name=name, grid=(T // tr,), in_specs=[row, vec, row, row], out_specs=[row, vec],
        out_shape=[jax.ShapeDtypeStruct((T, D), F32), jax.ShapeDtypeStruct((1, D), F32)],
        compiler_params=_cp(("arbitrary",)),
    )(h, g, dxn, dh)


def _shift_down(u, n, rows):
    return jnp.where(rows >= n, pltpu.roll(u, n, 0), 0.0)


def _shift_up(u, n, rows, total):
    return jnp.where(rows < total - n, pltpu.roll(u, total - n, 0), 0.0)


def _gate_fwd(z, conv_w, name):
    _, T, D = z.shape
    tc = LANES

    def body(b_ref, c_ref, h_ref, w_ref, y_ref):
        rows = lax.broadcasted_iota(jnp.int32, (T, tc), 0)
        u = c_ref[...].astype(F32) * h_ref[...].astype(F32)
        w0, w1, w2 = w_ref[0:1, :], w_ref[1:2, :], w_ref[2:3, :]
        conv = u * w2 + _shift_down(u, 1, rows) * w1 + _shift_down(u, 2, rows) * w0
        y_ref[...] = (b_ref[...].astype(F32) * conv).astype(BF16)

    part = lambda p: pl.BlockSpec((None, T, tc), lambda j, p=p: (p, 0, j))
    return pl.pallas_call(
        body, name=name, grid=(D // tc,),
        in_specs=[part(0), part(1), part(2), pl.BlockSpec((3, tc), lambda j: (0, j))],
        out_specs=pl.BlockSpec((T, tc), lambda j: (0, j)),
        out_shape=jax.ShapeDtypeStruct((T, D), BF16), compiler_params=_cp(("parallel",)),
    )(z, z, z, conv_w)


def _gate_bwd(z, conv_w, dy, name):
    _, T, D = z.shape
    tc = LANES

    def body(b_ref, c_ref, h_ref, w_ref, dy_ref, dz_ref, dw_ref):
        rows = lax.broadcasted_iota(jnp.int32, (T, tc), 0)
        cg, hh = c_ref[...].astype(F32), h_ref[...].astype(F32)
        u = cg * hh
        w0, w1, w2 = w_ref[0:1, :], w_ref[1:2, :], w_ref[2:3, :]
        s1, s2 = _shift_down(u, 1, rows), _shift_down(u, 2, rows)
        g = dy_ref[...]
        dz_ref[0] = (g * (u * w2 + s1 * w1 + s2 * w0)).astype(BF16)
        dconv = g * b_ref[...].astype(F32)
        dw_ref[0:1, :] = jnp.sum(dconv * s2, axis=0, keepdims=True)
        dw_ref[1:2, :] = jnp.sum(dconv * s1, axis=0, keepdims=True)
        dw_ref[2:3, :] = jnp.sum(dconv * u, axis=0, keepdims=True)
        du = dconv * w2 + _shift_up(dconv, 1, rows, T) * w1 + _shift_up(dconv, 2, rows, T) * w0
        dz_ref[1] = (du * hh).astype(BF16)
        dz_ref[2] = (du * cg).astype(BF16)

    part = lambda p: pl.BlockSpec((None, T, tc), lambda j, p=p: (p, 0, j))
    return pl.pallas_call(
        body, name=name, grid=(D // tc,),
        in_specs=[part(0), part(1), part(2), pl.BlockSpec((3, tc), lambda j: (0, j)),
                  pl.BlockSpec((T, tc), lambda j: (0, j))],
        out_specs=[pl.BlockSpec((3, T, tc), lambda j: (0, 0, j)), pl.BlockSpec((3, tc), lambda j: (0, j))],
        out_shape=[jax.ShapeDtypeStruct((3, T, D), BF16), jax.ShapeDtypeStruct((3, D), F32)],
        compiler_params=_cp(("parallel",)),
    )(z, z, z, conv_w, dy)


def _swiglu_fwd(z, name):
    _, T, Fd = z.shape
    tr, tc = _tile(T, 528, 16), _tile(Fd, 512, LANES)

    def body(g_ref, u_ref, o_ref):
        g = g_ref[...].astype(F32)
        o_ref[...] = (g * jax.nn.sigmoid(g) * u_ref[...].astype(F32)).astype(BF16)

    part = lambda p: pl.BlockSpec((None, tr, tc), lambda i, j, p=p: (p, i, j))
    return pl.pallas_call(
        body, name=name, grid=(T // tr, Fd // tc), in_specs=[part(0), part(1)],
        out_specs=pl.BlockSpec((tr, tc), lambda i, j: (i, j)),
        out_shape=jax.ShapeDtypeStruct((T, Fd), BF16), compiler_params=_cp(("parallel", "parallel")),
    )(z, z)


def _swiglu_bwd(z, da, name):
    _, T, Fd = z.shape
    tr, tc = _tile(T, 528, 16), _tile(Fd, 512, LANES)

    def body(g_ref, u_ref, da_ref, dz_ref):
        g, d = g_ref[...].astype(F32), da_ref[...].astype(F32)
        s = jax.nn.sigmoid(g)
        dz_ref[0] = (d * u_ref[...].astype(F32) * (s * (1.0 + g * (1.0 - s)))).astype(BF16)
        dz_ref[1] = (d * (g * s)).astype(BF16)

    part = lambda p: pl.BlockSpec((None, tr, tc), lambda i, j, p=p: (p, i, j))
    return pl.pallas_call(
        body, name=name, grid=(T // tr, Fd // tc),
        in_specs=[part(0), part(1), pl.BlockSpec((tr, tc), lambda i, j: (i, j))],
        out_specs=pl.BlockSpec((2, tr, tc), lambda i, j: (0, i, j)),
        out_shape=jax.ShapeDtypeStruct((2, T, Fd), BF16), compiler_params=_cp(("parallel", "parallel")),
    )(z, z, da)


def _headnorm_fwd(z, part, g, name):
    _, T, D = z.shape
    tr = _tile(T, 1056, 16)

    def body(z_ref, g_ref, o_ref):
        x = z_ref[...]
        r = lax.rsqrt(jnp.mean(x * x, axis=-1, keepdims=True) + EPS)
        o_ref[...] = (x * r * g_ref[...]).astype(BF16)

    return pl.pallas_call(
        body, name=name, grid=(T // tr, D // HEAD_DIM),
        in_specs=[pl.BlockSpec((None, tr, HEAD_DIM), lambda i, h: (part, i, h)),
                  pl.BlockSpec((1, HEAD_DIM), lambda i, h: (0, 0))],
        out_specs=pl.BlockSpec((tr, HEAD_DIM), lambda i, h: (i, h)),
        out_shape=jax.ShapeDtypeStruct((T, D), BF16), compiler_params=_cp(("parallel", "parallel")),
    )(z, g)


def _headnorm_bwd(z, part, g, dy, name):
    _, T, D = z.shape
    tr = _tile(T, 1056, 16)

    def body(z_ref, g_ref, dy_ref, dz_ref, dg_ref):
        x = z_ref[...]
        r = lax.rsqrt(jnp.mean(x * x, axis=-1, keepdims=True) + EPS)
        xh = x * r
        dy_ = dy_ref[...]
        dxh = dy_ * g_ref[...]
        dz_ref[...] = (r * (dxh - xh * jnp.mean(dxh * xh, axis=-1, keepdims=True))).astype(BF16)
        partial = jnp.sum(dy_ * xh, axis=0, keepdims=True)
        first = (pl.program_id(0) == 0) & (pl.program_id(1) == 0)

        @pl.when(first)
        def _():
            dg_ref[...] = partial

        @pl.when(jnp.logical_not(first))
        def _():
            dg_ref[...] += partial

    blk = pl.BlockSpec((tr, HEAD_DIM), lambda i, h: (i, h))
    vec = pl.BlockSpec((1, HEAD_DIM), lambda i, h: (0, 0))
    return pl.pallas_call(
        body, name=name, grid=(T // tr, D // HEAD_DIM),
        in_specs=[pl.BlockSpec((None, tr, HEAD_DIM), lambda i, h: (part, i, h)), vec, blk],
        out_specs=[blk, vec],
        out_shape=[jax.ShapeDtypeStruct((T, D), BF16), jax.ShapeDtypeStruct((1, HEAD_DIM), F32)],
        compiler_params=_cp(("arbitrary", "arbitrary")),
    )(z, g, dy)


def _cast_part(z, part, name):
    _, T, D = z.shape
    tr = _tile(T, 528, 16)

    def body(z_ref, o_ref):
        o_ref[...] = z_ref[...].astype(BF16)

    return pl.pallas_call(
        body, name=name, grid=(T // tr,),
        in_specs=[pl.BlockSpec((None, tr, D), lambda i: (part, i, 0))],
        out_specs=pl.BlockSpec((tr, D), lambda i: (i, 0)),
        out_shape=jax.ShapeDtypeStruct((T, D), BF16), compiler_params=_cp(("parallel",)),
    )(z)


def _split3(x):
    a = x.astype(BF16)
    r = x - a.astype(F32)
    b = r.astype(BF16)
    c = (r - b.astype(F32)).astype(BF16)
    return a, b, c


def _tri_matmul(tri, x):
    a, b, c = _split3(x)
    dot = lambda v: jnp.dot(tri, v, preferred_element_type=F32)
    return (dot(c) + dot(b)) + dot(a)


def _logf_cumsum(pre, bias, pad, name):
    T = pre.shape[0]
    nb = T // BLOCK

    def body(p_ref, b_ref, c_ref, carry):
        i = pl.program_id(0)

        @pl.when(i == 0)
        def _():
            carry[...] = jnp.zeros_like(carry)

        x = p_ref[...] + b_ref[...]
        lf = jnp.minimum(x, 0.0) - jnp.log(1.0 + jnp.exp(-jnp.abs(x)))
        rows = i * BLOCK + lax.broadcasted_iota(jnp.int32, (BLOCK, LANES), 0)
        lf = jnp.where(rows >= pad, lf, 0.0)
        r = lax.broadcasted_iota(jnp.int32, (BLOCK, BLOCK), 0)
        c = lax.broadcasted_iota(jnp.int32, (BLOCK, BLOCK), 1)
        tri = jnp.where(c <= r, 1.0, 0.0).astype(BF16)
        c_ref[...] = _tri_matmul(tri, lf) + carry[...]
        carry[...] = c_ref[BLOCK - 1:BLOCK, :]

    return pl.pallas_call(
        body, name=name, grid=(nb,),
        in_specs=[pl.BlockSpec((BLOCK, LANES), lambda i: (i, 0)), pl.BlockSpec((1, LANES), lambda i: (0, 0))],
        out_specs=pl.BlockSpec((BLOCK, LANES), lambda i: (i, 0)),
        out_shape=jax.ShapeDtypeStruct((T, LANES), F32),
        scratch_shapes=[pltpu.VMEM((1, LANES), F32)], compiler_params=_cp(("arbitrary",)),
    )(pre, bias)


def _logf_bwd(pre, bias, dc, pad, name):
    T = pre.shape[0]
    nb = T // BLOCK

    def body(p_ref, b_ref, dc_ref, dp_ref, db_ref, carry, dlf_ref):
        i = pl.program_id(0)

        @pl.when(i == 0)
        def _():
            carry[...] = jnp.zeros_like(carry)

        r = lax.broadcasted_iota(jnp.int32, (BLOCK, BLOCK), 0)
        c = lax.broadcasted_iota(jnp.int32, (BLOCK, BLOCK), 1)
        tri = jnp.where(c >= r, 1.0, 0.0).astype(BF16)
        dlf_ref[...] = _tri_matmul(tri, dc_ref[...]) + carry[...]
        carry[...] = dlf_ref[0:1, :]
        dlf = dlf_ref[...]
        x = p_ref[...] + b_ref[...]
        rows = (nb - 1 - i) * BLOCK + lax.broadcasted_iota(jnp.int32, (BLOCK, LANES), 0)
        dpre = jnp.where(rows >= pad, dlf * jax.nn.sigmoid(-x), 0.0)
        dp_ref[...] = dpre
        partial = jnp.sum(dpre, axis=0, keepdims=True)

        @pl.when(i == 0)
        def _():
            db_ref[...] = partial

        @pl.when(i > 0)
        def _():
            db_ref[...] += partial

    rev = pl.BlockSpec((BLOCK, LANES), lambda i: (nb - 1 - i, 0))
    vec = pl.BlockSpec((1, LANES), lambda i: (0, 0))
    return pl.pallas_call(
        body, name=name, grid=(nb,), in_specs=[rev, vec, rev], out_specs=[rev, vec],
        out_shape=[jax.ShapeDtypeStruct((T, LANES), F32), jax.ShapeDtypeStruct((1, LANES), F32)],
        scratch_shapes=[pltpu.VMEM((1, LANES), F32), pltpu.VMEM((BLOCK, LANES), F32)],
        compiler_params=_cp(("arbitrary",)),
    )(pre, bias, dc)


def _loss_head(h, target, first, name):
    T, D = h.shape
    tr = BLOCK
    skip = first // tr

    def body(h_ref, t_ref, dh_ref, loss_ref):
        i = pl.program_id(0)

        @pl.when(i == 0)
        def _():
            loss_ref[...] = jnp.zeros_like(loss_ref)

        @pl.when(i < skip)
        def _():
            dh_ref[...] = jnp.zeros_like(dh_ref)

        @pl.when(i >= skip)
        def _():
            err = h_ref[...] - t_ref[...]
            dh_ref[...] = err * (1.0 / D)
            loss_ref[...] += jnp.sum(err * err) * (0.5 / D)

    row = pl.BlockSpec((tr, D), lambda i: (i, 0))
    return pl.pallas_call(
        body, name=name, grid=(T // tr,),
        in_specs=[row, pl.BlockSpec((tr, D), lambda i: (jnp.maximum(i - skip, 0), 0))],
        out_specs=[row, pl.BlockSpec((8, LANES), lambda i: (0, 0))],
        out_shape=[jax.ShapeDtypeStruct((T, D), F32), jax.ShapeDtypeStruct((8, LANES), F32)],
        compiler_params=_cp(("arbitrary",)),
    )(h, target)


def _adamw(w, g, m, v, name):
    R, C = w.shape
    tr = _tile(R, max(8, TILE_BUDGET // (C * 4 * 7 * 3)), 8)
    bc1, bc2 = 1.0 - ADAM_B1 ** ADAM_STEP, 1.0 - ADAM_B2 ** ADAM_STEP

    def body(w_ref, g_ref, m_ref, v_ref, d_ref, mo_ref, vo_ref):
        g_ = g_ref[...]
        m_ = ADAM_B1 * m_ref[...] + (1.0 - ADAM_B1) * g_
        v_ = ADAM_B2 * v_ref[...] + (1.0 - ADAM_B2) * (g_ * g_)
        d_ref[...] = -ADAM_LR * ((m_ / bc1) / (jnp.sqrt(v_ / bc2) + ADAM_EPS) + ADAM_WD * w_ref[...])
        mo_ref[...] = m_
        vo_ref[...] = v_

    blk = pl.BlockSpec((tr, C), lambda i: (i, 0))
    sds = jax.ShapeDtypeStruct((R, C), F32)
    return pl.pallas_call(
        body, name=name, grid=(R // tr,), in_specs=[blk] * 4, out_specs=[blk] * 3, out_shape=[sds] * 3,
        compiler_params=_cp(("parallel",)),
    )(w, g, m, v)


def _pick_head(c_blk, h):
    lane = lax.broadcasted_iota(jnp.int32, c_blk.shape, 1)
    return jnp.sum(jnp.where(lane == h, c_blk, 0.0), axis=1, keepdims=True)


def _attn_fwd(q, k, v, c, ct, *, blk, pad, name):
    T, D = q.shape
    H, nq = D // HEAD_DIM, T // blk
    scale = 1.0 / math.sqrt(HEAD_DIM)

    def body(q_ref, k_ref, v_ref, c_ref, ct_ref, o_ref, lse_ref):
        h, i = pl.program_id(0), pl.program_id(1)
        qb = q_ref[...]
        cq = _pick_head(c_ref[...], h)

        def step_fn(masked):
            def step(j, carry):
                m, l, acc = carry
                off = j * blk if isinstance(j, int) else pl.multiple_of(j * blk, blk)
                kb = k_ref[pl.ds(off, blk), :]
                vb = v_ref[pl.ds(off, blk), :]
                s = lax.dot_general(qb, kb, (((1,), (1,)), ((), ())), preferred_element_type=F32) * scale
                s = s + (cq - ct_ref[j])
                if masked:
                    qpos = i * blk + lax.broadcasted_iota(jnp.int32, (blk, blk), 0)
                    kpos = j * blk + lax.broadcasted_iota(jnp.int32, (blk, blk), 1)
                    s = jnp.where((kpos <= qpos) & (kpos >= pad), s, NEG)
                m_new = jnp.maximum(m, jnp.max(s, axis=1, keepdims=True))
                p = jnp.exp(s - m_new)
                alpha = jnp.exp(m - m_new)
                l = alpha * l + jnp.sum(p, axis=1, keepdims=True)
                acc = alpha * acc + jnp.dot(p.astype(BF16), vb, preferred_element_type=F32)
                return m_new, l, acc
            return step

        carry = (jnp.full((blk, 1), NEG, F32), jnp.zeros((blk, 1), F32), jnp.zeros((blk, HEAD_DIM), F32))
        carry = step_fn(True)(0, carry)
        carry = lax.fori_loop(1, i, step_fn(False), carry)
        m, l, acc = lax.cond(i > 0, lambda c: step_fn(True)(i, c), lambda c: c, carry)
        rowpos = i * blk + lax.broadcasted_iota(jnp.int32, (blk, 1), 0)
        o_ref[...] = jnp.where(rowpos >= pad, acc / l, 0.0)
        lse_ref[...] = jnp.broadcast_to(m + jnp.log(l), (blk, LANES))

    return pl.pallas_call(
        body, name=name, grid=(H, nq),
        in_specs=[pl.BlockSpec((blk, HEAD_DIM), lambda h, i: (i, h)),
                  pl.BlockSpec((T, HEAD_DIM), lambda h, i: (0, h)),
                  pl.BlockSpec((T, HEAD_DIM), lambda h, i: (0, h)),
                  pl.BlockSpec((blk, LANES), lambda h, i: (i, 0)),
                  pl.BlockSpec((None, nq, 1, blk), lambda h, i: (h, 0, 0, 0))],
        out_specs=[pl.BlockSpec((blk, HEAD_DIM), lambda h, i: (i, h)),
                   pl.BlockSpec((None, blk, LANES), lambda h, i: (h, i, 0))],
        out_shape=[jax.ShapeDtypeStruct((T, D), F32), jax.ShapeDtypeStruct((H, T, LANES), F32)],
        compiler_params=_cp(("parallel", "arbitrary")),
    )(q, k, v, c, ct)


def _attn_bwd(q, k, v, o, do, lse, c, ct, prev, *, blk, pad, name):
    T, D = q.shape
    H, nq = D // HEAD_DIM, T // blk
    scale = 1.0 / math.sqrt(HEAD_DIM)
    has_prev = prev is not None

    nt_dims = (((1,), (1,)), ((), ()))
    tn_dims = (((0,), (0,)), ((), ()))

    def body(*refs):
        q_ref, k_ref, v_ref, o_ref, do_ref, lse_ref, c_ref, ct_ref = refs[:8]
        pdk_ref, pdv_ref, pdc_ref = refs[8:11] if has_prev else (None, None, None)
        dq_ref, dk_ref, dv_ref, dct_ref = refs[-4:]
        h, j = pl.program_id(0), pl.program_id(1)

        @pl.when(j == 0)
        def _():
            dq_ref[...] = jnp.zeros_like(dq_ref)

        kb, vb = k_ref[...], v_ref[...]
        ck = ct_ref[...]

        def step_fn(masked):
            def step(i, carry):
                dk, dv, dck = carry
                off = pl.multiple_of(i * blk, blk)
                qb = q_ref[pl.ds(off, blk), :]
                dob = do_ref[pl.ds(off, blk), :]
                lse_i = lse_ref[pl.ds(off, blk), :][:, 0:1]
                cq = _pick_head(c_ref[pl.ds(off, blk), :], h)
                delta = jnp.sum(dob.astype(F32) * o_ref[pl.ds(off, blk), :], axis=1, keepdims=True)
                s = lax.dot_general(qb, kb, nt_dims, preferred_element_type=F32) * scale
                p = jnp.exp(s + (cq - ck) - lse_i)
                if masked:
                    qpos = i * blk + lax.broadcasted_iota(jnp.int32, (blk, blk), 0)
                    kpos = j * blk + lax.broadcasted_iota(jnp.int32, (blk, blk), 1)
                    p = jnp.where((kpos <= qpos) & (kpos >= pad), p, 0.0)
                dp = lax.dot_general(dob, vb, nt_dims, preferred_element_type=F32)
                ds = p * (dp - delta)
                pb, dsb = p.astype(BF16), ds.astype(BF16)
                dv = dv + lax.dot_general(pb, dob, tn_dims, preferred_element_type=F32)
                dk = dk + lax.dot_general(dsb, qb, tn_dims, preferred_element_type=F32)
                dck = dck - jnp.sum(ds, axis=0, keepdims=True)
                dq_ref[pl.ds(off, blk), :] += jnp.dot(dsb, kb, preferred_element_type=F32) * scale
                return dk, dv, dck
            return step

        carry = (jnp.zeros((blk, HEAD_DIM), F32), jnp.zeros((blk, HEAD_DIM), F32), jnp.zeros((1, blk), F32))
        carry = step_fn(True)(j, carry)
        below = lambda masked: (lambda c: lax.fori_loop(j + 1, nq, step_fn(masked), c))
        dk, dv, dck = lax.cond(j == 0, below(True), below(False), carry)
        dk = dk * scale
        if has_prev:
            dk, dv, dck = dk + pdk_ref[...], dv + pdv_ref[...], dck + pdc_ref[...]
        dk_ref[...] = dk
        dv_ref[...] = dv
        dct_ref[...] = dck

    col = pl.BlockSpec((T, HEAD_DIM), lambda h, j: (0, h))
    kblk = pl.BlockSpec((blk, HEAD_DIM), lambda h, j: (j, h))
    ctb = pl.BlockSpec((None, None, 1, blk), lambda h, j: (h, j, 0, 0))
    in_specs = [col, kblk, kblk, col, col,
                pl.BlockSpec((None, T, LANES), lambda h, j: (h, 0, 0)),
                pl.BlockSpec((T, LANES), lambda h, j: (0, 0)), ctb]
    operands = [q, k, v, o, do, lse, c, ct]
    if has_prev:
        in_specs += [kblk, kblk, ctb]
        operands += list(prev)
    return pl.pallas_call(
        body, name=name, grid=(H, nq), in_specs=in_specs, out_specs=[col, kblk, kblk, ctb],
        out_shape=[jax.ShapeDtypeStruct((T, D), F32), jax.ShapeDtypeStruct((T, D), F32),
                   jax.ShapeDtypeStruct((T, D), F32), jax.ShapeDtypeStruct((H, nq, 1, blk), F32)],
        compiler_params=_cp(("parallel", "arbitrary")),
    )(*operands)


HBM_SPEC = pl.BlockSpec(memory_space=pltpu.HBM)


def _place():
    x, y, c = lax.axis_index("x"), lax.axis_index("y"), lax.axis_index("c")
    chips = [(1 - x, y), (x, 1 - y), (1 - x, 1 - y)]
    return x, y, c, chips


def _remote(src, dst, send_sems, recv_sems, k, to):
    return pltpu.make_async_remote_copy(src_ref=src, dst_ref=dst, send_sem=send_sems.at[k],
                                        recv_sem=recv_sems.at[k], device_id=to, device_id_type=MESH)


def _all_gather(shards, specs, name):
    n = len(shards)

    def full_shape(s, axis):
        return (s.shape[0] * N_CHIPS, s.shape[1]) if axis == 0 else (s.shape[0], s.shape[1] * N_CHIPS)

    def body(*refs):
        srcs, outs = refs[:n], refs[n:2 * n]
        send_sems, recv_sems, local_sems = refs[2 * n:]
        x, y, c, chips = _place()
        sibling = (x, y, 1 - c)

        def region(t, chip, half):
            rs, cs = shards[t].shape
            q = 2 * chip[0] + chip[1]
            axis, split = specs[t]
            nrow = rs // 2 if half is not None else rs
            r0 = 0 if half is None else half * nrow
            if axis == 0:
                return outs[t].at[pl.ds(q * rs + r0, nrow), :]
            return outs[t].at[pl.ds(r0, nrow), pl.ds(pl.multiple_of(q * cs, cs), cs)]

        def piece(t, half):
            rs = shards[t].shape[0]
            if half is None:
                return srcs[t]
            return srcs[t].at[pl.ds(half * (rs // 2), rs // 2), :]

        local = [pltpu.make_async_copy(srcs[t], region(t, (x, y), None), local_sems.at[t]) for t in range(n)]
        for cp in local:
            cp.start()
        sends = []
        for t in range(n):
            half = c if specs[t][1] else None
            for j, chip in enumerate(chips):
                cp = _remote(piece(t, half), region(t, (x, y), half), send_sems, recv_sems, 6 * t + j, (*chip, c))
                cp.start()
                sends.append(cp)
        for t in range(n):
            half = c if specs[t][1] else None
            for j, chip in enumerate(chips):
                landed = region(t, chip, half)
                _remote(landed, landed, send_sems, recv_sems, 6 * t + j, (*chip, c)).wait_recv()
                if specs[t][1]:
                    cp = _remote(landed, landed, send_sems, recv_sems, 6 * t + 3 + j, sibling)
                    cp.start()
                    sends.append(cp)
        for t in range(n):
            if specs[t][1]:
                for j, chip in enumerate(chips):
                    got = region(t, chip, 1 - c)
                    _remote(got, got, send_sems, recv_sems, 6 * t + 3 + j, sibling).wait_recv()
        for cp in sends:
            cp.wait_send()
        for cp in local:
            cp.wait()

    return pl.pallas_call(
        body, name=name, in_specs=[HBM_SPEC] * n, out_specs=[HBM_SPEC] * n,
        out_shape=[jax.ShapeDtypeStruct(full_shape(s, specs[t][0]), s.dtype) for t, s in enumerate(shards)],
        scratch_shapes=[pltpu.SemaphoreType.DMA((6 * n,)), pltpu.SemaphoreType.DMA((6 * n,)),
                        pltpu.SemaphoreType.DMA((n,))],
        compiler_params=pltpu.CompilerParams(has_side_effects=True),
    )(*shards)


def _grad_view(g, axis):
    R, C = g.shape
    nq = N_CHIPS if axis == 0 else 1
    return g.reshape(nq, 2, R // (2 * nq), C)


def _swap_halves(views, name):
    n = len(views)

    def body(*refs):
        srcs, outs, send_sems, recv_sems = refs[:n], refs[n:2 * n], refs[2 * n], refs[2 * n + 1]
        x, y, c, _ = _place()
        cps = [_remote(srcs[t].at[:, 1 - c], outs[t], send_sems, recv_sems, t, (x, y, 1 - c)) for t in range(n)]
        for cp in cps:
            cp.start()
        for cp in cps:
            cp.wait()

    return pl.pallas_call(
        body, name=name, in_specs=[HBM_SPEC] * n, out_specs=[HBM_SPEC] * n,
        out_shape=[jax.ShapeDtypeStruct((v.shape[0],) + v.shape[2:], v.dtype) for v in views],
        scratch_shapes=[pltpu.SemaphoreType.DMA((n,)), pltpu.SemaphoreType.DMA((n,))],
        compiler_params=pltpu.CompilerParams(has_side_effects=True),
    )(*views)


def _add_half(view, got, c_idx, name):
    nq, _, Rh, C = view.shape
    tr = _tile(Rh, max(16, (2 * 1024 * 1024) // (C * 2)), 16)

    def body(c_ref, a_ref, b_ref, o_ref):
        o_ref[...] = (a_ref[...].astype(F32) + b_ref[...].astype(F32)).astype(BF16)

    grid_spec = pltpu.PrefetchScalarGridSpec(
        num_scalar_prefetch=1, grid=(nq, Rh // tr),
        in_specs=[pl.BlockSpec((None, None, tr, C), lambda q, i, c_ref: (q, c_ref[0], i, 0)),
                  pl.BlockSpec((None, tr, C), lambda q, i, c_ref: (q, i, 0))],
        out_specs=pl.BlockSpec((None, tr, C), lambda q, i, c_ref: (q, i, 0)))
    return pl.pallas_call(
        body, name=name, grid_spec=grid_spec, out_shape=jax.ShapeDtypeStruct((nq, Rh, C), BF16),
        compiler_params=_cp(("parallel", "parallel")),
    )(c_idx, view, got)


SEM_SPEC = pl.BlockSpec(memory_space=pltpu.SEMAPHORE)
ANY_SPEC = pl.BlockSpec(memory_space=pl.ANY)
TOKEN_SPEC = pl.BlockSpec(memory_space=pltpu.VMEM)
TOKEN = jax.ShapeDtypeStruct((8, LANES), F32)
SPLIT_COPY = pltpu.CompilerParams(has_side_effects=pltpu.SideEffectType.DATAFLOW_SIDE_EFFECTING)


def _region(ref, axis, chip, half):
    q = 2 * chip[0] + chip[1]
    if axis == 0:
        rs = ref.shape[0] // N_CHIPS
        return ref.at[pl.ds(q * rs + half * (rs // 2), rs // 2), :]
    rh, cs = ref.shape[0] // 2, ref.shape[1] // N_CHIPS
    return ref.at[pl.ds(half * rh, rh), pl.ds(pl.multiple_of(q * cs, cs), cs)]


def _gather_start(fulls, axes, after, name):
    n = len(fulls)

    def body(*refs):
        ins = refs[:n]
        token = refs[-1]
        send_sems, recv_sems = refs[n + 1], refs[n + 2]
        x, y, c, chips = _place()
        for t in range(n):
            mine = _region(ins[t], axes[t], (x, y), c)
            for j, chip in enumerate(chips):
                _remote(mine, mine, send_sems, recv_sems, 3 * t + j, (*chip, c)).start()
        token[...] = jnp.zeros_like(token)

    sems = pltpu.SemaphoreType.DMA((3 * n,))
    outs = pl.pallas_call(
        body, name=name, in_specs=[HBM_SPEC] * n + [ANY_SPEC],
        out_specs=[SEM_SPEC, SEM_SPEC] + [HBM_SPEC] * n + [TOKEN_SPEC],
        out_shape=[sems, sems] + [jax.ShapeDtypeStruct(f.shape, f.dtype) for f in fulls] + [TOKEN],
        input_output_aliases={t: 2 + t for t in range(n)}, compiler_params=SPLIT_COPY,
    )(*fulls, after)
    return outs[0], outs[1], list(outs[2:2 + n]), outs[-1]


def _gather_wait(send_sems, recv_sems, fulls, axes, after, name):
    n = len(fulls)

    def body(*refs):
        ins = refs[:n]
        send_sems, recv_sems = refs[n], refs[n + 1]
        x, y, c, chips = _place()
        for t in range(n):
            mine = _region(ins[t], axes[t], (x, y), c)
            for j, chip in enumerate(chips):
                _remote(mine, mine, send_sems, recv_sems, 3 * t + j, (*chip, c)).wait_send()
                theirs = _region(ins[t], axes[t], chip, c)
                _remote(theirs, theirs, send_sems, recv_sems, 3 * t + j, (*chip, c)).wait_recv()

    outs = pl.pallas_call(
        body, name=name, in_specs=[HBM_SPEC] * n + [SEM_SPEC, SEM_SPEC, ANY_SPEC], out_specs=[HBM_SPEC] * n,
        out_shape=[jax.ShapeDtypeStruct(f.shape, f.dtype) for f in fulls],
        input_output_aliases={t: t for t in range(n)}, compiler_params=SPLIT_COPY,
    )(*fulls, send_sems, recv_sems, after)
    return list(outs)


def _gather_forward(fulls, axes, name):
    n = len(fulls)

    def body(*refs):
        outs = refs[n:2 * n]
        send_sems, recv_sems = refs[2 * n], refs[2 * n + 1]
        x, y, c, chips = _place()
        sibling = (x, y, 1 - c)
        cps = []
        for t in range(n):
            for j, chip in enumerate(chips):
                landed = _region(outs[t], axes[t], chip, c)
                cps.append(_remote(landed, landed, send_sems, recv_sems, 3 * t + j, sibling))
        for cp in cps:
            cp.start()
        for t in range(n):
            for j, chip in enumerate(chips):
                got = _region(outs[t], axes[t], chip, 1 - c)
                _remote(got, got, send_sems, recv_sems, 3 * t + j, sibling).wait_recv()
        for cp in cps:
            cp.wait_send()

    outs = pl.pallas_call(
        body, name=name, in_specs=[HBM_SPEC] * n, out_specs=[HBM_SPEC] * n,
        out_shape=[jax.ShapeDtypeStruct(f.shape, f.dtype) for f in fulls],
        scratch_shapes=[pltpu.SemaphoreType.DMA((3 * n,)), pltpu.SemaphoreType.DMA((3 * n,))],
        input_output_aliases={t: t for t in range(n)},
        compiler_params=pltpu.CompilerParams(has_side_effects=True),
    )(*fulls)
    return list(outs)


def _shard_cols(s, axis):
    return s.shape[2] if axis == 0 else s.shape[2] // N_CHIPS


def _piece(ref, axis, chip):
    q = 2 * chip[0] + chip[1]
    if axis == 0:
        return ref.at[q]
    cs = ref.shape[2] // N_CHIPS
    return ref.at[0, :, pl.ds(pl.multiple_of(q * cs, cs), cs)]


def _scatter_start(sums, axes, after, name):
    n = len(sums)

    def body(*refs):
        ins = refs[:n]
        send_sems, recv_sems = refs[n + 1], refs[n + 2]
        lands = refs[2 * n + 3:3 * n + 3]
        token = refs[-1]
        x, y, c, chips = _place()
        for t in range(n):
            for j, chip in enumerate(chips):
                _remote(_piece(ins[t], axes[t], chip), lands[t].at[j], send_sems, recv_sems, 3 * t + j, (*chip, c)).start()
        token[...] = jnp.zeros_like(token)

    sems = pltpu.SemaphoreType.DMA((3 * n,))
    land_shapes = [jax.ShapeDtypeStruct((3, s.shape[1], _shard_cols(s, a)), s.dtype) for s, a in zip(sums, axes)]
    outs = pl.pallas_call(
        body, name=name, in_specs=[HBM_SPEC] * n + [ANY_SPEC],
        out_specs=[SEM_SPEC, SEM_SPEC] + [HBM_SPEC] * (2 * n) + [TOKEN_SPEC],
        out_shape=[sems, sems] + [jax.ShapeDtypeStruct(s.shape, s.dtype) for s in sums] + land_shapes + [TOKEN],
        input_output_aliases={t: 2 + t for t in range(n)}, compiler_params=SPLIT_COPY,
    )(*sums, after)
    return outs[0], outs[1], list(outs[2:2 + n]), list(outs[2 + n:2 + 2 * n]), outs[-1]


def _scatter_wait(send_sems, recv_sems, sums, lands, axes, after, name):
    n = len(sums)

    def body(*refs):
        ins, lnd = refs[:n], refs[n:2 * n]
        send_sems, recv_sems = refs[2 * n], refs[2 * n + 1]
        x, y, c, chips = _place()
        for t in range(n):
            for j, chip in enumerate(chips):
                cp = _remote(_piece(ins[t], axes[t], chip), lnd[t].at[j], send_sems, recv_sems, 3 * t + j, (*chip, c))
                cp.wait_send()
                cp.wait_recv()

    outs = pl.pallas_call(
        body, name=name, in_specs=[HBM_SPEC] * (2 * n) + [SEM_SPEC, SEM_SPEC, ANY_SPEC], out_specs=[HBM_SPEC] * (2 * n),
        out_shape=[jax.ShapeDtypeStruct(s.shape, s.dtype) for s in sums + lands],
        input_output_aliases={t: t for t in range(2 * n)}, compiler_params=SPLIT_COPY,
    )(*sums, *lands, send_sems, recv_sems, after)
    return list(outs[:n]), list(outs[n:])


def _sum_chips(own, axis, got, q_idx, c_idx, name):
    _, Rh, cc = got.shape
    tr = _tile(Rh, max(16, (1024 * 1024) // (cc * 2)), 16)

    def body(q_ref, c_ref, a_ref, b0, b1, b2, o_ref):
        f = lambda r: r[...].astype(F32)
        o_ref[...] = ((f(a_ref) + f(b0)) + f(b1)) + f(b2)

    if axis == 0:
        own_spec = pl.BlockSpec((None, tr, cc), lambda i, q, c: (q[0], i, 0))
    else:
        own_spec = pl.BlockSpec((None, tr, cc), lambda i, q, c: (0, i, q[0]))
    slot = lambda j: pl.BlockSpec((None, tr, cc), lambda i, q, c, j=j: (j, i, 0))
    grid_spec = pltpu.PrefetchScalarGridSpec(
        num_scalar_prefetch=2, grid=(Rh // tr,), in_specs=[own_spec, slot(0), slot(1), slot(2)],
        out_specs=pl.BlockSpec((None, tr, cc), lambda i, q, c: (c[0], i, 0)))
    return pl.pallas_call(
        body, name=name, grid_spec=grid_spec, out_shape=jax.ShapeDtypeStruct((2, Rh, cc), F32),
        compiler_params=_cp(("parallel",)),
    )(q_idx, c_idx, own, got, got, got)


def _join_halves(pairs, name):
    n = len(pairs)

    def body(*refs):
        outs = refs[n:2 * n]
        send_sems, recv_sems = refs[2 * n], refs[2 * n + 1]
        x, y, c, _ = _place()
        cps = [_remote(outs[t].at[c], outs[t].at[c], send_sems, recv_sems, t, (x, y, 1 - c)) for t in range(n)]
        for cp in cps:
            cp.start()
        for t in range(n):
            cps[t].wait_send()
            _remote(outs[t].at[1 - c], outs[t].at[1 - c], send_sems, recv_sems, t, (x, y, 1 - c)).wait_recv()

    outs = pl.pallas_call(
        body, name=name, in_specs=[HBM_SPEC] * n, out_specs=[HBM_SPEC] * n,
        out_shape=[jax.ShapeDtypeStruct(p.shape, p.dtype) for p in pairs],
        scratch_shapes=[pltpu.SemaphoreType.DMA((n,)), pltpu.SemaphoreType.DMA((n,))],
        input_output_aliases={t: t for t in range(n)},
        compiler_params=pltpu.CompilerParams(has_side_effects=True),
    )(*pairs)
    return list(outs)


def _reduce_begin(grads, axes, c_idx, after, tag):
    views = [_grad_view(g, a) for g, a in zip(grads, axes)]
    got = _swap_halves(views, f"rs_swap_{tag}")
    sums = [_add_half(v, p, c_idx, f"rs_add_{tag}_{t}") for t, (v, p) in enumerate(zip(views, got))]
    send_sems, recv_sems, sums, lands, token = _scatter_start(sums, axes, after, f"rs_chips_start_{tag}")
    return (send_sems, recv_sems, sums, lands, axes, tag), token


def _reduce_finish(pending, q_idx, c_idx, after):
    send_sems, recv_sems, sums, lands, axes, tag = pending
    sums, lands = _scatter_wait(send_sems, recv_sems, sums, lands, axes, after, f"rs_chips_wait_{tag}")
    pairs = [_sum_chips(s, a, r, q_idx, c_idx, f"rs_sum_{tag}_{t}")
             for t, (s, a, r) in enumerate(zip(sums, axes, lands))]
    joined = _join_halves(pairs, f"rs_join_{tag}")
    return [j.reshape(2 * j.shape[1], j.shape[2]) for j in joined]


def _all_reduce_small(buf, name):
    R, C = buf.shape

    def body(b_ref, o_ref, slots, send_sems, recv_sems):
        x, y, c, _ = _place()
        me = 4 * x + 2 * y + c
        cps = []
        for k in range(1, 8):
            to = (x ^ (k >> 2), y ^ ((k >> 1) & 1), c ^ (k & 1))
            cps.append(pltpu.make_async_remote_copy(
                src_ref=b_ref, dst_ref=slots.at[me], send_sem=send_sems.at[k - 1], recv_sem=recv_sems.at[me],
                device_id=to, device_id_type=MESH))
        for cp in cps:
            cp.start()
        slots[me] = b_ref[...]
        for k in range(1, 8):
            src = me ^ k
            pltpu.make_async_remote_copy(
                src_ref=b_ref, dst_ref=slots.at[src], send_sem=send_sems.at[k - 1], recv_sem=recv_sems.at[src],
                device_id=(x, y, c), device_id_type=MESH).wait_recv()
        for cp in cps:
            cp.wait_send()
        total = slots[0]
        for d in range(1, 8):
            total = total + slots[d]
        o_ref[...] = total

    vm = pl.BlockSpec(memory_space=pltpu.VMEM)
    return pl.pallas_call(
        body, name=name, in_specs=[vm], out_specs=vm, out_shape=jax.ShapeDtypeStruct((R, C), F32),
        scratch_shapes=[pltpu.VMEM((8, R, C), F32), pltpu.SemaphoreType.DMA((7,)), pltpu.SemaphoreType.DMA((8,))],
        compiler_params=pltpu.CompilerParams(has_side_effects=True, vmem_limit_bytes=VMEM_LIMIT),
    )(buf)


def kernel(x, meta, a_norm, a_w_in, a_conv, a_w_out, kv_norm, w_kv, k_norm, w_f, b_f, b_norm, b_w_q, b_q_norm, b_w_o, ffn_norm, ffn_w_gu, ffn_w_down, loss_target, m_meta, m_a_norm, m_a_w_in, m_a_conv, m_a_w_out, m_kv_norm, m_w_kv, m_k_norm, m_w_f, m_b_f, m_b_norm, m_b_w_q, m_b_q_norm, m_b_w_o, m_ffn_norm, m_ffn_w_gu, m_ffn_w_down, v_meta, v_a_norm, v_a_w_in, v_a_conv, v_a_w_out, v_kv_norm, v_w_kv, v_k_norm, v_w_f, v_b_f, v_b_norm, v_b_w_q, v_b_q_norm, v_b_w_o, v_ffn_norm, v_ffn_w_gu, v_ffn_w_down):
    SEQ, D = x.shape[1], x.shape[2]
    n_meta = meta.shape[0]
    pad = BLOCK - n_meta
    first = pad + n_meta
    T = first + SEQ
    H = D // HEAD_DIM
    Ds = D // N_CHIPS
    n_a, n_b, depth = a_w_in.shape[0], b_w_q.shape[0], ffn_norm.shape[0]
    blk = _tile(T, 384, BLOCK)
    cx, cy, cc = lax.axis_index("x"), lax.axis_index("y"), lax.axis_index("c")
    q_me = 2 * cx + cy
    c_idx = jnp.reshape(cc, (1,)).astype(jnp.int32)
    q_idx = jnp.reshape(q_me, (1,)).astype(jnp.int32)
    rows8 = lambda v: jnp.pad(v, ((0, -v.shape[0] % 8), (0, 0)))

    col_parts = [meta, a_norm, a_conv.reshape(3 * n_a, Ds)]
    col_pack = jnp.concatenate([rows8(p) for p in col_parts], axis=0)
    col_full, w_f_full = _all_gather([col_pack, w_f], [(1, False), (0, False)], "ag_small")
    col_offs = [sum(rows8(p).shape[0] for p in col_parts[:i]) for i in range(3)]
    meta_f = col_full[:n_meta]
    a_norm_f = col_full[col_offs[1]:col_offs[1] + n_a]
    a_conv_f = col_full[col_offs[2]:col_offs[2] + 3 * n_a].reshape(n_a, 3, D)
    w_fp = jnp.pad(w_f_full, ((0, 0), (0, LANES - H))).astype(BF16)
    b_fp = jnp.pad(b_f, (0, LANES - H)).reshape(1, LANES)

    stages = []
    for l in range(n_a):
        stages += [[(a_w_in, l, 1), (a_w_out, l, 0)], [(ffn_w_gu, l, 1), (ffn_w_down, l, 0)]]
    for j in range(n_b):
        stages += [[(b_w_q, j, 0), (b_w_o, j, 0)], [(ffn_w_gu, n_a + j, 1), (ffn_w_down, n_a + j, 0)]]
    stages[2 * n_a].append((w_kv[None], 0, 1))

    def gather_begin(k, after):
        axes = [ax for _, _, ax in stages[k]]
        fulls = [_cast_into_full(w, l, ax, q_idx, f"cast_{k}_{i}") for i, (w, l, ax) in enumerate(stages[k])]
        send_sems, recv_sems, fulls, token = _gather_start(fulls, axes, after, f"ag_start_{k}")
        return (send_sems, recv_sems, fulls, axes, k), token

    def gather_end(handle, after):
        send_sems, recv_sems, fulls, axes, k = handle
        fulls = _gather_wait(send_sems, recv_sems, fulls, axes, after, f"ag_wait_{k}")
        return _gather_forward(fulls, axes, f"ag_forward_{k}")

    handle, _ = gather_begin(0, col_full)
    arrived = [gather_end(handle, col_full)]
    in_flight_gather = [None]

    def enter_segment():
        w = arrived[-1]
        token = None
        if len(arrived) < len(stages):
            in_flight_gather[0], token = gather_begin(len(arrived), w[0])
        return w, token

    def leave_segment(h_out):
        if in_flight_gather[0] is not None:
            arrived.append(gather_end(in_flight_gather[0], h_out))
            in_flight_gather[0] = None

    h = jnp.concatenate([jnp.zeros((pad, D), F32), meta_f, x[0]], axis=0)
    saved = []

    def ffn_fwd(h, layer):
        (w_gu, w_down), token = enter_segment()
        xn = _rms_fwd(h, ffn_norm[layer:layer + 1], f"ffn_norm_{layer}", dep=token)
        z = _matmul(xn, w_gu, mode="nn", out_dtype=BF16, name=f"ffn_gu_{layer}", out_parts=2)
        act = _swiglu_fwd(z, f"swiglu_{layer}")
        out = _matmul(act, w_down, mode="nn", out_dtype=F32, name=f"ffn_down_{layer}", res=h)
        leave_segment(out)
        return out, (h, xn, z, act), (w_gu, w_down)

    wa, wb = [], []
    for l in range(n_a):
        (w_in, w_out), token = enter_segment()
        xn = _rms_fwd(h, a_norm_f[l:l + 1], f"a_norm_{l}", dep=token)
        z = _matmul(xn, w_in, mode="nn", out_dtype=BF16, name=f"a_in_{l}", out_parts=3)
        y = _gate_fwd(z, a_conv_f[l], f"a_gate_{l}")
        h2 = _matmul(y, w_out, mode="nn", out_dtype=F32, name=f"a_out_{l}", res=h)
        leave_segment(h2)
        h3, ffn_saved, w_ffn = ffn_fwd(h2, l)
        saved.append((h, xn, z, y, ffn_saved))
        wa.append((w_in, w_out) + w_ffn)
        h = h3

    for j in range(n_b):
        layer = n_a + j
        w_mix, token = enter_segment()
        w_q, w_o = w_mix[:2]
        if j == 0:
            h_kv, w_kv_b = h, w_mix[2]
            xkv = _rms_fwd(h, kv_norm.reshape(1, D), "kv_norm", dep=token)
            token = None
            kvz = _matmul(xkv, w_kv_b, mode="nn", out_dtype=F32, name="kv_proj", out_parts=2)
            k_n = _headnorm_fwd(kvz, 0, k_norm.reshape(1, HEAD_DIM), "k_headnorm")
            v_b = _cast_part(kvz, 1, "v_cast")
            pre = _matmul(xkv, w_fp, mode="nn", out_dtype=F32, name="f_proj")
            c_cum = _logf_cumsum(pre, b_fp, pad, "logf_cumsum")
            c_t = c_cum[:, :H].T.reshape(H, T // blk, 1, blk)
        xn = _rms_fwd(h, b_norm[j:j + 1], f"b_norm_{j}", dep=token)
        qz = _matmul(xn, w_q, mode="nn", out_dtype=F32, name=f"b_q_{j}")[None]
        q_n = _headnorm_fwd(qz, 0, b_q_norm[j:j + 1], f"q_headnorm_{j}")
        o, lse = _attn_fwd(q_n, k_n, v_b, c_cum, c_t, blk=blk, pad=pad, name=f"attn_fwd_{j}")
        h2 = _matmul(o, w_o, mode="nn", out_dtype=F32, name=f"b_o_{j}", res=h)
        leave_segment(h2)
        h3, ffn_saved, w_ffn = ffn_fwd(h2, layer)
        saved.append((h, xn, qz, q_n, o, lse, ffn_saved))
        wb.append((w_q, w_o) + w_ffn)
        h = h3

    dh, loss_blk = _loss_head(h, loss_target[0], first, "loss_head")
    loss = lax.psum(loss_blk[0, 0], ("x", "y", "c"))

    shards = {}
    in_flight = [None]

    def reduce_later(names, grads, axes, tag, done):
        after = c_idx
        if in_flight[0] is not None:
            prev_names, pending = in_flight[0]
            got = _reduce_finish(pending, q_idx, c_idx, done)
            shards.update(zip(prev_names, got))
            after = got[0]
        pending, token = _reduce_begin(grads, axes, c_idx, after, tag)
        in_flight[0] = (names, pending)
        return token

    def ffn_bwd(dh, layer, w_gu, w_down, ffn_saved, dep):
        h_in, xn, z, act = ffn_saved
        da = _matmul(dh, w_down, mode="nt", out_dtype=BF16, name=f"ffn_down_dx_{layer}", dep=dep)
        g_down = _matmul(act, dh, mode="tn", out_dtype=BF16, name=f"ffn_down_dw_{layer}")
        dz = _swiglu_bwd(z, da, f"swiglu_bwd_{layer}")
        dxn = _matmul(dz, w_gu, mode="nt", out_dtype=F32, name=f"ffn_gu_dx_{layer}")
        g_gu = _matmul(xn, dz, mode="tn", out_dtype=BF16, name=f"ffn_gu_dw_{layer}")
        dh, dg = _rms_bwd(h_in, ffn_norm[layer:layer + 1], dxn, dh, f"ffn_norm_bwd_{layer}")
        token = reduce_later([("ffn_w_gu", layer), ("ffn_w_down", layer)], [g_gu, g_down], [1, 0], f"f{layer}", dh)
        return dh, dg, token

    d_ffn_norm, d_b_norm, d_q_norm, d_a_norm, d_a_conv = {}, {}, {}, {}, {}
    kv_prev = None
    token = None
    for j in reversed(range(n_b)):
        layer = n_a + j
        w_q, w_o, w_gu, w_down = wb[j]
        h_in, xn, qz, q_n, o, lse, ffn_saved = saved[layer]
        dh, d_ffn_norm[layer], token = ffn_bwd(dh, layer, w_gu, w_down, ffn_saved, token)
        do = _matmul(dh, w_o, mode="nt", out_dtype=BF16, name=f"b_o_dx_{j}", dep=token)
        g_o = _matmul(o, dh, mode="tn", out_dtype=BF16, name=f"b_o_dw_{j}")
        dq, dk, dv, dct = _attn_bwd(q_n, k_n, v_b, o, do, lse, c_cum, c_t, kv_prev, blk=blk, pad=pad,
                                    name=f"attn_bwd_{j}")
        kv_prev = (dk, dv, dct)
        dqz, d_q_norm[j] = _headnorm_bwd(qz, 0, b_q_norm[j:j + 1], dq, f"q_headnorm_bwd_{j}")
        dxn = _matmul(dqz, w_q, mode="nt", out_dtype=F32, name=f"b_q_dx_{j}")
        g_q = _matmul(xn, dqz, mode="tn", out_dtype=BF16, name=f"b_q_dw_{j}")
        dh, d_b_norm[j] = _rms_bwd(h_in, b_norm[j:j + 1], dxn, dh, f"b_norm_bwd_{j}")
        if j > 0:
            token = reduce_later([("b_w_q", j), ("b_w_o", j)], [g_q, g_o], [0, 0], f"b{j}", dh)

    dk, dv, dct = kv_prev
    dkz, d_k_norm = _headnorm_bwd(kvz, 0, k_norm.reshape(1, HEAD_DIM), dk, "k_headnorm_bwd")
    dvz = _cast_part(dv[None], 0, "dv_cast")
    dkv = jnp.stack([dkz, dvz])
    dc = jnp.pad(dct.reshape(H, T).T, ((0, 0), (0, LANES - H)))
    dpre, d_b_f = _logf_bwd(pre, b_fp, dc, pad, "logf_bwd")
    dxkv = _matmul(dkv, w_kv_b, mode="nt", out_dtype=F32, name="kv_proj_dx")
    dxkv = _matmul(dpre, w_fp, mode="nt", out_dtype=F32, name="f_proj_dx", res=dxkv)
    g_kv = _matmul(xkv, dkv, mode="tn", out_dtype=BF16, name="kv_proj_dw")
    d_w_f = _matmul(xkv, dpre, mode="tn", out_dtype=F32, name="f_proj_dw")
    dh, d_kv_norm = _rms_bwd(h_kv, kv_norm.reshape(1, D), dxkv, dh, "kv_norm_bwd")
    token = reduce_later([("b_w_q", 0), ("b_w_o", 0), ("w_kv", 0)], [g_q, g_o, g_kv], [0, 0, 1], "b0", dh)

    for l in reversed(range(n_a)):
        w_in, w_out, w_gu, w_down = wa[l]
        h_in, xn, z, y, ffn_saved = saved[l]
        dh, d_ffn_norm[l], token = ffn_bwd(dh, l, w_gu, w_down, ffn_saved, token)
        dy = _matmul(dh, w_out, mode="nt", out_dtype=F32, name=f"a_out_dx_{l}", dep=token)
        g_out = _matmul(y, dh, mode="tn", out_dtype=BF16, name=f"a_out_dw_{l}")
        dz, d_a_conv[l] = _gate_bwd(z, a_conv_f[l], dy, f"a_gate_bwd_{l}")
        dxn = _matmul(dz, w_in, mode="nt", out_dtype=F32, name=f"a_in_dx_{l}")
        g_in = _matmul(xn, dz, mode="tn", out_dtype=BF16, name=f"a_in_dw_{l}")
        dh, d_a_norm[l] = _rms_bwd(h_in, a_norm_f[l:l + 1], dxn, dh, f"a_norm_bwd_{l}")
        token = reduce_later([("a_w_in", l), ("a_w_out", l)], [g_in, g_out], [1, 0], f"a{l}", dh)

    prev_names, pending = in_flight[0]
    shards.update(zip(prev_names, _reduce_finish(pending, q_idx, c_idx, dh)))
    grad_x = dh[first:][None]

    widen = lambda v: jnp.pad(v, ((0, 0), (0, D - v.shape[1])))
    groups = [
        [dh[pad:first]],
        [d_a_norm[l] for l in range(n_a)],
        [d_a_conv[l] for l in range(n_a)],
        [d_kv_norm],
        [widen(d_k_norm)],
        [d_w_f[:, :H].T],
        [widen(d_b_f)],
        [d_b_norm[j] for j in range(n_b)],
        [widen(d_q_norm[j]) for j in range(n_b)],
        [d_ffn_norm[l] for l in range(depth)],
    ]
    pack = jnp.concatenate([rows8(p) for g in groups for p in g], axis=0)
    red = _all_reduce_small(pack, "ar_small")
    taken, off = [], 0
    for g in groups:
        r, rp = g[0].shape[0], rows8(g[0]).shape[0]
        taken.append(red[off:off + len(g) * rp].reshape(len(g), rp, D)[:, :r].reshape(len(g) * r, D))
        off += len(g) * rp
    take = lambda i: taken[i]
    mine = lambda a: lax.dynamic_slice_in_dim(a, q_me * Ds, Ds, axis=1)
    layers_of = lambda name, n: jnp.stack([shards[(name, l)] for l in range(n)])
    grads = {
        "meta": mine(take(0)),
        "a_norm": mine(take(1)),
        "a_w_in": layers_of("a_w_in", n_a),
        "a_conv": mine(take(2)).reshape(n_a, 3, Ds),
        "a_w_out": layers_of("a_w_out", n_a),
        "kv_norm": take(3).reshape(D),
        "w_kv": shards[("w_kv", 0)],
        "k_norm": take(4)[0, :HEAD_DIM],
        "w_f": mine(take(5)).T,
        "b_f": take(6)[0, :H],
        "b_norm": take(7),
        "b_w_q": layers_of("b_w_q", n_b),
        "b_q_norm": take(8)[:, :HEAD_DIM],
        "b_w_o": layers_of("b_w_o", n_b),
        "ffn_norm": take(9),
        "ffn_w_gu": layers_of("ffn_w_gu", depth),
        "ffn_w_down": layers_of("ffn_w_down", depth),
    }
    weights = dict(meta=meta, a_norm=a_norm, a_w_in=a_w_in, a_conv=a_conv, a_w_out=a_w_out, kv_norm=kv_norm, w_kv=w_kv,
                   k_norm=k_norm, w_f=w_f, b_f=b_f, b_norm=b_norm, b_w_q=b_w_q, b_q_norm=b_q_norm, b_w_o=b_w_o,
                   ffn_norm=ffn_norm, ffn_w_gu=ffn_w_gu, ffn_w_down=ffn_w_down)
    m_in = dict(meta=m_meta, a_norm=m_a_norm, a_w_in=m_a_w_in, a_conv=m_a_conv, a_w_out=m_a_w_out, kv_norm=m_kv_norm,
                w_kv=m_w_kv, k_norm=m_k_norm, w_f=m_w_f, b_f=m_b_f, b_norm=m_b_norm, b_w_q=m_b_w_q,
                b_q_norm=m_b_q_norm, b_w_o=m_b_w_o, ffn_norm=m_ffn_norm, ffn_w_gu=m_ffn_w_gu, ffn_w_down=m_ffn_w_down)
    v_in = dict(meta=v_meta, a_norm=v_a_norm, a_w_in=v_a_w_in, a_conv=v_a_conv, a_w_out=v_a_w_out, kv_norm=v_kv_norm,
                w_kv=v_w_kv, k_norm=v_k_norm, w_f=v_w_f, b_f=v_b_f, b_norm=v_b_norm, b_w_q=v_b_w_q,
                b_q_norm=v_b_q_norm, b_w_o=v_b_w_o, ffn_norm=v_ffn_norm, ffn_w_gu=v_ffn_w_gu, ffn_w_down=v_ffn_w_down)

    deltas, new_m, new_v = {}, {}, {}
    for name, w in weights.items():
        shape = w.shape
        two_d = (1, shape[0]) if w.ndim == 1 else (math.prod(shape[:-1]), shape[-1])
        r2 = lambda a: a.reshape(two_d)
        d_, m_, v_ = _adamw(r2(w), r2(grads[name]), r2(m_in[name]), r2(v_in[name]), f"adamw_{name}")
        deltas[name], new_m[name], new_v[name] = d_.reshape(shape), m_.reshape(shape), v_.reshape(shape)
        grads[name] = grads[name].reshape(shape)

    names = list(weights)
    return (loss, grad_x, *[grads[n] for n in names], *[deltas[n] for n in names],
            *[new_m[n] for n in names], *[new_v[n] for n in names])
```

```python
import functools
import math

import jax
import jax.numpy as jnp
from jax import lax
from jax.experimental import pallas as pl
from jax.experimental.pallas import tpu as pltpu

F32 = jnp.float32
BF16 = jnp.bfloat16
HEAD_DIM = 128
BLOCK = 128
LANES = 128
EPS = 1e-6
NEG = -1e30
ADAM_LR, ADAM_B1, ADAM_B2, ADAM_EPS, ADAM_WD, ADAM_STEP = 0.001, 0.9, 0.999, 1e-08, 0.01, 10
VMEM_LIMIT = 56 * 1024 * 1024
TILE_BUDGET = 40 * 1024 * 1024
MESH = pl.DeviceIdType.MESH
N_CHIPS = 4


def _tile(n, target, align):
    best = None
    for d in range(align, min(n, target) + 1, align):
        if n % d == 0:
            best = d
    return best if best is not None else n


def _cp(sem):
    return pltpu.CompilerParams(dimension_semantics=sem, vmem_limit_bytes=VMEM_LIMIT)


def _matmul(a, b, *, mode, out_dtype, name, res=None, out_parts=1, dep=None):
    a_parts = a.shape[0] if a.ndim == 3 else 1
    b_parts = b.shape[0] if b.ndim == 3 else 1
    if mode == "tn":
        K, M = a.shape
        Kp = K
    else:
        M, Kp = a.shape[-2:]
        K = Kp * a_parts
    N = b.shape[0] if mode == "nt" else b.shape[-1] * b_parts
    Np = N // max(b_parts, out_parts)
    tm = _tile(M, 1024, LANES) if mode == "tn" else _tile(M, 1056, 16)
    tn = _tile(Np, 1536, LANES)
    tk = _tile(Kp, 1056, 16) if mode == "tn" else _tile(Kp, 2048, LANES)
    ab, bb, ob = a.dtype.itemsize, b.dtype.itemsize, jnp.dtype(out_dtype).itemsize

    def vmem(tm_):
        blocks = 2 * (tm_ * tk * ab + tk * tn * bb + tm_ * tn * ob + (tm_ * tn * 4 if res is not None else 0))
        temps = tm_ * tn * 8 + (tm_ * tk * 2 if ab == 4 else 0) + (tk * tn * 2 if bb == 4 else 0)
        return blocks + temps

    while vmem(tm) > TILE_BUDGET and tm > 256:
        tm = _tile(M, tm // 2, LANES if mode == "tn" else 16)
    ni, nj, nk = M // tm, N // tn, K // tk
    nkp, njp = Kp // tk, Np // tn

    if mode == "tn":
        a_spec = pl.BlockSpec((tk, tm), lambda i, j, k: (k, i))
    elif a_parts > 1:
        a_spec = pl.BlockSpec((None, tm, tk), lambda i, j, k: (k // nkp, i, k % nkp))
    else:
        a_spec = pl.BlockSpec((tm, tk), lambda i, j, k: (i, k))
    if mode == "nt":
        b_spec = pl.BlockSpec((tn, tk), lambda i, j, k: (j, k))
    elif b_parts > 1:
        b_spec = pl.BlockSpec((None, tk, tn), lambda i, j, k: (j // njp, k, j % njp))
    else:
        b_spec = pl.BlockSpec((tk, tn), lambda i, j, k: (k, j))
    in_specs = [a_spec, b_spec]
    operands = [a, b]
    if res is not None:
        in_specs.append(pl.BlockSpec((tm, tn), lambda i, j, k: (i, j)))
        operands.append(res)
    if dep is not None:
        in_specs.append(pl.BlockSpec((8, LANES), lambda i, j, k: (0, 0)))
        operands.append(dep)
    n_in = len(operands)
    if out_parts > 1:
        out_spec = pl.BlockSpec((None, tm, tn), lambda i, j, k: (j // njp, i, j % njp))
        out_shape = jax.ShapeDtypeStruct((out_parts, M, Np), out_dtype)
    else:
        out_spec = pl.BlockSpec((tm, tn), lambda i, j, k: (i, j))
        out_shape = jax.ShapeDtypeStruct((M, N), out_dtype)
    dims = {"nn": (((1,), (0,)), ((), ())), "nt": (((1,), (1,)), ((), ())), "tn": (((0,), (0,)), ((), ()))}[mode]
    has_res = res is not None

    def body(*refs):
        a_ref, b_ref = refs[0], refs[1]
        res_ref = refs[2] if has_res else None
        o_ref = refs[n_in]
        d = lax.dot_general(a_ref[...].astype(BF16), b_ref[...].astype(BF16), dims, preferred_element_type=F32)
        if nk == 1:
            if has_res:
                d = d + res_ref[...]
            o_ref[...] = d.astype(out_dtype)
        else:
            acc_ref = refs[-1]
            k = pl.program_id(2)

            @pl.when(k == 0)
            def _():
                acc_ref[...] = d

            @pl.when(k > 0)
            def _():
                acc_ref[...] += d

            @pl.when(k == nk - 1)
            def _():
                r = acc_ref[...]
                if has_res:
                    r = r + res_ref[...]
                o_ref[...] = r.astype(out_dtype)

    return pl.pallas_call(
        body, name=name, grid=(ni, nj, nk), in_specs=in_specs, out_specs=out_spec, out_shape=out_shape,
        scratch_shapes=[pltpu.VMEM((tm, tn), F32)] if nk > 1 else [],
        compiler_params=_cp(("parallel", "parallel", "arbitrary")),
    )(*operands)


def _cast_into_full(w3, layer, axis, q_idx, name):
    _, R, C = w3.shape
    tr = _tile(R, max(16, (4 * 1024 * 1024) // (C * 4)), 16)
    nb = R // tr

    def body(q_ref, w_ref, o_ref):
        o_ref[...] = w_ref[...].astype(BF16)

    if axis == 0:
        out_spec = pl.BlockSpec((tr, C), lambda i, q_ref: (q_ref[0] * nb + i, 0))
        full = (N_CHIPS * R, C)
    else:
        out_spec = pl.BlockSpec((tr, C), lambda i, q_ref: (i, q_ref[0]))
        full = (R, N_CHIPS * C)
    grid_spec = pltpu.PrefetchScalarGridSpec(
        num_scalar_prefetch=1, grid=(nb,),
        in_specs=[pl.BlockSpec((None, tr, C), lambda i, q_ref: (layer, i, 0))], out_specs=out_spec)
    return pl.pallas_call(
        body, name=name, grid_spec=grid_spec, out_shape=jax.ShapeDtypeStruct(full, BF16),
        compiler_params=_cp(("parallel",)),
    )(q_idx, w3)


def _rms_fwd(h, g, name, dep=None):
    T, D = h.shape
    tr = _tile(T, 528, 16)

    def body(h_ref, g_ref, *rest):
        o_ref = rest[-1]
        x = h_ref[...]
        r = lax.rsqrt(jnp.mean(x * x, axis=-1, keepdims=True) + EPS)
        o_ref[...] = (x * r * g_ref[...]).astype(BF16)

    in_specs = [pl.BlockSpec((tr, D), lambda i: (i, 0)), pl.BlockSpec((1, D), lambda i: (0, 0))]
    operands = [h, g]
    if dep is not None:
        in_specs.append(pl.BlockSpec((8, LANES), lambda i: (0, 0)))
        operands.append(dep)
    return pl.pallas_call(
        body, name=name, grid=(T // tr,), in_specs=in_specs,
        out_specs=pl.BlockSpec((tr, D), lambda i: (i, 0)),
        out_shape=jax.ShapeDtypeStruct((T, D), BF16), compiler_params=_cp(("parallel",)),
    )(*operands)


def _rms_bwd(h, g, dxn, dh, name):
    T, D = h.shape
    tr = _tile(T, 264, 16)

    def body(h_ref, g_ref, dxn_ref, dh_ref, o_ref, ob_ref, dg_ref):
        x = h_ref[...]
        r = lax.rsqrt(jnp.mean(x * x, axis=-1, keepdims=True) + EPS)
        xh = x * r
        dy = dxn_ref[...]
        dxh = dy * g_ref[...]
        dx = r * (dxh - xh * jnp.mean(dxh * xh, axis=-1, keepdims=True))
        out = dh_ref[...] + dx
        o_ref[...] = out
        ob_ref[...] = out.astype(BF16)
        part = jnp.sum(dy * xh, axis=0, keepdims=True)

        @pl.when(pl.program_id(0) == 0)
        def _():
            dg_ref[...] = part

        @pl.when(pl.program_id(0) > 0)
        def _():
            dg_ref[...] += part

    row = pl.BlockSpec((tr, D), lambda i: (i, 0))
    vec = pl.BlockSpec((1, D), lambda i: (0, 0))
    return pl.pallas_call(
        body, name=name, grid=(T // tr,), in_specs=[row, vec, row, row], out_specs=[row, row, vec],
        out_shape=[jax.ShapeDtypeStruct((T, D), F32), jax.ShapeDtypeStruct((T, D), BF16),
                   jax.ShapeDtypeStruct((1, D), F32)],
        compiler_params=_cp(("arbitrary",)),
    )(h, g, dxn, dh)


def _shift_down(u, n, rows):
    return jnp.where(rows >= n, pltpu.roll(u, n, 0), 0.0)


def _shift_up(u, n, rows, total):
    return jnp.where(rows < total - n, pltpu.roll(u, total - n, 0), 0.0)


def _gate_fwd(z, conv_w, name):
    _, T, D = z.shape
    tc = LANES

    def body(b_ref, c_ref, h_ref, w_ref, y_ref):
        rows = lax.broadcasted_iota(jnp.int32, (T, tc), 0)
        u = c_ref[...].astype(F32) * h_ref[...].astype(F32)
        w0, w1, w2 = w_ref[0:1, :], w_ref[1:2, :], w_ref[2:3, :]
        conv = u * w2 + _shift_down(u, 1, rows) * w1 + _shift_down(u, 2, rows) * w0
        y_ref[...] = (b_ref[...].astype(F32) * conv).astype(BF16)

    part = lambda p: pl.BlockSpec((None, T, tc), lambda j, p=p: (p, 0, j))
    return pl.pallas_call(
        body, name=name, grid=(D // tc,),
        in_specs=[part(0), part(1), part(2), pl.BlockSpec((3, tc), lambda j: (0, j))],
        out_specs=pl.BlockSpec((T, tc), lambda j: (0, j)),
        out_shape=jax.ShapeDtypeStruct((T, D), BF16), compiler_params=_cp(("parallel",)),
    )(z, z, z, conv_w)


def _gate_bwd(z, conv_w, dy, name):
    _, T, D = z.shape
    tc = LANES

    def body(b_ref, c_ref, h_ref, w_ref, dy_ref, dz_ref, dw_ref):
        rows = lax.broadcasted_iota(jnp.int32, (T, tc), 0)
        cg, hh = c_ref[...].astype(F32), h_ref[...].astype(F32)
        u = cg * hh
        w0, w1, w2 = w_ref[0:1, :], w_ref[1:2, :], w_ref[2:3, :]
        s1, s2 = _shift_down(u, 1, rows), _shift_down(u, 2, rows)
        g = dy_ref[...]
        dz_ref[0] = (g * (u * w2 + s1 * w1 + s2 * w0)).astype(BF16)
        dconv = g * b_ref[...].astype(F32)
        dw_ref[0:1, :] = jnp.sum(dconv * s2, axis=0, keepdims=True)
        dw_ref[1:2, :] = jnp.sum(dconv * s1, axis=0, keepdims=True)
        dw_ref[2:3, :] = jnp.sum(dconv * u, axis=0, keepdims=True)
        du = dconv * w2 + _shift_up(dconv, 1, rows, T) * w1 + _shift_up(dconv, 2, rows, T) * w0
        dz_ref[1] = (du * hh).astype(BF16)
        dz_ref[2] = (du * cg).astype(BF16)

    part = lambda p: pl.BlockSpec((None, T, tc), lambda j, p=p: (p, 0, j))
    return pl.pallas_call(
        body, name=name, grid=(D // tc,),
        in_specs=[part(0), part(1), part(2), pl.BlockSpec((3, tc), lambda j: (0, j)),
                  pl.BlockSpec((T, tc), lambda j: (0, j))],
        out_specs=[pl.BlockSpec((3, T, tc), lambda j: (0, 0, j)), pl.BlockSpec((3, tc), lambda j: (0, j))],
        out_shape=[jax.ShapeDtypeStruct((3, T, D), BF16), jax.ShapeDtypeStruct((3, D), F32)],
        compiler_params=_cp(("parallel",)),
    )(z, z, z, conv_w, dy)


def _swiglu_fwd(z, name):
    _, T, Fd = z.shape
    tr, tc = _tile(T, 528, 16), _tile(Fd, 512, LANES)

    def body(g_ref, u_ref, o_ref):
        g = g_ref[...].astype(F32)
        o_ref[...] = (g * jax.nn.sigmoid(g) * u_ref[...].astype(F32)).astype(BF16)

    part = lambda p: pl.BlockSpec((None, tr, tc), lambda i, j, p=p: (p, i, j))
    return pl.pallas_call(
        body, name=name, grid=(T // tr, Fd // tc), in_specs=[part(0), part(1)],
        out_specs=pl.BlockSpec((tr, tc), lambda i, j: (i, j)),
        out_shape=jax.ShapeDtypeStruct((T, Fd), BF16), compiler_params=_cp(("parallel", "parallel")),
    )(z, z)


def _swiglu_bwd(z, da, name):
    _, T, Fd = z.shape
    tr, tc = _tile(T, 528, 16), _tile(Fd, 512, LANES)

    def body(g_ref, u_ref, da_ref, dz_ref):
        g, d = g_ref[...].astype(F32), da_ref[...].astype(F32)
        s = jax.nn.sigmoid(g)
        dz_ref[0] = (d * u_ref[...].astype(F32) * (s * (1.0 + g * (1.0 - s)))).astype(BF16)
        dz_ref[1] = (d * (g * s)).astype(BF16)

    part = lambda p: pl.BlockSpec((None, tr, tc), lambda i, j, p=p: (p, i, j))
    return pl.pallas_call(
        body, name=name, grid=(T // tr, Fd // tc),
        in_specs=[part(0), part(1), pl.BlockSpec((tr, tc), lambda i, j: (i, j))],
        out_specs=pl.BlockSpec((2, tr, tc), lambda i, j: (0, i, j)),
        out_shape=jax.ShapeDtypeStruct((2, T, Fd), BF16), compiler_params=_cp(("parallel", "parallel")),
    )(z, z, da)


def _headnorm_fwd(z, part, g, name):
    _, T, D = z.shape
    tr = _tile(T, 1056, 16)

    def body(z_ref, g_ref, o_ref):
        x = z_ref[...]
        r = lax.rsqrt(jnp.mean(x * x, axis=-1, keepdims=True) + EPS)
        o_ref[...] = (x * r * g_ref[...]).astype(BF16)

    return pl.pallas_call(
        body, name=name, grid=(T // tr, D // HEAD_DIM),
        in_specs=[pl.BlockSpec((None, tr, HEAD_DIM), lambda i, h: (part, i, h)),
                  pl.BlockSpec((1, HEAD_DIM), lambda i, h: (0, 0))],
        out_specs=pl.BlockSpec((tr, HEAD_DIM), lambda i, h: (i, h)),
        out_shape=jax.ShapeDtypeStruct((T, D), BF16), compiler_params=_cp(("parallel", "parallel")),
    )(z, g)


def _headnorm_bwd(z, part, g, dy, name):
    _, T, D = z.shape
    tr = _tile(T, 1056, 16)

    def body(z_ref, g_ref, dy_ref, dz_ref, dg_ref):
        x = z_ref[...]
        r = lax.rsqrt(jnp.mean(x * x, axis=-1, keepdims=True) + EPS)
        xh = x * r
        dy_ = dy_ref[...]
        dxh = dy_ * g_ref[...]
        dz_ref[...] = (r * (dxh - xh * jnp.mean(dxh * xh, axis=-1, keepdims=True))).astype(BF16)
        partial = jnp.sum(dy_ * xh, axis=0, keepdims=True)
        first = (pl.program_id(0) == 0) & (pl.program_id(1) == 0)

        @pl.when(first)
        def _():
            dg_ref[...] = partial

        @pl.when(jnp.logical_not(first))
        def _():
            dg_ref[...] += partial

    blk = pl.BlockSpec((tr, HEAD_DIM), lambda i, h: (i, h))
    vec = pl.BlockSpec((1, HEAD_DIM), lambda i, h: (0, 0))
    return pl.pallas_call(
        body, name=name, grid=(T // tr, D // HEAD_DIM),
        in_specs=[pl.BlockSpec((None, tr, HEAD_DIM), lambda i, h: (part, i, h)), vec, blk],
        out_specs=[blk, vec],
        out_shape=[jax.ShapeDtypeStruct((T, D), BF16), jax.ShapeDtypeStruct((1, HEAD_DIM), F32)],
        compiler_params=_cp(("arbitrary", "arbitrary")),
    )(z, g, dy)


def _cast_part(z, part, name):
    _, T, D = z.shape
    tr = _tile(T, 528, 16)

    def body(z_ref, o_ref):
        o_ref[...] = z_ref[...].astype(BF16)

    return pl.pallas_call(
        body, name=name, grid=(T // tr,),
        in_specs=[pl.BlockSpec((None, tr, D), lambda i: (part, i, 0))],
        out_specs=pl.BlockSpec((tr, D), lambda i: (i, 0)),
        out_shape=jax.ShapeDtypeStruct((T, D), BF16), compiler_params=_cp(("parallel",)),
    )(z)


def _split3(x):
    a = x.astype(BF16)
    r = x - a.astype(F32)
    b = r.astype(BF16)
    c = (r - b.astype(F32)).astype(BF16)
    return a, b, c


def _tri_matmul(tri, x):
    a, b, c = _split3(x)
    dot = lambda v: jnp.dot(tri, v, preferred_element_type=F32)
    return (dot(c) + dot(b)) + dot(a)


def _logf_cumsum(pre, bias, pad, name):
    T = pre.shape[0]
    nb = T // BLOCK

    def body(p_ref, b_ref, c_ref, carry):
        i = pl.program_id(0)

        @pl.when(i == 0)
        def _():
            carry[...] = jnp.zeros_like(carry)

        x = p_ref[...] + b_ref[...]
        lf = jnp.minimum(x, 0.0) - jnp.log(1.0 + jnp.exp(-jnp.abs(x)))
        rows = i * BLOCK + lax.broadcasted_iota(jnp.int32, (BLOCK, LANES), 0)
        lf = jnp.where(rows >= pad, lf, 0.0)
        r = lax.broadcasted_iota(jnp.int32, (BLOCK, BLOCK), 0)
        c = lax.broadcasted_iota(jnp.int32, (BLOCK, BLOCK), 1)
        tri = jnp.where(c <= r, 1.0, 0.0).astype(BF16)
        c_ref[...] = _tri_matmul(tri, lf) + carry[...]
        carry[...] = c_ref[BLOCK - 1:BLOCK, :]

    return pl.pallas_call(
        body, name=name, grid=(nb,),
        in_specs=[pl.BlockSpec((BLOCK, LANES), lambda i: (i, 0)), pl.BlockSpec((1, LANES), lambda i: (0, 0))],
        out_specs=pl.BlockSpec((BLOCK, LANES), lambda i: (i, 0)),
        out_shape=jax.ShapeDtypeStruct((T, LANES), F32),
        scratch_shapes=[pltpu.VMEM((1, LANES), F32)], compiler_params=_cp(("arbitrary",)),
    )(pre, bias)


def _logf_bwd(pre, bias, dc, pad, name):
    T = pre.shape[0]
    nb = T // BLOCK

    def body(p_ref, b_ref, dc_ref, dp_ref, db_ref, carry, dlf_ref):
        i = pl.program_id(0)

        @pl.when(i == 0)
        def _():
            carry[...] = jnp.zeros_like(carry)

        r = lax.broadcasted_iota(jnp.int32, (BLOCK, BLOCK), 0)
        c = lax.broadcasted_iota(jnp.int32, (BLOCK, BLOCK), 1)
        tri = jnp.where(c >= r, 1.0, 0.0).astype(BF16)
        dlf_ref[...] = _tri_matmul(tri, dc_ref[...]) + carry[...]
        carry[...] = dlf_ref[0:1, :]
        dlf = dlf_ref[...]
        x = p_ref[...] + b_ref[...]
        rows = (nb - 1 - i) * BLOCK + lax.broadcasted_iota(jnp.int32, (BLOCK, LANES), 0)
        dpre = jnp.where(rows >= pad, dlf * jax.nn.sigmoid(-x), 0.0)
        dp_ref[...] = dpre
        partial = jnp.sum(dpre, axis=0, keepdims=True)

        @pl.when(i == 0)
        def _():
            db_ref[...] = partial

        @pl.when(i > 0)
        def _():
            db_ref[...] += partial

    rev = pl.BlockSpec((BLOCK, LANES), lambda i: (nb - 1 - i, 0))
    vec = pl.BlockSpec((1, LANES), lambda i: (0, 0))
    return pl.pallas_call(
        body, name=name, grid=(nb,), in_specs=[rev, vec, rev], out_specs=[rev, vec],
        out_shape=[jax.ShapeDtypeStruct((T, LANES), F32), jax.ShapeDtypeStruct((1, LANES), F32)],
        scratch_shapes=[pltpu.VMEM((1, LANES), F32), pltpu.VMEM((BLOCK, LANES), F32)],
        compiler_params=_cp(("arbitrary",)),
    )(pre, bias, dc)


def _loss_head(h, target, first, name):
    T, D = h.shape
    tr = BLOCK
    skip = first // tr

    def body(h_ref, t_ref, dh_ref, dhb_ref, loss_ref):
        i = pl.program_id(0)

        @pl.when(i == 0)
        def _():
            loss_ref[...] = jnp.zeros_like(loss_ref)

        @pl.when(i < skip)
        def _():
            dh_ref[...] = jnp.zeros_like(dh_ref)
            dhb_ref[...] = jnp.zeros_like(dhb_ref)

        @pl.when(i >= skip)
        def _():
            err = h_ref[...] - t_ref[...]
            dh_ref[...] = err * (1.0 / D)
            dhb_ref[...] = (err * (1.0 / D)).astype(BF16)
            loss_ref[...] += jnp.sum(err * err) * (0.5 / D)

    row = pl.BlockSpec((tr, D), lambda i: (i, 0))
    return pl.pallas_call(
        body, name=name, grid=(T // tr,),
        in_specs=[row, pl.BlockSpec((tr, D), lambda i: (jnp.maximum(i - skip, 0), 0))],
        out_specs=[row, row, pl.BlockSpec((8, LANES), lambda i: (0, 0))],
        out_shape=[jax.ShapeDtypeStruct((T, D), F32), jax.ShapeDtypeStruct((T, D), BF16),
                   jax.ShapeDtypeStruct((8, LANES), F32)],
        compiler_params=_cp(("arbitrary",)),
    )(h, target)


def _adamw(w, g, m, v, name):
    R, C = w.shape
    tr = _tile(R, max(8, TILE_BUDGET // (C * 4 * 7 * 3)), 8)
    bc1, bc2 = 1.0 - ADAM_B1 ** ADAM_STEP, 1.0 - ADAM_B2 ** ADAM_STEP

    def body(w_ref, g_ref, m_ref, v_ref, d_ref, mo_ref, vo_ref):
        g_ = g_ref[...]
        m_ = ADAM_B1 * m_ref[...] + (1.0 - ADAM_B1) * g_
        v_ = ADAM_B2 * v_ref[...] + (1.0 - ADAM_B2) * (g_ * g_)
        d_ref[...] = -ADAM_LR * ((m_ / bc1) / (jnp.sqrt(v_ / bc2) + ADAM_EPS) + ADAM_WD * w_ref[...])
        mo_ref[...] = m_
        vo_ref[...] = v_

    blk = pl.BlockSpec((tr, C), lambda i: (i, 0))
    sds = jax.ShapeDtypeStruct((R, C), F32)
    return pl.pallas_call(
        body, name=name, grid=(R // tr,), in_specs=[blk] * 4, out_specs=[blk] * 3, out_shape=[sds] * 3,
        compiler_params=_cp(("parallel",)),
    )(w, g, m, v)


def _pick_head(c_blk, h):
    lane = lax.broadcasted_iota(jnp.int32, c_blk.shape, 1)
    return jnp.sum(jnp.where(lane == h, c_blk, 0.0), axis=1, keepdims=True)


def _attn_fwd(q, k, v, c, ct, *, blk, pad, name):
    T, D = q.shape
    H, nq = D // HEAD_DIM, T // blk
    scale = 1.0 / math.sqrt(HEAD_DIM)

    def body(q_ref, k_ref, v_ref, c_ref, ct_ref, o_ref, lse_ref):
        h, i = pl.program_id(0), pl.program_id(1)
        qb = q_ref[...]
        cq = _pick_head(c_ref[...], h)

        def step_fn(masked):
            def step(j, carry):
                m, l, acc = carry
                off = j * blk if isinstance(j, int) else pl.multiple_of(j * blk, blk)
                kb = k_ref[pl.ds(off, blk), :]
                vb = v_ref[pl.ds(off, blk), :]
                s = lax.dot_general(qb, kb, (((1,), (1,)), ((), ())), preferred_element_type=F32) * scale
                s = s + (cq - ct_ref[j])
                if masked:
                    qpos = i * blk + lax.broadcasted_iota(jnp.int32, (blk, blk), 0)
                    kpos = j * blk + lax.broadcasted_iota(jnp.int32, (blk, blk), 1)
                    s = jnp.where((kpos <= qpos) & (kpos >= pad), s, NEG)
                m_new = jnp.maximum(m, jnp.max(s, axis=1, keepdims=True))
                p = jnp.exp(s - m_new)
                alpha = jnp.exp(m - m_new)
                l = alpha * l + jnp.sum(p, axis=1, keepdims=True)
                acc = alpha * acc + jnp.dot(p.astype(BF16), vb, preferred_element_type=F32)
                return m_new, l, acc
            return step

        carry = (jnp.full((blk, 1), NEG, F32), jnp.zeros((blk, 1), F32), jnp.zeros((blk, HEAD_DIM), F32))
        carry = step_fn(True)(0, carry)
        carry = lax.fori_loop(1, i, step_fn(False), carry)
        m, l, acc = lax.cond(i > 0, lambda c: step_fn(True)(i, c), lambda c: c, carry)
        rowpos = i * blk + lax.broadcasted_iota(jnp.int32, (blk, 1), 0)
        o_ref[...] = jnp.where(rowpos >= pad, acc / l, 0.0)
        lse_ref[...] = jnp.broadcast_to(m + jnp.log(l), (blk, LANES))

    return pl.pallas_call(
        body, name=name, grid=(H, nq),
        in_specs=[pl.BlockSpec((blk, HEAD_DIM), lambda h, i: (i, h)),
                  pl.BlockSpec((T, HEAD_DIM), lambda h, i: (0, h)),
                  pl.BlockSpec((T, HEAD_DIM), lambda h, i: (0, h)),
                  pl.BlockSpec((blk, LANES), lambda h, i: (i, 0)),
                  pl.BlockSpec((None, nq, 1, blk), lambda h, i: (h, 0, 0, 0))],
        out_specs=[pl.BlockSpec((blk, HEAD_DIM), lambda h, i: (i, h)),
                   pl.BlockSpec((None, blk, LANES), lambda h, i: (h, i, 0))],
        out_shape=[jax.ShapeDtypeStruct((T, D), F32), jax.ShapeDtypeStruct((H, T, LANES), F32)],
        compiler_params=_cp(("parallel", "arbitrary")),
    )(q, k, v, c, ct)


def _attn_bwd(q, k, v, o, do, lse, c, ct, prev, *, blk, pad, name):
    T, D = q.shape
    H, nq = D // HEAD_DIM, T // blk
    scale = 1.0 / math.sqrt(HEAD_DIM)
    has_prev = prev is not None

    nt_dims = (((1,), (1,)), ((), ()))
    tn_dims = (((0,), (0,)), ((), ()))

    def body(*refs):
        q_ref, k_ref, v_ref, o_ref, do_ref, lse_ref, c_ref, ct_ref = refs[:8]
        pdk_ref, pdv_ref, pdc_ref = refs[8:11] if has_prev else (None, None, None)
        dq_ref, dk_ref, dv_ref, dct_ref = refs[-4:]
        h, j = pl.program_id(0), pl.program_id(1)

        @pl.when(j == 0)
        def _():
            dq_ref[...] = jnp.zeros_like(dq_ref)

        kb, vb = k_ref[...], v_ref[...]
        ck = ct_ref[...]

        def step_fn(masked):
            def step(i, carry):
                dk, dv, dck = carry
                off = pl.multiple_of(i * blk, blk)
                qb = q_ref[pl.ds(off, blk), :]
                dob = do_ref[pl.ds(off, blk), :]
                lse_i = lse_ref[pl.ds(off, blk), :][:, 0:1]
                cq = _pick_head(c_ref[pl.ds(off, blk), :], h)
                delta = jnp.sum(dob.astype(F32) * o_ref[pl.ds(off, blk), :], axis=1, keepdims=True)
                s = lax.dot_general(qb, kb, nt_dims, preferred_element_type=F32) * scale
                p = jnp.exp(s + (cq - ck) - lse_i)
                if masked:
                    qpos = i * blk + lax.broadcasted_iota(jnp.int32, (blk, blk), 0)
                    kpos = j * blk + lax.broadcasted_iota(jnp.int32, (blk, blk), 1)
                    p = jnp.where((kpos <= qpos) & (kpos >= pad), p, 0.0)
                dp = lax.dot_general(dob, vb, nt_dims, preferred_element_type=F32)
                ds = p * (dp - delta)
                pb, dsb = p.astype(BF16), ds.astype(BF16)
                dv = dv + lax.dot_general(pb, dob, tn_dims, preferred_element_type=F32)
                dk = dk + lax.dot_general(dsb, qb, tn_dims, preferred_element_type=F32)
                dck = dck - jnp.sum(ds, axis=0, keepdims=True)
                dq_ref[pl.ds(off, blk), :] += jnp.dot(dsb, kb, preferred_element_type=F32) * scale
                return dk, dv, dck
            return step

        carry = (jnp.zeros((blk, HEAD_DIM), F32), jnp.zeros((blk, HEAD_DIM), F32), jnp.zeros((1, blk), F32))
        carry = step_fn(True)(j, carry)
        below = lambda masked: (lambda c: lax.fori_loop(j + 1, nq, step_fn(masked), c))
        dk, dv, dck = lax.cond(j == 0, below(True), below(False), carry)
        dk = dk * scale
        if has_prev:
            dk, dv, dck = dk + pdk_ref[...], dv + pdv_ref[...], dck + pdc_ref[...]
        dk_ref[...] = dk
        dv_ref[...] = dv
        dct_ref[...] = dck

    col = pl.BlockSpec((T, HEAD_DIM), lambda h, j: (0, h))
    kblk = pl.BlockSpec((blk, HEAD_DIM), lambda h, j: (j, h))
    ctb = pl.BlockSpec((None, None, 1, blk), lambda h, j: (h, j, 0, 0))
    in_specs = [col, kblk, kblk, col, col,
                pl.BlockSpec((None, T, LANES), lambda h, j: (h, 0, 0)),
                pl.BlockSpec((T, LANES), lambda h, j: (0, 0)), ctb]
    operands = [q, k, v, o, do, lse, c, ct]
    if has_prev:
        in_specs += [kblk, kblk, ctb]
        operands += list(prev)
    return pl.pallas_call(
        body, name=name, grid=(H, nq), in_specs=in_specs, out_specs=[col, kblk, kblk, ctb],
        out_shape=[jax.ShapeDtypeStruct((T, D), F32), jax.ShapeDtypeStruct((T, D), F32),
                   jax.ShapeDtypeStruct((T, D), F32), jax.ShapeDtypeStruct((H, nq, 1, blk), F32)],
        compiler_params=_cp(("parallel", "arbitrary")),
    )(*operands)


HBM_SPEC = pl.BlockSpec(memory_space=pltpu.HBM)


def _place():
    x, y, c = lax.axis_index("x"), lax.axis_index("y"), lax.axis_index("c")
    chips = [(1 - x, y), (x, 1 - y), (1 - x, 1 - y)]
    return x, y, c, chips


def _remote(src, dst, send_sems, recv_sems, k, to):
    return pltpu.make_async_remote_copy(src_ref=src, dst_ref=dst, send_sem=send_sems.at[k],
                                        recv_sem=recv_sems.at[k], device_id=to, device_id_type=MESH)


def _all_gather(shards, specs, name):
    n = len(shards)

    def full_shape(s, axis):
        return (s.shape[0] * N_CHIPS, s.shape[1]) if axis == 0 else (s.shape[0], s.shape[1] * N_CHIPS)

    def body(*refs):
        srcs, outs = refs[:n], refs[n:2 * n]
        send_sems, recv_sems, local_sems = refs[2 * n:]
        x, y, c, chips = _place()
        sibling = (x, y, 1 - c)

        def region(t, chip, half):
            rs, cs = shards[t].shape
            q = 2 * chip[0] + chip[1]
            axis, split = specs[t]
            nrow = rs // 2 if half is not None else rs
            r0 = 0 if half is None else half * nrow
            if axis == 0:
                return outs[t].at[pl.ds(q * rs + r0, nrow), :]
            return outs[t].at[pl.ds(r0, nrow), pl.ds(pl.multiple_of(q * cs, cs), cs)]

        def piece(t, half):
            rs = shards[t].shape[0]
            if half is None:
                return srcs[t]
            return srcs[t].at[pl.ds(half * (rs // 2), rs // 2), :]

        local = [pltpu.make_async_copy(srcs[t], region(t, (x, y), None), local_sems.at[t]) for t in range(n)]
        for cp in local:
            cp.start()
        sends = []
        for t in range(n):
            half = c if specs[t][1] else None
            for j, chip in enumerate(chips):
                cp = _remote(piece(t, half), region(t, (x, y), half), send_sems, recv_sems, 6 * t + j, (*chip, c))
                cp.start()
                sends.append(cp)
        for t in range(n):
            half = c if specs[t][1] else None
            for j, chip in enumerate(chips):
                landed = region(t, chip, half)
                _remote(landed, landed, send_sems, recv_sems, 6 * t + j, (*chip, c)).wait_recv()
                if specs[t][1]:
                    cp = _remote(landed, landed, send_sems, recv_sems, 6 * t + 3 + j, sibling)
                    cp.start()
                    sends.append(cp)
        for t in range(n):
            if specs[t][1]:
                for j, chip in enumerate(chips):
                    got = region(t, chip, 1 - c)
                    _remote(got, got, send_sems, recv_sems, 6 * t + 3 + j, sibling).wait_recv()
        for cp in sends:
            cp.wait_send()
        for cp in local:
            cp.wait()

    return pl.pallas_call(
        body, name=name, in_specs=[HBM_SPEC] * n, out_specs=[HBM_SPEC] * n,
        out_shape=[jax.ShapeDtypeStruct(full_shape(s, specs[t][0]), s.dtype) for t, s in enumerate(shards)],
        scratch_shapes=[pltpu.SemaphoreType.DMA((6 * n,)), pltpu.SemaphoreType.DMA((6 * n,)),
                        pltpu.SemaphoreType.DMA((n,))],
        compiler_params=pltpu.CompilerParams(has_side_effects=True),
    )(*shards)


def _grad_view(g, axis):
    R, C = g.shape
    nq = N_CHIPS if axis == 0 else 1
    return g.reshape(nq, 2, R // (2 * nq), C)


def _swap_halves(views, name):
    n = len(views)

    def body(*refs):
        srcs, outs, send_sems, recv_sems = refs[:n], refs[n:2 * n], refs[2 * n], refs[2 * n + 1]
        x, y, c, _ = _place()
        cps = [_remote(srcs[t].at[:, 1 - c], outs[t], send_sems, recv_sems, t, (x, y, 1 - c)) for t in range(n)]
        for cp in cps:
            cp.start()
        for cp in cps:
            cp.wait()

    return pl.pallas_call(
        body, name=name, in_specs=[HBM_SPEC] * n, out_specs=[HBM_SPEC] * n,
        out_shape=[jax.ShapeDtypeStruct((v.shape[0],) + v.shape[2:], v.dtype) for v in views],
        scratch_shapes=[pltpu.SemaphoreType.DMA((n,)), pltpu.SemaphoreType.DMA((n,))],
        compiler_params=pltpu.CompilerParams(has_side_effects=True),
    )(*views)


def _add_half(view, got, c_idx, name):
    nq, _, Rh, C = view.shape
    tr = _tile(Rh, max(16, (2 * 1024 * 1024) // (C * 2)), 16)

    def body(c_ref, a_ref, b_ref, o_ref):
        o_ref[...] = (a_ref[...].astype(F32) + b_ref[...].astype(F32)).astype(BF16)

    grid_spec = pltpu.PrefetchScalarGridSpec(
        num_scalar_prefetch=1, grid=(nq, Rh // tr),
        in_specs=[pl.BlockSpec((None, None, tr, C), lambda q, i, c_ref: (q, c_ref[0], i, 0)),
                  pl.BlockSpec((None, tr, C), lambda q, i, c_ref: (q, i, 0))],
        out_specs=pl.BlockSpec((None, tr, C), lambda q, i, c_ref: (q, i, 0)))
    return pl.pallas_call(
        body, name=name, grid_spec=grid_spec, out_shape=jax.ShapeDtypeStruct((nq, Rh, C), BF16),
        compiler_params=_cp(("parallel", "parallel")),
    )(c_idx, view, got)


SEM_SPEC = pl.BlockSpec(memory_space=pltpu.SEMAPHORE)
ANY_SPEC = pl.BlockSpec(memory_space=pl.ANY)
TOKEN_SPEC = pl.BlockSpec(memory_space=pltpu.VMEM)
TOKEN = jax.ShapeDtypeStruct((8, LANES), F32)
SPLIT_COPY = pltpu.CompilerParams(has_side_effects=pltpu.SideEffectType.DATAFLOW_SIDE_EFFECTING)


def _region(ref, axis, chip, half):
    q = 2 * chip[0] + chip[1]
    if axis == 0:
        rs = ref.shape[0] // N_CHIPS
        return ref.at[pl.ds(q * rs + half * (rs // 2), rs // 2), :]
    rh, cs = ref.shape[0] // 2, ref.shape[1] // N_CHIPS
    return ref.at[pl.ds(half * rh, rh), pl.ds(pl.multiple_of(q * cs, cs), cs)]


def _gather_start(fulls, axes, after, name):
    n = len(fulls)

    def body(*refs):
        ins = refs[:n]
        token = refs[-1]
        send_sems, recv_sems = refs[n + 1], refs[n + 2]
        x, y, c, chips = _place()
        for t in range(n):
            mine = _region(ins[t], axes[t], (x, y), c)
            for j, chip in enumerate(chips):
                _remote(mine, mine, send_sems, recv_sems, 3 * t + j, (*chip, c)).start()
        token[...] = jnp.zeros_like(token)

    sems = pltpu.SemaphoreType.DMA((3 * n,))
    outs = pl.pallas_call(
        body, name=name, in_specs=[HBM_SPEC] * n + [ANY_SPEC],
        out_specs=[SEM_SPEC, SEM_SPEC] + [HBM_SPEC] * n + [TOKEN_SPEC],
        out_shape=[sems, sems] + [jax.ShapeDtypeStruct(f.shape, f.dtype) for f in fulls] + [TOKEN],
        input_output_aliases={t: 2 + t for t in range(n)}, compiler_params=SPLIT_COPY,
    )(*fulls, after)
    return outs[0], outs[1], list(outs[2:2 + n]), outs[-1]


def _gather_wait(send_sems, recv_sems, fulls, axes, after, name):
    n = len(fulls)

    def body(*refs):
        ins = refs[:n]
        send_sems, recv_sems = refs[n], refs[n + 1]
        x, y, c, chips = _place()
        for t in range(n):
            mine = _region(ins[t], axes[t], (x, y), c)
            for j, chip in enumerate(chips):
                _remote(mine, mine, send_sems, recv_sems, 3 * t + j, (*chip, c)).wait_send()
                theirs = _region(ins[t], axes[t], chip, c)
                _remote(theirs, theirs, send_sems, recv_sems, 3 * t + j, (*chip, c)).wait_recv()

    outs = pl.pallas_call(
        body, name=name, in_specs=[HBM_SPEC] * n + [SEM_SPEC, SEM_SPEC, ANY_SPEC], out_specs=[HBM_SPEC] * n,
        out_shape=[jax.ShapeDtypeStruct(f.shape, f.dtype) for f in fulls],
        input_output_aliases={t: t for t in range(n)}, compiler_params=SPLIT_COPY,
    )(*fulls, send_sems, recv_sems, after)
    return list(outs)


def _gather_forward(fulls, axes, name):
    n = len(fulls)

    def body(*refs):
        outs = refs[n:2 * n]
        send_sems, recv_sems = refs[2 * n], refs[2 * n + 1]
        x, y, c, chips = _place()
        sibling = (x, y, 1 - c)
        cps = []
        for t in range(n):
            for j, chip in enumerate(chips):
                landed = _region(outs[t], axes[t], chip, c)
                cps.append(_remote(landed, landed, send_sems, recv_sems, 3 * t + j, sibling))
        for cp in cps:
            cp.start()
        for t in range(n):
            for j, chip in enumerate(chips):
                got = _region(outs[t], axes[t], chip, 1 - c)
                _remote(got, got, send_sems, recv_sems, 3 * t + j, sibling).wait_recv()
        for cp in cps:
            cp.wait_send()

    outs = pl.pallas_call(
        body, name=name, in_specs=[HBM_SPEC] * n, out_specs=[HBM_SPEC] * n,
        out_shape=[jax.ShapeDtypeStruct(f.shape, f.dtype) for f in fulls],
        scratch_shapes=[pltpu.SemaphoreType.DMA((3 * n,)), pltpu.SemaphoreType.DMA((3 * n,))],
        input_output_aliases={t: t for t in range(n)},
        compiler_params=pltpu.CompilerParams(has_side_effects=True),
    )(*fulls)
    return list(outs)


def _shard_cols(s, axis):
    return s.shape[2] if axis == 0 else s.shape[2] // N_CHIPS


def _piece(ref, axis, chip):
    q = 2 * chip[0] + chip[1]
    if axis == 0:
        return ref.at[q]
    cs = ref.shape[2] // N_CHIPS
    return ref.at[0, :, pl.ds(pl.multiple_of(q * cs, cs), cs)]


def _scatter_start(sums, axes, after, name):
    n = len(sums)

    def body(*refs):
        ins = refs[:n]
        send_sems, recv_sems = refs[n + 1], refs[n + 2]
        lands = refs[2 * n + 3:3 * n + 3]
        token = refs[-1]
        x, y, c, chips = _place()
        for t in range(n):
            for j, chip in enumerate(chips):
                _remote(_piece(ins[t], axes[t], chip), lands[t].at[j], send_sems, recv_sems, 3 * t + j, (*chip, c)).start()
        token[...] = jnp.zeros_like(token)

    sems = pltpu.SemaphoreType.DMA((3 * n,))
    land_shapes = [jax.ShapeDtypeStruct((3, s.shape[1], _shard_cols(s, a)), s.dtype) for s, a in zip(sums, axes)]
    outs = pl.pallas_call(
        body, name=name, in_specs=[HBM_SPEC] * n + [ANY_SPEC],
        out_specs=[SEM_SPEC, SEM_SPEC] + [HBM_SPEC] * (2 * n) + [TOKEN_SPEC],
        out_shape=[sems, sems] + [jax.ShapeDtypeStruct(s.shape, s.dtype) for s in sums] + land_shapes + [TOKEN],
        input_output_aliases={t: 2 + t for t in range(n)}, compiler_params=SPLIT_COPY,
    )(*sums, after)
    return outs[0], outs[1], list(outs[2:2 + n]), list(outs[2 + n:2 + 2 * n]), outs[-1]


def _scatter_wait(send_sems, recv_sems, sums, lands, axes, after, name):
    n = len(sums)

    def body(*refs):
        ins, lnd = refs[:n], refs[n:2 * n]
        send_sems, recv_sems = refs[2 * n], refs[2 * n + 1]
        x, y, c, chips = _place()
        for t in range(n):
            for j, chip in enumerate(chips):
                cp = _remote(_piece(ins[t], axes[t], chip), lnd[t].at[j], send_sems, recv_sems, 3 * t + j, (*chip, c))
                cp.wait_send()
                cp.wait_recv()

    outs = pl.pallas_call(
        body, name=name, in_specs=[HBM_SPEC] * (2 * n) + [SEM_SPEC, SEM_SPEC, ANY_SPEC], out_specs=[HBM_SPEC] * (2 * n),
        out_shape=[jax.ShapeDtypeStruct(s.shape, s.dtype) for s in sums + lands],
        input_output_aliases={t: t for t in range(2 * n)}, compiler_params=SPLIT_COPY,
    )(*sums, *lands, send_sems, recv_sems, after)
    return list(outs[:n]), list(outs[n:])


def _sum_chips(own, axis, got, q_idx, c_idx, name):
    _, Rh, cc = got.shape
    tr = _tile(Rh, max(16, (1024 * 1024) // (cc * 2)), 16)

    def body(q_ref, c_ref, a_ref, b0, b1, b2, o_ref):
        f = lambda r: r[...].astype(F32)
        o_ref[...] = ((f(a_ref) + f(b0)) + f(b1)) + f(b2)

    if axis == 0:
        own_spec = pl.BlockSpec((None, tr, cc), lambda i, q, c: (q[0], i, 0))
    else:
        own_spec = pl.BlockSpec((None, tr, cc), lambda i, q, c: (0, i, q[0]))
    slot = lambda j: pl.BlockSpec((None, tr, cc), lambda i, q, c, j=j: (j, i, 0))
    grid_spec = pltpu.PrefetchScalarGridSpec(
        num_scalar_prefetch=2, grid=(Rh // tr,), in_specs=[own_spec, slot(0), slot(1), slot(2)],
        out_specs=pl.BlockSpec((None, tr, cc), lambda i, q, c: (c[0], i, 0)))
    return pl.pallas_call(
        body, name=name, grid_spec=grid_spec, out_shape=jax.ShapeDtypeStruct((2, Rh, cc), F32),
        compiler_params=_cp(("parallel",)),
    )(q_idx, c_idx, own, got, got, got)


def _join_halves(pairs, name):
    n = len(pairs)

    def body(*refs):
        outs = refs[n:2 * n]
        send_sems, recv_sems = refs[2 * n], refs[2 * n + 1]
        x, y, c, _ = _place()
        cps = [_remote(outs[t].at[c], outs[t].at[c], send_sems, recv_sems, t, (x, y, 1 - c)) for t in range(n)]
        for cp in cps:
            cp.start()
        for t in range(n):
            cps[t].wait_send()
            _remote(outs[t].at[1 - c], outs[t].at[1 - c], send_sems, recv_sems, t, (x, y, 1 - c)).wait_recv()

    outs = pl.pallas_call(
        body, name=name, in_specs=[HBM_SPEC] * n, out_specs=[HBM_SPEC] * n,
        out_shape=[jax.ShapeDtypeStruct(p.shape, p.dtype) for p in pairs],
        scratch_shapes=[pltpu.SemaphoreType.DMA((n,)), pltpu.SemaphoreType.DMA((n,))],
        input_output_aliases={t: t for t in range(n)},
        compiler_params=pltpu.CompilerParams(has_side_effects=True),
    )(*pairs)
    return list(outs)


def _reduce_begin(grads, axes, c_idx, after, tag):
    views = [_grad_view(g, a) for g, a in zip(grads, axes)]
    got = _swap_halves(views, f"rs_swap_{tag}")
    sums = [_add_half(v, p, c_idx, f"rs_add_{tag}_{t}") for t, (v, p) in enumerate(zip(views, got))]
    send_sems, recv_sems, sums, lands, token = _scatter_start(sums, axes, after, f"rs_chips_start_{tag}")
    return (send_sems, recv_sems, sums, lands, axes, tag), token


def _reduce_finish(pending, q_idx, c_idx, after):
    send_sems, recv_sems, sums, lands, axes, tag = pending
    sums, lands = _scatter_wait(send_sems, recv_sems, sums, lands, axes, after, f"rs_chips_wait_{tag}")
    pairs = [_sum_chips(s, a, r, q_idx, c_idx, f"rs_sum_{tag}_{t}")
             for t, (s, a, r) in enumerate(zip(sums, axes, lands))]
    joined = _join_halves(pairs, f"rs_join_{tag}")
    return [j.reshape(2 * j.shape[1], j.shape[2]) for j in joined]


def _all_reduce_small(buf, name):
    R, C = buf.shape

    def body(b_ref, o_ref, slots, send_sems, recv_sems):
        x, y, c, _ = _place()
        me = 4 * x + 2 * y + c
        cps = []
        for k in range(1, 8):
            to = (x ^ (k >> 2), y ^ ((k >> 1) & 1), c ^ (k & 1))
            cps.append(pltpu.make_async_remote_copy(
                src_ref=b_ref, dst_ref=slots.at[me], send_sem=send_sems.at[k - 1], recv_sem=recv_sems.at[me],
                device_id=to, device_id_type=MESH))
        for cp in cps:
            cp.start()
        slots[me] = b_ref[...]
        for k in range(1, 8):
            src = me ^ k
            pltpu.make_async_remote_copy(
                src_ref=b_ref, dst_ref=slots.at[src], send_sem=send_sems.at[k - 1], recv_sem=recv_sems.at[src],
                device_id=(x, y, c), device_id_type=MESH).wait_recv()
        for cp in cps:
            cp.wait_send()
        total = slots[0]
        for d in range(1, 8):
            total = total + slots[d]
        o_ref[...] = total

    vm = pl.BlockSpec(memory_space=pltpu.VMEM)
    return pl.pallas_call(
        body, name=name, in_specs=[vm], out_specs=vm, out_shape=jax.ShapeDtypeStruct((R, C), F32),
        scratch_shapes=[pltpu.VMEM((8, R, C), F32), pltpu.SemaphoreType.DMA((7,)), pltpu.SemaphoreType.DMA((8,))],
        compiler_params=pltpu.CompilerParams(has_side_effects=True, vmem_limit_bytes=VMEM_LIMIT),
    )(buf)


def kernel(x, meta, a_norm, a_w_in, a_conv, a_w_out, kv_norm, w_kv, k_norm, w_f, b_f, b_norm, b_w_q, b_q_norm, b_w_o, ffn_norm, ffn_w_gu, ffn_w_down, loss_target, m_meta, m_a_norm, m_a_w_in, m_a_conv, m_a_w_out, m_kv_norm, m_w_kv, m_k_norm, m_w_f, m_b_f, m_b_norm, m_b_w_q, m_b_q_norm, m_b_w_o, m_ffn_norm, m_ffn_w_gu, m_ffn_w_down, v_meta, v_a_norm, v_a_w_in, v_a_conv, v_a_w_out, v_kv_norm, v_w_kv, v_k_norm, v_w_f, v_b_f, v_b_norm, v_b_w_q, v_b_q_norm, v_b_w_o, v_ffn_norm, v_ffn_w_gu, v_ffn_w_down):
    SEQ, D = x.shape[1], x.shape[2]
    n_meta = meta.shape[0]
    pad = BLOCK - n_meta
    first = pad + n_meta
    T = first + SEQ
    H = D // HEAD_DIM
    Ds = D // N_CHIPS
    n_a, n_b, depth = a_w_in.shape[0], b_w_q.shape[0], ffn_norm.shape[0]
    blk = _tile(T, 384, BLOCK)
    cx, cy, cc = lax.axis_index("x"), lax.axis_index("y"), lax.axis_index("c")
    q_me = 2 * cx + cy
    c_idx = jnp.reshape(cc, (1,)).astype(jnp.int32)
    q_idx = jnp.reshape(q_me, (1,)).astype(jnp.int32)
    rows8 = lambda v: jnp.pad(v, ((0, -v.shape[0] % 8), (0, 0)))

    col_parts = [meta, a_norm, a_conv.reshape(3 * n_a, Ds)]
    col_pack = jnp.concatenate([rows8(p) for p in col_parts], axis=0)
    col_full, w_f_full = _all_gather([col_pack, w_f], [(1, False), (0, False)], "ag_small")
    col_offs = [sum(rows8(p).shape[0] for p in col_parts[:i]) for i in range(3)]
    meta_f = col_full[:n_meta]
    a_norm_f = col_full[col_offs[1]:col_offs[1] + n_a]
    a_conv_f = col_full[col_offs[2]:col_offs[2] + 3 * n_a].reshape(n_a, 3, D)
    w_fp = jnp.pad(w_f_full, ((0, 0), (0, LANES - H))).astype(BF16)
    b_fp = jnp.pad(b_f, (0, LANES - H)).reshape(1, LANES)

    stages = []
    for l in range(n_a):
        stages += [[(a_w_in, l, 1), (a_w_out, l, 0)], [(ffn_w_gu, l, 1), (ffn_w_down, l, 0)]]
    for j in range(n_b):
        stages += [[(b_w_q, j, 0), (b_w_o, j, 0)], [(ffn_w_gu, n_a + j, 1), (ffn_w_down, n_a + j, 0)]]
    stages[2 * n_a].append((w_kv[None], 0, 1))

    def gather_begin(k, after):
        axes = [ax for _, _, ax in stages[k]]
        fulls = [_cast_into_full(w, l, ax, q_idx, f"cast_{k}_{i}") for i, (w, l, ax) in enumerate(stages[k])]
        send_sems, recv_sems, fulls, token = _gather_start(fulls, axes, after, f"ag_start_{k}")
        return (send_sems, recv_sems, fulls, axes, k), token

    def gather_end(handle, after):
        send_sems, recv_sems, fulls, axes, k = handle
        fulls = _gather_wait(send_sems, recv_sems, fulls, axes, after, f"ag_wait_{k}")
        return _gather_forward(fulls, axes, f"ag_forward_{k}")

    handle, _ = gather_begin(0, col_full)
    arrived = [gather_end(handle, col_full)]
    gathering = []

    def enter_segment():
        w = arrived[-1]
        after, token = w[0], None
        while len(gathering) < 2 and len(arrived) + len(gathering) < len(stages):
            handle, token = gather_begin(len(arrived) + len(gathering), after)
            gathering.append(handle)
            after = token
        return w, token

    def leave_segment(h_out):
        if gathering:
            arrived.append(gather_end(gathering.pop(0), h_out))

    h = jnp.concatenate([jnp.zeros((pad, D), F32), meta_f, x[0]], axis=0)
    saved = []

    def ffn_fwd(h, layer):
        (w_gu, w_down), token = enter_segment()
        xn = _rms_fwd(h, ffn_norm[layer:layer + 1], f"ffn_norm_{layer}", dep=token)
        z = _matmul(xn, w_gu, mode="nn", out_dtype=BF16, name=f"ffn_gu_{layer}", out_parts=2)
        act = _swiglu_fwd(z, f"swiglu_{layer}")
        out = _matmul(act, w_down, mode="nn", out_dtype=F32, name=f"ffn_down_{layer}", res=h)
        leave_segment(out)
        return out, (h, xn, z, act), (w_gu, w_down)

    wa, wb = [], []
    for l in range(n_a):
        (w_in, w_out), token = enter_segment()
        xn = _rms_fwd(h, a_norm_f[l:l + 1], f"a_norm_{l}", dep=token)
        z = _matmul(xn, w_in, mode="nn", out_dtype=BF16, name=f"a_in_{l}", out_parts=3)
        y = _gate_fwd(z, a_conv_f[l], f"a_gate_{l}")
        h2 = _matmul(y, w_out, mode="nn", out_dtype=F32, name=f"a_out_{l}", res=h)
        leave_segment(h2)
        h3, ffn_saved, w_ffn = ffn_fwd(h2, l)
        saved.append((h, xn, z, y, ffn_saved))
        wa.append((w_in, w_out) + w_ffn)
        h = h3

    for j in range(n_b):
        layer = n_a + j
        w_mix, token = enter_segment()
        w_q, w_o = w_mix[:2]
        if j == 0:
            h_kv, w_kv_b = h, w_mix[2]
            xkv = _rms_fwd(h, kv_norm.reshape(1, D), "kv_norm", dep=token)
            token = None
            kvz = _matmul(xkv, w_kv_b, mode="nn", out_dtype=F32, name="kv_proj", out_parts=2)
            k_n = _headnorm_fwd(kvz, 0, k_norm.reshape(1, HEAD_DIM), "k_headnorm")
            v_b = _cast_part(kvz, 1, "v_cast")
            pre = _matmul(xkv, w_fp, mode="nn", out_dtype=F32, name="f_proj")
            c_cum = _logf_cumsum(pre, b_fp, pad, "logf_cumsum")
            c_t = c_cum[:, :H].T.reshape(H, T // blk, 1, blk)
        xn = _rms_fwd(h, b_norm[j:j + 1], f"b_norm_{j}", dep=token)
        qz = _matmul(xn, w_q, mode="nn", out_dtype=F32, name=f"b_q_{j}")[None]
        q_n = _headnorm_fwd(qz, 0, b_q_norm[j:j + 1], f"q_headnorm_{j}")
        o, lse = _attn_fwd(q_n, k_n, v_b, c_cum, c_t, blk=blk, pad=pad, name=f"attn_fwd_{j}")
        h2 = _matmul(o, w_o, mode="nn", out_dtype=F32, name=f"b_o_{j}", res=h)
        leave_segment(h2)
        h3, ffn_saved, w_ffn = ffn_fwd(h2, layer)
        saved.append((h, xn, qz, q_n, o, lse, ffn_saved))
        wb.append((w_q, w_o) + w_ffn)
        h = h3

    dh, dhb, loss_blk = _loss_head(h, loss_target[0], first, "loss_head")
    loss = lax.psum(loss_blk[0, 0], ("x", "y", "c"))

    shards = {}
    reducing = []

    def reduce_later(names, grads, axes, tag, done):
        after = c_idx
        if len(reducing) == 2:
            prev_names, pending = reducing.pop(0)
            got = _reduce_finish(pending, q_idx, c_idx, done)
            shards.update(zip(prev_names, got))
            after = got[0]
        pending, token = _reduce_begin(grads, axes, c_idx, after, tag)
        reducing.append((names, pending))
        return token

    def ffn_bwd(dh, dhb, layer, w_gu, w_down, ffn_saved, dep):
        h_in, xn, z, act = ffn_saved
        da = _matmul(dhb, w_down, mode="nt", out_dtype=BF16, name=f"ffn_down_dx_{layer}", dep=dep)
        g_down = _matmul(act.T, dhb, mode="nn", out_dtype=BF16, name=f"ffn_down_dw_{layer}")
        dz = _swiglu_bwd(z, da, f"swiglu_bwd_{layer}")
        dxn = _matmul(dz, w_gu, mode="nt", out_dtype=F32, name=f"ffn_gu_dx_{layer}")
        g_gu = _matmul(xn.T, dz, mode="nn", out_dtype=BF16, name=f"ffn_gu_dw_{layer}")
        dh, dhb, dg = _rms_bwd(h_in, ffn_norm[layer:layer + 1], dxn, dh, f"ffn_norm_bwd_{layer}")
        token = reduce_later([("ffn_w_gu", layer), ("ffn_w_down", layer)], [g_gu, g_down], [1, 0], f"f{layer}", dh)
        return dh, dhb, dg, token

    d_ffn_norm, d_b_norm, d_q_norm, d_a_norm, d_a_conv = {}, {}, {}, {}, {}
    kv_prev = None
    token = None
    for j in reversed(range(n_b)):
        layer = n_a + j
        w_q, w_o, w_gu, w_down = wb[j]
        h_in, xn, qz, q_n, o, lse, ffn_saved = saved[layer]
        dh, dhb, d_ffn_norm[layer], token = ffn_bwd(dh, dhb, layer, w_gu, w_down, ffn_saved, token)
        do = _matmul(dhb, w_o, mode="nt", out_dtype=BF16, name=f"b_o_dx_{j}", dep=token)
        g_o = _matmul(o.astype(BF16).T, dhb, mode="nn", out_dtype=BF16, name=f"b_o_dw_{j}")
        dq, dk, dv, dct = _attn_bwd(q_n, k_n, v_b, o, do, lse, c_cum, c_t, kv_prev, blk=blk, pad=pad,
                                    name=f"attn_bwd_{j}")
        kv_prev = (dk, dv, dct)
        dqz, d_q_norm[j] = _headnorm_bwd(qz, 0, b_q_norm[j:j + 1], dq, f"q_headnorm_bwd_{j}")
        dxn = _matmul(dqz, w_q, mode="nt", out_dtype=F32, name=f"b_q_dx_{j}")
        g_q = _matmul(xn.T, dqz, mode="nn", out_dtype=BF16, name=f"b_q_dw_{j}")
        dh, dhb, d_b_norm[j] = _rms_bwd(h_in, b_norm[j:j + 1], dxn, dh, f"b_norm_bwd_{j}")
        if j > 0:
            token = reduce_later([("b_w_q", j), ("b_w_o", j)], [g_q, g_o], [0, 0], f"b{j}", dh)

    dk, dv, dct = kv_prev
    dkz, d_k_norm = _headnorm_bwd(kvz, 0, k_norm.reshape(1, HEAD_DIM), dk, "k_headnorm_bwd")
    dvz = _cast_part(dv[None], 0, "dv_cast")
    dkv = jnp.stack([dkz, dvz])
    dc = jnp.pad(dct.reshape(H, T).T, ((0, 0), (0, LANES - H)))
    dpre, d_b_f = _logf_bwd(pre, b_fp, dc, pad, "logf_bwd")
    dxkv = _matmul(dkv, w_kv_b, mode="nt", out_dtype=F32, name="kv_proj_dx")
    dxkv = _matmul(dpre, w_fp, mode="nt", out_dtype=F32, name="f_proj_dx", res=dxkv)
    xkv_t = xkv.T
    g_kv = _matmul(xkv_t, dkv, mode="nn", out_dtype=BF16, name="kv_proj_dw")
    d_w_f = _matmul(xkv_t, dpre, mode="nn", out_dtype=F32, name="f_proj_dw")
    dh, dhb, d_kv_norm = _rms_bwd(h_kv, kv_norm.reshape(1, D), dxkv, dh, "kv_norm_bwd")
    token = reduce_later([("b_w_q", 0), ("b_w_o", 0), ("w_kv", 0)], [g_q, g_o, g_kv], [0, 0, 1], "b0", dh)

    for l in reversed(range(n_a)):
        w_in, w_out, w_gu, w_down = wa[l]
        h_in, xn, z, y, ffn_saved = saved[l]
        dh, dhb, d_ffn_norm[l], token = ffn_bwd(dh, dhb, l, w_gu, w_down, ffn_saved, token)
        dy = _matmul(dhb, w_out, mode="nt", out_dtype=F32, name=f"a_out_dx_{l}", dep=token)
        g_out = _matmul(y.T, dhb, mode="nn", out_dtype=BF16, name=f"a_out_dw_{l}")
        dz, d_a_conv[l] = _gate_bwd(z, a_conv_f[l], dy, f"a_gate_bwd_{l}")
        dxn = _matmul(dz, w_in, mode="nt", out_dtype=F32, name=f"a_in_dx_{l}")
        g_in = _matmul(xn.T, dz, mode="nn", out_dtype=BF16, name=f"a_in_dw_{l}")
        dh, dhb, d_a_norm[l] = _rms_bwd(h_in, a_norm_f[l:l + 1], dxn, dh, f"a_norm_bwd_{l}")
        token = reduce_later([("a_w_in", l), ("a_w_out", l)], [g_in, g_out], [1, 0], f"a{l}", dh)

    for prev_names, pending in reducing:
        shards.update(zip(prev_names, _reduce_finish(pending, q_idx, c_idx, dh)))
    grad_x = dh[first:][None]

    widen = lambda v: jnp.pad(v, ((0, 0), (0, D - v.shape[1])))
    groups = [
        [dh[pad:first]],
        [d_a_norm[l] for l in range(n_a)],
        [d_a_conv[l] for l in range(n_a)],
        [d_kv_norm],
        [widen(d_k_norm)],
        [d_w_f[:, :H].T],
        [widen(d_b_f)],
        [d_b_norm[j] for j in range(n_b)],
        [widen(d_q_norm[j]) for j in range(n_b)],
        [d_ffn_norm[l] for l in range(depth)],
    ]
    pack = jnp.concatenate([rows8(p) for g in groups for p in g], axis=0)
    red = _all_reduce_small(pack, "ar_small")
    taken, off = [], 0
    for g in groups:
        r, rp = g[0].shape[0], rows8(g[0]).shape[0]
        taken.append(red[off:off + len(g) * rp].reshape(len(g), rp, D)[:, :r].reshape(len(g) * r, D))
        off += len(g) * rp
    take = lambda i: taken[i]
    mine = lambda a: lax.dynamic_slice_in_dim(a, q_me * Ds, Ds, axis=1)
    layers_of = lambda name, n: jnp.stack([shards[(name, l)] for l in range(n)])
    grads = {
        "meta": mine(take(0)),
        "a_norm": mine(take(1)),
        "a_w_in": layers_of("a_w_in", n_a),
        "a_conv": mine(take(2)).reshape(n_a, 3, Ds),
        "a_w_out": layers_of("a_w_out", n_a),
        "kv_norm": take(3).reshape(D),
        "w_kv": shards[("w_kv", 0)],
        "k_norm": take(4)[0, :HEAD_DIM],
        "w_f": mine(take(5)).T,
        "b_f": take(6)[0, :H],
        "b_norm": take(7),
        "b_w_q": layers_of("b_w_q", n_b),
        "b_q_norm": take(8)[:, :HEAD_DIM],
        "b_w_o": layers_of("b_w_o", n_b),
        "ffn_norm": take(9),
        "ffn_w_gu": layers_of("ffn_w_gu", depth),
        "ffn_w_down": layers_of("ffn_w_down", depth),
    }
    weights = dict(meta=meta, a_norm=a_norm, a_w_in=a_w_in, a_conv=a_conv, a_w_out=a_w_out, kv_norm=kv_norm, w_kv=w_kv,
                   k_norm=k_norm, w_f=w_f, b_f=b_f, b_norm=b_norm, b_w_q=b_w_q, b_q_norm=b_q_norm, b_w_o=b_w_o,
                   ffn_norm=ffn_norm, ffn_w_gu=ffn_w_gu, ffn_w_down=ffn_w_down)
    m_in = dict(meta=m_meta, a_norm=m_a_norm, a_w_in=m_a_w_in, a_conv=m_a_conv, a_w_out=m_a_w_out, kv_norm=m_kv_norm,
                w_kv=m_w_kv, k_norm=m_k_norm, w_f=m_w_f, b_f=m_b_f, b_norm=m_b_norm, b_w_q=m_b_w_q,
                b_q_norm=m_b_q_norm, b_w_o=m_b_w_o, ffn_norm=m_ffn_norm, ffn_w_gu=m_ffn_w_gu, ffn_w_down=m_ffn_w_down)
    v_in = dict(meta=v_meta, a_norm=v_a_norm, a_w_in=v_a_w_in, a_conv=v_a_conv, a_w_out=v_a_w_out, kv_norm=v_kv_norm,
                w_kv=v_w_kv, k_norm=v_k_norm, w_f=v_w_f, b_f=v_b_f, b_norm=v_b_norm, b_w_q=v_b_w_q,
                b_q_norm=v_b_q_norm, b_w_o=v_b_w_o, ffn_norm=v_ffn_norm, ffn_w_gu=v_ffn_w_gu, ffn_w_down=v_ffn_w_down)

    deltas, new_m, new_v = {}, {}, {}
    for name, w in weights.items():
        shape = w.shape
        two_d = (1, shape[0]) if w.ndim == 1 else (math.prod(shape[:-1]), shape[-1])
        r2 = lambda a: a.reshape(two_d)
        d_, m_, v_ = _adamw(r2(w), r2(grads[name]), r2(m_in[name]), r2(v_in[name]), f"adamw_{name}")
        deltas[name], new_m[name], new_v[name] = d_.reshape(shape), m_.reshape(shape), v_.reshape(shape)
        grads[name] = grads[name].reshape(shape)

    names = list(weights)
    return (loss, grad_x, *[grads[n] for n in names], *[deltas[n] for n in names],
            *[new_m[n] for n in names], *[new_v[n] for n in names])
```

```python
import functools
import math

import jax
import jax.numpy as jnp
from jax import lax
from jax.experimental import pallas as pl
from jax.experimental.pallas import tpu as pltpu

F32 = jnp.float32
BF16 = jnp.bfloat16
HEAD_DIM = 128
BLOCK = 128
LANES = 128
EPS = 1e-6
NEG = -1e30
ADAM_LR, ADAM_B1, ADAM_B2, ADAM_EPS, ADAM_WD, ADAM_STEP = 0.001, 0.9, 0.999, 1e-08, 0.01, 10
VMEM_LIMIT = 56 * 1024 * 1024
TILE_BUDGET = 40 * 1024 * 1024
MESH = pl.DeviceIdType.MESH
N_CHIPS = 4


def _tile(n, target, align):
    best = None
    for d in range(align, min(n, target) + 1, align):
        if n % d == 0:
            best = d
    return best if best is not None else n


def _cp(sem):
    return pltpu.CompilerParams(dimension_semantics=sem, vmem_limit_bytes=VMEM_LIMIT)


def _matmul(a, b, *, mode, out_dtype, name, res=None, out_parts=1, dep=None):
    a_parts = a.shape[0] if a.ndim == 3 else 1
    b_parts = b.shape[0] if b.ndim == 3 else 1
    if mode == "tn":
        K, M = a.shape
        Kp = K
    else:
        M, Kp = a.shape[-2:]
        K = Kp * a_parts
    N = b.shape[0] if mode == "nt" else b.shape[-1] * b_parts
    Np = N // max(b_parts, out_parts)
    tm = _tile(M, 1024, LANES) if mode == "tn" else _tile(M, 1056, 16)
    tn = _tile(Np, 1536, LANES)
    tk = _tile(Kp, 1056, 16) if mode == "tn" else _tile(Kp, 2048, LANES)
    ab, bb, ob = a.dtype.itemsize, b.dtype.itemsize, jnp.dtype(out_dtype).itemsize

    def vmem(tm_):
        blocks = 2 * (tm_ * tk * ab + tk * tn * bb + tm_ * tn * ob + (tm_ * tn * 4 if res is not None else 0))
        temps = tm_ * tn * 8 + (tm_ * tk * 2 if ab == 4 else 0) + (tk * tn * 2 if bb == 4 else 0)
        return blocks + temps

    while vmem(tm) > TILE_BUDGET and tm > 256:
        tm = _tile(M, tm // 2, LANES if mode == "tn" else 16)
    ni, nj, nk = M // tm, N // tn, K // tk
    nkp, njp = Kp // tk, Np // tn

    if mode == "tn":
        a_spec = pl.BlockSpec((tk, tm), lambda i, j, k: (k, i))
    elif a_parts > 1:
        a_spec = pl.BlockSpec((None, tm, tk), lambda i, j, k: (k // nkp, i, k % nkp))
    else:
        a_spec = pl.BlockSpec((tm, tk), lambda i, j, k: (i, k))
    if mode == "nt":
        b_spec = pl.BlockSpec((tn, tk), lambda i, j, k: (j, k))
    elif b_parts > 1:
        b_spec = pl.BlockSpec((None, tk, tn), lambda i, j, k: (j // njp, k, j % njp))
    else:
        b_spec = pl.BlockSpec((tk, tn), lambda i, j, k: (k, j))
    in_specs = [a_spec, b_spec]
    operands = [a, b]
    if res is not None:
        in_specs.append(pl.BlockSpec((tm, tn), lambda i, j, k: (i, j)))
        operands.append(res)
    if dep is not None:
        in_specs.append(pl.BlockSpec((8, LANES), lambda i, j, k: (0, 0)))
        operands.append(dep)
    n_in = len(operands)
    if out_parts > 1:
        out_spec = pl.BlockSpec((None, tm, tn), lambda i, j, k: (j // njp, i, j % njp))
        out_shape = jax.ShapeDtypeStruct((out_parts, M, Np), out_dtype)
    else:
        out_spec = pl.BlockSpec((tm, tn), lambda i, j, k: (i, j))
        out_shape = jax.ShapeDtypeStruct((M, N), out_dtype)
    dims = {"nn": (((1,), (0,)), ((), ())), "nt": (((1,), (1,)), ((), ())), "tn": (((0,), (0,)), ((), ()))}[mode]
    has_res = res is not None

    def body(*refs):
        a_ref, b_ref = refs[0], refs[1]
        res_ref = refs[2] if has_res else None
        o_ref = refs[n_in]
        d = lax.dot_general(a_ref[...].astype(BF16), b_ref[...].astype(BF16), dims, preferred_element_type=F32)
        if nk == 1:
            if has_res:
                d = d + res_ref[...]
            o_ref[...] = d.astype(out_dtype)
        else:
            acc_ref = refs[-1]
            k = pl.program_id(2)

            @pl.when(k == 0)
            def _():
                acc_ref[...] = d

            @pl.when(k > 0)
            def _():
                acc_ref[...] += d

            @pl.when(k == nk - 1)
            def _():
                r = acc_ref[...]
                if has_res:
                    r = r + res_ref[...]
                o_ref[...] = r.astype(out_dtype)

    return pl.pallas_call(
        body, name=name, grid=(ni, nj, nk), in_specs=in_specs, out_specs=out_spec, out_shape=out_shape,
        scratch_shapes=[pltpu.VMEM((tm, tn), F32)] if nk > 1 else [],
        compiler_params=_cp(("parallel", "parallel", "arbitrary")),
    )(*operands)


def _cast_into_full(w3, layer, axis, q_idx, name):
    _, R, C = w3.shape
    tr = _tile(R, max(16, (4 * 1024 * 1024) // (C * 4)), 16)
    nb = R // tr

    def body(q_ref, w_ref, o_ref):
        o_ref[...] = w_ref[...].astype(BF16)

    if axis == 0:
        out_spec = pl.BlockSpec((tr, C), lambda i, q_ref: (q_ref[0] * nb + i, 0))
        full = (N_CHIPS * R, C)
    else:
        out_spec = pl.BlockSpec((tr, C), lambda i, q_ref: (i, q_ref[0]))
        full = (R, N_CHIPS * C)
    grid_spec = pltpu.PrefetchScalarGridSpec(
        num_scalar_prefetch=1, grid=(nb,),
        in_specs=[pl.BlockSpec((None, tr, C), lambda i, q_ref: (layer, i, 0))], out_specs=out_spec)
    return pl.pallas_call(
        body, name=name, grid_spec=grid_spec, out_shape=jax.ShapeDtypeStruct(full, BF16),
        compiler_params=_cp(("parallel",)),
    )(q_idx, w3)


def _rms_fwd(h, g, name, dep=None):
    T, D = h.shape
    tr = _tile(T, 384, LANES)

    def body(h_ref, g_ref, *rest):
        o_ref, ot_ref = rest[-2:]
        x = h_ref[...]
        r = lax.rsqrt(jnp.mean(x * x, axis=-1, keepdims=True) + EPS)
        y = x * r * g_ref[...]
        o_ref[...] = y.astype(BF16)
        ot_ref[...] = y.T.astype(BF16)

    in_specs = [pl.BlockSpec((tr, D), lambda i: (i, 0)), pl.BlockSpec((1, D), lambda i: (0, 0))]
    operands = [h, g]
    if dep is not None:
        in_specs.append(pl.BlockSpec((8, LANES), lambda i: (0, 0)))
        operands.append(dep)
    return pl.pallas_call(
        body, name=name, grid=(T // tr,), in_specs=in_specs,
        out_specs=[pl.BlockSpec((tr, D), lambda i: (i, 0)), pl.BlockSpec((D, tr), lambda i: (0, i))],
        out_shape=[jax.ShapeDtypeStruct((T, D), BF16), jax.ShapeDtypeStruct((D, T), BF16)],
        compiler_params=_cp(("parallel",)),
    )(*operands)


def _rms_bwd(h, g, dxn, dh, name):
    T, D = h.shape
    tr = _tile(T, 264, 16)

    def body(h_ref, g_ref, dxn_ref, dh_ref, o_ref, ob_ref, dg_ref):
        x = h_ref[...]
        r = lax.rsqrt(jnp.mean(x * x, axis=-1, keepdims=True) + EPS)
        xh = x * r
        dy = dxn_ref[...]
        dxh = dy * g_ref[...]
        dx = r * (dxh - xh * jnp.mean(dxh * xh, axis=-1, keepdims=True))
        out = dh_ref[...] + dx
        o_ref[...] = out
        ob_ref[...] = out.astype(BF16)
        part = jnp.sum(dy * xh, axis=0, keepdims=True)

        @pl.when(pl.program_id(0) == 0)
        def _():
            dg_ref[...] = part

        @pl.when(pl.program_id(0) > 0)
        def _():
            dg_ref[...] += part

    row = pl.BlockSpec((tr, D), lambda i: (i, 0))
    vec = pl.BlockSpec((1, D), lambda i: (0, 0))
    return pl.pallas_call(
        body, name=name, grid=(T // tr,), in_specs=[row, vec, row, row], out_specs=[row, row, vec],
        out_shape=[jax.ShapeDtypeStruct((T, D), F32), jax.ShapeDtypeStruct((T, D), BF16),
                   jax.ShapeDtypeStruct((1, D), F32)],
        compiler_params=_cp(("arbitrary",)),
    )(h, g, dxn, dh)


def _shift_down(u, n, rows):
    return jnp.where(rows >= n, pltpu.roll(u, n, 0), 0.0)


def _shift_up(u, n, rows, total):
    return jnp.where(rows < total - n, pltpu.roll(u, total - n, 0), 0.0)


def _gate_fwd(z, conv_w, name):
    _, T, D = z.shape
    tc = LANES

    def body(b_ref, c_ref, h_ref, w_ref, y_ref, yt_ref):
        rows = lax.broadcasted_iota(jnp.int32, (T, tc), 0)
        u = c_ref[...].astype(F32) * h_ref[...].astype(F32)
        w0, w1, w2 = w_ref[0:1, :], w_ref[1:2, :], w_ref[2:3, :]
        conv = u * w2 + _shift_down(u, 1, rows) * w1 + _shift_down(u, 2, rows) * w0
        y = b_ref[...].astype(F32) * conv
        y_ref[...] = y.astype(BF16)
        yt_ref[...] = y.T.astype(BF16)

    part = lambda p: pl.BlockSpec((None, T, tc), lambda j, p=p: (p, 0, j))
    return pl.pallas_call(
        body, name=name, grid=(D // tc,),
        in_specs=[part(0), part(1), part(2), pl.BlockSpec((3, tc), lambda j: (0, j))],
        out_specs=[pl.BlockSpec((T, tc), lambda j: (0, j)), pl.BlockSpec((tc, T), lambda j: (j, 0))],
        out_shape=[jax.ShapeDtypeStruct((T, D), BF16), jax.ShapeDtypeStruct((D, T), BF16)],
        compiler_params=_cp(("parallel",)),
    )(z, z, z, conv_w)


def _gate_bwd(z, conv_w, dy, name):
    _, T, D = z.shape
    tc = LANES

    def body(b_ref, c_ref, h_ref, w_ref, dy_ref, dz_ref, dw_ref):
        rows = lax.broadcasted_iota(jnp.int32, (T, tc), 0)
        cg, hh = c_ref[...].astype(F32), h_ref[...].astype(F32)
        u = cg * hh
        w0, w1, w2 = w_ref[0:1, :], w_ref[1:2, :], w_ref[2:3, :]
        s1, s2 = _shift_down(u, 1, rows), _shift_down(u, 2, rows)
        g = dy_ref[...]
        dz_ref[0] = (g * (u * w2 + s1 * w1 + s2 * w0)).astype(BF16)
        dconv = g * b_ref[...].astype(F32)
        dw_ref[0:1, :] = jnp.sum(dconv * s2, axis=0, keepdims=True)
        dw_ref[1:2, :] = jnp.sum(dconv * s1, axis=0, keepdims=True)
        dw_ref[2:3, :] = jnp.sum(dconv * u, axis=0, keepdims=True)
        du = dconv * w2 + _shift_up(dconv, 1, rows, T) * w1 + _shift_up(dconv, 2, rows, T) * w0
        dz_ref[1] = (du * hh).astype(BF16)
        dz_ref[2] = (du * cg).astype(BF16)

    part = lambda p: pl.BlockSpec((None, T, tc), lambda j, p=p: (p, 0, j))
    return pl.pallas_call(
        body, name=name, grid=(D // tc,),
        in_specs=[part(0), part(1), part(2), pl.BlockSpec((3, tc), lambda j: (0, j)),
                  pl.BlockSpec((T, tc), lambda j: (0, j))],
        out_specs=[pl.BlockSpec((3, T, tc), lambda j: (0, 0, j)), pl.BlockSpec((3, tc), lambda j: (0, j))],
        out_shape=[jax.ShapeDtypeStruct((3, T, D), BF16), jax.ShapeDtypeStruct((3, D), F32)],
        compiler_params=_cp(("parallel",)),
    )(z, z, z, conv_w, dy)


def _swiglu_fwd(z, name):
    _, T, Fd = z.shape
    tr, tc = _tile(T, 384, LANES), _tile(Fd, 1408, LANES)

    def body(g_ref, u_ref, o_ref, ot_ref):
        g = g_ref[...].astype(F32)
        a = g * jax.nn.sigmoid(g) * u_ref[...].astype(F32)
        o_ref[...] = a.astype(BF16)
        ot_ref[...] = a.T.astype(BF16)

    part = lambda p: pl.BlockSpec((None, tr, tc), lambda i, j, p=p: (p, i, j))
    return pl.pallas_call(
        body, name=name, grid=(T // tr, Fd // tc), in_specs=[part(0), part(1)],
        out_specs=[pl.BlockSpec((tr, tc), lambda i, j: (i, j)), pl.BlockSpec((tc, tr), lambda i, j: (j, i))],
        out_shape=[jax.ShapeDtypeStruct((T, Fd), BF16), jax.ShapeDtypeStruct((Fd, T), BF16)],
        compiler_params=_cp(("parallel", "parallel")),
    )(z, z)


def _swiglu_bwd(z, da, name):
    _, T, Fd = z.shape
    tr, tc = _tile(T, 528, 16), _tile(Fd, 1408, LANES)

    def body(g_ref, u_ref, da_ref, dz_ref):
        g, d = g_ref[...].astype(F32), da_ref[...].astype(F32)
        s = jax.nn.sigmoid(g)
        dz_ref[0] = (d * u_ref[...].astype(F32) * (s * (1.0 + g * (1.0 - s)))).astype(BF16)
        dz_ref[1] = (d * (g * s)).astype(BF16)

    part = lambda p: pl.BlockSpec((None, tr, tc), lambda i, j, p=p: (p, i, j))
    return pl.pallas_call(
        body, name=name, grid=(T // tr, Fd // tc),
        in_specs=[part(0), part(1), pl.BlockSpec((tr, tc), lambda i, j: (i, j))],
        out_specs=pl.BlockSpec((2, tr, tc), lambda i, j: (0, i, j)),
        out_shape=jax.ShapeDtypeStruct((2, T, Fd), BF16), compiler_params=_cp(("parallel", "parallel")),
    )(z, z, da)


def _headnorm_fwd(z, part, g, name):
    _, T, D = z.shape
    tr = _tile(T, 1056, 16)

    def body(z_ref, g_ref, o_ref):
        x = z_ref[...]
        r = lax.rsqrt(jnp.mean(x * x, axis=-1, keepdims=True) + EPS)
        o_ref[...] = (x * r * g_ref[...]).astype(BF16)

    return pl.pallas_call(
        body, name=name, grid=(T // tr, D // HEAD_DIM),
        in_specs=[pl.BlockSpec((None, tr, HEAD_DIM), lambda i, h: (part, i, h)),
                  pl.BlockSpec((1, HEAD_DIM), lambda i, h: (0, 0))],
        out_specs=pl.BlockSpec((tr, HEAD_DIM), lambda i, h: (i, h)),
        out_shape=jax.ShapeDtypeStruct((T, D), BF16), compiler_params=_cp(("parallel", "parallel")),
    )(z, g)


def _headnorm_bwd(z, part, g, dy, name):
    _, T, D = z.shape
    tr = _tile(T, 1056, 16)

    def body(z_ref, g_ref, dy_ref, dz_ref, dg_ref):
        x = z_ref[...]
        r = lax.rsqrt(jnp.mean(x * x, axis=-1, keepdims=True) + EPS)
        xh = x * r
        dy_ = dy_ref[...]
        dxh = dy_ * g_ref[...]
        dz_ref[...] = (r * (dxh - xh * jnp.mean(dxh * xh, axis=-1, keepdims=True))).astype(BF16)
        partial = jnp.sum(dy_ * xh, axis=0, keepdims=True)
        first = (pl.program_id(0) == 0) & (pl.program_id(1) == 0)

        @pl.when(first)
        def _():
            dg_ref[...] = partial

        @pl.when(jnp.logical_not(first))
        def _():
            dg_ref[...] += partial

    blk = pl.BlockSpec((tr, HEAD_DIM), lambda i, h: (i, h))
    vec = pl.BlockSpec((1, HEAD_DIM), lambda i, h: (0, 0))
    return pl.pallas_call(
        body, name=name, grid=(T // tr, D // HEAD_DIM),
        in_specs=[pl.BlockSpec((None, tr, HEAD_DIM), lambda i, h: (part, i, h)), vec, blk],
        out_specs=[blk, vec],
        out_shape=[jax.ShapeDtypeStruct((T, D), BF16), jax.ShapeDtypeStruct((1, HEAD_DIM), F32)],
        compiler_params=_cp(("arbitrary", "arbitrary")),
    )(z, g, dy)


def _cast_part(z, part, name):
    _, T, D = z.shape
    tr = _tile(T, 528, 16)

    def body(z_ref, o_ref):
        o_ref[...] = z_ref[...].astype(BF16)

    return pl.pallas_call(
        body, name=name, grid=(T // tr,),
        in_specs=[pl.BlockSpec((None, tr, D), lambda i: (part, i, 0))],
        out_specs=pl.BlockSpec((tr, D), lambda i: (i, 0)),
        out_shape=jax.ShapeDtypeStruct((T, D), BF16), compiler_params=_cp(("parallel",)),
    )(z)


def _split3(x):
    a = x.astype(BF16)
    r = x - a.astype(F32)
    b = r.astype(BF16)
    c = (r - b.astype(F32)).astype(BF16)
    return a, b, c


def _tri_matmul(tri, x):
    a, b, c = _split3(x)
    dot = lambda v: jnp.dot(tri, v, preferred_element_type=F32)
    return (dot(c) + dot(b)) + dot(a)


def _logf_cumsum(pre, bias, pad, name):
    T = pre.shape[0]
    nb = T // BLOCK

    def body(p_ref, b_ref, c_ref, carry):
        i = pl.program_id(0)

        @pl.when(i == 0)
        def _():
            carry[...] = jnp.zeros_like(carry)

        x = p_ref[...] + b_ref[...]
        lf = jnp.minimum(x, 0.0) - jnp.log(1.0 + jnp.exp(-jnp.abs(x)))
        rows = i * BLOCK + lax.broadcasted_iota(jnp.int32, (BLOCK, LANES), 0)
        lf = jnp.where(rows >= pad, lf, 0.0)
        r = lax.broadcasted_iota(jnp.int32, (BLOCK, BLOCK), 0)
        c = lax.broadcasted_iota(jnp.int32, (BLOCK, BLOCK), 1)
        tri = jnp.where(c <= r, 1.0, 0.0).astype(BF16)
        c_ref[...] = _tri_matmul(tri, lf) + carry[...]
        carry[...] = c_ref[BLOCK - 1:BLOCK, :]

    return pl.pallas_call(
        body, name=name, grid=(nb,),
        in_specs=[pl.BlockSpec((BLOCK, LANES), lambda i: (i, 0)), pl.BlockSpec((1, LANES), lambda i: (0, 0))],
        out_specs=pl.BlockSpec((BLOCK, LANES), lambda i: (i, 0)),
        out_shape=jax.ShapeDtypeStruct((T, LANES), F32),
        scratch_shapes=[pltpu.VMEM((1, LANES), F32)], compiler_params=_cp(("arbitrary",)),
    )(pre, bias)


def _logf_bwd(pre, bias, dc, pad, name):
    T = pre.shape[0]
    nb = T // BLOCK

    def body(p_ref, b_ref, dc_ref, dp_ref, db_ref, carry, dlf_ref):
        i = pl.program_id(0)

        @pl.when(i == 0)
        def _():
            carry[...] = jnp.zeros_like(carry)

        r = lax.broadcasted_iota(jnp.int32, (BLOCK, BLOCK), 0)
        c = lax.broadcasted_iota(jnp.int32, (BLOCK, BLOCK), 1)
        tri = jnp.where(c >= r, 1.0, 0.0).astype(BF16)
        dlf_ref[...] = _tri_matmul(tri, dc_ref[...]) + carry[...]
        carry[...] = dlf_ref[0:1, :]
        dlf = dlf_ref[...]
        x = p_ref[...] + b_ref[...]
        rows = (nb - 1 - i) * BLOCK + lax.broadcasted_iota(jnp.int32, (BLOCK, LANES), 0)
        dpre = jnp.where(rows >= pad, dlf * jax.nn.sigmoid(-x), 0.0)
        dp_ref[...] = dpre
        partial = jnp.sum(dpre, axis=0, keepdims=True)

        @pl.when(i == 0)
        def _():
            db_ref[...] = partial

        @pl.when(i > 0)
        def _():
            db_ref[...] += partial

    rev = pl.BlockSpec((BLOCK, LANES), lambda i: (nb - 1 - i, 0))
    vec = pl.BlockSpec((1, LANES), lambda i: (0, 0))
    return pl.pallas_call(
        body, name=name, grid=(nb,), in_specs=[rev, vec, rev], out_specs=[rev, vec],
        out_shape=[jax.ShapeDtypeStruct((T, LANES), F32), jax.ShapeDtypeStruct((1, LANES), F32)],
        scratch_shapes=[pltpu.VMEM((1, LANES), F32), pltpu.VMEM((BLOCK, LANES), F32)],
        compiler_params=_cp(("arbitrary",)),
    )(pre, bias, dc)


def _loss_head(h, target, first, name):
    T, D = h.shape
    tr = BLOCK
    skip = first // tr

    def body(h_ref, t_ref, dh_ref, dhb_ref, loss_ref):
        i = pl.program_id(0)

        @pl.when(i == 0)
        def _():
            loss_ref[...] = jnp.zeros_like(loss_ref)

        @pl.when(i < skip)
        def _():
            dh_ref[...] = jnp.zeros_like(dh_ref)
            dhb_ref[...] = jnp.zeros_like(dhb_ref)

        @pl.when(i >= skip)
        def _():
            err = h_ref[...] - t_ref[...]
            dh_ref[...] = err * (1.0 / D)
            dhb_ref[...] = (err * (1.0 / D)).astype(BF16)
            loss_ref[...] += jnp.sum(err * err) * (0.5 / D)

    row = pl.BlockSpec((tr, D), lambda i: (i, 0))
    return pl.pallas_call(
        body, name=name, grid=(T // tr,),
        in_specs=[row, pl.BlockSpec((tr, D), lambda i: (jnp.maximum(i - skip, 0), 0))],
        out_specs=[row, row, pl.BlockSpec((8, LANES), lambda i: (0, 0))],
        out_shape=[jax.ShapeDtypeStruct((T, D), F32), jax.ShapeDtypeStruct((T, D), BF16),
                   jax.ShapeDtypeStruct((8, LANES), F32)],
        compiler_params=_cp(("arbitrary",)),
    )(h, target)


def _adamw(w, g, m, v, name):
    R, C = w.shape
    tr = _tile(R, max(8, TILE_BUDGET // (C * 4 * 7 * 3)), 8)
    bc1, bc2 = 1.0 - ADAM_B1 ** ADAM_STEP, 1.0 - ADAM_B2 ** ADAM_STEP

    def body(w_ref, g_ref, m_ref, v_ref, d_ref, mo_ref, vo_ref):
        g_ = g_ref[...]
        m_ = ADAM_B1 * m_ref[...] + (1.0 - ADAM_B1) * g_
        v_ = ADAM_B2 * v_ref[...] + (1.0 - ADAM_B2) * (g_ * g_)
        d_ref[...] = -ADAM_LR * ((m_ / bc1) / (jnp.sqrt(v_ / bc2) + ADAM_EPS) + ADAM_WD * w_ref[...])
        mo_ref[...] = m_
        vo_ref[...] = v_

    blk = pl.BlockSpec((tr, C), lambda i: (i, 0))
    sds = jax.ShapeDtypeStruct((R, C), F32)
    return pl.pallas_call(
        body, name=name, grid=(R // tr,), in_specs=[blk] * 4, out_specs=[blk] * 3, out_shape=[sds] * 3,
        compiler_params=_cp(("parallel",)),
    )(w, g, m, v)


def _pick_head(c_blk, h):
    lane = lax.broadcasted_iota(jnp.int32, c_blk.shape, 1)
    return jnp.sum(jnp.where(lane == h, c_blk, 0.0), axis=1, keepdims=True)


def _attn_fwd(q, k, v, c, ct, *, blk, pad, name):
    T, D = q.shape
    H, nq = D // HEAD_DIM, T // blk
    scale = 1.0 / math.sqrt(HEAD_DIM)

    def body(q_ref, k_ref, v_ref, c_ref, ct_ref, o_ref, lse_ref):
        h, i = pl.program_id(0), pl.program_id(1)
        qb = q_ref[...]
        cq = _pick_head(c_ref[...], h)

        def step_fn(masked):
            def step(j, carry):
                m, l, acc = carry
                off = j * blk if isinstance(j, int) else pl.multiple_of(j * blk, blk)
                kb = k_ref[pl.ds(off, blk), :]
                vb = v_ref[pl.ds(off, blk), :]
                s = lax.dot_general(qb, kb, (((1,), (1,)), ((), ())), preferred_element_type=F32) * scale
                s = s + (cq - ct_ref[j])
                if masked:
                    qpos = i * blk + lax.broadcasted_iota(jnp.int32, (blk, blk), 0)
                    kpos = j * blk + lax.broadcasted_iota(jnp.int32, (blk, blk), 1)
                    s = jnp.where((kpos <= qpos) & (kpos >= pad), s, NEG)
                m_new = jnp.maximum(m, jnp.max(s, axis=1, keepdims=True))
                p = jnp.exp(s - m_new)
                alpha = jnp.exp(m - m_new)
                l = alpha * l + jnp.sum(p, axis=1, keepdims=True)
                acc = alpha * acc + jnp.dot(p.astype(BF16), vb, preferred_element_type=F32)
                return m_new, l, acc
            return step

        carry = (jnp.full((blk, 1), NEG, F32), jnp.zeros((blk, 1), F32), jnp.zeros((blk, HEAD_DIM), F32))
        carry = step_fn(True)(0, carry)
        carry = lax.fori_loop(1, i, step_fn(False), carry)
        m, l, acc = lax.cond(i > 0, lambda c: step_fn(True)(i, c), lambda c: c, carry)
        rowpos = i * blk + lax.broadcasted_iota(jnp.int32, (blk, 1), 0)
        o_ref[...] = jnp.where(rowpos >= pad, acc / l, 0.0)
        lse_ref[...] = jnp.broadcast_to(m + jnp.log(l), (blk, LANES))

    return pl.pallas_call(
        body, name=name, grid=(H, nq),
        in_specs=[pl.BlockSpec((blk, HEAD_DIM), lambda h, i: (i, h)),
                  pl.BlockSpec((T, HEAD_DIM), lambda h, i: (0, h)),
                  pl.BlockSpec((T, HEAD_DIM), lambda h, i: (0, h)),
                  pl.BlockSpec((blk, LANES), lambda h, i: (i, 0)),
                  pl.BlockSpec((None, nq, 1, blk), lambda h, i: (h, 0, 0, 0))],
        out_specs=[pl.BlockSpec((blk, HEAD_DIM), lambda h, i: (i, h)),
                   pl.BlockSpec((None, blk, LANES), lambda h, i: (h, i, 0))],
        out_shape=[jax.ShapeDtypeStruct((T, D), F32), jax.ShapeDtypeStruct((H, T, LANES), F32)],
        compiler_params=_cp(("parallel", "arbitrary")),
    )(q, k, v, c, ct)


def _attn_bwd(q, k, v, o, do, lse, c, ct, prev, *, blk, pad, name):
    T, D = q.shape
    H, nq = D // HEAD_DIM, T // blk
    scale = 1.0 / math.sqrt(HEAD_DIM)
    has_prev = prev is not None

    nt_dims = (((1,), (1,)), ((), ()))
    tn_dims = (((0,), (0,)), ((), ()))

    def body(*refs):
        q_ref, k_ref, v_ref, o_ref, do_ref, lse_ref, c_ref, ct_ref = refs[:8]
        pdk_ref, pdv_ref, pdc_ref = refs[8:11] if has_prev else (None, None, None)
        dq_ref, dk_ref, dv_ref, dct_ref = refs[-4:]
        h, j = pl.program_id(0), pl.program_id(1)

        @pl.when(j == 0)
        def _():
            dq_ref[...] = jnp.zeros_like(dq_ref)

        kb, vb = k_ref[...], v_ref[...]
        ck = ct_ref[...]

        def step_fn(masked):
            def step(i, carry):
                dk, dv, dck = carry
                off = pl.multiple_of(i * blk, blk)
                qb = q_ref[pl.ds(off, blk), :]
                dob = do_ref[pl.ds(off, blk), :]
                lse_i = lse_ref[pl.ds(off, blk), :][:, 0:1]
                cq = _pick_head(c_ref[pl.ds(off, blk), :], h)
                delta = jnp.sum(dob.astype(F32) * o_ref[pl.ds(off, blk), :], axis=1, keepdims=True)
                s = lax.dot_general(qb, kb, nt_dims, preferred_element_type=F32) * scale
                p = jnp.exp(s + (cq - ck) - lse_i)
                if masked:
                    qpos = i * blk + lax.broadcasted_iota(jnp.int32, (blk, blk), 0)
                    kpos = j * blk + lax.broadcasted_iota(jnp.int32, (blk, blk), 1)
                    p = jnp.where((kpos <= qpos) & (kpos >= pad), p, 0.0)
                dp = lax.dot_general(dob, vb, nt_dims, preferred_element_type=F32)
                ds = p * (dp - delta)
                pb, dsb = p.astype(BF16), ds.astype(BF16)
                dv = dv + lax.dot_general(pb, dob, tn_dims, preferred_element_type=F32)
                dk = dk + lax.dot_general(dsb, qb, tn_dims, preferred_element_type=F32)
                dck = dck - jnp.sum(ds, axis=0, keepdims=True)
                dq_ref[pl.ds(off, blk), :] += jnp.dot(dsb, kb, preferred_element_type=F32) * scale
                return dk, dv, dck
            return step

        carry = (jnp.zeros((blk, HEAD_DIM), F32), jnp.zeros((blk, HEAD_DIM), F32), jnp.zeros((1, blk), F32))
        carry = step_fn(True)(j, carry)
        below = lambda masked: (lambda c: lax.fori_loop(j + 1, nq, step_fn(masked), c))
        dk, dv, dck = lax.cond(j == 0, below(True), below(False), carry)
        dk = dk * scale
        if has_prev:
            dk, dv, dck = dk + pdk_ref[...], dv + pdv_ref[...], dck + pdc_ref[...]
        dk_ref[...] = dk
        dv_ref[...] = dv
        dct_ref[...] = dck

    col = pl.BlockSpec((T, HEAD_DIM), lambda h, j: (0, h))
    kblk = pl.BlockSpec((blk, HEAD_DIM), lambda h, j: (j, h))
    ctb = pl.BlockSpec((None, None, 1, blk), lambda h, j: (h, j, 0, 0))
    in_specs = [col, kblk, kblk, col, col,
                pl.BlockSpec((None, T, LANES), lambda h, j: (h, 0, 0)),
                pl.BlockSpec((T, LANES), lambda h, j: (0, 0)), ctb]
    operands = [q, k, v, o, do, lse, c, ct]
    if has_prev:
        in_specs += [kblk, kblk, ctb]
        operands += list(prev)
    return pl.pallas_call(
        body, name=name, grid=(H, nq), in_specs=in_specs, out_specs=[col, kblk, kblk, ctb],
        out_shape=[jax.ShapeDtypeStruct((T, D), F32), jax.ShapeDtypeStruct((T, D), F32),
                   jax.ShapeDtypeStruct((T, D), F32), jax.ShapeDtypeStruct((H, nq, 1, blk), F32)],
        compiler_params=_cp(("parallel", "arbitrary")),
    )(*operands)


HBM_SPEC = pl.BlockSpec(memory_space=pltpu.HBM)


def _place():
    x, y, c = lax.axis_index("x"), lax.axis_index("y"), lax.axis_index("c")
    chips = [(1 - x, y), (x, 1 - y), (1 - x, 1 - y)]
    return x, y, c, chips


def _remote(src, dst, send_sems, recv_sems, k, to):
    return pltpu.make_async_remote_copy(src_ref=src, dst_ref=dst, send_sem=send_sems.at[k],
                                        recv_sem=recv_sems.at[k], device_id=to, device_id_type=MESH)


def _all_gather(shards, specs, name):
    n = len(shards)

    def full_shape(s, axis):
        return (s.shape[0] * N_CHIPS, s.shape[1]) if axis == 0 else (s.shape[0], s.shape[1] * N_CHIPS)

    def body(*refs):
        srcs, outs = refs[:n], refs[n:2 * n]
        send_sems, recv_sems, local_sems = refs[2 * n:]
        x, y, c, chips = _place()
        sibling = (x, y, 1 - c)

        def region(t, chip, half):
            rs, cs = shards[t].shape
            q = 2 * chip[0] + chip[1]
            axis, split = specs[t]
            nrow = rs // 2 if half is not None else rs
            r0 = 0 if half is None else half * nrow
            if axis == 0:
                return outs[t].at[pl.ds(q * rs + r0, nrow), :]
            return outs[t].at[pl.ds(r0, nrow), pl.ds(pl.multiple_of(q * cs, cs), cs)]

        def piece(t, half):
            rs = shards[t].shape[0]
            if half is None:
                return srcs[t]
            return srcs[t].at[pl.ds(half * (rs // 2), rs // 2), :]

        local = [pltpu.make_async_copy(srcs[t], region(t, (x, y), None), local_sems.at[t]) for t in range(n)]
        for cp in local:
            cp.start()
        sends = []
        for t in range(n):
            half = c if specs[t][1] else None
            for j, chip in enumerate(chips):
                cp = _remote(piece(t, half), region(t, (x, y), half), send_sems, recv_sems, 6 * t + j, (*chip, c))
                cp.start()
                sends.append(cp)
        for t in range(n):
            half = c if specs[t][1] else None
            for j, chip in enumerate(chips):
                landed = region(t, chip, half)
                _remote(landed, landed, send_sems, recv_sems, 6 * t + j, (*chip, c)).wait_recv()
                if specs[t][1]:
                    cp = _remote(landed, landed, send_sems, recv_sems, 6 * t + 3 + j, sibling)
                    cp.start()
                    sends.append(cp)
        for t in range(n):
            if specs[t][1]:
                for j, chip in enumerate(chips):
                    got = region(t, chip, 1 - c)
                    _remote(got, got, send_sems, recv_sems, 6 * t + 3 + j, sibling).wait_recv()
        for cp in sends:
            cp.wait_send()
        for cp in local:
            cp.wait()

    return pl.pallas_call(
        body, name=name, in_specs=[HBM_SPEC] * n, out_specs=[HBM_SPEC] * n,
        out_shape=[jax.ShapeDtypeStruct(full_shape(s, specs[t][0]), s.dtype) for t, s in enumerate(shards)],
        scratch_shapes=[pltpu.SemaphoreType.DMA((6 * n,)), pltpu.SemaphoreType.DMA((6 * n,)),
                        pltpu.SemaphoreType.DMA((n,))],
        compiler_params=pltpu.CompilerParams(has_side_effects=True),
    )(*shards)


def _grad_view(g, axis):
    R, C = g.shape
    nq = N_CHIPS if axis == 0 else 1
    return g.reshape(nq, 2, R // (2 * nq), C)


def _swap_halves(views, name):
    n = len(views)

    def body(*refs):
        srcs, outs, send_sems, recv_sems = refs[:n], refs[n:2 * n], refs[2 * n], refs[2 * n + 1]
        x, y, c, _ = _place()
        cps = [_remote(srcs[t].at[:, 1 - c], outs[t], send_sems, recv_sems, t, (x, y, 1 - c)) for t in range(n)]
        for cp in cps:
            cp.start()
        for cp in cps:
            cp.wait()

    return pl.pallas_call(
        body, name=name, in_specs=[HBM_SPEC] * n, out_specs=[HBM_SPEC] * n,
        out_shape=[jax.ShapeDtypeStruct((v.shape[0],) + v.shape[2:], v.dtype) for v in views],
        scratch_shapes=[pltpu.SemaphoreType.DMA((n,)), pltpu.SemaphoreType.DMA((n,))],
        compiler_params=pltpu.CompilerParams(has_side_effects=True),
    )(*views)


def _add_half(view, got, c_idx, name):
    nq, _, Rh, C = view.shape
    tr = _tile(Rh, max(16, (2 * 1024 * 1024) // (C * 2)), 16)

    def body(c_ref, a_ref, b_ref, o_ref):
        o_ref[...] = (a_ref[...].astype(F32) + b_ref[...].astype(F32)).astype(BF16)

    grid_spec = pltpu.PrefetchScalarGridSpec(
        num_scalar_prefetch=1, grid=(nq, Rh // tr),
        in_specs=[pl.BlockSpec((None, None, tr, C), lambda q, i, c_ref: (q, c_ref[0], i, 0)),
                  pl.BlockSpec((None, tr, C), lambda q, i, c_ref: (q, i, 0))],
        out_specs=pl.BlockSpec((None, tr, C), lambda q, i, c_ref: (q, i, 0)))
    return pl.pallas_call(
        body, name=name, grid_spec=grid_spec, out_shape=jax.ShapeDtypeStruct((nq, Rh, C), BF16),
        compiler_params=_cp(("parallel", "parallel")),
    )(c_idx, view, got)


SEM_SPEC = pl.BlockSpec(memory_space=pltpu.SEMAPHORE)
ANY_SPEC = pl.BlockSpec(memory_space=pl.ANY)
TOKEN_SPEC = pl.BlockSpec(memory_space=pltpu.VMEM)
TOKEN = jax.ShapeDtypeStruct((8, LANES), F32)
SPLIT_COPY = pltpu.CompilerParams(has_side_effects=pltpu.SideEffectType.DATAFLOW_SIDE_EFFECTING)


def _region(ref, axis, chip, half):
    q = 2 * chip[0] + chip[1]
    if axis == 0:
        rs = ref.shape[0] // N_CHIPS
        return ref.at[pl.ds(q * rs + half * (rs // 2), rs // 2), :]
    rh, cs = ref.shape[0] // 2, ref.shape[1] // N_CHIPS
    return ref.at[pl.ds(half * rh, rh), pl.ds(pl.multiple_of(q * cs, cs), cs)]


def _gather_start(fulls, axes, after, name):
    n = len(fulls)

    def body(*refs):
        ins = refs[:n]
        token = refs[-1]
        send_sems, recv_sems = refs[n + 1], refs[n + 2]
        x, y, c, chips = _place()
        for t in range(n):
            mine = _region(ins[t], axes[t], (x, y), c)
            for j, chip in enumerate(chips):
                _remote(mine, mine, send_sems, recv_sems, 3 * t + j, (*chip, c)).start()
        token[...] = jnp.zeros_like(token)

    sems = pltpu.SemaphoreType.DMA((3 * n,))
    outs = pl.pallas_call(
        body, name=name, in_specs=[HBM_SPEC] * n + [ANY_SPEC],
        out_specs=[SEM_SPEC, SEM_SPEC] + [HBM_SPEC] * n + [TOKEN_SPEC],
        out_shape=[sems, sems] + [jax.ShapeDtypeStruct(f.shape, f.dtype) for f in fulls] + [TOKEN],
        input_output_aliases={t: 2 + t for t in range(n)}, compiler_params=SPLIT_COPY,
    )(*fulls, after)
    return outs[0], outs[1], list(outs[2:2 + n]), outs[-1]


def _gather_wait(send_sems, recv_sems, fulls, axes, after, name):
    n = len(fulls)

    def body(*refs):
        ins = refs[:n]
        send_sems, recv_sems = refs[n], refs[n + 1]
        x, y, c, chips = _place()
        for t in range(n):
            mine = _region(ins[t], axes[t], (x, y), c)
            for j, chip in enumerate(chips):
                _remote(mine, mine, send_sems, recv_sems, 3 * t + j, (*chip, c)).wait_send()
                theirs = _region(ins[t], axes[t], chip, c)
                _remote(theirs, theirs, send_sems, recv_sems, 3 * t + j, (*chip, c)).wait_recv()

    outs = pl.pallas_call(
        body, name=name, in_specs=[HBM_SPEC] * n + [SEM_SPEC, SEM_SPEC, ANY_SPEC], out_specs=[HBM_SPEC] * n,
        out_shape=[jax.ShapeDtypeStruct(f.shape, f.dtype) for f in fulls],
        input_output_aliases={t: t for t in range(n)}, compiler_params=SPLIT_COPY,
    )(*fulls, send_sems, recv_sems, after)
    return list(outs)


def _gather_forward(fulls, axes, name):
    n = len(fulls)

    def body(*refs):
        outs = refs[n:2 * n]
        send_sems, recv_sems = refs[2 * n], refs[2 * n + 1]
        x, y, c, chips = _place()
        sibling = (x, y, 1 - c)
        cps = []
        for t in range(n):
            for j, chip in enumerate(chips):
                landed = _region(outs[t], axes[t], chip, c)
                cps.append(_remote(landed, landed, send_sems, recv_sems, 3 * t + j, sibling))
        for cp in cps:
            cp.start()
        for t in range(n):
            for j, chip in enumerate(chips):
                got = _region(outs[t], axes[t], chip, 1 - c)
                _remote(got, got, send_sems, recv_sems, 3 * t + j, sibling).wait_recv()
        for cp in cps:
            cp.wait_send()

    outs = pl.pallas_call(
        body, name=name, in_specs=[HBM_SPEC] * n, out_specs=[HBM_SPEC] * n,
        out_shape=[jax.ShapeDtypeStruct(f.shape, f.dtype) for f in fulls],
        scratch_shapes=[pltpu.SemaphoreType.DMA((3 * n,)), pltpu.SemaphoreType.DMA((3 * n,))],
        input_output_aliases={t: t for t in range(n)},
        compiler_params=pltpu.CompilerParams(has_side_effects=True),
    )(*fulls)
    return list(outs)


def _shard_cols(s, axis):
    return s.shape[2] if axis == 0 else s.shape[2] // N_CHIPS


def _piece(ref, axis, chip):
    q = 2 * chip[0] + chip[1]
    if axis == 0:
        return ref.at[q]
    cs = ref.shape[2] // N_CHIPS
    return ref.at[0, :, pl.ds(pl.multiple_of(q * cs, cs), cs)]


def _scatter_start(sums, axes, after, name):
    n = len(sums)

    def body(*refs):
        ins = refs[:n]
        send_sems, recv_sems = refs[n + 1], refs[n + 2]
        lands = refs[2 * n + 3:3 * n + 3]
        token = refs[-1]
        x, y, c, chips = _place()
        for t in range(n):
            for j, chip in enumerate(chips):
                _remote(_piece(ins[t], axes[t], chip), lands[t].at[j], send_sems, recv_sems, 3 * t + j, (*chip, c)).start()
        token[...] = jnp.zeros_like(token)

    sems = pltpu.SemaphoreType.DMA((3 * n,))
    land_shapes = [jax.ShapeDtypeStruct((3, s.shape[1], _shard_cols(s, a)), s.dtype) for s, a in zip(sums, axes)]
    outs = pl.pallas_call(
        body, name=name, in_specs=[HBM_SPEC] * n + [ANY_SPEC],
        out_specs=[SEM_SPEC, SEM_SPEC] + [HBM_SPEC] * (2 * n) + [TOKEN_SPEC],
        out_shape=[sems, sems] + [jax.ShapeDtypeStruct(s.shape, s.dtype) for s in sums] + land_shapes + [TOKEN],
        input_output_aliases={t: 2 + t for t in range(n)}, compiler_params=SPLIT_COPY,
    )(*sums, after)
    return outs[0], outs[1], list(outs[2:2 + n]), list(outs[2 + n:2 + 2 * n]), outs[-1]


def _scatter_wait(send_sems, recv_sems, sums, lands, axes, after, name):
    n = len(sums)

    def body(*refs):
        ins, lnd = refs[:n], refs[n:2 * n]
        send_sems, recv_sems = refs[2 * n], refs[2 * n + 1]
        x, y, c, chips = _place()
        for t in range(n):
            for j, chip in enumerate(chips):
                cp = _remote(_piece(ins[t], axes[t], chip), lnd[t].at[j], send_sems, recv_sems, 3 * t + j, (*chip, c))
                cp.wait_send()
                cp.wait_recv()

    outs = pl.pallas_call(
        body, name=name, in_specs=[HBM_SPEC] * (2 * n) + [SEM_SPEC, SEM_SPEC, ANY_SPEC], out_specs=[HBM_SPEC] * (2 * n),
        out_shape=[jax.ShapeDtypeStruct(s.shape, s.dtype) for s in sums + lands],
        input_output_aliases={t: t for t in range(2 * n)}, compiler_params=SPLIT_COPY,
    )(*sums, *lands, send_sems, recv_sems, after)
    return list(outs[:n]), list(outs[n:])


def _sum_chips(own, axis, got, q_idx, c_idx, name):
    _, Rh, cc = got.shape
    tr = _tile(Rh, max(16, (1024 * 1024) // (cc * 2)), 16)

    def body(q_ref, c_ref, a_ref, b0, b1, b2, o_ref):
        f = lambda r: r[...].astype(F32)
        o_ref[...] = ((f(a_ref) + f(b0)) + f(b1)) + f(b2)

    if axis == 0:
        own_spec = pl.BlockSpec((None, tr, cc), lambda i, q, c: (q[0], i, 0))
    else:
        own_spec = pl.BlockSpec((None, tr, cc), lambda i, q, c: (0, i, q[0]))
    slot = lambda j: pl.BlockSpec((None, tr, cc), lambda i, q, c, j=j: (j, i, 0))
    grid_spec = pltpu.PrefetchScalarGridSpec(
        num_scalar_prefetch=2, grid=(Rh // tr,), in_specs=[own_spec, slot(0), slot(1), slot(2)],
        out_specs=pl.BlockSpec((None, tr, cc), lambda i, q, c: (c[0], i, 0)))
    return pl.pallas_call(
        body, name=name, grid_spec=grid_spec, out_shape=jax.ShapeDtypeStruct((2, Rh, cc), F32),
        compiler_params=_cp(("parallel",)),
    )(q_idx, c_idx, own, got, got, got)


def _join_halves(pairs, name):
    n = len(pairs)

    def body(*refs):
        outs = refs[n:2 * n]
        send_sems, recv_sems = refs[2 * n], refs[2 * n + 1]
        x, y, c, _ = _place()
        cps = [_remote(outs[t].at[c], outs[t].at[c], send_sems, recv_sems, t, (x, y, 1 - c)) for t in range(n)]
        for cp in cps:
            cp.start()
        for t in range(n):
            cps[t].wait_send()
            _remote(outs[t].at[1 - c], outs[t].at[1 - c], send_sems, recv_sems, t, (x, y, 1 - c)).wait_recv()

    outs = pl.pallas_call(
        body, name=name, in_specs=[HBM_SPEC] * n, out_specs=[HBM_SPEC] * n,
        out_shape=[jax.ShapeDtypeStruct(p.shape, p.dtype) for p in pairs],
        scratch_shapes=[pltpu.SemaphoreType.DMA((n,)), pltpu.SemaphoreType.DMA((n,))],
        input_output_aliases={t: t for t in range(n)},
        compiler_params=pltpu.CompilerParams(has_side_effects=True),
    )(*pairs)
    return list(outs)


def _reduce_begin(grads, axes, c_idx, after, tag):
    views = [_grad_view(g, a) for g, a in zip(grads, axes)]
    got = _swap_halves(views, f"rs_swap_{tag}")
    sums = [_add_half(v, p, c_idx, f"rs_add_{tag}_{t}") for t, (v, p) in enumerate(zip(views, got))]
    send_sems, recv_sems, sums, lands, token = _scatter_start(sums, axes, after, f"rs_chips_start_{tag}")
    return (send_sems, recv_sems, sums, lands, axes, tag), token


def _reduce_finish(pending, q_idx, c_idx, after):
    send_sems, recv_sems, sums, lands, axes, tag = pending
    sums, lands = _scatter_wait(send_sems, recv_sems, sums, lands, axes, after, f"rs_chips_wait_{tag}")
    pairs = [_sum_chips(s, a, r, q_idx, c_idx, f"rs_sum_{tag}_{t}")
             for t, (s, a, r) in enumerate(zip(sums, axes, lands))]
    joined = _join_halves(pairs, f"rs_join_{tag}")
    return [j.reshape(2 * j.shape[1], j.shape[2]) for j in joined]


def _all_reduce_small(buf, name):
    R, C = buf.shape

    def body(b_ref, o_ref, slots, send_sems, recv_sems):
        x, y, c, _ = _place()
        me = 4 * x + 2 * y + c
        cps = []
        for k in range(1, 8):
            to = (x ^ (k >> 2), y ^ ((k >> 1) & 1), c ^ (k & 1))
            cps.append(pltpu.make_async_remote_copy(
                src_ref=b_ref, dst_ref=slots.at[me], send_sem=send_sems.at[k - 1], recv_sem=recv_sems.at[me],
                device_id=to, device_id_type=MESH))
        for cp in cps:
            cp.start()
        slots[me] = b_ref[...]
        for k in range(1, 8):
            src = me ^ k
            pltpu.make_async_remote_copy(
                src_ref=b_ref, dst_ref=slots.at[src], send_sem=send_sems.at[k - 1], recv_sem=recv_sems.at[src],
                device_id=(x, y, c), device_id_type=MESH).wait_recv()
        for cp in cps:
            cp.wait_send()
        total = slots[0]
        for d in range(1, 8):
            total = total + slots[d]
        o_ref[...] = total

    vm = pl.BlockSpec(memory_space=pltpu.VMEM)
    return pl.pallas_call(
        body, name=name, in_specs=[vm], out_specs=vm, out_shape=jax.ShapeDtypeStruct((R, C), F32),
        scratch_shapes=[pltpu.VMEM((8, R, C), F32), pltpu.SemaphoreType.DMA((7,)), pltpu.SemaphoreType.DMA((8,))],
        compiler_params=pltpu.CompilerParams(has_side_effects=True, vmem_limit_bytes=VMEM_LIMIT),
    )(buf)


def kernel(x, meta, a_norm, a_w_in, a_conv, a_w_out, kv_norm, w_kv, k_norm, w_f, b_f, b_norm, b_w_q, b_q_norm, b_w_o, ffn_norm, ffn_w_gu, ffn_w_down, loss_target, m_meta, m_a_norm, m_a_w_in, m_a_conv, m_a_w_out, m_kv_norm, m_w_kv, m_k_norm, m_w_f, m_b_f, m_b_norm, m_b_w_q, m_b_q_norm, m_b_w_o, m_ffn_norm, m_ffn_w_gu, m_ffn_w_down, v_meta, v_a_norm, v_a_w_in, v_a_conv, v_a_w_out, v_kv_norm, v_w_kv, v_k_norm, v_w_f, v_b_f, v_b_norm, v_b_w_q, v_b_q_norm, v_b_w_o, v_ffn_norm, v_ffn_w_gu, v_ffn_w_down):
    SEQ, D = x.shape[1], x.shape[2]
    n_meta = meta.shape[0]
    pad = BLOCK - n_meta
    first = pad + n_meta
    T = first + SEQ
    H = D // HEAD_DIM
    Ds = D // N_CHIPS
    n_a, n_b, depth = a_w_in.shape[0], b_w_q.shape[0], ffn_norm.shape[0]
    blk = _tile(T, 384, BLOCK)
    cx, cy, cc = lax.axis_index("x"), lax.axis_index("y"), lax.axis_index("c")
    q_me = 2 * cx + cy
    c_idx = jnp.reshape(cc, (1,)).astype(jnp.int32)
    q_idx = jnp.reshape(q_me, (1,)).astype(jnp.int32)
    rows8 = lambda v: jnp.pad(v, ((0, -v.shape[0] % 8), (0, 0)))

    col_parts = [meta, a_norm, a_conv.reshape(3 * n_a, Ds)]
    col_pack = jnp.concatenate([rows8(p) for p in col_parts], axis=0)
    col_full, w_f_full = _all_gather([col_pack, w_f], [(1, False), (0, False)], "ag_small")
    col_offs = [sum(rows8(p).shape[0] for p in col_parts[:i]) for i in range(3)]
    meta_f = col_full[:n_meta]
    a_norm_f = col_full[col_offs[1]:col_offs[1] + n_a]
    a_conv_f = col_full[col_offs[2]:col_offs[2] + 3 * n_a].reshape(n_a, 3, D)
    w_fp = jnp.pad(w_f_full, ((0, 0), (0, LANES - H))).astype(BF16)
    b_fp = jnp.pad(b_f, (0, LANES - H)).reshape(1, LANES)

    stages = []
    for l in range(n_a):
        stages += [[(a_w_in, l, 1), (a_w_out, l, 0)], [(ffn_w_gu, l, 1), (ffn_w_down, l, 0)]]
    for j in range(n_b):
        stages += [[(b_w_q, j, 0), (b_w_o, j, 0)], [(ffn_w_gu, n_a + j, 1), (ffn_w_down, n_a + j, 0)]]
    stages[2 * n_a].append((w_kv[None], 0, 1))

    def gather_begin(k, after):
        axes = [ax for _, _, ax in stages[k]]
        fulls = [_cast_into_full(w, l, ax, q_idx, f"cast_{k}_{i}") for i, (w, l, ax) in enumerate(stages[k])]
        send_sems, recv_sems, fulls, token = _gather_start(fulls, axes, after, f"ag_start_{k}")
        return (send_sems, recv_sems, fulls, axes, k), token

    def gather_end(handle, after):
        send_sems, recv_sems, fulls, axes, k = handle
        fulls = _gather_wait(send_sems, recv_sems, fulls, axes, after, f"ag_wait_{k}")
        return _gather_forward(fulls, axes, f"ag_forward_{k}")

    handle, _ = gather_begin(0, col_full)
    arrived = [gather_end(handle, col_full)]
    gathering = []

    def enter_segment():
        w = arrived[-1]
        after, token = w[0], None
        while len(gathering) < 2 and len(arrived) + len(gathering) < len(stages):
            handle, token = gather_begin(len(arrived) + len(gathering), after)
            gathering.append(handle)
            after = token
        return w, token

    def leave_segment(h_out):
        if gathering:
            arrived.append(gather_end(gathering.pop(0), h_out))

    h = jnp.concatenate([jnp.zeros((pad, D), F32), meta_f, x[0]], axis=0)
    saved = []

    def ffn_fwd(h, layer):
        (w_gu, w_down), token = enter_segment()
        xn, xn_t = _rms_fwd(h, ffn_norm[layer:layer + 1], f"ffn_norm_{layer}", dep=token)
        z = _matmul(xn, w_gu, mode="nn", out_dtype=BF16, name=f"ffn_gu_{layer}", out_parts=2)
        act, act_t = _swiglu_fwd(z, f"swiglu_{layer}")
        out = _matmul(act, w_down, mode="nn", out_dtype=F32, name=f"ffn_down_{layer}", res=h)
        leave_segment(out)
        return out, (h, xn_t, z, act_t), (w_gu, w_down)

    wa, wb = [], []
    for l in range(n_a):
        (w_in, w_out), token = enter_segment()
        xn, xn_t = _rms_fwd(h, a_norm_f[l:l + 1], f"a_norm_{l}", dep=token)
        z = _matmul(xn, w_in, mode="nn", out_dtype=BF16, name=f"a_in_{l}", out_parts=3)
        y, y_t = _gate_fwd(z, a_conv_f[l], f"a_gate_{l}")
        h2 = _matmul(y, w_out, mode="nn", out_dtype=F32, name=f"a_out_{l}", res=h)
        leave_segment(h2)
        h3, ffn_saved, w_ffn = ffn_fwd(h2, l)
        saved.append((h, xn_t, z, y_t, ffn_saved))
        wa.append((w_in, w_out) + w_ffn)
        h = h3

    for j in range(n_b):
        layer = n_a + j
        w_mix, token = enter_segment()
        w_q, w_o = w_mix[:2]
        if j == 0:
            h_kv, w_kv_b = h, w_mix[2]
            xkv, xkv_t = _rms_fwd(h, kv_norm.reshape(1, D), "kv_norm", dep=token)
            token = None
            kvz = _matmul(xkv, w_kv_b, mode="nn", out_dtype=F32, name="kv_proj", out_parts=2)
            k_n = _headnorm_fwd(kvz, 0, k_norm.reshape(1, HEAD_DIM), "k_headnorm")
            v_b = _cast_part(kvz, 1, "v_cast")
            pre = _matmul(xkv, w_fp, mode="nn", out_dtype=F32, name="f_proj")
            c_cum = _logf_cumsum(pre, b_fp, pad, "logf_cumsum")
            c_t = c_cum[:, :H].T.reshape(H, T // blk, 1, blk)
        xn, xn_t = _rms_fwd(h, b_norm[j:j + 1], f"b_norm_{j}", dep=token)
        qz = _matmul(xn, w_q, mode="nn", out_dtype=F32, name=f"b_q_{j}")[None]
        q_n = _headnorm_fwd(qz, 0, b_q_norm[j:j + 1], f"q_headnorm_{j}")
        o, lse = _attn_fwd(q_n, k_n, v_b, c_cum, c_t, blk=blk, pad=pad, name=f"attn_fwd_{j}")
        h2 = _matmul(o, w_o, mode="nn", out_dtype=F32, name=f"b_o_{j}", res=h)
        leave_segment(h2)
        h3, ffn_saved, w_ffn = ffn_fwd(h2, layer)
        saved.append((h, xn_t, qz, q_n, o, lse, ffn_saved))
        wb.append((w_q, w_o) + w_ffn)
        h = h3

    dh, dhb, loss_blk = _loss_head(h, loss_target[0], first, "loss_head")
    loss = lax.psum(loss_blk[0, 0], ("x", "y", "c"))

    shards = {}
    reducing = []

    def reduce_later(names, grads, axes, tag, done):
        after = c_idx
        if len(reducing) == 2:
            prev_names, pending = reducing.pop(0)
            got = _reduce_finish(pending, q_idx, c_idx, done)
            shards.update(zip(prev_names, got))
            after = got[0]
        pending, token = _reduce_begin(grads, axes, c_idx, after, tag)
        reducing.append((names, pending))
        return token

    def ffn_bwd(dh, dhb, layer, w_gu, w_down, ffn_saved, dep):
        h_in, xn_t, z, act_t = ffn_saved
        da = _matmul(dhb, w_down, mode="nt", out_dtype=BF16, name=f"ffn_down_dx_{layer}", dep=dep)
        g_down = _matmul(act_t, dhb, mode="nn", out_dtype=BF16, name=f"ffn_down_dw_{layer}")
        dz = _swiglu_bwd(z, da, f"swiglu_bwd_{layer}")
        dxn = _matmul(dz, w_gu, mode="nt", out_dtype=F32, name=f"ffn_gu_dx_{layer}")
        g_gu = _matmul(xn_t, dz, mode="nn", out_dtype=BF16, name=f"ffn_gu_dw_{layer}")
        dh, dhb, dg = _rms_bwd(h_in, ffn_norm[layer:layer + 1], dxn, dh, f"ffn_norm_bwd_{layer}")
        token = reduce_later([("ffn_w_gu", layer), ("ffn_w_down", layer)], [g_gu, g_down], [1, 0], f"f{layer}", dh)
        return dh, dhb, dg, token

    d_ffn_norm, d_b_norm, d_q_norm, d_a_norm, d_a_conv = {}, {}, {}, {}, {}
    kv_prev = None
    token = None
    for j in reversed(range(n_b)):
        layer = n_a + j
        w_q, w_o, w_gu, w_down = wb[j]
        h_in, xn_t, qz, q_n, o, lse, ffn_saved = saved[layer]
        dh, dhb, d_ffn_norm[layer], token = ffn_bwd(dh, dhb, layer, w_gu, w_down, ffn_saved, token)
        do = _matmul(dhb, w_o, mode="nt", out_dtype=BF16, name=f"b_o_dx_{j}", dep=token)
        g_o = _matmul(o.astype(BF16).T, dhb, mode="nn", out_dtype=BF16, name=f"b_o_dw_{j}")
        dq, dk, dv, dct = _attn_bwd(q_n, k_n, v_b, o, do, lse, c_cum, c_t, kv_prev, blk=blk, pad=pad,
                                    name=f"attn_bwd_{j}")
        kv_prev = (dk, dv, dct)
        dqz, d_q_norm[j] = _headnorm_bwd(qz, 0, b_q_norm[j:j + 1], dq, f"q_headnorm_bwd_{j}")
        dxn = _matmul(dqz, w_q, mode="nt", out_dtype=F32, name=f"b_q_dx_{j}")
        g_q = _matmul(xn_t, dqz, mode="nn", out_dtype=BF16, name=f"b_q_dw_{j}")
        dh, dhb, d_b_norm[j] = _rms_bwd(h_in, b_norm[j:j + 1], dxn, dh, f"b_norm_bwd_{j}")
        if j > 0:
            token = reduce_later([("b_w_q", j), ("b_w_o", j)], [g_q, g_o], [0, 0], f"b{j}", dh)

    dk, dv, dct = kv_prev
    dkz, d_k_norm = _headnorm_bwd(kvz, 0, k_norm.reshape(1, HEAD_DIM), dk, "k_headnorm_bwd")
    dvz = _cast_part(dv[None], 0, "dv_cast")
    dkv = jnp.stack([dkz, dvz])
    dc = jnp.pad(dct.reshape(H, T).T, ((0, 0), (0, LANES - H)))
    dpre, d_b_f = _logf_bwd(pre, b_fp, dc, pad, "logf_bwd")
    dxkv = _matmul(dkv, w_kv_b, mode="nt", out_dtype=F32, name="kv_proj_dx")
    dxkv = _matmul(dpre, w_fp, mode="nt", out_dtype=F32, name="f_proj_dx", res=dxkv)
    g_kv = _matmul(xkv_t, dkv, mode="nn", out_dtype=BF16, name="kv_proj_dw")
    d_w_f = _matmul(xkv_t, dpre, mode="nn", out_dtype=F32, name="f_proj_dw")
    dh, dhb, d_kv_norm = _rms_bwd(h_kv, kv_norm.reshape(1, D), dxkv, dh, "kv_norm_bwd")
    token = reduce_later([("b_w_q", 0), ("b_w_o", 0), ("w_kv", 0)], [g_q, g_o, g_kv], [0, 0, 1], "b0", dh)

    for l in reversed(range(n_a)):
        w_in, w_out, w_gu, w_down = wa[l]
        h_in, xn_t, z, y_t, ffn_saved = saved[l]
        dh, dhb, d_ffn_norm[l], token = ffn_bwd(dh, dhb, l, w_gu, w_down, ffn_saved, token)
        dy = _matmul(dhb, w_out, mode="nt", out_dtype=F32, name=f"a_out_dx_{l}", dep=token)
        g_out = _matmul(y_t, dhb, mode="nn", out_dtype=BF16, name=f"a_out_dw_{l}")
        dz, d_a_conv[l] = _gate_bwd(z, a_conv_f[l], dy, f"a_gate_bwd_{l}")
        dxn = _matmul(dz, w_in, mode="nt", out_dtype=F32, name=f"a_in_dx_{l}")
        g_in = _matmul(xn_t, dz, mode="nn", out_dtype=BF16, name=f"a_in_dw_{l}")
        dh, dhb, d_a_norm[l] = _rms_bwd(h_in, a_norm_f[l:l + 1], dxn, dh, f"a_norm_bwd_{l}")
        token = reduce_later([("a_w_in", l), ("a_w_out", l)], [g_in, g_out], [1, 0], f"a{l}", dh)

    last_names, last_pending = reducing.pop()
    for prev_names, pending in reducing:
        shards.update(zip(prev_names, _reduce_finish(pending, q_idx, c_idx, dh)))
    grad_x = dh[first:][None]

    widen = lambda v: jnp.pad(v, ((0, 0), (0, D - v.shape[1])))
    groups = [
        [dh[pad:first]],
        [d_a_norm[l] for l in range(n_a)],
        [d_a_conv[l] for l in range(n_a)],
        [d_kv_norm],
        [widen(d_k_norm)],
        [d_w_f[:, :H].T],
        [widen(d_b_f)],
        [d_b_norm[j] for j in range(n_b)],
        [widen(d_q_norm[j]) for j in range(n_b)],
        [d_ffn_norm[l] for l in range(depth)],
    ]
    pack = jnp.concatenate([rows8(p) for g in groups for p in g], axis=0)
    red = _all_reduce_small(pack, "ar_small")
    taken, off = [], 0
    for g in groups:
        r, rp = g[0].shape[0], rows8(g[0]).shape[0]
        taken.append(red[off:off + len(g) * rp].reshape(len(g), rp, D)[:, :r].reshape(len(g) * r, D))
        off += len(g) * rp
    take = lambda i: taken[i]
    mine = lambda a: lax.dynamic_slice_in_dim(a, q_me * Ds, Ds, axis=1)
    layers_of = lambda name, n: (lambda: jnp.stack([shards[(name, l)] for l in range(n)]))
    grad_of = {
        "meta": lambda: mine(take(0)),
        "a_norm": lambda: mine(take(1)),
        "a_w_in": layers_of("a_w_in", n_a),
        "a_conv": lambda: mine(take(2)).reshape(n_a, 3, Ds),
        "a_w_out": layers_of("a_w_out", n_a),
        "kv_norm": lambda: take(3).reshape(D),
        "w_kv": lambda: shards[("w_kv", 0)],
        "k_norm": lambda: take(4)[0, :HEAD_DIM],
        "w_f": lambda: mine(take(5)).T,
        "b_f": lambda: take(6)[0, :H],
        "b_norm": lambda: take(7),
        "b_w_q": layers_of("b_w_q", n_b),
        "b_q_norm": lambda: take(8)[:, :HEAD_DIM],
        "b_w_o": layers_of("b_w_o", n_b),
        "ffn_norm": lambda: take(9),
        "ffn_w_gu": layers_of("ffn_w_gu", depth),
        "ffn_w_down": layers_of("ffn_w_down", depth),
    }
    weights = dict(meta=meta, a_norm=a_norm, a_w_in=a_w_in, a_conv=a_conv, a_w_out=a_w_out, kv_norm=kv_norm, w_kv=w_kv,
                   k_norm=k_norm, w_f=w_f, b_f=b_f, b_norm=b_norm, b_w_q=b_w_q, b_q_norm=b_q_norm, b_w_o=b_w_o,
                   ffn_norm=ffn_norm, ffn_w_gu=ffn_w_gu, ffn_w_down=ffn_w_down)
    m_in = dict(meta=m_meta, a_norm=m_a_norm, a_w_in=m_a_w_in, a_conv=m_a_conv, a_w_out=m_a_w_out, kv_norm=m_kv_norm,
                w_kv=m_w_kv, k_norm=m_k_norm, w_f=m_w_f, b_f=m_b_f, b_norm=m_b_norm, b_w_q=m_b_w_q,
                b_q_norm=m_b_q_norm, b_w_o=m_b_w_o, ffn_norm=m_ffn_norm, ffn_w_gu=m_ffn_w_gu, ffn_w_down=m_ffn_w_down)
    v_in = dict(meta=v_meta, a_norm=v_a_norm, a_w_in=v_a_w_in, a_conv=v_a_conv, a_w_out=v_a_w_out, kv_norm=v_kv_norm,
                w_kv=v_w_kv, k_norm=v_k_norm, w_f=v_w_f, b_f=v_b_f, b_norm=v_b_norm, b_w_q=v_b_w_q,
                b_q_norm=v_b_q_norm, b_w_o=v_b_w_o, ffn_norm=v_ffn_norm, ffn_w_gu=v_ffn_w_gu, ffn_w_down=v_ffn_w_down)

    grads, deltas, new_m, new_v = {}, {}, {}, {}

    def update(name):
        w = weights[name]
        shape = w.shape
        two_d = (1, shape[0]) if w.ndim == 1 else (math.prod(shape[:-1]), shape[-1])
        r2 = lambda a: a.reshape(two_d)
        g = grad_of[name]()
        d_, m_, v_ = _adamw(r2(w), r2(g), r2(m_in[name]), r2(v_in[name]), f"adamw_{name}")
        deltas[name], new_m[name], new_v[name] = d_.reshape(shape), m_.reshape(shape), v_.reshape(shape)
        grads[name] = g.reshape(shape)

    late = {n for n, _ in last_names}
    for name in weights:
        if name not in late:
            update(name)
    shards.update(zip(last_names, _reduce_finish(last_pending, q_idx, c_idx, deltas["ffn_w_gu"])))
    for name in weights:
        if name in late:
            update(name)

    names = list(weights)
    return (loss, grad_x, *[grads[n] for n in names], *[deltas[n] for n in names],
            *[new_m[n] for n in names], *[new_v[n] for n in names])
```

```python
import functools
import math

import jax
import jax.numpy as jnp
from jax import lax
from jax.experimental import pallas as pl
from jax.experimental.pallas import tpu as pltpu

F32 = jnp.float32
BF16 = jnp.bfloat16
HEAD_DIM = 128
BLOCK = 128
LANES = 128
EPS = 1e-6
NEG = -1e30
ADAM_LR, ADAM_B1, ADAM_B2, ADAM_EPS, ADAM_WD, ADAM_STEP = 0.001, 0.9, 0.999, 1e-08, 0.01, 10
VMEM_LIMIT = 56 * 1024 * 1024
TILE_BUDGET = 40 * 1024 * 1024
MESH = pl.DeviceIdType.MESH
N_CHIPS = 4


def _tile(n, target, align):
    best = None
    for d in range(align, min(n, target) + 1, align):
        if n % d == 0:
            best = d
    return best if best is not None else n


def _cp(sem):
    return pltpu.CompilerParams(dimension_semantics=sem, vmem_limit_bytes=VMEM_LIMIT)


def _matmul(a, b, *, mode, out_dtype, name, res=None, out_parts=1, dep=None):
    a_parts = a.shape[0] if a.ndim == 3 else 1
    b_parts = b.shape[0] if b.ndim == 3 else 1
    if mode == "tn":
        K, M = a.shape
        Kp = K
    else:
        M, Kp = a.shape[-2:]
        K = Kp * a_parts
    N = b.shape[0] if mode == "nt" else b.shape[-1] * b_parts
    Np = N // max(b_parts, out_parts)
    tm = _tile(M, 1024, LANES) if mode == "tn" else _tile(M, 1056, 16)
    tn = _tile(Np, 1536, LANES)
    tk = _tile(Kp, 1056, 16) if mode == "tn" else _tile(Kp, 2048, LANES)
    ab, bb, ob = a.dtype.itemsize, b.dtype.itemsize, jnp.dtype(out_dtype).itemsize

    def vmem(tm_):
        blocks = 2 * (tm_ * tk * ab + tk * tn * bb + tm_ * tn * ob + (tm_ * tn * 4 if res is not None else 0))
        temps = tm_ * tn * 8 + (tm_ * tk * 2 if ab == 4 else 0) + (tk * tn * 2 if bb == 4 else 0)
        return blocks + temps

    while vmem(tm) > TILE_BUDGET and tm > 256:
        tm = _tile(M, tm // 2, LANES if mode == "tn" else 16)
    ni, nj, nk = M // tm, N // tn, K // tk
    nkp, njp = Kp // tk, Np // tn

    if mode == "tn":
        a_spec = pl.BlockSpec((tk, tm), lambda i, j, k: (k, i))
    elif a_parts > 1:
        a_spec = pl.BlockSpec((None, tm, tk), lambda i, j, k: (k // nkp, i, k % nkp))
    else:
        a_spec = pl.BlockSpec((tm, tk), lambda i, j, k: (i, k))
    if mode == "nt":
        b_spec = pl.BlockSpec((tn, tk), lambda i, j, k: (j, k))
    elif b_parts > 1:
        b_spec = pl.BlockSpec((None, tk, tn), lambda i, j, k: (j // njp, k, j % njp))
    else:
        b_spec = pl.BlockSpec((tk, tn), lambda i, j, k: (k, j))
    in_specs = [a_spec, b_spec]
    operands = [a, b]
    if res is not None:
        in_specs.append(pl.BlockSpec((tm, tn), lambda i, j, k: (i, j)))
        operands.append(res)
    if dep is not None:
        in_specs.append(pl.BlockSpec((8, LANES), lambda i, j, k: (0, 0)))
        operands.append(dep)
    n_in = len(operands)
    if out_parts > 1:
        out_spec = pl.BlockSpec((None, tm, tn), lambda i, j, k: (j // njp, i, j % njp))
        out_shape = jax.ShapeDtypeStruct((out_parts, M, Np), out_dtype)
    else:
        out_spec = pl.BlockSpec((tm, tn), lambda i, j, k: (i, j))
        out_shape = jax.ShapeDtypeStruct((M, N), out_dtype)
    dims = {"nn": (((1,), (0,)), ((), ())), "nt": (((1,), (1,)), ((), ())), "tn": (((0,), (0,)), ((), ()))}[mode]
    has_res = res is not None

    def body(*refs):
        a_ref, b_ref = refs[0], refs[1]
        res_ref = refs[2] if has_res else None
        o_ref = refs[n_in]
        d = lax.dot_general(a_ref[...].astype(BF16), b_ref[...].astype(BF16), dims, preferred_element_type=F32)
        if nk == 1:
            if has_res:
                d = d + res_ref[...]
            o_ref[...] = d.astype(out_dtype)
        else:
            acc_ref = refs[-1]
            k = pl.program_id(2)

            @pl.when(k == 0)
            def _():
                acc_ref[...] = d

            @pl.when(k > 0)
            def _():
                acc_ref[...] += d

            @pl.when(k == nk - 1)
            def _():
                r = acc_ref[...]
                if has_res:
                    r = r + res_ref[...]
                o_ref[...] = r.astype(out_dtype)

    return pl.pallas_call(
        body, name=name, grid=(ni, nj, nk), in_specs=in_specs, out_specs=out_spec, out_shape=out_shape,
        scratch_shapes=[pltpu.VMEM((tm, tn), F32)] if nk > 1 else [],
        compiler_params=_cp(("parallel", "parallel", "arbitrary")),
    )(*operands)


def _cast_into_full(w3, layer, axis, q_idx, name):
    _, R, C = w3.shape
    tr = _tile(R, max(16, (4 * 1024 * 1024) // (C * 4)), 16)
    nb = R // tr

    def body(q_ref, w_ref, o_ref):
        o_ref[...] = w_ref[...].astype(BF16)

    if axis == 0:
        out_spec = pl.BlockSpec((tr, C), lambda i, q_ref: (q_ref[0] * nb + i, 0))
        full = (N_CHIPS * R, C)
    else:
        out_spec = pl.BlockSpec((tr, C), lambda i, q_ref: (i, q_ref[0]))
        full = (R, N_CHIPS * C)
    grid_spec = pltpu.PrefetchScalarGridSpec(
        num_scalar_prefetch=1, grid=(nb,),
        in_specs=[pl.BlockSpec((None, tr, C), lambda i, q_ref: (layer, i, 0))], out_specs=out_spec)
    return pl.pallas_call(
        body, name=name, grid_spec=grid_spec, out_shape=jax.ShapeDtypeStruct(full, BF16),
        compiler_params=_cp(("parallel",)),
    )(q_idx, w3)


def _rms_fwd(h, g, name, dep=None):
    T, D = h.shape
    tr = _tile(T, 384, LANES)

    def body(h_ref, g_ref, *rest):
        o_ref, ot_ref = rest[-2:]
        x = h_ref[...]
        r = lax.rsqrt(jnp.mean(x * x, axis=-1, keepdims=True) + EPS)
        y = x * r * g_ref[...]
        o_ref[...] = y.astype(BF16)
        ot_ref[...] = y.T.astype(BF16)

    in_specs = [pl.BlockSpec((tr, D), lambda i: (i, 0)), pl.BlockSpec((1, D), lambda i: (0, 0))]
    operands = [h, g]
    if dep is not None:
        in_specs.append(pl.BlockSpec((8, LANES), lambda i: (0, 0)))
        operands.append(dep)
    return pl.pallas_call(
        body, name=name, grid=(T // tr,), in_specs=in_specs,
        out_specs=[pl.BlockSpec((tr, D), lambda i: (i, 0)), pl.BlockSpec((D, tr), lambda i: (0, i))],
        out_shape=[jax.ShapeDtypeStruct((T, D), BF16), jax.ShapeDtypeStruct((D, T), BF16)],
        compiler_params=_cp(("parallel",)),
    )(*operands)


def _rms_bwd(h, g, dxn, dh, name):
    T, D = h.shape
    tr = _tile(T, 264, 16)

    def body(h_ref, g_ref, dxn_ref, dh_ref, o_ref, ob_ref, dg_ref):
        x = h_ref[...]
        r = lax.rsqrt(jnp.mean(x * x, axis=-1, keepdims=True) + EPS)
        xh = x * r
        dy = dxn_ref[...]
        dxh = dy * g_ref[...]
        dx = r * (dxh - xh * jnp.mean(dxh * xh, axis=-1, keepdims=True))
        out = dh_ref[...] + dx
        o_ref[...] = out
        ob_ref[...] = out.astype(BF16)
        part = jnp.sum(dy * xh, axis=0, keepdims=True)

        @pl.when(pl.program_id(0) == 0)
        def _():
            dg_ref[...] = part

        @pl.when(pl.program_id(0) > 0)
        def _():
            dg_ref[...] += part

    row = pl.BlockSpec((tr, D), lambda i: (i, 0))
    vec = pl.BlockSpec((1, D), lambda i: (0, 0))
    return pl.pallas_call(
        body, name=name, grid=(T // tr,), in_specs=[row, vec, row, row], out_specs=[row, row, vec],
        out_shape=[jax.ShapeDtypeStruct((T, D), F32), jax.ShapeDtypeStruct((T, D), BF16),
                   jax.ShapeDtypeStruct((1, D), F32)],
        compiler_params=_cp(("arbitrary",)),
    )(h, g, dxn, dh)


def _shift_down(u, n, rows):
    return jnp.where(rows >= n, pltpu.roll(u, n, 0), 0.0)


def _shift_up(u, n, rows, total):
    return jnp.where(rows < total - n, pltpu.roll(u, total - n, 0), 0.0)


def _gate_fwd(z, conv_w, name):
    _, T, D = z.shape
    tc = LANES

    def body(b_ref, c_ref, h_ref, w_ref, y_ref, yt_ref):
        rows = lax.broadcasted_iota(jnp.int32, (T, tc), 0)
        u = c_ref[...].astype(F32) * h_ref[...].astype(F32)
        w0, w1, w2 = w_ref[0:1, :], w_ref[1:2, :], w_ref[2:3, :]
        conv = u * w2 + _shift_down(u, 1, rows) * w1 + _shift_down(u, 2, rows) * w0
        y = b_ref[...].astype(F32) * conv
        y_ref[...] = y.astype(BF16)
        yt_ref[...] = y.T.astype(BF16)

    part = lambda p: pl.BlockSpec((None, T, tc), lambda j, p=p: (p, 0, j))
    return pl.pallas_call(
        body, name=name, grid=(D // tc,),
        in_specs=[part(0), part(1), part(2), pl.BlockSpec((3, tc), lambda j: (0, j))],
        out_specs=[pl.BlockSpec((T, tc), lambda j: (0, j)), pl.BlockSpec((tc, T), lambda j: (j, 0))],
        out_shape=[jax.ShapeDtypeStruct((T, D), BF16), jax.ShapeDtypeStruct((D, T), BF16)],
        compiler_params=_cp(("parallel",)),
    )(z, z, z, conv_w)


def _gate_bwd(z, conv_w, dy, name):
    _, T, D = z.shape
    tc = LANES

    def body(b_ref, c_ref, h_ref, w_ref, dy_ref, dz_ref, dw_ref):
        rows = lax.broadcasted_iota(jnp.int32, (T, tc), 0)
        cg, hh = c_ref[...].astype(F32), h_ref[...].astype(F32)
        u = cg * hh
        w0, w1, w2 = w_ref[0:1, :], w_ref[1:2, :], w_ref[2:3, :]
        s1, s2 = _shift_down(u, 1, rows), _shift_down(u, 2, rows)
        g = dy_ref[...]
        dz_ref[0] = (g * (u * w2 + s1 * w1 + s2 * w0)).astype(BF16)
        dconv = g * b_ref[...].astype(F32)
        dw_ref[0:1, :] = jnp.sum(dconv * s2, axis=0, keepdims=True)
        dw_ref[1:2, :] = jnp.sum(dconv * s1, axis=0, keepdims=True)
        dw_ref[2:3, :] = jnp.sum(dconv * u, axis=0, keepdims=True)
        du = dconv * w2 + _shift_up(dconv, 1, rows, T) * w1 + _shift_up(dconv, 2, rows, T) * w0
        dz_ref[1] = (du * hh).astype(BF16)
        dz_ref[2] = (du * cg).astype(BF16)

    part = lambda p: pl.BlockSpec((None, T, tc), lambda j, p=p: (p, 0, j))
    return pl.pallas_call(
        body, name=name, grid=(D // tc,),
        in_specs=[part(0), part(1), part(2), pl.BlockSpec((3, tc), lambda j: (0, j)),
                  pl.BlockSpec((T, tc), lambda j: (0, j))],
        out_specs=[pl.BlockSpec((3, T, tc), lambda j: (0, 0, j)), pl.BlockSpec((3, tc), lambda j: (0, j))],
        out_shape=[jax.ShapeDtypeStruct((3, T, D), BF16), jax.ShapeDtypeStruct((3, D), F32)],
        compiler_params=_cp(("parallel",)),
    )(z, z, z, conv_w, dy)


def _swiglu_fwd(z, name):
    _, T, Fd = z.shape
    tr, tc = _tile(T, 384, LANES), _tile(Fd, 1408, LANES)

    def body(g_ref, u_ref, o_ref, ot_ref):
        g = g_ref[...].astype(F32)
        a = g * jax.nn.sigmoid(g) * u_ref[...].astype(F32)
        o_ref[...] = a.astype(BF16)
        ot_ref[...] = a.T.astype(BF16)

    part = lambda p: pl.BlockSpec((None, tr, tc), lambda i, j, p=p: (p, i, j))
    return pl.pallas_call(
        body, name=name, grid=(T // tr, Fd // tc), in_specs=[part(0), part(1)],
        out_specs=[pl.BlockSpec((tr, tc), lambda i, j: (i, j)), pl.BlockSpec((tc, tr), lambda i, j: (j, i))],
        out_shape=[jax.ShapeDtypeStruct((T, Fd), BF16), jax.ShapeDtypeStruct((Fd, T), BF16)],
        compiler_params=_cp(("parallel", "parallel")),
    )(z, z)


def _swiglu_bwd(z, da, name):
    _, T, Fd = z.shape
    tr, tc = _tile(T, 528, 16), _tile(Fd, 1408, LANES)

    def body(g_ref, u_ref, da_ref, dz_ref):
        g, d = g_ref[...].astype(F32), da_ref[...].astype(F32)
        s = jax.nn.sigmoid(g)
        dz_ref[0] = (d * u_ref[...].astype(F32) * (s * (1.0 + g * (1.0 - s)))).astype(BF16)
        dz_ref[1] = (d * (g * s)).astype(BF16)

    part = lambda p: pl.BlockSpec((None, tr, tc), lambda i, j, p=p: (p, i, j))
    return pl.pallas_call(
        body, name=name, grid=(T // tr, Fd // tc),
        in_specs=[part(0), part(1), pl.BlockSpec((tr, tc), lambda i, j: (i, j))],
        out_specs=pl.BlockSpec((2, tr, tc), lambda i, j: (0, i, j)),
        out_shape=jax.ShapeDtypeStruct((2, T, Fd), BF16), compiler_params=_cp(("parallel", "parallel")),
    )(z, z, da)


def _headnorm_fwd(z, part, g, name):
    _, T, D = z.shape
    tr = _tile(T, 1056, 16)

    def body(z_ref, g_ref, o_ref):
        x = z_ref[...]
        r = lax.rsqrt(jnp.mean(x * x, axis=-1, keepdims=True) + EPS)
        o_ref[...] = (x * r * g_ref[...]).astype(BF16)

    return pl.pallas_call(
        body, name=name, grid=(T // tr, D // HEAD_DIM),
        in_specs=[pl.BlockSpec((None, tr, HEAD_DIM), lambda i, h: (part, i, h)),
                  pl.BlockSpec((1, HEAD_DIM), lambda i, h: (0, 0))],
        out_specs=pl.BlockSpec((tr, HEAD_DIM), lambda i, h: (i, h)),
        out_shape=jax.ShapeDtypeStruct((T, D), BF16), compiler_params=_cp(("parallel", "parallel")),
    )(z, g)


def _headnorm_bwd(z, part, g, dy, name):
    _, T, D = z.shape
    tr = _tile(T, 1056, 16)

    def body(z_ref, g_ref, dy_ref, dz_ref, dg_ref):
        x = z_ref[...]
        r = lax.rsqrt(jnp.mean(x * x, axis=-1, keepdims=True) + EPS)
        xh = x * r
        dy_ = dy_ref[...]
        dxh = dy_ * g_ref[...]
        dz_ref[...] = (r * (dxh - xh * jnp.mean(dxh * xh, axis=-1, keepdims=True))).astype(BF16)
        partial = jnp.sum(dy_ * xh, axis=0, keepdims=True)
        first = (pl.program_id(0) == 0) & (pl.program_id(1) == 0)

        @pl.when(first)
        def _():
            dg_ref[...] = partial

        @pl.when(jnp.logical_not(first))
        def _():
            dg_ref[...] += partial

    blk = pl.BlockSpec((tr, HEAD_DIM), lambda i, h: (i, h))
    vec = pl.BlockSpec((1, HEAD_DIM), lambda i, h: (0, 0))
    return pl.pallas_call(
        body, name=name, grid=(T // tr, D // HEAD_DIM),
        in_specs=[pl.BlockSpec((None, tr, HEAD_DIM), lambda i, h: (part, i, h)), vec, blk],
        out_specs=[blk, vec],
        out_shape=[jax.ShapeDtypeStruct((T, D), BF16), jax.ShapeDtypeStruct((1, HEAD_DIM), F32)],
        compiler_params=_cp(("arbitrary", "arbitrary")),
    )(z, g, dy)


def _cast_part(z, part, name):
    _, T, D = z.shape
    tr = _tile(T, 528, 16)

    def body(z_ref, o_ref):
        o_ref[...] = z_ref[...].astype(BF16)

    return pl.pallas_call(
        body, name=name, grid=(T // tr,),
        in_specs=[pl.BlockSpec((None, tr, D), lambda i: (part, i, 0))],
        out_specs=pl.BlockSpec((tr, D), lambda i: (i, 0)),
        out_shape=jax.ShapeDtypeStruct((T, D), BF16), compiler_params=_cp(("parallel",)),
    )(z)


def _split3(x):
    a = x.astype(BF16)
    r = x - a.astype(F32)
    b = r.astype(BF16)
    c = (r - b.astype(F32)).astype(BF16)
    return a, b, c


def _tri_matmul(tri, x):
    a, b, c = _split3(x)
    dot = lambda v: jnp.dot(tri, v, preferred_element_type=F32)
    return (dot(c) + dot(b)) + dot(a)


def _logf_cumsum(pre, bias, pad, name):
    T = pre.shape[0]
    nb = T // BLOCK

    def body(p_ref, b_ref, c_ref, carry):
        i = pl.program_id(0)

        @pl.when(i == 0)
        def _():
            carry[...] = jnp.zeros_like(carry)

        x = p_ref[...] + b_ref[...]
        lf = jnp.minimum(x, 0.0) - jnp.log(1.0 + jnp.exp(-jnp.abs(x)))
        rows = i * BLOCK + lax.broadcasted_iota(jnp.int32, (BLOCK, LANES), 0)
        lf = jnp.where(rows >= pad, lf, 0.0)
        r = lax.broadcasted_iota(jnp.int32, (BLOCK, BLOCK), 0)
        c = lax.broadcasted_iota(jnp.int32, (BLOCK, BLOCK), 1)
        tri = jnp.where(c <= r, 1.0, 0.0).astype(BF16)
        c_ref[...] = _tri_matmul(tri, lf) + carry[...]
        carry[...] = c_ref[BLOCK - 1:BLOCK, :]

    return pl.pallas_call(
        body, name=name, grid=(nb,),
        in_specs=[pl.BlockSpec((BLOCK, LANES), lambda i: (i, 0)), pl.BlockSpec((1, LANES), lambda i: (0, 0))],
        out_specs=pl.BlockSpec((BLOCK, LANES), lambda i: (i, 0)),
        out_shape=jax.ShapeDtypeStruct((T, LANES), F32),
        scratch_shapes=[pltpu.VMEM((1, LANES), F32)], compiler_params=_cp(("arbitrary",)),
    )(pre, bias)


def _logf_bwd(pre, bias, dc, pad, name):
    T = pre.shape[0]
    nb = T // BLOCK

    def body(p_ref, b_ref, dc_ref, dp_ref, db_ref, carry, dlf_ref):
        i = pl.program_id(0)

        @pl.when(i == 0)
        def _():
            carry[...] = jnp.zeros_like(carry)

        r = lax.broadcasted_iota(jnp.int32, (BLOCK, BLOCK), 0)
        c = lax.broadcasted_iota(jnp.int32, (BLOCK, BLOCK), 1)
        tri = jnp.where(c >= r, 1.0, 0.0).astype(BF16)
        dlf_ref[...] = _tri_matmul(tri, dc_ref[...]) + carry[...]
        carry[...] = dlf_ref[0:1, :]
        dlf = dlf_ref[...]
        x = p_ref[...] + b_ref[...]
        rows = (nb - 1 - i) * BLOCK + lax.broadcasted_iota(jnp.int32, (BLOCK, LANES), 0)
        dpre = jnp.where(rows >= pad, dlf * jax.nn.sigmoid(-x), 0.0)
        dp_ref[...] = dpre
        partial = jnp.sum(dpre, axis=0, keepdims=True)

        @pl.when(i == 0)
        def _():
            db_ref[...] = partial

        @pl.when(i > 0)
        def _():
            db_ref[...] += partial

    rev = pl.BlockSpec((BLOCK, LANES), lambda i: (nb - 1 - i, 0))
    vec = pl.BlockSpec((1, LANES), lambda i: (0, 0))
    return pl.pallas_call(
        body, name=name, grid=(nb,), in_specs=[rev, vec, rev], out_specs=[rev, vec],
        out_shape=[jax.ShapeDtypeStruct((T, LANES), F32), jax.ShapeDtypeStruct((1, LANES), F32)],
        scratch_shapes=[pltpu.VMEM((1, LANES), F32), pltpu.VMEM((BLOCK, LANES), F32)],
        compiler_params=_cp(("arbitrary",)),
    )(pre, bias, dc)


def _loss_head(h, target, first, name):
    T, D = h.shape
    tr = BLOCK
    skip = first // tr

    def body(h_ref, t_ref, dh_ref, dhb_ref, loss_ref):
        i = pl.program_id(0)

        @pl.when(i == 0)
        def _():
            loss_ref[...] = jnp.zeros_like(loss_ref)

        @pl.when(i < skip)
        def _():
            dh_ref[...] = jnp.zeros_like(dh_ref)
            dhb_ref[...] = jnp.zeros_like(dhb_ref)

        @pl.when(i >= skip)
        def _():
            err = h_ref[...] - t_ref[...]
            dh_ref[...] = err * (1.0 / D)
            dhb_ref[...] = (err * (1.0 / D)).astype(BF16)
            loss_ref[...] += jnp.sum(err * err) * (0.5 / D)

    row = pl.BlockSpec((tr, D), lambda i: (i, 0))
    return pl.pallas_call(
        body, name=name, grid=(T // tr,),
        in_specs=[row, pl.BlockSpec((tr, D), lambda i: (jnp.maximum(i - skip, 0), 0))],
        out_specs=[row, row, pl.BlockSpec((8, LANES), lambda i: (0, 0))],
        out_shape=[jax.ShapeDtypeStruct((T, D), F32), jax.ShapeDtypeStruct((T, D), BF16),
                   jax.ShapeDtypeStruct((8, LANES), F32)],
        compiler_params=_cp(("arbitrary",)),
    )(h, target)


def _adamw(w, g, m, v, name):
    R, C = w.shape
    tr = _tile(R, max(8, TILE_BUDGET // (C * 4 * 7 * 3)), 8)
    bc1, bc2 = 1.0 - ADAM_B1 ** ADAM_STEP, 1.0 - ADAM_B2 ** ADAM_STEP

    def body(w_ref, g_ref, m_ref, v_ref, d_ref, mo_ref, vo_ref):
        g_ = g_ref[...]
        m_ = ADAM_B1 * m_ref[...] + (1.0 - ADAM_B1) * g_
        v_ = ADAM_B2 * v_ref[...] + (1.0 - ADAM_B2) * (g_ * g_)
        d_ref[...] = -ADAM_LR * ((m_ / bc1) / (jnp.sqrt(v_ / bc2) + ADAM_EPS) + ADAM_WD * w_ref[...])
        mo_ref[...] = m_
        vo_ref[...] = v_

    blk = pl.BlockSpec((tr, C), lambda i: (i, 0))
    sds = jax.ShapeDtypeStruct((R, C), F32)
    return pl.pallas_call(
        body, name=name, grid=(R // tr,), in_specs=[blk] * 4, out_specs=[blk] * 3, out_shape=[sds] * 3,
        compiler_params=_cp(("parallel",)),
    )(w, g, m, v)


def _pick_head(c_blk, h):
    lane = lax.broadcasted_iota(jnp.int32, c_blk.shape, 1)
    return jnp.sum(jnp.where(lane == h, c_blk, 0.0), axis=1, keepdims=True)


def _attn_fwd(q, k, v, c, ct, *, blk, pad, name):
    T, D = q.shape
    H, nk = D // HEAD_DIM, T // blk
    bq = blk
    nq = T // bq
    scale = 1.0 / math.sqrt(HEAD_DIM)

    def body(q_ref, k_ref, v_ref, c_ref, ct_ref, o_ref, lse_ref):
        h, i = pl.program_id(0), pl.program_id(1)
        qb = q_ref[...]
        cq = _pick_head(c_ref[...], h)
        last = (i * bq) // blk

        def step_fn(masked):
            def step(j, carry):
                m, l, acc = carry
                off = j * blk if isinstance(j, int) else pl.multiple_of(j * blk, blk)
                kb = k_ref[pl.ds(off, blk), :]
                vb = v_ref[pl.ds(off, blk), :]
                s = lax.dot_general(qb, kb, (((1,), (1,)), ((), ())), preferred_element_type=F32) * scale
                s = s + (cq - ct_ref[j])
                if masked:
                    qpos = i * bq + lax.broadcasted_iota(jnp.int32, (bq, blk), 0)
                    kpos = j * blk + lax.broadcasted_iota(jnp.int32, (bq, blk), 1)
                    s = jnp.where((kpos <= qpos) & (kpos >= pad), s, NEG)
                m_new = jnp.maximum(m, jnp.max(s, axis=1, keepdims=True))
                p = jnp.exp(s - m_new)
                alpha = jnp.exp(m - m_new)
                l = alpha * l + jnp.sum(p, axis=1, keepdims=True)
                acc = alpha * acc + jnp.dot(p.astype(BF16), vb, preferred_element_type=F32)
                return m_new, l, acc
            return step

        carry = (jnp.full((bq, 1), NEG, F32), jnp.zeros((bq, 1), F32), jnp.zeros((bq, HEAD_DIM), F32))
        carry = step_fn(True)(0, carry)
        carry = lax.fori_loop(1, last, step_fn(False), carry)
        m, l, acc = lax.cond(last > 0, lambda c: step_fn(True)(last, c), lambda c: c, carry)
        rowpos = i * bq + lax.broadcasted_iota(jnp.int32, (bq, 1), 0)
        o_ref[...] = jnp.where(rowpos >= pad, acc / l, 0.0)
        lse_ref[...] = jnp.broadcast_to(m + jnp.log(l), (bq, LANES))

    return pl.pallas_call(
        body, name=name, grid=(H, nq),
        in_specs=[pl.BlockSpec((bq, HEAD_DIM), lambda h, i: (i, h)),
                  pl.BlockSpec((T, HEAD_DIM), lambda h, i: (0, h)),
                  pl.BlockSpec((T, HEAD_DIM), lambda h, i: (0, h)),
                  pl.BlockSpec((bq, LANES), lambda h, i: (i, 0)),
                  pl.BlockSpec((None, nk, 1, blk), lambda h, i: (h, 0, 0, 0))],
        out_specs=[pl.BlockSpec((bq, HEAD_DIM), lambda h, i: (i, h)),
                   pl.BlockSpec((None, bq, LANES), lambda h, i: (h, i, 0))],
        out_shape=[jax.ShapeDtypeStruct((T, D), F32), jax.ShapeDtypeStruct((H, T, LANES), F32)],
        compiler_params=_cp(("parallel", "arbitrary")),
    )(q, k, v, c, ct)


def _attn_bwd(q, k, v, o, do, lse, c, ct, prev, *, blk, pad, name):
    T, D = q.shape
    H, nq = D // HEAD_DIM, T // blk
    scale = 1.0 / math.sqrt(HEAD_DIM)
    has_prev = prev is not None

    nt_dims = (((1,), (1,)), ((), ()))
    tn_dims = (((0,), (0,)), ((), ()))

    def body(*refs):
        q_ref, k_ref, v_ref, o_ref, do_ref, lse_ref, c_ref, ct_ref = refs[:8]
        pdk_ref, pdv_ref, pdc_ref = refs[8:11] if has_prev else (None, None, None)
        dq_ref, dk_ref, dv_ref, dct_ref = refs[-4:]
        h, j = pl.program_id(0), pl.program_id(1)

        @pl.when(j == 0)
        def _():
            dq_ref[...] = jnp.zeros_like(dq_ref)

        kb, vb = k_ref[...], v_ref[...]
        ck = ct_ref[...]

        def step_fn(masked):
            def step(i, carry):
                dk, dv, dck = carry
                off = pl.multiple_of(i * blk, blk)
                qb = q_ref[pl.ds(off, blk), :]
                dob = do_ref[pl.ds(off, blk), :]
                lse_i = lse_ref[pl.ds(off, blk), :][:, 0:1]
                cq = _pick_head(c_ref[pl.ds(off, blk), :], h)
                delta = jnp.sum(dob.astype(F32) * o_ref[pl.ds(off, blk), :], axis=1, keepdims=True)
                s = lax.dot_general(qb, kb, nt_dims, preferred_element_type=F32) * scale
                p = jnp.exp(s + (cq - ck) - lse_i)
                if masked:
                    qpos = i * blk + lax.broadcasted_iota(jnp.int32, (blk, blk), 0)
                    kpos = j * blk + lax.broadcasted_iota(jnp.int32, (blk, blk), 1)
                    p = jnp.where((kpos <= qpos) & (kpos >= pad), p, 0.0)
                dp = lax.dot_general(dob, vb, nt_dims, preferred_element_type=F32)
                ds = p * (dp - delta)
                pb, dsb = p.astype(BF16), ds.astype(BF16)
                dv = dv + lax.dot_general(pb, dob, tn_dims, preferred_element_type=F32)
                dk = dk + lax.dot_general(dsb, qb, tn_dims, preferred_element_type=F32)
                dck = dck - jnp.sum(ds, axis=0, keepdims=True)
                dq_ref[pl.ds(off, blk), :] += jnp.dot(dsb, kb, preferred_element_type=F32) * scale
                return dk, dv, dck
            return step

        carry = (jnp.zeros((blk, HEAD_DIM), F32), jnp.zeros((blk, HEAD_DIM), F32), jnp.zeros((1, blk), F32))
        carry = step_fn(True)(j, carry)
        below = lambda masked: (lambda c: lax.fori_loop(j + 1, nq, step_fn(masked), c))
        dk, dv, dck = lax.cond(j == 0, below(True), below(False), carry)
        dk = dk * scale
        if has_prev:
            dk, dv, dck = dk + pdk_ref[...], dv + pdv_ref[...], dck + pdc_ref[...]
        dk_ref[...] = dk
        dv_ref[...] = dv
        dct_ref[...] = dck

    col = pl.BlockSpec((T, HEAD_DIM), lambda h, j: (0, h))
    kblk = pl.BlockSpec((blk, HEAD_DIM), lambda h, j: (j, h))
    ctb = pl.BlockSpec((None, None, 1, blk), lambda h, j: (h, j, 0, 0))
    in_specs = [col, kblk, kblk, col, col,
                pl.BlockSpec((None, T, LANES), lambda h, j: (h, 0, 0)),
                pl.BlockSpec((T, LANES), lambda h, j: (0, 0)), ctb]
    operands = [q, k, v, o, do, lse, c, ct]
    if has_prev:
        in_specs += [kblk, kblk, ctb]
        operands += list(prev)
    return pl.pallas_call(
        body, name=name, grid=(H, nq), in_specs=in_specs, out_specs=[col, kblk, kblk, ctb],
        out_shape=[jax.ShapeDtypeStruct((T, D), F32), jax.ShapeDtypeStruct((T, D), F32),
                   jax.ShapeDtypeStruct((T, D), F32), jax.ShapeDtypeStruct((H, nq, 1, blk), F32)],
        compiler_params=_cp(("parallel", "arbitrary")),
    )(*operands)


HBM_SPEC = pl.BlockSpec(memory_space=pltpu.HBM)


def _place():
    x, y, c = lax.axis_index("x"), lax.axis_index("y"), lax.axis_index("c")
    chips = [(1 - x, y), (x, 1 - y), (1 - x, 1 - y)]
    return x, y, c, chips


def _remote(src, dst, send_sems, recv_sems, k, to):
    return pltpu.make_async_remote_copy(src_ref=src, dst_ref=dst, send_sem=send_sems.at[k],
                                        recv_sem=recv_sems.at[k], device_id=to, device_id_type=MESH)


def _all_gather(shards, specs, name):
    n = len(shards)

    def full_shape(s, axis):
        return (s.shape[0] * N_CHIPS, s.shape[1]) if axis == 0 else (s.shape[0], s.shape[1] * N_CHIPS)

    def body(*refs):
        srcs, outs = refs[:n], refs[n:2 * n]
        send_sems, recv_sems, local_sems = refs[2 * n:]
        x, y, c, chips = _place()
        sibling = (x, y, 1 - c)

        def region(t, chip, half):
            rs, cs = shards[t].shape
            q = 2 * chip[0] + chip[1]
            axis, split = specs[t]
            nrow = rs // 2 if half is not None else rs
            r0 = 0 if half is None else half * nrow
            if axis == 0:
                return outs[t].at[pl.ds(q * rs + r0, nrow), :]
            return outs[t].at[pl.ds(r0, nrow), pl.ds(pl.multiple_of(q * cs, cs), cs)]

        def piece(t, half):
            rs = shards[t].shape[0]
            if half is None:
                return srcs[t]
            return srcs[t].at[pl.ds(half * (rs // 2), rs // 2), :]

        local = [pltpu.make_async_copy(srcs[t], region(t, (x, y), None), local_sems.at[t]) for t in range(n)]
        for cp in local:
            cp.start()
        sends = []
        for t in range(n):
            half = c if specs[t][1] else None
            for j, chip in enumerate(chips):
                cp = _remote(piece(t, half), region(t, (x, y), half), send_sems, recv_sems, 6 * t + j, (*chip, c))
                cp.start()
                sends.append(cp)
        for t in range(n):
            half = c if specs[t][1] else None
            for j, chip in enumerate(chips):
                landed = region(t, chip, half)
                _remote(landed, landed, send_sems, recv_sems, 6 * t + j, (*chip, c)).wait_recv()
                if specs[t][1]:
                    cp = _remote(landed, landed, send_sems, recv_sems, 6 * t + 3 + j, sibling)
                    cp.start()
                    sends.append(cp)
        for t in range(n):
            if specs[t][1]:
                for j, chip in enumerate(chips):
                    got = region(t, chip, 1 - c)
                    _remote(got, got, send_sems, recv_sems, 6 * t + 3 + j, sibling).wait_recv()
        for cp in sends:
            cp.wait_send()
        for cp in local:
            cp.wait()

    return pl.pallas_call(
        body, name=name, in_specs=[HBM_SPEC] * n, out_specs=[HBM_SPEC] * n,
        out_shape=[jax.ShapeDtypeStruct(full_shape(s, specs[t][0]), s.dtype) for t, s in enumerate(shards)],
        scratch_shapes=[pltpu.SemaphoreType.DMA((6 * n,)), pltpu.SemaphoreType.DMA((6 * n,)),
                        pltpu.SemaphoreType.DMA((n,))],
        compiler_params=pltpu.CompilerParams(has_side_effects=True),
    )(*shards)


def _grad_view(g, axis):
    R, C = g.shape
    nq = N_CHIPS if axis == 0 else 1
    return g.reshape(nq, 2, R // (2 * nq), C)


def _add_half(view, got, c_idx, name):
    nq, _, Rh, C = view.shape
    tr = _tile(Rh, max(16, (2 * 1024 * 1024) // (C * 2)), 16)

    def body(c_ref, a_ref, b_ref, o_ref):
        o_ref[...] = (a_ref[...].astype(F32) + b_ref[...].astype(F32)).astype(BF16)

    grid_spec = pltpu.PrefetchScalarGridSpec(
        num_scalar_prefetch=1, grid=(nq, Rh // tr),
        in_specs=[pl.BlockSpec((None, None, tr, C), lambda q, i, c_ref: (q, c_ref[0], i, 0)),
                  pl.BlockSpec((None, tr, C), lambda q, i, c_ref: (q, i, 0))],
        out_specs=pl.BlockSpec((None, tr, C), lambda q, i, c_ref: (q, i, 0)))
    return pl.pallas_call(
        body, name=name, grid_spec=grid_spec, out_shape=jax.ShapeDtypeStruct((nq, Rh, C), BF16),
        compiler_params=_cp(("parallel", "parallel")),
    )(c_idx, view, got)


SEM_SPEC = pl.BlockSpec(memory_space=pltpu.SEMAPHORE)
ANY_SPEC = pl.BlockSpec(memory_space=pl.ANY)
TOKEN_SPEC = pl.BlockSpec(memory_space=pltpu.VMEM)
TOKEN = jax.ShapeDtypeStruct((8, LANES), F32)
SPLIT_COPY = pltpu.CompilerParams(has_side_effects=pltpu.SideEffectType.DATAFLOW_SIDE_EFFECTING)


def _region(ref, axis, chip, half):
    q = 2 * chip[0] + chip[1]
    if axis == 0:
        rs = ref.shape[0] // N_CHIPS
        return ref.at[pl.ds(q * rs + half * (rs // 2), rs // 2), :]
    rh, cs = ref.shape[0] // 2, ref.shape[1] // N_CHIPS
    return ref.at[pl.ds(half * rh, rh), pl.ds(pl.multiple_of(q * cs, cs), cs)]


def _gather_start(fulls, axes, after, name):
    n = len(fulls)

    def body(*refs):
        ins = refs[:n]
        token = refs[-1]
        send_sems, recv_sems = refs[n + 1], refs[n + 2]
        x, y, c, chips = _place()
        for t in range(n):
            mine = _region(ins[t], axes[t], (x, y), c)
            for j, chip in enumerate(chips):
                _remote(mine, mine, send_sems, recv_sems, 3 * t + j, (*chip, c)).start()
        token[...] = jnp.zeros_like(token)

    sems = pltpu.SemaphoreType.DMA((3 * n,))
    outs = pl.pallas_call(
        body, name=name, in_specs=[HBM_SPEC] * n + [ANY_SPEC],
        out_specs=[SEM_SPEC, SEM_SPEC] + [HBM_SPEC] * n + [TOKEN_SPEC],
        out_shape=[sems, sems] + [jax.ShapeDtypeStruct(f.shape, f.dtype) for f in fulls] + [TOKEN],
        input_output_aliases={t: 2 + t for t in range(n)}, compiler_params=SPLIT_COPY,
    )(*fulls, after)
    return outs[0], outs[1], list(outs[2:2 + n]), outs[-1]


def _gather_wait(send_sems, recv_sems, fulls, axes, after, name):
    n = len(fulls)

    def body(*refs):
        ins = refs[:n]
        send_sems, recv_sems = refs[n], refs[n + 1]
        x, y, c, chips = _place()
        for t in range(n):
            mine = _region(ins[t], axes[t], (x, y), c)
            for j, chip in enumerate(chips):
                _remote(mine, mine, send_sems, recv_sems, 3 * t + j, (*chip, c)).wait_send()
                theirs = _region(ins[t], axes[t], chip, c)
                _remote(theirs, theirs, send_sems, recv_sems, 3 * t + j, (*chip, c)).wait_recv()

    outs = pl.pallas_call(
        body, name=name, in_specs=[HBM_SPEC] * n + [SEM_SPEC, SEM_SPEC, ANY_SPEC], out_specs=[HBM_SPEC] * n,
        out_shape=[jax.ShapeDtypeStruct(f.shape, f.dtype) for f in fulls],
        input_output_aliases={t: t for t in range(n)}, compiler_params=SPLIT_COPY,
    )(*fulls, send_sems, recv_sems, after)
    return list(outs)


def _gather_forward(fulls, axes, name):
    n = len(fulls)

    def body(*refs):
        outs = refs[n:2 * n]
        send_sems, recv_sems = refs[2 * n], refs[2 * n + 1]
        x, y, c, chips = _place()
        sibling = (x, y, 1 - c)
        cps = []
        for t in range(n):
            for j, chip in enumerate(chips):
                landed = _region(outs[t], axes[t], chip, c)
                cps.append(_remote(landed, landed, send_sems, recv_sems, 3 * t + j, sibling))
        for cp in cps:
            cp.start()
        for t in range(n):
            for j, chip in enumerate(chips):
                got = _region(outs[t], axes[t], chip, 1 - c)
                _remote(got, got, send_sems, recv_sems, 3 * t + j, sibling).wait_recv()
        for cp in cps:
            cp.wait_send()

    outs = pl.pallas_call(
        body, name=name, in_specs=[HBM_SPEC] * n, out_specs=[HBM_SPEC] * n,
        out_shape=[jax.ShapeDtypeStruct(f.shape, f.dtype) for f in fulls],
        scratch_shapes=[pltpu.SemaphoreType.DMA((3 * n,)), pltpu.SemaphoreType.DMA((3 * n,))],
        input_output_aliases={t: t for t in range(n)},
        compiler_params=pltpu.CompilerParams(has_side_effects=True),
    )(*fulls)
    return list(outs)


def _shard_cols(s, axis):
    return s.shape[2] if axis == 0 else s.shape[2] // N_CHIPS


def _piece(ref, axis, chip):
    q = 2 * chip[0] + chip[1]
    if axis == 0:
        return ref.at[q]
    cs = ref.shape[2] // N_CHIPS
    return ref.at[0, :, pl.ds(pl.multiple_of(q * cs, cs), cs)]


def _scatter_start(sums, axes, after, name):
    n = len(sums)

    def body(*refs):
        ins = refs[:n]
        send_sems, recv_sems = refs[n + 1], refs[n + 2]
        lands = refs[2 * n + 3:3 * n + 3]
        token = refs[-1]
        x, y, c, chips = _place()
        for t in range(n):
            for j, chip in enumerate(chips):
                _remote(_piece(ins[t], axes[t], chip), lands[t].at[j], send_sems, recv_sems, 3 * t + j, (*chip, c)).start()
        token[...] = jnp.zeros_like(token)

    sems = pltpu.SemaphoreType.DMA((3 * n,))
    land_shapes = [jax.ShapeDtypeStruct((3, s.shape[1], _shard_cols(s, a)), s.dtype) for s, a in zip(sums, axes)]
    outs = pl.pallas_call(
        body, name=name, in_specs=[HBM_SPEC] * n + [ANY_SPEC],
        out_specs=[SEM_SPEC, SEM_SPEC] + [HBM_SPEC] * (2 * n) + [TOKEN_SPEC],
        out_shape=[sems, sems] + [jax.ShapeDtypeStruct(s.shape, s.dtype) for s in sums] + land_shapes + [TOKEN],
        input_output_aliases={t: 2 + t for t in range(n)}, compiler_params=SPLIT_COPY,
    )(*sums, after)
    return outs[0], outs[1], list(outs[2:2 + n]), list(outs[2 + n:2 + 2 * n]), outs[-1]


def _scatter_wait(send_sems, recv_sems, sums, lands, axes, after, name):
    n = len(sums)

    def body(*refs):
        ins, lnd = refs[:n], refs[n:2 * n]
        send_sems, recv_sems = refs[2 * n], refs[2 * n + 1]
        x, y, c, chips = _place()
        for t in range(n):
            for j, chip in enumerate(chips):
                cp = _remote(_piece(ins[t], axes[t], chip), lnd[t].at[j], send_sems, recv_sems, 3 * t + j, (*chip, c))
                cp.wait_send()
                cp.wait_recv()

    outs = pl.pallas_call(
        body, name=name, in_specs=[HBM_SPEC] * (2 * n) + [SEM_SPEC, SEM_SPEC, ANY_SPEC], out_specs=[HBM_SPEC] * (2 * n),
        out_shape=[jax.ShapeDtypeStruct(s.shape, s.dtype) for s in sums + lands],
        input_output_aliases={t: t for t in range(2 * n)}, compiler_params=SPLIT_COPY,
    )(*sums, *lands, send_sems, recv_sems, after)
    return list(outs[:n]), list(outs[n:])


def _sum_chips(own, axis, got, q_idx, c_idx, name):
    _, Rh, cc = got.shape
    tr = _tile(Rh, max(16, (1024 * 1024) // (cc * 2)), 16)

    def body(q_ref, c_ref, a_ref, b0, b1, b2, o_ref):
        f = lambda r: r[...].astype(F32)
        o_ref[...] = ((f(a_ref) + f(b0)) + f(b1)) + f(b2)

    if axis == 0:
        own_spec = pl.BlockSpec((None, tr, cc), lambda i, q, c: (q[0], i, 0))
    else:
        own_spec = pl.BlockSpec((None, tr, cc), lambda i, q, c: (0, i, q[0]))
    slot = lambda j: pl.BlockSpec((None, tr, cc), lambda i, q, c, j=j: (j, i, 0))
    grid_spec = pltpu.PrefetchScalarGridSpec(
        num_scalar_prefetch=2, grid=(Rh // tr,), in_specs=[own_spec, slot(0), slot(1), slot(2)],
        out_specs=pl.BlockSpec((None, tr, cc), lambda i, q, c: (c[0], i, 0)))
    return pl.pallas_call(
        body, name=name, grid_spec=grid_spec, out_shape=jax.ShapeDtypeStruct((2, Rh, cc), F32),
        compiler_params=_cp(("parallel",)),
    )(q_idx, c_idx, own, got, got, got)


def _join_halves(pairs, name):
    n = len(pairs)

    def body(*refs):
        outs = refs[n:2 * n]
        send_sems, recv_sems = refs[2 * n], refs[2 * n + 1]
        x, y, c, _ = _place()
        cps = [_remote(outs[t].at[c], outs[t].at[c], send_sems, recv_sems, t, (x, y, 1 - c)) for t in range(n)]
        for cp in cps:
            cp.start()
        for t in range(n):
            cps[t].wait_send()
            _remote(outs[t].at[1 - c], outs[t].at[1 - c], send_sems, recv_sems, t, (x, y, 1 - c)).wait_recv()

    outs = pl.pallas_call(
        body, name=name, in_specs=[HBM_SPEC] * n, out_specs=[HBM_SPEC] * n,
        out_shape=[jax.ShapeDtypeStruct(p.shape, p.dtype) for p in pairs],
        scratch_shapes=[pltpu.SemaphoreType.DMA((n,)), pltpu.SemaphoreType.DMA((n,))],
        input_output_aliases={t: t for t in range(n)},
        compiler_params=pltpu.CompilerParams(has_side_effects=True),
    )(*pairs)
    return list(outs)


def _swap_start(views, after, name):
    n = len(views)

    def body(*refs):
        ins = refs[:n]
        send_sems, recv_sems = refs[n + 1], refs[n + 2]
        lands = refs[2 * n + 3:3 * n + 3]
        token = refs[-1]
        x, y, c, _ = _place()
        for t in range(n):
            _remote(ins[t].at[:, 1 - c], lands[t], send_sems, recv_sems, t, (x, y, 1 - c)).start()
        token[...] = jnp.zeros_like(token)

    sems = pltpu.SemaphoreType.DMA((n,))
    land_shapes = [jax.ShapeDtypeStruct((v.shape[0],) + v.shape[2:], v.dtype) for v in views]
    outs = pl.pallas_call(
        body, name=name, in_specs=[HBM_SPEC] * n + [ANY_SPEC],
        out_specs=[SEM_SPEC, SEM_SPEC] + [HBM_SPEC] * (2 * n) + [TOKEN_SPEC],
        out_shape=[sems, sems] + [jax.ShapeDtypeStruct(v.shape, v.dtype) for v in views] + land_shapes + [TOKEN],
        input_output_aliases={t: 2 + t for t in range(n)}, compiler_params=SPLIT_COPY,
    )(*views, after)
    return outs[0], outs[1], list(outs[2:2 + n]), list(outs[2 + n:2 + 2 * n]), outs[-1]


def _swap_wait(send_sems, recv_sems, views, lands, after, name):
    n = len(views)

    def body(*refs):
        ins, lnd = refs[:n], refs[n:2 * n]
        send_sems, recv_sems = refs[2 * n], refs[2 * n + 1]
        x, y, c, _ = _place()
        for t in range(n):
            cp = _remote(ins[t].at[:, 1 - c], lnd[t], send_sems, recv_sems, t, (x, y, 1 - c))
            cp.wait_send()
            cp.wait_recv()

    outs = pl.pallas_call(
        body, name=name, in_specs=[HBM_SPEC] * (2 * n) + [SEM_SPEC, SEM_SPEC, ANY_SPEC], out_specs=[HBM_SPEC] * (2 * n),
        out_shape=[jax.ShapeDtypeStruct(a.shape, a.dtype) for a in views + lands],
        input_output_aliases={t: t for t in range(2 * n)}, compiler_params=SPLIT_COPY,
    )(*views, *lands, send_sems, recv_sems, after)
    return list(outs[:n]), list(outs[n:])


def _reduce_begin(views, got, axes, c_idx, after, tag):
    sums = [_add_half(v, p, c_idx, f"rs_add_{tag}_{t}") for t, (v, p) in enumerate(zip(views, got))]
    send_sems, recv_sems, sums, lands, token = _scatter_start(sums, axes, after, f"rs_chips_start_{tag}")
    return (send_sems, recv_sems, sums, lands, axes, tag), token


def _reduce_finish(pending, q_idx, c_idx, after):
    send_sems, recv_sems, sums, lands, axes, tag = pending
    sums, lands = _scatter_wait(send_sems, recv_sems, sums, lands, axes, after, f"rs_chips_wait_{tag}")
    pairs = [_sum_chips(s, a, r, q_idx, c_idx, f"rs_sum_{tag}_{t}")
             for t, (s, a, r) in enumerate(zip(sums, axes, lands))]
    joined = _join_halves(pairs, f"rs_join_{tag}")
    return [j.reshape(2 * j.shape[1], j.shape[2]) for j in joined]


def _all_reduce_small(buf, name):
    R, C = buf.shape

    def body(b_ref, o_ref, slots, send_sems, recv_sems):
        x, y, c, _ = _place()
        me = 4 * x + 2 * y + c
        cps = []
        for k in range(1, 8):
            to = (x ^ (k >> 2), y ^ ((k >> 1) & 1), c ^ (k & 1))
            cps.append(pltpu.make_async_remote_copy(
                src_ref=b_ref, dst_ref=slots.at[me], send_sem=send_sems.at[k - 1], recv_sem=recv_sems.at[me],
                device_id=to, device_id_type=MESH))
        for cp in cps:
            cp.start()
        slots[me] = b_ref[...]
        for k in range(1, 8):
            src = me ^ k
            pltpu.make_async_remote_copy(
                src_ref=b_ref, dst_ref=slots.at[src], send_sem=send_sems.at[k - 1], recv_sem=recv_sems.at[src],
                device_id=(x, y, c), device_id_type=MESH).wait_recv()
        for cp in cps:
            cp.wait_send()
        total = slots[0]
        for d in range(1, 8):
            total = total + slots[d]
        o_ref[...] = total

    vm = pl.BlockSpec(memory_space=pltpu.VMEM)
    return pl.pallas_call(
        body, name=name, in_specs=[vm], out_specs=vm, out_shape=jax.ShapeDtypeStruct((R, C), F32),
        scratch_shapes=[pltpu.VMEM((8, R, C), F32), pltpu.SemaphoreType.DMA((7,)), pltpu.SemaphoreType.DMA((8,))],
        compiler_params=pltpu.CompilerParams(has_side_effects=True, vmem_limit_bytes=VMEM_LIMIT),
    )(buf)


def kernel(x, meta, a_norm, a_w_in, a_conv, a_w_out, kv_norm, w_kv, k_norm, w_f, b_f, b_norm, b_w_q, b_q_norm, b_w_o, ffn_norm, ffn_w_gu, ffn_w_down, loss_target, m_meta, m_a_norm, m_a_w_in, m_a_conv, m_a_w_out, m_kv_norm, m_w_kv, m_k_norm, m_w_f, m_b_f, m_b_norm, m_b_w_q, m_b_q_norm, m_b_w_o, m_ffn_norm, m_ffn_w_gu, m_ffn_w_down, v_meta, v_a_norm, v_a_w_in, v_a_conv, v_a_w_out, v_kv_norm, v_w_kv, v_k_norm, v_w_f, v_b_f, v_b_norm, v_b_w_q, v_b_q_norm, v_b_w_o, v_ffn_norm, v_ffn_w_gu, v_ffn_w_down):
    SEQ, D = x.shape[1], x.shape[2]
    n_meta = meta.shape[0]
    pad = BLOCK - n_meta
    first = pad + n_meta
    T = first + SEQ
    H = D // HEAD_DIM
    Ds = D // N_CHIPS
    n_a, n_b, depth = a_w_in.shape[0], b_w_q.shape[0], ffn_norm.shape[0]
    blk = _tile(T, 384, BLOCK)
    cx, cy, cc = lax.axis_index("x"), lax.axis_index("y"), lax.axis_index("c")
    q_me = 2 * cx + cy
    c_idx = jnp.reshape(cc, (1,)).astype(jnp.int32)
    q_idx = jnp.reshape(q_me, (1,)).astype(jnp.int32)
    rows8 = lambda v: jnp.pad(v, ((0, -v.shape[0] % 8), (0, 0)))

    col_parts = [meta, a_norm, a_conv.reshape(3 * n_a, Ds)]
    col_pack = jnp.concatenate([rows8(p) for p in col_parts], axis=0)
    col_full, w_f_full = _all_gather([col_pack, w_f], [(1, False), (0, False)], "ag_small")
    col_offs = [sum(rows8(p).shape[0] for p in col_parts[:i]) for i in range(3)]
    meta_f = col_full[:n_meta]
    a_norm_f = col_full[col_offs[1]:col_offs[1] + n_a]
    a_conv_f = col_full[col_offs[2]:col_offs[2] + 3 * n_a].reshape(n_a, 3, D)
    w_fp = jnp.pad(w_f_full, ((0, 0), (0, LANES - H))).astype(BF16)
    b_fp = jnp.pad(b_f, (0, LANES - H)).reshape(1, LANES)

    stages = []
    for l in range(n_a):
        stages += [[(a_w_in, l, 1), (a_w_out, l, 0)], [(ffn_w_gu, l, 1), (ffn_w_down, l, 0)]]
    for j in range(n_b):
        stages += [[(b_w_q, j, 0), (b_w_o, j, 0)], [(ffn_w_gu, n_a + j, 1), (ffn_w_down, n_a + j, 0)]]
    stages[2 * n_a].append((w_kv[None], 0, 1))

    def gather_begin(k, after):
        axes = [ax for _, _, ax in stages[k]]
        fulls = [_cast_into_full(w, l, ax, q_idx, f"cast_{k}_{i}") for i, (w, l, ax) in enumerate(stages[k])]
        send_sems, recv_sems, fulls, token = _gather_start(fulls, axes, after, f"ag_start_{k}")
        return (send_sems, recv_sems, fulls, axes, k), token

    def gather_end(handle, after):
        send_sems, recv_sems, fulls, axes, k = handle
        fulls = _gather_wait(send_sems, recv_sems, fulls, axes, after, f"ag_wait_{k}")
        return _gather_forward(fulls, axes, f"ag_forward_{k}")

    handle, _ = gather_begin(0, col_full)
    arrived = [gather_end(handle, col_full)]
    gathering = []

    def enter_segment():
        w = arrived[-1]
        after, token = w[0], None
        while len(gathering) < 2 and len(arrived) + len(gathering) < len(stages):
            handle, token = gather_begin(len(arrived) + len(gathering), after)
            gathering.append(handle)
            after = token
        return w, token

    def leave_segment(h_out):
        if gathering:
            arrived.append(gather_end(gathering.pop(0), h_out))

    h = jnp.concatenate([jnp.zeros((pad, D), F32), meta_f, x[0]], axis=0)
    saved = []

    def ffn_fwd(h, layer):
        (w_gu, w_down), token = enter_segment()
        xn, xn_t = _rms_fwd(h, ffn_norm[layer:layer + 1], f"ffn_norm_{layer}", dep=token)
        z = _matmul(xn, w_gu, mode="nn", out_dtype=BF16, name=f"ffn_gu_{layer}", out_parts=2)
        act, act_t = _swiglu_fwd(z, f"swiglu_{layer}")
        out = _matmul(act, w_down, mode="nn", out_dtype=F32, name=f"ffn_down_{layer}", res=h)
        leave_segment(out)
        return out, (h, xn_t, z, act_t), (w_gu, w_down)

    wa, wb = [], []
    for l in range(n_a):
        (w_in, w_out), token = enter_segment()
        xn, xn_t = _rms_fwd(h, a_norm_f[l:l + 1], f"a_norm_{l}", dep=token)
        z = _matmul(xn, w_in, mode="nn", out_dtype=BF16, name=f"a_in_{l}", out_parts=3)
        y, y_t = _gate_fwd(z, a_conv_f[l], f"a_gate_{l}")
        h2 = _matmul(y, w_out, mode="nn", out_dtype=F32, name=f"a_out_{l}", res=h)
        leave_segment(h2)
        h3, ffn_saved, w_ffn = ffn_fwd(h2, l)
        saved.append((h, xn_t, z, y_t, ffn_saved))
        wa.append((w_in, w_out) + w_ffn)
        h = h3

    for j in range(n_b):
        layer = n_a + j
        w_mix, token = enter_segment()
        w_q, w_o = w_mix[:2]
        if j == 0:
            h_kv, w_kv_b = h, w_mix[2]
            xkv, xkv_t = _rms_fwd(h, kv_norm.reshape(1, D), "kv_norm", dep=token)
            token = None
            kvz = _matmul(xkv, w_kv_b, mode="nn", out_dtype=F32, name="kv_proj", out_parts=2)
            k_n = _headnorm_fwd(kvz, 0, k_norm.reshape(1, HEAD_DIM), "k_headnorm")
            v_b = _cast_part(kvz, 1, "v_cast")
            pre = _matmul(xkv, w_fp, mode="nn", out_dtype=F32, name="f_proj")
            c_cum = _logf_cumsum(pre, b_fp, pad, "logf_cumsum")
            c_t = c_cum[:, :H].T.reshape(H, T // blk, 1, blk)
        xn, xn_t = _rms_fwd(h, b_norm[j:j + 1], f"b_norm_{j}", dep=token)
        qz = _matmul(xn, w_q, mode="nn", out_dtype=F32, name=f"b_q_{j}")[None]
        q_n = _headnorm_fwd(qz, 0, b_q_norm[j:j + 1], f"q_headnorm_{j}")
        o, lse = _attn_fwd(q_n, k_n, v_b, c_cum, c_t, blk=blk, pad=pad, name=f"attn_fwd_{j}")
        h2 = _matmul(o, w_o, mode="nn", out_dtype=F32, name=f"b_o_{j}", res=h)
        leave_segment(h2)
        h3, ffn_saved, w_ffn = ffn_fwd(h2, layer)
        saved.append((h, xn_t, qz, q_n, o, lse, ffn_saved))
        wb.append((w_q, w_o) + w_ffn)
        h = h3

    dh, dhb, loss_blk = _loss_head(h, loss_target[0], first, "loss_head")
    loss = lax.psum(loss_blk[0, 0], ("x", "y", "c"))

    shards = {}
    swapping = []
    reducing = []

    def advance(done, after):
        token = None
        if len(reducing) == 2:
            prev_names, pending = reducing.pop(0)
            got = _reduce_finish(pending, q_idx, c_idx, done)
            shards.update(zip(prev_names, got))
            after = got[0]
        if swapping:
            names, (send_sems, recv_sems, views, lands), axes, tag = swapping.pop()
            views, lands = _swap_wait(send_sems, recv_sems, views, lands, done, f"rs_swap_wait_{tag}")
            pending, token = _reduce_begin(views, lands, axes, c_idx, after, tag)
            reducing.append((names, pending))
            after = token
        return after, token

    def reduce_later(names, grads, axes, tag, done):
        after, _ = advance(done, c_idx)
        views = [_grad_view(g, a) for g, a in zip(grads, axes)]
        send_sems, recv_sems, views, lands, token = _swap_start(views, after, f"rs_swap_start_{tag}")
        swapping.append((names, (send_sems, recv_sems, views, lands), axes, tag))
        return token

    def ffn_bwd(dh, dhb, layer, w_gu, w_down, ffn_saved, dep):
        h_in, xn_t, z, act_t = ffn_saved
        da = _matmul(dhb, w_down, mode="nt", out_dtype=BF16, name=f"ffn_down_dx_{layer}", dep=dep)
        g_down = _matmul(act_t, dhb, mode="nn", out_dtype=BF16, name=f"ffn_down_dw_{layer}")
        dz = _swiglu_bwd(z, da, f"swiglu_bwd_{layer}")
        dxn = _matmul(dz, w_gu, mode="nt", out_dtype=F32, name=f"ffn_gu_dx_{layer}")
        g_gu = _matmul(xn_t, dz, mode="nn", out_dtype=BF16, name=f"ffn_gu_dw_{layer}")
        dh, dhb, dg = _rms_bwd(h_in, ffn_norm[layer:layer + 1], dxn, dh, f"ffn_norm_bwd_{layer}")
        token = reduce_later([("ffn_w_gu", layer), ("ffn_w_down", layer)], [g_gu, g_down], [1, 0], f"f{layer}", dh)
        return dh, dhb, dg, token

    d_ffn_norm, d_b_norm, d_q_norm, d_a_norm, d_a_conv = {}, {}, {}, {}, {}
    kv_prev = None
    token = None
    for j in reversed(range(n_b)):
        layer = n_a + j
        w_q, w_o, w_gu, w_down = wb[j]
        h_in, xn_t, qz, q_n, o, lse, ffn_saved = saved[layer]
        dh, dhb, d_ffn_norm[layer], token = ffn_bwd(dh, dhb, layer, w_gu, w_down, ffn_saved, token)
        do = _matmul(dhb, w_o, mode="nt", out_dtype=BF16, name=f"b_o_dx_{j}", dep=token)
        g_o = _matmul(o.astype(BF16).T, dhb, mode="nn", out_dtype=BF16, name=f"b_o_dw_{j}")
        dq, dk, dv, dct = _attn_bwd(q_n, k_n, v_b, o, do, lse, c_cum, c_t, kv_prev, blk=blk, pad=pad,
                                    name=f"attn_bwd_{j}")
        kv_prev = (dk, dv, dct)
        dqz, d_q_norm[j] = _headnorm_bwd(qz, 0, b_q_norm[j:j + 1], dq, f"q_headnorm_bwd_{j}")
        dxn = _matmul(dqz, w_q, mode="nt", out_dtype=F32, name=f"b_q_dx_{j}")
        g_q = _matmul(xn_t, dqz, mode="nn", out_dtype=BF16, name=f"b_q_dw_{j}")
        dh, dhb, d_b_norm[j] = _rms_bwd(h_in, b_norm[j:j + 1], dxn, dh, f"b_norm_bwd_{j}")
        if j > 0:
            token = reduce_later([("b_w_q", j), ("b_w_o", j)], [g_q, g_o], [0, 0], f"b{j}", dh)

    dk, dv, dct = kv_prev
    dkz, d_k_norm = _headnorm_bwd(kvz, 0, k_norm.reshape(1, HEAD_DIM), dk, "k_headnorm_bwd")
    dvz = _cast_part(dv[None], 0, "dv_cast")
    dkv = jnp.stack([dkz, dvz])
    dc = jnp.pad(dct.reshape(H, T).T, ((0, 0), (0, LANES - H)))
    dpre, d_b_f = _logf_bwd(pre, b_fp, dc, pad, "logf_bwd")
    dxkv = _matmul(dkv, w_kv_b, mode="nt", out_dtype=F32, name="kv_proj_dx")
    dxkv = _matmul(dpre, w_fp, mode="nt", out_dtype=F32, name="f_proj_dx", res=dxkv)
    g_kv = _matmul(xkv_t, dkv, mode="nn", out_dtype=BF16, name="kv_proj_dw")
    d_w_f = _matmul(xkv_t, dpre, mode="nn", out_dtype=F32, name="f_proj_dw")
    dh, dhb, d_kv_norm = _rms_bwd(h_kv, kv_norm.reshape(1, D), dxkv, dh, "kv_norm_bwd")
    token = reduce_later([("b_w_q", 0), ("b_w_o", 0), ("w_kv", 0)], [g_q, g_o, g_kv], [0, 0, 1], "b0", dh)

    for l in reversed(range(n_a)):
        w_in, w_out, w_gu, w_down = wa[l]
        h_in, xn_t, z, y_t, ffn_saved = saved[l]
        dh, dhb, d_ffn_norm[l], token = ffn_bwd(dh, dhb, l, w_gu, w_down, ffn_saved, token)
        dy = _matmul(dhb, w_out, mode="nt", out_dtype=F32, name=f"a_out_dx_{l}", dep=token)
        g_out = _matmul(y_t, dhb, mode="nn", out_dtype=BF16, name=f"a_out_dw_{l}")
        dz, d_a_conv[l] = _gate_bwd(z, a_conv_f[l], dy, f"a_gate_bwd_{l}")
        dxn = _matmul(dz, w_in, mode="nt", out_dtype=F32, name=f"a_in_dx_{l}")
        g_in = _matmul(xn_t, dz, mode="nn", out_dtype=BF16, name=f"a_in_dw_{l}")
        dh, dhb, d_a_norm[l] = _rms_bwd(h_in, a_norm_f[l:l + 1], dxn, dh, f"a_norm_bwd_{l}")
        token = reduce_later([("a_w_in", l), ("a_w_out", l)], [g_in, g_out], [1, 0], f"a{l}", dh)

    advance(dh, c_idx)
    last_names, last_pending = reducing.pop()
    for prev_names, pending in reducing:
        shards.update(zip(prev_names, _reduce_finish(pending, q_idx, c_idx, dh)))
    grad_x = dh[first:][None]

    widen = lambda v: jnp.pad(v, ((0, 0), (0, D - v.shape[1])))
    groups = [
        [dh[pad:first]],
        [d_a_norm[l] for l in range(n_a)],
        [d_a_conv[l] for l in range(n_a)],
        [d_kv_norm],
        [widen(d_k_norm)],
        [d_w_f[:, :H].T],
        [widen(d_b_f)],
        [d_b_norm[j] for j in range(n_b)],
        [widen(d_q_norm[j]) for j in range(n_b)],
        [d_ffn_norm[l] for l in range(depth)],
    ]
    pack = jnp.concatenate([rows8(p) for g in groups for p in g], axis=0)
    red = _all_reduce_small(pack, "ar_small")
    taken, off = [], 0
    for g in groups:
        r, rp = g[0].shape[0], rows8(g[0]).shape[0]
        taken.append(red[off:off + len(g) * rp].reshape(len(g), rp, D)[:, :r].reshape(len(g) * r, D))
        off += len(g) * rp
    take = lambda i: taken[i]
    mine = lambda a: lax.dynamic_slice_in_dim(a, q_me * Ds, Ds, axis=1)
    layers_of = lambda name, n: (lambda: jnp.stack([shards[(name, l)] for l in range(n)]))
    grad_of = {
        "meta": lambda: mine(take(0)),
        "a_norm": lambda: mine(take(1)),
        "a_w_in": layers_of("a_w_in", n_a),
        "a_conv": lambda: mine(take(2)).reshape(n_a, 3, Ds),
        "a_w_out": layers_of("a_w_out", n_a),
        "kv_norm": lambda: take(3).reshape(D),
        "w_kv": lambda: shards[("w_kv", 0)],
        "k_norm": lambda: take(4)[0, :HEAD_DIM],
        "w_f": lambda: mine(take(5)).T,
        "b_f": lambda: take(6)[0, :H],
        "b_norm": lambda: take(7),
        "b_w_q": layers_of("b_w_q", n_b),
        "b_q_norm": lambda: take(8)[:, :HEAD_DIM],
        "b_w_o": layers_of("b_w_o", n_b),
        "ffn_norm": lambda: take(9),
        "ffn_w_gu": layers_of("ffn_w_gu", depth),
        "ffn_w_down": layers_of("ffn_w_down", depth),
    }
    weights = dict(meta=meta, a_norm=a_norm, a_w_in=a_w_in, a_conv=a_conv, a_w_out=a_w_out, kv_norm=kv_norm, w_kv=w_kv,
                   k_norm=k_norm, w_f=w_f, b_f=b_f, b_norm=b_norm, b_w_q=b_w_q, b_q_norm=b_q_norm, b_w_o=b_w_o,
                   ffn_norm=ffn_norm, ffn_w_gu=ffn_w_gu, ffn_w_down=ffn_w_down)
    m_in = dict(meta=m_meta, a_norm=m_a_norm, a_w_in=m_a_w_in, a_conv=m_a_conv, a_w_out=m_a_w_out, kv_norm=m_kv_norm,
                w_kv=m_w_kv, k_norm=m_k_norm, w_f=m_w_f, b_f=m_b_f, b_norm=m_b_norm, b_w_q=m_b_w_q,
                b_q_norm=m_b_q_norm, b_w_o=m_b_w_o, ffn_norm=m_ffn_norm, ffn_w_gu=m_ffn_w_gu, ffn_w_down=m_ffn_w_down)
    v_in = dict(meta=v_meta, a_norm=v_a_norm, a_w_in=v_a_w_in, a_conv=v_a_conv, a_w_out=v_a_w_out, kv_norm=v_kv_norm,
                w_kv=v_w_kv, k_norm=v_k_norm, w_f=v_w_f, b_f=v_b_f, b_norm=v_b_norm, b_w_q=v_b_w_q,
                b_q_norm=v_b_q_norm, b_w_o=v_b_w_o, ffn_norm=v_ffn_norm, ffn_w_gu=v_ffn_w_gu, ffn_w_down=v_ffn_w_down)

    grads, deltas, new_m, new_v = {}, {}, {}, {}

    def update(name):
        w = weights[name]
        shape = w.shape
        two_d = (1, shape[0]) if w.ndim == 1 else (math.prod(shape[:-1]), shape[-1])
        r2 = lambda a: a.reshape(two_d)
        g = grad_of[name]()
        d_, m_, v_ = _adamw(r2(w), r2(g), r2(m_in[name]), r2(v_in[name]), f"adamw_{name}")
        deltas[name], new_m[name], new_v[name] = d_.reshape(shape), m_.reshape(shape), v_.reshape(shape)
        grads[name] = g.reshape(shape)

    late = {n for n, _ in last_names}
    for name in weights:
        if name not in late:
            update(name)
    shards.update(zip(last_names, _reduce_finish(last_pending, q_idx, c_idx, deltas["ffn_w_gu"])))
    for name in weights:
        if name in late:
            update(name)

    names = list(weights)
    return (loss, grad_x, *[grads[n] for n in names], *[deltas[n] for n in names],
            *[new_m[n] for n in names], *[new_v[n] for n in names])
```

```python
import functools
import math

import jax
import jax.numpy as jnp
from jax import lax
from jax.experimental import pallas as pl
from jax.experimental.pallas import tpu as pltpu

F32 = jnp.float32
BF16 = jnp.bfloat16
HEAD_DIM = 128
BLOCK = 128
LANES = 128
EPS = 1e-6
NEG = -1e30
ADAM_LR, ADAM_B1, ADAM_B2, ADAM_EPS, ADAM_WD, ADAM_STEP = 0.001, 0.9, 0.999, 1e-08, 0.01, 10
VMEM_LIMIT = 56 * 1024 * 1024
TILE_BUDGET = 40 * 1024 * 1024
MESH = pl.DeviceIdType.MESH
N_CHIPS = 4


def _tile(n, target, align):
    best = None
    for d in range(align, min(n, target) + 1, align):
        if n % d == 0:
            best = d
    return best if best is not None else n


def _cp(sem):
    return pltpu.CompilerParams(dimension_semantics=sem, vmem_limit_bytes=VMEM_LIMIT)


def _matmul(a, b, *, mode, out_dtype, name, res=None, out_parts=1, dep=None):
    a_parts = a.shape[0] if a.ndim == 3 else 1
    b_parts = b.shape[0] if b.ndim == 3 else 1
    if mode == "tn":
        K, M = a.shape
        Kp = K
    else:
        M, Kp = a.shape[-2:]
        K = Kp * a_parts
    N = b.shape[0] if mode == "nt" else b.shape[-1] * b_parts
    Np = N // max(b_parts, out_parts)
    tm = _tile(M, 1024, LANES) if mode == "tn" else _tile(M, 1056, 16)
    tn = _tile(Np, 1536, LANES)
    tk = _tile(Kp, 1056, 16) if mode == "tn" else _tile(Kp, 2048, LANES)
    ab, bb, ob = a.dtype.itemsize, b.dtype.itemsize, jnp.dtype(out_dtype).itemsize

    def vmem(tm_):
        blocks = 2 * (tm_ * tk * ab + tk * tn * bb + tm_ * tn * ob + (tm_ * tn * 4 if res is not None else 0))
        temps = tm_ * tn * 8 + (tm_ * tk * 2 if ab == 4 else 0) + (tk * tn * 2 if bb == 4 else 0)
        return blocks + temps

    while vmem(tm) > TILE_BUDGET and tm > 256:
        tm = _tile(M, tm // 2, LANES if mode == "tn" else 16)
    ni, nj, nk = M // tm, N // tn, K // tk
    nkp, njp = Kp // tk, Np // tn

    if mode == "tn":
        a_spec = pl.BlockSpec((tk, tm), lambda i, j, k: (k, i))
    elif a_parts > 1:
        a_spec = pl.BlockSpec((None, tm, tk), lambda i, j, k: (k // nkp, i, k % nkp))
    else:
        a_spec = pl.BlockSpec((tm, tk), lambda i, j, k: (i, k))
    if mode == "nt":
        b_spec = pl.BlockSpec((tn, tk), lambda i, j, k: (j, k))
    elif b_parts > 1:
        b_spec = pl.BlockSpec((None, tk, tn), lambda i, j, k: (j // njp, k, j % njp))
    else:
        b_spec = pl.BlockSpec((tk, tn), lambda i, j, k: (k, j))
    in_specs = [a_spec, b_spec]
    operands = [a, b]
    if res is not None:
        in_specs.append(pl.BlockSpec((tm, tn), lambda i, j, k: (i, j)))
        operands.append(res)
    if dep is not None:
        in_specs.append(pl.BlockSpec((8, LANES), lambda i, j, k: (0, 0)))
        operands.append(dep)
    n_in = len(operands)
    if out_parts > 1:
        out_spec = pl.BlockSpec((None, tm, tn), lambda i, j, k: (j // njp, i, j % njp))
        out_shape = jax.ShapeDtypeStruct((out_parts, M, Np), out_dtype)
    else:
        out_spec = pl.BlockSpec((tm, tn), lambda i, j, k: (i, j))
        out_shape = jax.ShapeDtypeStruct((M, N), out_dtype)
    dims = {"nn": (((1,), (0,)), ((), ())), "nt": (((1,), (1,)), ((), ())), "tn": (((0,), (0,)), ((), ()))}[mode]
    has_res = res is not None

    def body(*refs):
        a_ref, b_ref = refs[0], refs[1]
        res_ref = refs[2] if has_res else None
        o_ref = refs[n_in]
        d = lax.dot_general(a_ref[...].astype(BF16), b_ref[...].astype(BF16), dims, preferred_element_type=F32)
        if nk == 1:
            if has_res:
                d = d + res_ref[...]
            o_ref[...] = d.astype(out_dtype)
        else:
            acc_ref = refs[-1]
            k = pl.program_id(2)

            @pl.when(k == 0)
            def _():
                acc_ref[...] = d

            @pl.when(k > 0)
            def _():
                acc_ref[...] += d

            @pl.when(k == nk - 1)
            def _():
                r = acc_ref[...]
                if has_res:
                    r = r + res_ref[...]
                o_ref[...] = r.astype(out_dtype)

    return pl.pallas_call(
        body, name=name, grid=(ni, nj, nk), in_specs=in_specs, out_specs=out_spec, out_shape=out_shape,
        scratch_shapes=[pltpu.VMEM((tm, tn), F32)] if nk > 1 else [],
        compiler_params=_cp(("parallel", "parallel", "arbitrary")),
    )(*operands)


def _cast_into_full(w3, layer, axis, q_idx, name):
    _, R, C = w3.shape
    tr = _tile(R, max(16, (4 * 1024 * 1024) // (C * 4)), 16)
    nb = R // tr

    def body(q_ref, w_ref, o_ref):
        o_ref[...] = w_ref[...].astype(BF16)

    if axis == 0:
        out_spec = pl.BlockSpec((tr, C), lambda i, q_ref: (q_ref[0] * nb + i, 0))
        full = (N_CHIPS * R, C)
    else:
        out_spec = pl.BlockSpec((tr, C), lambda i, q_ref: (i, q_ref[0]))
        full = (R, N_CHIPS * C)
    grid_spec = pltpu.PrefetchScalarGridSpec(
        num_scalar_prefetch=1, grid=(nb,),
        in_specs=[pl.BlockSpec((None, tr, C), lambda i, q_ref: (layer, i, 0))], out_specs=out_spec)
    return pl.pallas_call(
        body, name=name, grid_spec=grid_spec, out_shape=jax.ShapeDtypeStruct(full, BF16),
        compiler_params=_cp(("parallel",)),
    )(q_idx, w3)


def _rms_fwd(h, g, name, dep=None):
    T, D = h.shape
    tr = _tile(T, 384, LANES)

    def body(h_ref, g_ref, *rest):
        o_ref, ot_ref = rest[-2:]
        x = h_ref[...]
        r = lax.rsqrt(jnp.mean(x * x, axis=-1, keepdims=True) + EPS)
        y = x * r * g_ref[...]
        o_ref[...] = y.astype(BF16)
        ot_ref[...] = y.T.astype(BF16)

    in_specs = [pl.BlockSpec((tr, D), lambda i: (i, 0)), pl.BlockSpec((1, D), lambda i: (0, 0))]
    operands = [h, g]
    if dep is not None:
        in_specs.append(pl.BlockSpec((8, LANES), lambda i: (0, 0)))
        operands.append(dep)
    return pl.pallas_call(
        body, name=name, grid=(T // tr,), in_specs=in_specs,
        out_specs=[pl.BlockSpec((tr, D), lambda i: (i, 0)), pl.BlockSpec((D, tr), lambda i: (0, i))],
        out_shape=[jax.ShapeDtypeStruct((T, D), BF16), jax.ShapeDtypeStruct((D, T), BF16)],
        compiler_params=_cp(("parallel",)),
    )(*operands)


def _rms_bwd(h, g, dxn, dh, name):
    T, D = h.shape
    tr = _tile(T, 264, 16)

    def body(h_ref, g_ref, dxn_ref, dh_ref, o_ref, ob_ref, dg_ref):
        x = h_ref[...]
        r = lax.rsqrt(jnp.mean(x * x, axis=-1, keepdims=True) + EPS)
        xh = x * r
        dy = dxn_ref[...]
        dxh = dy * g_ref[...]
        dx = r * (dxh - xh * jnp.mean(dxh * xh, axis=-1, keepdims=True))
        out = dh_ref[...] + dx
        o_ref[...] = out
        ob_ref[...] = out.astype(BF16)
        part = jnp.sum(dy * xh, axis=0, keepdims=True)

        @pl.when(pl.program_id(0) == 0)
        def _():
            dg_ref[...] = part

        @pl.when(pl.program_id(0) > 0)
        def _():
            dg_ref[...] += part

    row = pl.BlockSpec((tr, D), lambda i: (i, 0))
    vec = pl.BlockSpec((1, D), lambda i: (0, 0))
    return pl.pallas_call(
        body, name=name, grid=(T // tr,), in_specs=[row, vec, row, row], out_specs=[row, row, vec],
        out_shape=[jax.ShapeDtypeStruct((T, D), F32), jax.ShapeDtypeStruct((T, D), BF16),
                   jax.ShapeDtypeStruct((1, D), F32)],
        compiler_params=_cp(("arbitrary",)),
    )(h, g, dxn, dh)


def _shift_down(u, n, rows):
    return jnp.where(rows >= n, pltpu.roll(u, n, 0), 0.0)


def _shift_up(u, n, rows, total):
    return jnp.where(rows < total - n, pltpu.roll(u, total - n, 0), 0.0)


def _gate_fwd(z, conv_w, name):
    _, T, D = z.shape
    tc = LANES

    def body(b_ref, c_ref, h_ref, w_ref, y_ref, yt_ref):
        rows = lax.broadcasted_iota(jnp.int32, (T, tc), 0)
        u = c_ref[...].astype(F32) * h_ref[...].astype(F32)
        w0, w1, w2 = w_ref[0:1, :], w_ref[1:2, :], w_ref[2:3, :]
        conv = u * w2 + _shift_down(u, 1, rows) * w1 + _shift_down(u, 2, rows) * w0
        y = b_ref[...].astype(F32) * conv
        y_ref[...] = y.astype(BF16)
        yt_ref[...] = y.T.astype(BF16)

    part = lambda p: pl.BlockSpec((None, T, tc), lambda j, p=p: (p, 0, j))
    return pl.pallas_call(
        body, name=name, grid=(D // tc,),
        in_specs=[part(0), part(1), part(2), pl.BlockSpec((3, tc), lambda j: (0, j))],
        out_specs=[pl.BlockSpec((T, tc), lambda j: (0, j)), pl.BlockSpec((tc, T), lambda j: (j, 0))],
        out_shape=[jax.ShapeDtypeStruct((T, D), BF16), jax.ShapeDtypeStruct((D, T), BF16)],
        compiler_params=_cp(("parallel",)),
    )(z, z, z, conv_w)


def _gate_bwd(z, conv_w, dy, name):
    _, T, D = z.shape
    tc = LANES

    def body(b_ref, c_ref, h_ref, w_ref, dy_ref, dz_ref, dw_ref):
        rows = lax.broadcasted_iota(jnp.int32, (T, tc), 0)
        cg, hh = c_ref[...].astype(F32), h_ref[...].astype(F32)
        u = cg * hh
        w0, w1, w2 = w_ref[0:1, :], w_ref[1:2, :], w_ref[2:3, :]
        s1, s2 = _shift_down(u, 1, rows), _shift_down(u, 2, rows)
        g = dy_ref[...]
        dz_ref[0] = (g * (u * w2 + s1 * w1 + s2 * w0)).astype(BF16)
        dconv = g * b_ref[...].astype(F32)
        dw_ref[0:1, :] = jnp.sum(dconv * s2, axis=0, keepdims=True)
        dw_ref[1:2, :] = jnp.sum(dconv * s1, axis=0, keepdims=True)
        dw_ref[2:3, :] = jnp.sum(dconv * u, axis=0, keepdims=True)
        du = dconv * w2 + _shift_up(dconv, 1, rows, T) * w1 + _shift_up(dconv, 2, rows, T) * w0
        dz_ref[1] = (du * hh).astype(BF16)
        dz_ref[2] = (du * cg).astype(BF16)

    part = lambda p: pl.BlockSpec((None, T, tc), lambda j, p=p: (p, 0, j))
    return pl.pallas_call(
        body, name=name, grid=(D // tc,),
        in_specs=[part(0), part(1), part(2), pl.BlockSpec((3, tc), lambda j: (0, j)),
                  pl.BlockSpec((T, tc), lambda j: (0, j))],
        out_specs=[pl.BlockSpec((3, T, tc), lambda j: (0, 0, j)), pl.BlockSpec((3, tc), lambda j: (0, j))],
        out_shape=[jax.ShapeDtypeStruct((3, T, D), BF16), jax.ShapeDtypeStruct((3, D), F32)],
        compiler_params=_cp(("parallel",)),
    )(z, z, z, conv_w, dy)


def _swiglu_fwd(z, name):
    _, T, Fd = z.shape
    tr, tc = _tile(T, 384, LANES), _tile(Fd, 1408, LANES)

    def body(g_ref, u_ref, o_ref, ot_ref):
        g = g_ref[...].astype(F32)
        a = g * jax.nn.sigmoid(g) * u_ref[...].astype(F32)
        o_ref[...] = a.astype(BF16)
        ot_ref[...] = a.T.astype(BF16)

    part = lambda p: pl.BlockSpec((None, tr, tc), lambda i, j, p=p: (p, i, j))
    return pl.pallas_call(
        body, name=name, grid=(T // tr, Fd // tc), in_specs=[part(0), part(1)],
        out_specs=[pl.BlockSpec((tr, tc), lambda i, j: (i, j)), pl.BlockSpec((tc, tr), lambda i, j: (j, i))],
        out_shape=[jax.ShapeDtypeStruct((T, Fd), BF16), jax.ShapeDtypeStruct((Fd, T), BF16)],
        compiler_params=_cp(("parallel", "parallel")),
    )(z, z)


def _swiglu_bwd(z, da, name):
    _, T, Fd = z.shape
    tr, tc = _tile(T, 528, 16), _tile(Fd, 1408, LANES)

    def body(g_ref, u_ref, da_ref, dz_ref):
        g, d = g_ref[...].astype(F32), da_ref[...].astype(F32)
        s = jax.nn.sigmoid(g)
        dz_ref[0] = (d * u_ref[...].astype(F32) * (s * (1.0 + g * (1.0 - s)))).astype(BF16)
        dz_ref[1] = (d * (g * s)).astype(BF16)

    part = lambda p: pl.BlockSpec((None, tr, tc), lambda i, j, p=p: (p, i, j))
    return pl.pallas_call(
        body, name=name, grid=(T // tr, Fd // tc),
        in_specs=[part(0), part(1), pl.BlockSpec((tr, tc), lambda i, j: (i, j))],
        out_specs=pl.BlockSpec((2, tr, tc), lambda i, j: (0, i, j)),
        out_shape=jax.ShapeDtypeStruct((2, T, Fd), BF16), compiler_params=_cp(("parallel", "parallel")),
    )(z, z, da)


def _headnorm_fwd(z, part, g, name):
    _, T, D = z.shape
    tr = _tile(T, 1056, 16)

    def body(z_ref, g_ref, o_ref):
        x = z_ref[...]
        r = lax.rsqrt(jnp.mean(x * x, axis=-1, keepdims=True) + EPS)
        o_ref[...] = (x * r * g_ref[...]).astype(BF16)

    return pl.pallas_call(
        body, name=name, grid=(T // tr, D // HEAD_DIM),
        in_specs=[pl.BlockSpec((None, tr, HEAD_DIM), lambda i, h: (part, i, h)),
                  pl.BlockSpec((1, HEAD_DIM), lambda i, h: (0, 0))],
        out_specs=pl.BlockSpec((tr, HEAD_DIM), lambda i, h: (i, h)),
        out_shape=jax.ShapeDtypeStruct((T, D), BF16), compiler_params=_cp(("parallel", "parallel")),
    )(z, g)


def _headnorm_bwd(z, part, g, dy, name):
    _, T, D = z.shape
    tr = _tile(T, 1056, 16)

    def body(z_ref, g_ref, dy_ref, dz_ref, dg_ref):
        x = z_ref[...]
        r = lax.rsqrt(jnp.mean(x * x, axis=-1, keepdims=True) + EPS)
        xh = x * r
        dy_ = dy_ref[...]
        dxh = dy_ * g_ref[...]
        dz_ref[...] = (r * (dxh - xh * jnp.mean(dxh * xh, axis=-1, keepdims=True))).astype(BF16)
        partial = jnp.sum(dy_ * xh, axis=0, keepdims=True)
        first = (pl.program_id(0) == 0) & (pl.program_id(1) == 0)

        @pl.when(first)
        def _():
            dg_ref[...] = partial

        @pl.when(jnp.logical_not(first))
        def _():
            dg_ref[...] += partial

    blk = pl.BlockSpec((tr, HEAD_DIM), lambda i, h: (i, h))
    vec = pl.BlockSpec((1, HEAD_DIM), lambda i, h: (0, 0))
    return pl.pallas_call(
        body, name=name, grid=(T // tr, D // HEAD_DIM),
        in_specs=[pl.BlockSpec((None, tr, HEAD_DIM), lambda i, h: (part, i, h)), vec, blk],
        out_specs=[blk, vec],
        out_shape=[jax.ShapeDtypeStruct((T, D), BF16), jax.ShapeDtypeStruct((1, HEAD_DIM), F32)],
        compiler_params=_cp(("arbitrary", "arbitrary")),
    )(z, g, dy)


def _cast_part(z, part, name):
    _, T, D = z.shape
    tr = _tile(T, 528, 16)

    def body(z_ref, o_ref):
        o_ref[...] = z_ref[...].astype(BF16)

    return pl.pallas_call(
        body, name=name, grid=(T // tr,),
        in_specs=[pl.BlockSpec((None, tr, D), lambda i: (part, i, 0))],
        out_specs=pl.BlockSpec((tr, D), lambda i: (i, 0)),
        out_shape=jax.ShapeDtypeStruct((T, D), BF16), compiler_params=_cp(("parallel",)),
    )(z)


def _split3(x):
    a = x.astype(BF16)
    r = x - a.astype(F32)
    b = r.astype(BF16)
    c = (r - b.astype(F32)).astype(BF16)
    return a, b, c


def _tri_matmul(tri, x):
    a, b, c = _split3(x)
    dot = lambda v: jnp.dot(tri, v, preferred_element_type=F32)
    return (dot(c) + dot(b)) + dot(a)


def _logf_cumsum(pre, bias, pad, name):
    T = pre.shape[0]
    nb = T // BLOCK

    def body(p_ref, b_ref, c_ref, carry):
        i = pl.program_id(0)

        @pl.when(i == 0)
        def _():
            carry[...] = jnp.zeros_like(carry)

        x = p_ref[...] + b_ref[...]
        lf = jnp.minimum(x, 0.0) - jnp.log(1.0 + jnp.exp(-jnp.abs(x)))
        rows = i * BLOCK + lax.broadcasted_iota(jnp.int32, (BLOCK, LANES), 0)
        lf = jnp.where(rows >= pad, lf, 0.0)
        r = lax.broadcasted_iota(jnp.int32, (BLOCK, BLOCK), 0)
        c = lax.broadcasted_iota(jnp.int32, (BLOCK, BLOCK), 1)
        tri = jnp.where(c <= r, 1.0, 0.0).astype(BF16)
        c_ref[...] = _tri_matmul(tri, lf) + carry[...]
        carry[...] = c_ref[BLOCK - 1:BLOCK, :]

    return pl.pallas_call(
        body, name=name, grid=(nb,),
        in_specs=[pl.BlockSpec((BLOCK, LANES), lambda i: (i, 0)), pl.BlockSpec((1, LANES), lambda i: (0, 0))],
        out_specs=pl.BlockSpec((BLOCK, LANES), lambda i: (i, 0)),
        out_shape=jax.ShapeDtypeStruct((T, LANES), F32),
        scratch_shapes=[pltpu.VMEM((1, LANES), F32)], compiler_params=_cp(("arbitrary",)),
    )(pre, bias)


def _logf_bwd(pre, bias, dc, pad, name):
    T = pre.shape[0]
    nb = T // BLOCK

    def body(p_ref, b_ref, dc_ref, dp_ref, db_ref, carry, dlf_ref):
        i = pl.program_id(0)

        @pl.when(i == 0)
        def _():
            carry[...] = jnp.zeros_like(carry)

        r = lax.broadcasted_iota(jnp.int32, (BLOCK, BLOCK), 0)
        c = lax.broadcasted_iota(jnp.int32, (BLOCK, BLOCK), 1)
        tri = jnp.where(c >= r, 1.0, 0.0).astype(BF16)
        dlf_ref[...] = _tri_matmul(tri, dc_ref[...]) + carry[...]
        carry[...] = dlf_ref[0:1, :]
        dlf = dlf_ref[...]
        x = p_ref[...] + b_ref[...]
        rows = (nb - 1 - i) * BLOCK + lax.broadcasted_iota(jnp.int32, (BLOCK, LANES), 0)
        dpre = jnp.where(rows >= pad, dlf * jax.nn.sigmoid(-x), 0.0)
        dp_ref[...] = dpre
        partial = jnp.sum(dpre, axis=0, keepdims=True)

        @pl.when(i == 0)
        def _():
            db_ref[...] = partial

        @pl.when(i > 0)
        def _():
            db_ref[...] += partial

    rev = pl.BlockSpec((BLOCK, LANES), lambda i: (nb - 1 - i, 0))
    vec = pl.BlockSpec((1, LANES), lambda i: (0, 0))
    return pl.pallas_call(
        body, name=name, grid=(nb,), in_specs=[rev, vec, rev], out_specs=[rev, vec],
        out_shape=[jax.ShapeDtypeStruct((T, LANES), F32), jax.ShapeDtypeStruct((1, LANES), F32)],
        scratch_shapes=[pltpu.VMEM((1, LANES), F32), pltpu.VMEM((BLOCK, LANES), F32)],
        compiler_params=_cp(("arbitrary",)),
    )(pre, bias, dc)


def _loss_head(h, target, first, name):
    T, D = h.shape
    tr = BLOCK
    skip = first // tr

    def body(h_ref, t_ref, dh_ref, dhb_ref, loss_ref):
        i = pl.program_id(0)

        @pl.when(i == 0)
        def _():
            loss_ref[...] = jnp.zeros_like(loss_ref)

        @pl.when(i < skip)
        def _():
            dh_ref[...] = jnp.zeros_like(dh_ref)
            dhb_ref[...] = jnp.zeros_like(dhb_ref)

        @pl.when(i >= skip)
        def _():
            err = h_ref[...] - t_ref[...]
            dh_ref[...] = err * (1.0 / D)
            dhb_ref[...] = (err * (1.0 / D)).astype(BF16)
            loss_ref[...] += jnp.sum(err * err) * (0.5 / D)

    row = pl.BlockSpec((tr, D), lambda i: (i, 0))
    return pl.pallas_call(
        body, name=name, grid=(T // tr,),
        in_specs=[row, pl.BlockSpec((tr, D), lambda i: (jnp.maximum(i - skip, 0), 0))],
        out_specs=[row, row, pl.BlockSpec((8, LANES), lambda i: (0, 0))],
        out_shape=[jax.ShapeDtypeStruct((T, D), F32), jax.ShapeDtypeStruct((T, D), BF16),
                   jax.ShapeDtypeStruct((8, LANES), F32)],
        compiler_params=_cp(("arbitrary",)),
    )(h, target)


def _adamw(w, g, m, v, name):
    R, C = w.shape
    tr = _tile(R, max(8, TILE_BUDGET // (C * 4 * 7 * 3)), 8)
    bc1, bc2 = 1.0 - ADAM_B1 ** ADAM_STEP, 1.0 - ADAM_B2 ** ADAM_STEP

    def body(w_ref, g_ref, m_ref, v_ref, d_ref, mo_ref, vo_ref):
        g_ = g_ref[...]
        m_ = ADAM_B1 * m_ref[...] + (1.0 - ADAM_B1) * g_
        v_ = ADAM_B2 * v_ref[...] + (1.0 - ADAM_B2) * (g_ * g_)
        d_ref[...] = -ADAM_LR * ((m_ / bc1) / (jnp.sqrt(v_ / bc2) + ADAM_EPS) + ADAM_WD * w_ref[...])
        mo_ref[...] = m_
        vo_ref[...] = v_

    blk = pl.BlockSpec((tr, C), lambda i: (i, 0))
    sds = jax.ShapeDtypeStruct((R, C), F32)
    return pl.pallas_call(
        body, name=name, grid=(R // tr,), in_specs=[blk] * 4, out_specs=[blk] * 3, out_shape=[sds] * 3,
        compiler_params=_cp(("parallel",)),
    )(w, g, m, v)


def _pick_head(c_blk, h):
    lane = lax.broadcasted_iota(jnp.int32, c_blk.shape, 1)
    return jnp.sum(jnp.where(lane == h, c_blk, 0.0), axis=1, keepdims=True)


def _attn_fwd(q, k, v, c, ct, *, blk, pad, name):
    T, D = q.shape
    H, nk = D // HEAD_DIM, T // blk
    bq = blk
    nq = T // bq
    scale = 1.0 / math.sqrt(HEAD_DIM)

    def body(q_ref, k_ref, v_ref, c_ref, ct_ref, o_ref, lse_ref):
        h, i = pl.program_id(0), pl.program_id(1)
        qb = q_ref[...]
        cq = _pick_head(c_ref[...], h)
        last = (i * bq) // blk

        def step_fn(masked):
            def step(j, carry):
                m, l, acc = carry
                off = j * blk if isinstance(j, int) else pl.multiple_of(j * blk, blk)
                kb = k_ref[pl.ds(off, blk), :]
                vb = v_ref[pl.ds(off, blk), :]
                s = lax.dot_general(qb, kb, (((1,), (1,)), ((), ())), preferred_element_type=F32) * scale
                s = s + (cq - ct_ref[j])
                if masked:
                    qpos = i * bq + lax.broadcasted_iota(jnp.int32, (bq, blk), 0)
                    kpos = j * blk + lax.broadcasted_iota(jnp.int32, (bq, blk), 1)
                    s = jnp.where((kpos <= qpos) & (kpos >= pad), s, NEG)
                m_new = jnp.maximum(m, jnp.max(s, axis=1, keepdims=True))
                p = jnp.exp(s - m_new)
                alpha = jnp.exp(m - m_new)
                l = alpha * l + jnp.sum(p, axis=1, keepdims=True)
                acc = alpha * acc + jnp.dot(p.astype(BF16), vb, preferred_element_type=F32)
                return m_new, l, acc
            return step

        carry = (jnp.full((bq, 1), NEG, F32), jnp.zeros((bq, 1), F32), jnp.zeros((bq, HEAD_DIM), F32))
        carry = step_fn(True)(0, carry)
        carry = lax.fori_loop(1, last, step_fn(False), carry)
        m, l, acc = lax.cond(last > 0, lambda c: step_fn(True)(last, c), lambda c: c, carry)
        rowpos = i * bq + lax.broadcasted_iota(jnp.int32, (bq, 1), 0)
        o_ref[...] = jnp.where(rowpos >= pad, acc / l, 0.0)
        lse_ref[...] = jnp.broadcast_to(m + jnp.log(l), (bq, LANES))

    return pl.pallas_call(
        body, name=name, grid=(H, nq),
        in_specs=[pl.BlockSpec((bq, HEAD_DIM), lambda h, i: (i, h)),
                  pl.BlockSpec((T, HEAD_DIM), lambda h, i: (0, h)),
                  pl.BlockSpec((T, HEAD_DIM), lambda h, i: (0, h)),
                  pl.BlockSpec((bq, LANES), lambda h, i: (i, 0)),
                  pl.BlockSpec((None, nk, 1, blk), lambda h, i: (h, 0, 0, 0))],
        out_specs=[pl.BlockSpec((bq, HEAD_DIM), lambda h, i: (i, h)),
                   pl.BlockSpec((None, bq, LANES), lambda h, i: (h, i, 0))],
        out_shape=[jax.ShapeDtypeStruct((T, D), F32), jax.ShapeDtypeStruct((H, T, LANES), F32)],
        compiler_params=_cp(("parallel", "arbitrary")),
    )(q, k, v, c, ct)


def _attn_bwd(q, k, v, o, do, lse, c, ct, prev, *, blk, pad, name):
    T, D = q.shape
    H, nq = D // HEAD_DIM, T // blk
    scale = 1.0 / math.sqrt(HEAD_DIM)
    has_prev = prev is not None

    nt_dims = (((1,), (1,)), ((), ()))
    tn_dims = (((0,), (0,)), ((), ()))

    def body(*refs):
        q_ref, k_ref, v_ref, o_ref, do_ref, lse_ref, c_ref, ct_ref = refs[:8]
        pdk_ref, pdv_ref, pdc_ref = refs[8:11] if has_prev else (None, None, None)
        dq_ref, dk_ref, dv_ref, dct_ref = refs[-4:]
        h, j = pl.program_id(0), pl.program_id(1)

        @pl.when(j == 0)
        def _():
            dq_ref[...] = jnp.zeros_like(dq_ref)

        kb, vb = k_ref[...], v_ref[...]
        ck = ct_ref[...]

        def step_fn(masked):
            def step(i, carry):
                dk, dv, dck = carry
                off = pl.multiple_of(i * blk, blk)
                qb = q_ref[pl.ds(off, blk), :]
                dob = do_ref[pl.ds(off, blk), :]
                lse_i = lse_ref[pl.ds(off, blk), :][:, 0:1]
                cq = _pick_head(c_ref[pl.ds(off, blk), :], h)
                delta = jnp.sum(dob.astype(F32) * o_ref[pl.ds(off, blk), :], axis=1, keepdims=True)
                s = lax.dot_general(qb, kb, nt_dims, preferred_element_type=F32) * scale
                p = jnp.exp(s + (cq - ck) - lse_i)
                if masked:
                    qpos = i * blk + lax.broadcasted_iota(jnp.int32, (blk, blk), 0)
                    kpos = j * blk + lax.broadcasted_iota(jnp.int32, (blk, blk), 1)
                    p = jnp.where((kpos <= qpos) & (kpos >= pad), p, 0.0)
                dp = lax.dot_general(dob, vb, nt_dims, preferred_element_type=F32)
                ds = p * (dp - delta)
                pb, dsb = p.astype(BF16), ds.astype(BF16)
                dv = dv + lax.dot_general(pb, dob, tn_dims, preferred_element_type=F32)
                dk = dk + lax.dot_general(dsb, qb, tn_dims, preferred_element_type=F32)
                dck = dck - jnp.sum(ds, axis=0, keepdims=True)
                dq_ref[pl.ds(off, blk), :] += jnp.dot(dsb, kb, preferred_element_type=F32) * scale
                return dk, dv, dck
            return step

        carry = (jnp.zeros((blk, HEAD_DIM), F32), jnp.zeros((blk, HEAD_DIM), F32), jnp.zeros((1, blk), F32))
        carry = step_fn(True)(j, carry)
        below = lambda masked: (lambda c: lax.fori_loop(j + 1, nq, step_fn(masked), c))
        dk, dv, dck = lax.cond(j == 0, below(True), below(False), carry)
        dk = dk * scale
        if has_prev:
            dk, dv, dck = dk + pdk_ref[...], dv + pdv_ref[...], dck + pdc_ref[...]
        dk_ref[...] = dk
        dv_ref[...] = dv
        dct_ref[...] = dck

    col = pl.BlockSpec((T, HEAD_DIM), lambda h, j: (0, h))
    kblk = pl.BlockSpec((blk, HEAD_DIM), lambda h, j: (j, h))
    ctb = pl.BlockSpec((None, None, 1, blk), lambda h, j: (h, j, 0, 0))
    in_specs = [col, kblk, kblk, col, col,
                pl.BlockSpec((None, T, LANES), lambda h, j: (h, 0, 0)),
                pl.BlockSpec((T, LANES), lambda h, j: (0, 0)), ctb]
    operands = [q, k, v, o, do, lse, c, ct]
    if has_prev:
        in_specs += [kblk, kblk, ctb]
        operands += list(prev)
    return pl.pallas_call(
        body, name=name, grid=(H, nq), in_specs=in_specs, out_specs=[col, kblk, kblk, ctb],
        out_shape=[jax.ShapeDtypeStruct((T, D), F32), jax.ShapeDtypeStruct((T, D), F32),
                   jax.ShapeDtypeStruct((T, D), F32), jax.ShapeDtypeStruct((H, nq, 1, blk), F32)],
        compiler_params=_cp(("parallel", "arbitrary")),
    )(*operands)


HBM_SPEC = pl.BlockSpec(memory_space=pltpu.HBM)


def _place():
    x, y, c = lax.axis_index("x"), lax.axis_index("y"), lax.axis_index("c")
    chips = [(1 - x, y), (x, 1 - y), (1 - x, 1 - y)]
    return x, y, c, chips


def _remote(src, dst, send_sems, recv_sems, k, to):
    return pltpu.make_async_remote_copy(src_ref=src, dst_ref=dst, send_sem=send_sems.at[k],
                                        recv_sem=recv_sems.at[k], device_id=to, device_id_type=MESH)


def _all_gather(shards, specs, name):
    n = len(shards)

    def full_shape(s, axis):
        return (s.shape[0] * N_CHIPS, s.shape[1]) if axis == 0 else (s.shape[0], s.shape[1] * N_CHIPS)

    def body(*refs):
        srcs, outs = refs[:n], refs[n:2 * n]
        send_sems, recv_sems, local_sems = refs[2 * n:]
        x, y, c, chips = _place()
        sibling = (x, y, 1 - c)

        def region(t, chip, half):
            rs, cs = shards[t].shape
            q = 2 * chip[0] + chip[1]
            axis, split = specs[t]
            nrow = rs // 2 if half is not None else rs
            r0 = 0 if half is None else half * nrow
            if axis == 0:
                return outs[t].at[pl.ds(q * rs + r0, nrow), :]
            return outs[t].at[pl.ds(r0, nrow), pl.ds(pl.multiple_of(q * cs, cs), cs)]

        def piece(t, half):
            rs = shards[t].shape[0]
            if half is None:
                return srcs[t]
            return srcs[t].at[pl.ds(half * (rs // 2), rs // 2), :]

        local = [pltpu.make_async_copy(srcs[t], region(t, (x, y), None), local_sems.at[t]) for t in range(n)]
        for cp in local:
            cp.start()
        sends = []
        for t in range(n):
            half = c if specs[t][1] else None
            for j, chip in enumerate(chips):
                cp = _remote(piece(t, half), region(t, (x, y), half), send_sems, recv_sems, 6 * t + j, (*chip, c))
                cp.start()
                sends.append(cp)
        for t in range(n):
            half = c if specs[t][1] else None
            for j, chip in enumerate(chips):
                landed = region(t, chip, half)
                _remote(landed, landed, send_sems, recv_sems, 6 * t + j, (*chip, c)).wait_recv()
                if specs[t][1]:
                    cp = _remote(landed, landed, send_sems, recv_sems, 6 * t + 3 + j, sibling)
                    cp.start()
                    sends.append(cp)
        for t in range(n):
            if specs[t][1]:
                for j, chip in enumerate(chips):
                    got = region(t, chip, 1 - c)
                    _remote(got, got, send_sems, recv_sems, 6 * t + 3 + j, sibling).wait_recv()
        for cp in sends:
            cp.wait_send()
        for cp in local:
            cp.wait()

    return pl.pallas_call(
        body, name=name, in_specs=[HBM_SPEC] * n, out_specs=[HBM_SPEC] * n,
        out_shape=[jax.ShapeDtypeStruct(full_shape(s, specs[t][0]), s.dtype) for t, s in enumerate(shards)],
        scratch_shapes=[pltpu.SemaphoreType.DMA((6 * n,)), pltpu.SemaphoreType.DMA((6 * n,)),
                        pltpu.SemaphoreType.DMA((n,))],
        compiler_params=pltpu.CompilerParams(has_side_effects=True),
    )(*shards)


def _grad_view(g, axis):
    R, C = g.shape
    nq = N_CHIPS if axis == 0 else 1
    return g.reshape(nq, 2, R // (2 * nq), C)


def _add_half(view, got, c_idx, name):
    nq, _, Rh, C = view.shape
    tr = _tile(Rh, max(16, (2 * 1024 * 1024) // (C * 2)), 16)

    def body(c_ref, a_ref, b_ref, o_ref):
        o_ref[...] = (a_ref[...].astype(F32) + b_ref[...].astype(F32)).astype(BF16)

    grid_spec = pltpu.PrefetchScalarGridSpec(
        num_scalar_prefetch=1, grid=(nq, Rh // tr),
        in_specs=[pl.BlockSpec((None, None, tr, C), lambda q, i, c_ref: (q, c_ref[0], i, 0)),
                  pl.BlockSpec((None, tr, C), lambda q, i, c_ref: (q, i, 0))],
        out_specs=pl.BlockSpec((None, tr, C), lambda q, i, c_ref: (q, i, 0)))
    return pl.pallas_call(
        body, name=name, grid_spec=grid_spec, out_shape=jax.ShapeDtypeStruct((nq, Rh, C), BF16),
        compiler_params=_cp(("parallel", "parallel")),
    )(c_idx, view, got)


SEM_SPEC = pl.BlockSpec(memory_space=pltpu.SEMAPHORE)
ANY_SPEC = pl.BlockSpec(memory_space=pl.ANY)
TOKEN_SPEC = pl.BlockSpec(memory_space=pltpu.VMEM)
TOKEN = jax.ShapeDtypeStruct((8, LANES), F32)
SPLIT_COPY = pltpu.CompilerParams(has_side_effects=pltpu.SideEffectType.DATAFLOW_SIDE_EFFECTING)


def _region(ref, axis, chip, half):
    q = 2 * chip[0] + chip[1]
    if axis == 0:
        rs = ref.shape[0] // N_CHIPS
        return ref.at[pl.ds(q * rs + half * (rs // 2), rs // 2), :]
    rh, cs = ref.shape[0] // 2, ref.shape[1] // N_CHIPS
    return ref.at[pl.ds(half * rh, rh), pl.ds(pl.multiple_of(q * cs, cs), cs)]


def _gather_start(fulls, axes, after, name):
    n = len(fulls)

    def body(*refs):
        ins = refs[:n]
        token = refs[-1]
        send_sems, recv_sems = refs[n + 1], refs[n + 2]
        x, y, c, chips = _place()
        for t in range(n):
            mine = _region(ins[t], axes[t], (x, y), c)
            for j, chip in enumerate(chips):
                _remote(mine, mine, send_sems, recv_sems, 3 * t + j, (*chip, c)).start()
        token[...] = jnp.zeros_like(token)

    sems = pltpu.SemaphoreType.DMA((3 * n,))
    outs = pl.pallas_call(
        body, name=name, in_specs=[HBM_SPEC] * n + [ANY_SPEC],
        out_specs=[SEM_SPEC, SEM_SPEC] + [HBM_SPEC] * n + [TOKEN_SPEC],
        out_shape=[sems, sems] + [jax.ShapeDtypeStruct(f.shape, f.dtype) for f in fulls] + [TOKEN],
        input_output_aliases={t: 2 + t for t in range(n)}, compiler_params=SPLIT_COPY,
    )(*fulls, after)
    return outs[0], outs[1], list(outs[2:2 + n]), outs[-1]


def _gather_wait(send_sems, recv_sems, fulls, axes, after, name):
    n = len(fulls)

    def body(*refs):
        ins = refs[:n]
        send_sems, recv_sems = refs[n], refs[n + 1]
        x, y, c, chips = _place()
        for t in range(n):
            mine = _region(ins[t], axes[t], (x, y), c)
            for j, chip in enumerate(chips):
                _remote(mine, mine, send_sems, recv_sems, 3 * t + j, (*chip, c)).wait_send()
                theirs = _region(ins[t], axes[t], chip, c)
                _remote(theirs, theirs, send_sems, recv_sems, 3 * t + j, (*chip, c)).wait_recv()

    outs = pl.pallas_call(
        body, name=name, in_specs=[HBM_SPEC] * n + [SEM_SPEC, SEM_SPEC, ANY_SPEC], out_specs=[HBM_SPEC] * n,
        out_shape=[jax.ShapeDtypeStruct(f.shape, f.dtype) for f in fulls],
        input_output_aliases={t: t for t in range(n)}, compiler_params=SPLIT_COPY,
    )(*fulls, send_sems, recv_sems, after)
    return list(outs)


def _gather_forward(fulls, axes, name):
    n = len(fulls)

    def body(*refs):
        outs = refs[n:2 * n]
        send_sems, recv_sems = refs[2 * n], refs[2 * n + 1]
        x, y, c, chips = _place()
        sibling = (x, y, 1 - c)
        cps = []
        for t in range(n):
            for j, chip in enumerate(chips):
                landed = _region(outs[t], axes[t], chip, c)
                cps.append(_remote(landed, landed, send_sems, recv_sems, 3 * t + j, sibling))
        for cp in cps:
            cp.start()
        for t in range(n):
            for j, chip in enumerate(chips):
                got = _region(outs[t], axes[t], chip, 1 - c)
                _remote(got, got, send_sems, recv_sems, 3 * t + j, sibling).wait_recv()
        for cp in cps:
            cp.wait_send()

    outs = pl.pallas_call(
        body, name=name, in_specs=[HBM_SPEC] * n, out_specs=[HBM_SPEC] * n,
        out_shape=[jax.ShapeDtypeStruct(f.shape, f.dtype) for f in fulls],
        scratch_shapes=[pltpu.SemaphoreType.DMA((3 * n,)), pltpu.SemaphoreType.DMA((3 * n,))],
        input_output_aliases={t: t for t in range(n)},
        compiler_params=pltpu.CompilerParams(has_side_effects=True),
    )(*fulls)
    return list(outs)


def _shard_cols(s, axis):
    return s.shape[2] if axis == 0 else s.shape[2] // N_CHIPS


def _piece(ref, axis, chip):
    q = 2 * chip[0] + chip[1]
    if axis == 0:
        return ref.at[q]
    cs = ref.shape[2] // N_CHIPS
    return ref.at[0, :, pl.ds(pl.multiple_of(q * cs, cs), cs)]


def _scatter_start(sums, axes, after, name):
    n = len(sums)

    def body(*refs):
        ins = refs[:n]
        send_sems, recv_sems = refs[n + 1], refs[n + 2]
        lands = refs[2 * n + 3:3 * n + 3]
        token = refs[-1]
        x, y, c, chips = _place()
        for t in range(n):
            for j, chip in enumerate(chips):
                _remote(_piece(ins[t], axes[t], chip), lands[t].at[j], send_sems, recv_sems, 3 * t + j, (*chip, c)).start()
        token[...] = jnp.zeros_like(token)

    sems = pltpu.SemaphoreType.DMA((3 * n,))
    land_shapes = [jax.ShapeDtypeStruct((3, s.shape[1], _shard_cols(s, a)), s.dtype) for s, a in zip(sums, axes)]
    outs = pl.pallas_call(
        body, name=name, in_specs=[HBM_SPEC] * n + [ANY_SPEC],
        out_specs=[SEM_SPEC, SEM_SPEC] + [HBM_SPEC] * (2 * n) + [TOKEN_SPEC],
        out_shape=[sems, sems] + [jax.ShapeDtypeStruct(s.shape, s.dtype) for s in sums] + land_shapes + [TOKEN],
        input_output_aliases={t: 2 + t for t in range(n)}, compiler_params=SPLIT_COPY,
    )(*sums, after)
    return outs[0], outs[1], list(outs[2:2 + n]), list(outs[2 + n:2 + 2 * n]), outs[-1]


def _scatter_wait(send_sems, recv_sems, sums, lands, axes, after, name):
    n = len(sums)

    def body(*refs):
        ins, lnd = refs[:n], refs[n:2 * n]
        send_sems, recv_sems = refs[2 * n], refs[2 * n + 1]
        x, y, c, chips = _place()
        for t in range(n):
            for j, chip in enumerate(chips):
                cp = _remote(_piece(ins[t], axes[t], chip), lnd[t].at[j], send_sems, recv_sems, 3 * t + j, (*chip, c))
                cp.wait_send()
                cp.wait_recv()

    outs = pl.pallas_call(
        body, name=name, in_specs=[HBM_SPEC] * (2 * n) + [SEM_SPEC, SEM_SPEC, ANY_SPEC], out_specs=[HBM_SPEC] * (2 * n),
        out_shape=[jax.ShapeDtypeStruct(s.shape, s.dtype) for s in sums + lands],
        input_output_aliases={t: t for t in range(2 * n)}, compiler_params=SPLIT_COPY,
    )(*sums, *lands, send_sems, recv_sems, after)
    return list(outs[:n]), list(outs[n:])


def _sum_chips(own, axis, got, q_idx, c_idx, name):
    _, Rh, cc = got.shape
    tr = _tile(Rh, max(16, (1024 * 1024) // (cc * 2)), 16)

    def body(q_ref, c_ref, a_ref, b0, b1, b2, o_ref):
        f = lambda r: r[...].astype(F32)
        o_ref[...] = ((f(a_ref) + f(b0)) + f(b1)) + f(b2)

    if axis == 0:
        own_spec = pl.BlockSpec((None, tr, cc), lambda i, q, c: (q[0], i, 0))
    else:
        own_spec = pl.BlockSpec((None, tr, cc), lambda i, q, c: (0, i, q[0]))
    slot = lambda j: pl.BlockSpec((None, tr, cc), lambda i, q, c, j=j: (j, i, 0))
    grid_spec = pltpu.PrefetchScalarGridSpec(
        num_scalar_prefetch=2, grid=(Rh // tr,), in_specs=[own_spec, slot(0), slot(1), slot(2)],
        out_specs=pl.BlockSpec((None, tr, cc), lambda i, q, c: (c[0], i, 0)))
    return pl.pallas_call(
        body, name=name, grid_spec=grid_spec, out_shape=jax.ShapeDtypeStruct((2, Rh, cc), F32),
        compiler_params=_cp(("parallel",)),
    )(q_idx, c_idx, own, got, got, got)


def _join_halves(pairs, name):
    n = len(pairs)

    def body(*refs):
        outs = refs[n:2 * n]
        send_sems, recv_sems = refs[2 * n], refs[2 * n + 1]
        x, y, c, _ = _place()
        cps = [_remote(outs[t].at[c], outs[t].at[c], send_sems, recv_sems, t, (x, y, 1 - c)) for t in range(n)]
        for cp in cps:
            cp.start()
        for t in range(n):
            cps[t].wait_send()
            _remote(outs[t].at[1 - c], outs[t].at[1 - c], send_sems, recv_sems, t, (x, y, 1 - c)).wait_recv()

    outs = pl.pallas_call(
        body, name=name, in_specs=[HBM_SPEC] * n, out_specs=[HBM_SPEC] * n,
        out_shape=[jax.ShapeDtypeStruct(p.shape, p.dtype) for p in pairs],
        scratch_shapes=[pltpu.SemaphoreType.DMA((n,)), pltpu.SemaphoreType.DMA((n,))],
        input_output_aliases={t: t for t in range(n)},
        compiler_params=pltpu.CompilerParams(has_side_effects=True),
    )(*pairs)
    return list(outs)


def _swap_halves(views, name):
    n = len(views)

    def body(*refs):
        srcs, outs, send_sems, recv_sems = refs[:n], refs[n:2 * n], refs[2 * n], refs[2 * n + 1]
        x, y, c, _ = _place()
        cps = [_remote(srcs[t].at[:, 1 - c], outs[t], send_sems, recv_sems, t, (x, y, 1 - c)) for t in range(n)]
        for cp in cps:
            cp.start()
        for cp in cps:
            cp.wait()

    return pl.pallas_call(
        body, name=name, in_specs=[HBM_SPEC] * n, out_specs=[HBM_SPEC] * n,
        out_shape=[jax.ShapeDtypeStruct((v.shape[0],) + v.shape[2:], v.dtype) for v in views],
        scratch_shapes=[pltpu.SemaphoreType.DMA((n,)), pltpu.SemaphoreType.DMA((n,))],
        compiler_params=pltpu.CompilerParams(has_side_effects=True),
    )(*views)


def _swap_start(views, after, name):
    n = len(views)

    def body(*refs):
        ins = refs[:n]
        send_sems, recv_sems = refs[n + 1], refs[n + 2]
        lands = refs[2 * n + 3:3 * n + 3]
        token = refs[-1]
        x, y, c, _ = _place()
        for t in range(n):
            _remote(ins[t].at[:, 1 - c], lands[t], send_sems, recv_sems, t, (x, y, 1 - c)).start()
        token[...] = jnp.zeros_like(token)

    sems = pltpu.SemaphoreType.DMA((n,))
    land_shapes = [jax.ShapeDtypeStruct((v.shape[0],) + v.shape[2:], v.dtype) for v in views]
    outs = pl.pallas_call(
        body, name=name, in_specs=[HBM_SPEC] * n + [ANY_SPEC],
        out_specs=[SEM_SPEC, SEM_SPEC] + [HBM_SPEC] * (2 * n) + [TOKEN_SPEC],
        out_shape=[sems, sems] + [jax.ShapeDtypeStruct(v.shape, v.dtype) for v in views] + land_shapes + [TOKEN],
        input_output_aliases={t: 2 + t for t in range(n)}, compiler_params=SPLIT_COPY,
    )(*views, after)
    return outs[0], outs[1], list(outs[2:2 + n]), list(outs[2 + n:2 + 2 * n]), outs[-1]


def _swap_wait(send_sems, recv_sems, views, lands, after, name):
    n = len(views)

    def body(*refs):
        ins, lnd = refs[:n], refs[n:2 * n]
        send_sems, recv_sems = refs[2 * n], refs[2 * n + 1]
        x, y, c, _ = _place()
        for t in range(n):
            cp = _remote(ins[t].at[:, 1 - c], lnd[t], send_sems, recv_sems, t, (x, y, 1 - c))
            cp.wait_send()
            cp.wait_recv()

    outs = pl.pallas_call(
        body, name=name, in_specs=[HBM_SPEC] * (2 * n) + [SEM_SPEC, SEM_SPEC, ANY_SPEC], out_specs=[HBM_SPEC] * (2 * n),
        out_shape=[jax.ShapeDtypeStruct(a.shape, a.dtype) for a in views + lands],
        input_output_aliases={t: t for t in range(2 * n)}, compiler_params=SPLIT_COPY,
    )(*views, *lands, send_sems, recv_sems, after)
    return list(outs[:n]), list(outs[n:])


def _reduce_begin(views, got, axes, c_idx, after, tag):
    sums = [_add_half(v, p, c_idx, f"rs_add_{tag}_{t}") for t, (v, p) in enumerate(zip(views, got))]
    send_sems, recv_sems, sums, lands, token = _scatter_start(sums, axes, after, f"rs_chips_start_{tag}")
    return (send_sems, recv_sems, sums, lands, axes, tag), token


def _reduce_finish(pending, q_idx, c_idx, after):
    send_sems, recv_sems, sums, lands, axes, tag = pending
    sums, lands = _scatter_wait(send_sems, recv_sems, sums, lands, axes, after, f"rs_chips_wait_{tag}")
    pairs = [_sum_chips(s, a, r, q_idx, c_idx, f"rs_sum_{tag}_{t}")
             for t, (s, a, r) in enumerate(zip(sums, axes, lands))]
    joined = _join_halves(pairs, f"rs_join_{tag}")
    return [j.reshape(2 * j.shape[1], j.shape[2]) for j in joined]


def _all_reduce_small(buf, name):
    R, C = buf.shape

    def body(b_ref, o_ref, slots, send_sems, recv_sems):
        x, y, c, _ = _place()
        me = 4 * x + 2 * y + c
        cps = []
        for k in range(1, 8):
            to = (x ^ (k >> 2), y ^ ((k >> 1) & 1), c ^ (k & 1))
            cps.append(pltpu.make_async_remote_copy(
                src_ref=b_ref, dst_ref=slots.at[me], send_sem=send_sems.at[k - 1], recv_sem=recv_sems.at[me],
                device_id=to, device_id_type=MESH))
        for cp in cps:
            cp.start()
        slots[me] = b_ref[...]
        for k in range(1, 8):
            src = me ^ k
            pltpu.make_async_remote_copy(
                src_ref=b_ref, dst_ref=slots.at[src], send_sem=send_sems.at[k - 1], recv_sem=recv_sems.at[src],
                device_id=(x, y, c), device_id_type=MESH).wait_recv()
        for cp in cps:
            cp.wait_send()
        total = slots[0]
        for d in range(1, 8):
            total = total + slots[d]
        o_ref[...] = total

    vm = pl.BlockSpec(memory_space=pltpu.VMEM)
    return pl.pallas_call(
        body, name=name, in_specs=[vm], out_specs=vm, out_shape=jax.ShapeDtypeStruct((R, C), F32),
        scratch_shapes=[pltpu.VMEM((8, R, C), F32), pltpu.SemaphoreType.DMA((7,)), pltpu.SemaphoreType.DMA((8,))],
        compiler_params=pltpu.CompilerParams(has_side_effects=True, vmem_limit_bytes=VMEM_LIMIT),
    )(buf)


def kernel(x, meta, a_norm, a_w_in, a_conv, a_w_out, kv_norm, w_kv, k_norm, w_f, b_f, b_norm, b_w_q, b_q_norm, b_w_o, ffn_norm, ffn_w_gu, ffn_w_down, loss_target, m_meta, m_a_norm, m_a_w_in, m_a_conv, m_a_w_out, m_kv_norm, m_w_kv, m_k_norm, m_w_f, m_b_f, m_b_norm, m_b_w_q, m_b_q_norm, m_b_w_o, m_ffn_norm, m_ffn_w_gu, m_ffn_w_down, v_meta, v_a_norm, v_a_w_in, v_a_conv, v_a_w_out, v_kv_norm, v_w_kv, v_k_norm, v_w_f, v_b_f, v_b_norm, v_b_w_q, v_b_q_norm, v_b_w_o, v_ffn_norm, v_ffn_w_gu, v_ffn_w_down):
    SEQ, D = x.shape[1], x.shape[2]
    n_meta = meta.shape[0]
    pad = BLOCK - n_meta
    first = pad + n_meta
    T = first + SEQ
    H = D // HEAD_DIM
    Ds = D // N_CHIPS
    n_a, n_b, depth = a_w_in.shape[0], b_w_q.shape[0], ffn_norm.shape[0]
    blk = _tile(T, 384, BLOCK)
    cx, cy, cc = lax.axis_index("x"), lax.axis_index("y"), lax.axis_index("c")
    q_me = 2 * cx + cy
    c_idx = jnp.reshape(cc, (1,)).astype(jnp.int32)
    q_idx = jnp.reshape(q_me, (1,)).astype(jnp.int32)
    rows8 = lambda v: jnp.pad(v, ((0, -v.shape[0] % 8), (0, 0)))

    col_parts = [meta, a_norm, a_conv.reshape(3 * n_a, Ds)]
    col_pack = jnp.concatenate([rows8(p) for p in col_parts], axis=0)
    col_full, w_f_full = _all_gather([col_pack, w_f], [(1, False), (0, False)], "ag_small")
    col_offs = [sum(rows8(p).shape[0] for p in col_parts[:i]) for i in range(3)]
    meta_f = col_full[:n_meta]
    a_norm_f = col_full[col_offs[1]:col_offs[1] + n_a]
    a_conv_f = col_full[col_offs[2]:col_offs[2] + 3 * n_a].reshape(n_a, 3, D)
    w_fp = jnp.pad(w_f_full, ((0, 0), (0, LANES - H))).astype(BF16)
    b_fp = jnp.pad(b_f, (0, LANES - H)).reshape(1, LANES)

    stages = []
    for l in range(n_a):
        stages += [[(a_w_in, l, 1), (a_w_out, l, 0)], [(ffn_w_gu, l, 1), (ffn_w_down, l, 0)]]
    for j in range(n_b):
        stages += [[(b_w_q, j, 0), (b_w_o, j, 0)], [(ffn_w_gu, n_a + j, 1), (ffn_w_down, n_a + j, 0)]]
    stages[2 * n_a].append((w_kv[None], 0, 1))

    def gather_begin(k, after):
        axes = [ax for _, _, ax in stages[k]]
        fulls = [_cast_into_full(w, l, ax, q_idx, f"cast_{k}_{i}") for i, (w, l, ax) in enumerate(stages[k])]
        send_sems, recv_sems, fulls, token = _gather_start(fulls, axes, after, f"ag_start_{k}")
        return (send_sems, recv_sems, fulls, axes, k), token

    def gather_end(handle, after):
        send_sems, recv_sems, fulls, axes, k = handle
        fulls = _gather_wait(send_sems, recv_sems, fulls, axes, after, f"ag_wait_{k}")
        return _gather_forward(fulls, axes, f"ag_forward_{k}")

    handle, _ = gather_begin(0, col_full)
    arrived = [gather_end(handle, col_full)]
    gathering = []

    def enter_segment():
        w = arrived[-1]
        after, token = w[0], None
        while len(gathering) < 2 and len(arrived) + len(gathering) < len(stages):
            handle, token = gather_begin(len(arrived) + len(gathering), after)
            gathering.append(handle)
            after = token
        return w, token

    def leave_segment(h_out):
        if gathering:
            arrived.append(gather_end(gathering.pop(0), h_out))

    h = jnp.concatenate([jnp.zeros((pad, D), F32), meta_f, x[0]], axis=0)
    saved = []

    def ffn_fwd(h, layer):
        (w_gu, w_down), token = enter_segment()
        xn, xn_t = _rms_fwd(h, ffn_norm[layer:layer + 1], f"ffn_norm_{layer}", dep=token)
        z = _matmul(xn, w_gu, mode="nn", out_dtype=BF16, name=f"ffn_gu_{layer}", out_parts=2)
        act, act_t = _swiglu_fwd(z, f"swiglu_{layer}")
        out = _matmul(act, w_down, mode="nn", out_dtype=F32, name=f"ffn_down_{layer}", res=h)
        leave_segment(out)
        return out, (h, xn_t, z, act_t), (w_gu, w_down)

    wa, wb = [], []
    for l in range(n_a):
        (w_in, w_out), token = enter_segment()
        xn, xn_t = _rms_fwd(h, a_norm_f[l:l + 1], f"a_norm_{l}", dep=token)
        z = _matmul(xn, w_in, mode="nn", out_dtype=BF16, name=f"a_in_{l}", out_parts=3)
        y, y_t = _gate_fwd(z, a_conv_f[l], f"a_gate_{l}")
        h2 = _matmul(y, w_out, mode="nn", out_dtype=F32, name=f"a_out_{l}", res=h)
        leave_segment(h2)
        h3, ffn_saved, w_ffn = ffn_fwd(h2, l)
        saved.append((h, xn_t, z, y_t, ffn_saved))
        wa.append((w_in, w_out) + w_ffn)
        h = h3

    for j in range(n_b):
        layer = n_a + j
        w_mix, token = enter_segment()
        w_q, w_o = w_mix[:2]
        if j == 0:
            h_kv, w_kv_b = h, w_mix[2]
            xkv, xkv_t = _rms_fwd(h, kv_norm.reshape(1, D), "kv_norm", dep=token)
            token = None
            kvz = _matmul(xkv, w_kv_b, mode="nn", out_dtype=F32, name="kv_proj", out_parts=2)
            k_n = _headnorm_fwd(kvz, 0, k_norm.reshape(1, HEAD_DIM), "k_headnorm")
            v_b = _cast_part(kvz, 1, "v_cast")
            pre = _matmul(xkv, w_fp, mode="nn", out_dtype=F32, name="f_proj")
            c_cum = _logf_cumsum(pre, b_fp, pad, "logf_cumsum")
            c_t = c_cum[:, :H].T.reshape(H, T // blk, 1, blk)
        xn, xn_t = _rms_fwd(h, b_norm[j:j + 1], f"b_norm_{j}", dep=token)
        qz = _matmul(xn, w_q, mode="nn", out_dtype=F32, name=f"b_q_{j}")[None]
        q_n = _headnorm_fwd(qz, 0, b_q_norm[j:j + 1], f"q_headnorm_{j}")
        o, lse = _attn_fwd(q_n, k_n, v_b, c_cum, c_t, blk=blk, pad=pad, name=f"attn_fwd_{j}")
        h2 = _matmul(o, w_o, mode="nn", out_dtype=F32, name=f"b_o_{j}", res=h)
        leave_segment(h2)
        h3, ffn_saved, w_ffn = ffn_fwd(h2, layer)
        saved.append((h, xn_t, qz, q_n, o, lse, ffn_saved))
        wb.append((w_q, w_o) + w_ffn)
        h = h3

    dh, dhb, loss_blk = _loss_head(h, loss_target[0], first, "loss_head")
    loss = lax.psum(loss_blk[0, 0], ("x", "y", "c"))

    shards = {}
    swapping = []
    reducing = []

    def advance(done, after):
        token = None
        while len(reducing) >= 2:
            prev_names, pending = reducing.pop(0)
            got = _reduce_finish(pending, q_idx, c_idx, done)
            shards.update(zip(prev_names, got))
            after = got[0]
        if swapping:
            names, (send_sems, recv_sems, views, lands), axes, tag = swapping.pop()
            views, lands = _swap_wait(send_sems, recv_sems, views, lands, done, f"rs_swap_wait_{tag}")
            pending, token = _reduce_begin(views, lands, axes, c_idx, after, tag)
            reducing.append((names, pending))
            after = token
        return after, token

    def reduce_later(names, grads, axes, tag, done, now=False):
        after, _ = advance(done, c_idx)
        views = [_grad_view(g, a) for g, a in zip(grads, axes)]
        if now:
            got = _swap_halves(views, f"rs_swap_{tag}")
            pending, token = _reduce_begin(views, got, axes, c_idx, after, tag)
            reducing.append((names, pending))
            return token
        send_sems, recv_sems, views, lands, token = _swap_start(views, after, f"rs_swap_start_{tag}")
        swapping.append((names, (send_sems, recv_sems, views, lands), axes, tag))
        return token

    def ffn_bwd(dh, dhb, layer, w_gu, w_down, ffn_saved, dep):
        h_in, xn_t, z, act_t = ffn_saved
        da = _matmul(dhb, w_down, mode="nt", out_dtype=BF16, name=f"ffn_down_dx_{layer}", dep=dep)
        g_down = _matmul(act_t, dhb, mode="nn", out_dtype=BF16, name=f"ffn_down_dw_{layer}")
        dz = _swiglu_bwd(z, da, f"swiglu_bwd_{layer}")
        dxn = _matmul(dz, w_gu, mode="nt", out_dtype=F32, name=f"ffn_gu_dx_{layer}")
        g_gu = _matmul(xn_t, dz, mode="nn", out_dtype=BF16, name=f"ffn_gu_dw_{layer}")
        dh, dhb, dg = _rms_bwd(h_in, ffn_norm[layer:layer + 1], dxn, dh, f"ffn_norm_bwd_{layer}")
        token = reduce_later([("ffn_w_gu", layer), ("ffn_w_down", layer)], [g_gu, g_down], [1, 0], f"f{layer}", dh,
                             now=layer == 0)
        return dh, dhb, dg, token

    d_ffn_norm, d_b_norm, d_q_norm, d_a_norm, d_a_conv = {}, {}, {}, {}, {}
    kv_prev = None
    token = None
    for j in reversed(range(n_b)):
        layer = n_a + j
        w_q, w_o, w_gu, w_down = wb[j]
        h_in, xn_t, qz, q_n, o, lse, ffn_saved = saved[layer]
        dh, dhb, d_ffn_norm[layer], token = ffn_bwd(dh, dhb, layer, w_gu, w_down, ffn_saved, token)
        do = _matmul(dhb, w_o, mode="nt", out_dtype=BF16, name=f"b_o_dx_{j}", dep=token)
        g_o = _matmul(o.astype(BF16).T, dhb, mode="nn", out_dtype=BF16, name=f"b_o_dw_{j}")
        dq, dk, dv, dct = _attn_bwd(q_n, k_n, v_b, o, do, lse, c_cum, c_t, kv_prev, blk=blk, pad=pad,
                                    name=f"attn_bwd_{j}")
        kv_prev = (dk, dv, dct)
        dqz, d_q_norm[j] = _headnorm_bwd(qz, 0, b_q_norm[j:j + 1], dq, f"q_headnorm_bwd_{j}")
        dxn = _matmul(dqz, w_q, mode="nt", out_dtype=F32, name=f"b_q_dx_{j}")
        g_q = _matmul(xn_t, dqz, mode="nn", out_dtype=BF16, name=f"b_q_dw_{j}")
        dh, dhb, d_b_norm[j] = _rms_bwd(h_in, b_norm[j:j + 1], dxn, dh, f"b_norm_bwd_{j}")
        if j > 0:
            token = reduce_later([("b_w_q", j), ("b_w_o", j)], [g_q, g_o], [0, 0], f"b{j}", dh)

    dk, dv, dct = kv_prev
    dkz, d_k_norm = _headnorm_bwd(kvz, 0, k_norm.reshape(1, HEAD_DIM), dk, "k_headnorm_bwd")
    dvz = _cast_part(dv[None], 0, "dv_cast")
    dkv = jnp.stack([dkz, dvz])
    dc = jnp.pad(dct.reshape(H, T).T, ((0, 0), (0, LANES - H)))
    dpre, d_b_f = _logf_bwd(pre, b_fp, dc, pad, "logf_bwd")
    dxkv = _matmul(dkv, w_kv_b, mode="nt", out_dtype=F32, name="kv_proj_dx")
    dxkv = _matmul(dpre, w_fp, mode="nt", out_dtype=F32, name="f_proj_dx", res=dxkv)
    g_kv = _matmul(xkv_t, dkv, mode="nn", out_dtype=BF16, name="kv_proj_dw")
    d_w_f = _matmul(xkv_t, dpre, mode="nn", out_dtype=F32, name="f_proj_dw")
    dh, dhb, d_kv_norm = _rms_bwd(h_kv, kv_norm.reshape(1, D), dxkv, dh, "kv_norm_bwd")
    token = reduce_later([("b_w_q", 0), ("b_w_o", 0), ("w_kv", 0)], [g_q, g_o, g_kv], [0, 0, 1], "b0", dh)

    for l in reversed(range(n_a)):
        w_in, w_out, w_gu, w_down = wa[l]
        h_in, xn_t, z, y_t, ffn_saved = saved[l]
        dh, dhb, d_ffn_norm[l], token = ffn_bwd(dh, dhb, l, w_gu, w_down, ffn_saved, token)
        dy = _matmul(dhb, w_out, mode="nt", out_dtype=F32, name=f"a_out_dx_{l}", dep=token)
        g_out = _matmul(y_t, dhb, mode="nn", out_dtype=BF16, name=f"a_out_dw_{l}")
        dz, d_a_conv[l] = _gate_bwd(z, a_conv_f[l], dy, f"a_gate_bwd_{l}")
        dxn = _matmul(dz, w_in, mode="nt", out_dtype=F32, name=f"a_in_dx_{l}")
        g_in = _matmul(xn_t, dz, mode="nn", out_dtype=BF16, name=f"a_in_dw_{l}")
        dh, dhb, d_a_norm[l] = _rms_bwd(h_in, a_norm_f[l:l + 1], dxn, dh, f"a_norm_bwd_{l}")
        token = reduce_later([("a_w_in", l), ("a_w_out", l)], [g_in, g_out], [1, 0], f"a{l}", dh, now=l == 0)

    advance(dh, c_idx)
    last_names, last_pending = reducing.pop()
    for prev_names, pending in reducing:
        shards.update(zip(prev_names, _reduce_finish(pending, q_idx, c_idx, dh)))
    grad_x = dh[first:][None]

    widen = lambda v: jnp.pad(v, ((0, 0), (0, D - v.shape[1])))
    groups = [
        [dh[pad:first]],
        [d_a_norm[l] for l in range(n_a)],
        [d_a_conv[l] for l in range(n_a)],
        [d_kv_norm],
        [widen(d_k_norm)],
        [d_w_f[:, :H].T],
        [widen(d_b_f)],
        [d_b_norm[j] for j in range(n_b)],
        [widen(d_q_norm[j]) for j in range(n_b)],
        [d_ffn_norm[l] for l in range(depth)],
    ]
    pack = jnp.concatenate([rows8(p) for g in groups for p in g], axis=0)
    red = _all_reduce_small(pack, "ar_small")
    taken, off = [], 0
    for g in groups:
        r, rp = g[0].shape[0], rows8(g[0]).shape[0]
        taken.append(red[off:off + len(g) * rp].reshape(len(g), rp, D)[:, :r].reshape(len(g) * r, D))
        off += len(g) * rp
    take = lambda i: taken[i]
    mine = lambda a: lax.dynamic_slice_in_dim(a, q_me * Ds, Ds, axis=1)
    layers_of = lambda name, n: (lambda: jnp.stack([shards[(name, l)] for l in range(n)]))
    grad_of = {
        "meta": lambda: mine(take(0)),
        "a_norm": lambda: mine(take(1)),
        "a_w_in": layers_of("a_w_in", n_a),
        "a_conv": lambda: mine(take(2)).reshape(n_a, 3, Ds),
        "a_w_out": layers_of("a_w_out", n_a),
        "kv_norm": lambda: take(3).reshape(D),
        "w_kv": lambda: shards[("w_kv", 0)],
        "k_norm": lambda: take(4)[0, :HEAD_DIM],
        "w_f": lambda: mine(take(5)).T,
        "b_f": lambda: take(6)[0, :H],
        "b_norm": lambda: take(7),
        "b_w_q": layers_of("b_w_q", n_b),
        "b_q_norm": lambda: take(8)[:, :HEAD_DIM],
        "b_w_o": layers_of("b_w_o", n_b),
        "ffn_norm": lambda: take(9),
        "ffn_w_gu": layers_of("ffn_w_gu", depth),
        "ffn_w_down": layers_of("ffn_w_down", depth),
    }
    weights = dict(meta=meta, a_norm=a_norm, a_w_in=a_w_in, a_conv=a_conv, a_w_out=a_w_out, kv_norm=kv_norm, w_kv=w_kv,
                   k_norm=k_norm, w_f=w_f, b_f=b_f, b_norm=b_norm, b_w_q=b_w_q, b_q_norm=b_q_norm, b_w_o=b_w_o,
                   ffn_norm=ffn_norm, ffn_w_gu=ffn_w_gu, ffn_w_down=ffn_w_down)
    m_in = dict(meta=m_meta, a_norm=m_a_norm, a_w_in=m_a_w_in, a_conv=m_a_conv, a_w_out=m_a_w_out, kv_norm=m_kv_norm,
                w_kv=m_w_kv, k_norm=m_k_norm, w_f=m_w_f, b_f=m_b_f, b_norm=m_b_norm, b_w_q=m_b_w_q,
                b_q_norm=m_b_q_norm, b_w_o=m_b_w_o, ffn_norm=m_ffn_norm, ffn_w_gu=m_ffn_w_gu, ffn_w_down=m_ffn_w_down)
    v_in = dict(meta=v_meta, a_norm=v_a_norm, a_w_in=v_a_w_in, a_conv=v_a_conv, a_w_out=v_a_w_out, kv_norm=v_kv_norm,
                w_kv=v_w_kv, k_norm=v_k_norm, w_f=v_w_f, b_f=v_b_f, b_norm=v_b_norm, b_w_q=v_b_w_q,
                b_q_norm=v_b_q_norm, b_w_o=v_b_w_o, ffn_norm=v_ffn_norm, ffn_w_gu=v_ffn_w_gu, ffn_w_down=v_ffn_w_down)

    grads, deltas, new_m, new_v = {}, {}, {}, {}

    def update(name):
        w = weights[name]
        shape = w.shape
        two_d = (1, shape[0]) if w.ndim == 1 else (math.prod(shape[:-1]), shape[-1])
        r2 = lambda a: a.reshape(two_d)
        g = grad_of[name]()
        d_, m_, v_ = _adamw(r2(w), r2(g), r2(m_in[name]), r2(v_in[name]), f"adamw_{name}")
        deltas[name], new_m[name], new_v[name] = d_.reshape(shape), m_.reshape(shape), v_.reshape(shape)
        grads[name] = g.reshape(shape)

    late = {n for n, _ in last_names}
    for name in weights:
        if name not in late:
            update(name)
    shards.update(zip(last_names, _reduce_finish(last_pending, q_idx, c_idx, deltas["ffn_w_gu"])))
    for name in weights:
        if name in late:
            update(name)

    names = list(weights)
    return (loss, grad_x, *[grads[n] for n in names], *[deltas[n] for n in names],
            *[new_m[n] for n in names], *[new_v[n] for n in names])
```

```python
import functools
import math

import jax
import jax.numpy as jnp
from jax import lax
from jax.experimental import pallas as pl
from jax.experimental.pallas import tpu as pltpu

F32 = jnp.float32
BF16 = jnp.bfloat16
HEAD_DIM = 128
BLOCK = 128
LANES = 128
EPS = 1e-6
NEG = -1e30
ADAM_LR, ADAM_B1, ADAM_B2, ADAM_EPS, ADAM_WD, ADAM_STEP = 0.001, 0.9, 0.999, 1e-08, 0.01, 10
VMEM_LIMIT = 56 * 1024 * 1024
TILE_BUDGET = 40 * 1024 * 1024
MESH = pl.DeviceIdType.MESH
N_CHIPS = 4


def _tile(n, target, align):
    best = None
    for d in range(align, min(n, target) + 1, align):
        if n % d == 0:
            best = d
    return best if best is not None else n


def _cp(sem):
    return pltpu.CompilerParams(dimension_semantics=sem, vmem_limit_bytes=VMEM_LIMIT)


def _matmul(a, b, *, mode, out_dtype, name, res=None, out_parts=1, dep=None):
    a_parts = a.shape[0] if a.ndim == 3 else 1
    b_parts = b.shape[0] if b.ndim == 3 else 1
    if mode == "tn":
        K, M = a.shape
        Kp = K
    else:
        M, Kp = a.shape[-2:]
        K = Kp * a_parts
    N = b.shape[0] if mode == "nt" else b.shape[-1] * b_parts
    Np = N // max(b_parts, out_parts)
    tm = _tile(M, 1024, LANES) if mode == "tn" else _tile(M, 1056, 16)
    tn = _tile(Np, 1536, LANES)
    tk = _tile(Kp, 1056, 16) if mode == "tn" else _tile(Kp, 2048, LANES)
    ab, bb, ob = a.dtype.itemsize, b.dtype.itemsize, jnp.dtype(out_dtype).itemsize

    def vmem(tm_):
        blocks = 2 * (tm_ * tk * ab + tk * tn * bb + tm_ * tn * ob + (tm_ * tn * 4 if res is not None else 0))
        temps = tm_ * tn * 8 + (tm_ * tk * 2 if ab == 4 else 0) + (tk * tn * 2 if bb == 4 else 0)
        return blocks + temps

    while vmem(tm) > TILE_BUDGET and tm > 256:
        tm = _tile(M, tm // 2, LANES if mode == "tn" else 16)
    ni, nj, nk = M // tm, N // tn, K // tk
    nkp, njp = Kp // tk, Np // tn

    if mode == "tn":
        a_spec = pl.BlockSpec((tk, tm), lambda i, j, k: (k, i))
    elif a_parts > 1:
        a_spec = pl.BlockSpec((None, tm, tk), lambda i, j, k: (k // nkp, i, k % nkp))
    else:
        a_spec = pl.BlockSpec((tm, tk), lambda i, j, k: (i, k))
    if mode == "nt":
        b_spec = pl.BlockSpec((tn, tk), lambda i, j, k: (j, k))
    elif b_parts > 1:
        b_spec = pl.BlockSpec((None, tk, tn), lambda i, j, k: (j // njp, k, j % njp))
    else:
        b_spec = pl.BlockSpec((tk, tn), lambda i, j, k: (k, j))
    in_specs = [a_spec, b_spec]
    operands = [a, b]
    if res is not None:
        in_specs.append(pl.BlockSpec((tm, tn), lambda i, j, k: (i, j)))
        operands.append(res)
    if dep is not None:
        in_specs.append(pl.BlockSpec((8, LANES), lambda i, j, k: (0, 0)))
        operands.append(dep)
    n_in = len(operands)
    if out_parts > 1:
        out_spec = pl.BlockSpec((None, tm, tn), lambda i, j, k: (j // njp, i, j % njp))
        out_shape = jax.ShapeDtypeStruct((out_parts, M, Np), out_dtype)
    else:
        out_spec = pl.BlockSpec((tm, tn), lambda i, j, k: (i, j))
        out_shape = jax.ShapeDtypeStruct((M, N), out_dtype)
    dims = {"nn": (((1,), (0,)), ((), ())), "nt": (((1,), (1,)), ((), ())), "tn": (((0,), (0,)), ((), ()))}[mode]
    has_res = res is not None

    def body(*refs):
        a_ref, b_ref = refs[0], refs[1]
        res_ref = refs[2] if has_res else None
        o_ref = refs[n_in]
        d = lax.dot_general(a_ref[...].astype(BF16), b_ref[...].astype(BF16), dims, preferred_element_type=F32)
        if nk == 1:
            if has_res:
                d = d + res_ref[...]
            o_ref[...] = d.astype(out_dtype)
        else:
            acc_ref = refs[-1]
            k = pl.program_id(2)

            @pl.when(k == 0)
            def _():
                acc_ref[...] = d

            @pl.when(k > 0)
            def _():
                acc_ref[...] += d

            @pl.when(k == nk - 1)
            def _():
                r = acc_ref[...]
                if has_res:
                    r = r + res_ref[...]
                o_ref[...] = r.astype(out_dtype)

    return pl.pallas_call(
        body, name=name, grid=(ni, nj, nk), in_specs=in_specs, out_specs=out_spec, out_shape=out_shape,
        scratch_shapes=[pltpu.VMEM((tm, tn), F32)] if nk > 1 else [],
        compiler_params=_cp(("parallel", "parallel", "arbitrary")),
    )(*operands)


def _cast_into_full(w3, layer, axis, q_idx, name):
    _, R, C = w3.shape
    tr = _tile(R, max(16, (4 * 1024 * 1024) // (C * 4)), 16)
    nb = R // tr

    def body(q_ref, w_ref, o_ref):
        o_ref[...] = w_ref[...].astype(BF16)

    if axis == 0:
        out_spec = pl.BlockSpec((tr, C), lambda i, q_ref: (q_ref[0] * nb + i, 0))
        full = (N_CHIPS * R, C)
    else:
        out_spec = pl.BlockSpec((tr, C), lambda i, q_ref: (i, q_ref[0]))
        full = (R, N_CHIPS * C)
    grid_spec = pltpu.PrefetchScalarGridSpec(
        num_scalar_prefetch=1, grid=(nb,),
        in_specs=[pl.BlockSpec((None, tr, C), lambda i, q_ref: (layer, i, 0))], out_specs=out_spec)
    return pl.pallas_call(
        body, name=name, grid_spec=grid_spec, out_shape=jax.ShapeDtypeStruct(full, BF16),
        compiler_params=_cp(("parallel",)),
    )(q_idx, w3)


def _rms_fwd(h, g, name, dep=None):
    T, D = h.shape
    tr = _tile(T, 384, LANES)

    def body(h_ref, g_ref, *rest):
        o_ref, ot_ref = rest[-2:]
        x = h_ref[...]
        r = lax.rsqrt(jnp.mean(x * x, axis=-1, keepdims=True) + EPS)
        y = x * r * g_ref[...]
        o_ref[...] = y.astype(BF16)
        ot_ref[...] = y.T.astype(BF16)

    in_specs = [pl.BlockSpec((tr, D), lambda i: (i, 0)), pl.BlockSpec((1, D), lambda i: (0, 0))]
    operands = [h, g]
    if dep is not None:
        in_specs.append(pl.BlockSpec((8, LANES), lambda i: (0, 0)))
        operands.append(dep)
    return pl.pallas_call(
        body, name=name, grid=(T // tr,), in_specs=in_specs,
        out_specs=[pl.BlockSpec((tr, D), lambda i: (i, 0)), pl.BlockSpec((D, tr), lambda i: (0, i))],
        out_shape=[jax.ShapeDtypeStruct((T, D), BF16), jax.ShapeDtypeStruct((D, T), BF16)],
        compiler_params=_cp(("parallel",)),
    )(*operands)


def _rms_bwd(h, g, dxn, dh, name):
    T, D = h.shape
    tr = _tile(T, 264, 16)

    def body(h_ref, g_ref, dxn_ref, dh_ref, o_ref, ob_ref, dg_ref):
        x = h_ref[...]
        r = lax.rsqrt(jnp.mean(x * x, axis=-1, keepdims=True) + EPS)
        xh = x * r
        dy = dxn_ref[...]
        dxh = dy * g_ref[...]
        dx = r * (dxh - xh * jnp.mean(dxh * xh, axis=-1, keepdims=True))
        out = dh_ref[...] + dx
        o_ref[...] = out
        ob_ref[...] = out.astype(BF16)
        part = jnp.sum(dy * xh, axis=0, keepdims=True)

        @pl.when(pl.program_id(0) == 0)
        def _():
            dg_ref[...] = part

        @pl.when(pl.program_id(0) > 0)
        def _():
            dg_ref[...] += part

    row = pl.BlockSpec((tr, D), lambda i: (i, 0))
    vec = pl.BlockSpec((1, D), lambda i: (0, 0))
    return pl.pallas_call(
        body, name=name, grid=(T // tr,), in_specs=[row, vec, row, row], out_specs=[row, row, vec],
        out_shape=[jax.ShapeDtypeStruct((T, D), F32), jax.ShapeDtypeStruct((T, D), BF16),
                   jax.ShapeDtypeStruct((1, D), F32)],
        compiler_params=_cp(("arbitrary",)),
    )(h, g, dxn, dh)


def _shift_down(u, n, rows):
    return jnp.where(rows >= n, pltpu.roll(u, n, 0), 0.0)


def _shift_up(u, n, rows, total):
    return jnp.where(rows < total - n, pltpu.roll(u, total - n, 0), 0.0)


def _gate_fwd(z, conv_w, name):
    _, T, D = z.shape
    tc = LANES

    def body(b_ref, c_ref, h_ref, w_ref, y_ref, yt_ref):
        rows = lax.broadcasted_iota(jnp.int32, (T, tc), 0)
        u = c_ref[...].astype(F32) * h_ref[...].astype(F32)
        w0, w1, w2 = w_ref[0:1, :], w_ref[1:2, :], w_ref[2:3, :]
        conv = u * w2 + _shift_down(u, 1, rows) * w1 + _shift_down(u, 2, rows) * w0
        y = b_ref[...].astype(F32) * conv
        y_ref[...] = y.astype(BF16)
        yt_ref[...] = y.T.astype(BF16)

    part = lambda p: pl.BlockSpec((None, T, tc), lambda j, p=p: (p, 0, j))
    return pl.pallas_call(
        body, name=name, grid=(D // tc,),
        in_specs=[part(0), part(1), part(2), pl.BlockSpec((3, tc), lambda j: (0, j))],
        out_specs=[pl.BlockSpec((T, tc), lambda j: (0, j)), pl.BlockSpec((tc, T), lambda j: (j, 0))],
        out_shape=[jax.ShapeDtypeStruct((T, D), BF16), jax.ShapeDtypeStruct((D, T), BF16)],
        compiler_params=_cp(("parallel",)),
    )(z, z, z, conv_w)


def _gate_bwd(z, conv_w, dy, name):
    _, T, D = z.shape
    tc = LANES

    def body(b_ref, c_ref, h_ref, w_ref, dy_ref, dz_ref, dw_ref):
        rows = lax.broadcasted_iota(jnp.int32, (T, tc), 0)
        cg, hh = c_ref[...].astype(F32), h_ref[...].astype(F32)
        u = cg * hh
        w0, w1, w2 = w_ref[0:1, :], w_ref[1:2, :], w_ref[2:3, :]
        s1, s2 = _shift_down(u, 1, rows), _shift_down(u, 2, rows)
        g = dy_ref[...]
        dz_ref[0] = (g * (u * w2 + s1 * w1 + s2 * w0)).astype(BF16)
        dconv = g * b_ref[...].astype(F32)
        dw_ref[0:1, :] = jnp.sum(dconv * s2, axis=0, keepdims=True)
        dw_ref[1:2, :] = jnp.sum(dconv * s1, axis=0, keepdims=True)
        dw_ref[2:3, :] = jnp.sum(dconv * u, axis=0, keepdims=True)
        du = dconv * w2 + _shift_up(dconv, 1, rows, T) * w1 + _shift_up(dconv, 2, rows, T) * w0
        dz_ref[1] = (du * hh).astype(BF16)
        dz_ref[2] = (du * cg).astype(BF16)

    part = lambda p: pl.BlockSpec((None, T, tc), lambda j, p=p: (p, 0, j))
    return pl.pallas_call(
        body, name=name, grid=(D // tc,),
        in_specs=[part(0), part(1), part(2), pl.BlockSpec((3, tc), lambda j: (0, j)),
                  pl.BlockSpec((T, tc), lambda j: (0, j))],
        out_specs=[pl.BlockSpec((3, T, tc), lambda j: (0, 0, j)), pl.BlockSpec((3, tc), lambda j: (0, j))],
        out_shape=[jax.ShapeDtypeStruct((3, T, D), BF16), jax.ShapeDtypeStruct((3, D), F32)],
        compiler_params=_cp(("parallel",)),
    )(z, z, z, conv_w, dy)


def _swiglu_fwd(z, name):
    _, T, Fd = z.shape
    tr, tc = _tile(T, 384, LANES), _tile(Fd, 1408, LANES)

    def body(g_ref, u_ref, o_ref, ot_ref):
        g = g_ref[...].astype(F32)
        a = g * jax.nn.sigmoid(g) * u_ref[...].astype(F32)
        o_ref[...] = a.astype(BF16)
        ot_ref[...] = a.T.astype(BF16)

    part = lambda p: pl.BlockSpec((None, tr, tc), lambda i, j, p=p: (p, i, j))
    return pl.pallas_call(
        body, name=name, grid=(T // tr, Fd // tc), in_specs=[part(0), part(1)],
        out_specs=[pl.BlockSpec((tr, tc), lambda i, j: (i, j)), pl.BlockSpec((tc, tr), lambda i, j: (j, i))],
        out_shape=[jax.ShapeDtypeStruct((T, Fd), BF16), jax.ShapeDtypeStruct((Fd, T), BF16)],
        compiler_params=_cp(("parallel", "parallel")),
    )(z, z)


def _swiglu_bwd(z, da, name):
    _, T, Fd = z.shape
    tr, tc = _tile(T, 528, 16), _tile(Fd, 1408, LANES)

    def body(g_ref, u_ref, da_ref, dz_ref):
        g, d = g_ref[...].astype(F32), da_ref[...].astype(F32)
        s = jax.nn.sigmoid(g)
        dz_ref[0] = (d * u_ref[...].astype(F32) * (s * (1.0 + g * (1.0 - s)))).astype(BF16)
        dz_ref[1] = (d * (g * s)).astype(BF16)

    part = lambda p: pl.BlockSpec((None, tr, tc), lambda i, j, p=p: (p, i, j))
    return pl.pallas_call(
        body, name=name, grid=(T // tr, Fd // tc),
        in_specs=[part(0), part(1), pl.BlockSpec((tr, tc), lambda i, j: (i, j))],
        out_specs=pl.BlockSpec((2, tr, tc), lambda i, j: (0, i, j)),
        out_shape=jax.ShapeDtypeStruct((2, T, Fd), BF16), compiler_params=_cp(("parallel", "parallel")),
    )(z, z, da)


def _headnorm_fwd(z, part, g, name):
    _, T, D = z.shape
    tr = _tile(T, 352, 16)

    def body(z_ref, g_ref, o_ref):
        g_ = g_ref[...]
        for h in range(D // HEAD_DIM):
            cols = slice(h * HEAD_DIM, (h + 1) * HEAD_DIM)
            x = z_ref[:, cols]
            r = lax.rsqrt(jnp.mean(x * x, axis=-1, keepdims=True) + EPS)
            o_ref[:, cols] = (x * r * g_).astype(BF16)

    return pl.pallas_call(
        body, name=name, grid=(T // tr,),
        in_specs=[pl.BlockSpec((None, tr, D), lambda i: (part, i, 0)),
                  pl.BlockSpec((1, HEAD_DIM), lambda i: (0, 0))],
        out_specs=pl.BlockSpec((tr, D), lambda i: (i, 0)),
        out_shape=jax.ShapeDtypeStruct((T, D), BF16), compiler_params=_cp(("parallel",)),
    )(z, g)


def _headnorm_bwd(z, part, g, dy, name):
    _, T, D = z.shape
    tr = _tile(T, 352, 16)

    def body(z_ref, g_ref, dy_ref, dz_ref, dg_ref):
        g_ = g_ref[...]
        partial = jnp.zeros((1, HEAD_DIM), F32)
        for h in range(D // HEAD_DIM):
            cols = slice(h * HEAD_DIM, (h + 1) * HEAD_DIM)
            x = z_ref[:, cols]
            r = lax.rsqrt(jnp.mean(x * x, axis=-1, keepdims=True) + EPS)
            xh = x * r
            dy_ = dy_ref[:, cols]
            dxh = dy_ * g_
            dz_ref[:, cols] = (r * (dxh - xh * jnp.mean(dxh * xh, axis=-1, keepdims=True))).astype(BF16)
            partial = partial + jnp.sum(dy_ * xh, axis=0, keepdims=True)

        @pl.when(pl.program_id(0) == 0)
        def _():
            dg_ref[...] = partial

        @pl.when(pl.program_id(0) > 0)
        def _():
            dg_ref[...] += partial

    blk = pl.BlockSpec((tr, D), lambda i: (i, 0))
    vec = pl.BlockSpec((1, HEAD_DIM), lambda i: (0, 0))
    return pl.pallas_call(
        body, name=name, grid=(T // tr,),
        in_specs=[pl.BlockSpec((None, tr, D), lambda i: (part, i, 0)), vec, blk],
        out_specs=[blk, vec],
        out_shape=[jax.ShapeDtypeStruct((T, D), BF16), jax.ShapeDtypeStruct((1, HEAD_DIM), F32)],
        compiler_params=_cp(("arbitrary",)),
    )(z, g, dy)


def _cast_part(z, part, name):
    _, T, D = z.shape
    tr = _tile(T, 528, 16)

    def body(z_ref, o_ref):
        o_ref[...] = z_ref[...].astype(BF16)

    return pl.pallas_call(
        body, name=name, grid=(T // tr,),
        in_specs=[pl.BlockSpec((None, tr, D), lambda i: (part, i, 0))],
        out_specs=pl.BlockSpec((tr, D), lambda i: (i, 0)),
        out_shape=jax.ShapeDtypeStruct((T, D), BF16), compiler_params=_cp(("parallel",)),
    )(z)


def _split3(x):
    a = x.astype(BF16)
    r = x - a.astype(F32)
    b = r.astype(BF16)
    c = (r - b.astype(F32)).astype(BF16)
    return a, b, c


def _tri_matmul(tri, x):
    a, b, c = _split3(x)
    dot = lambda v: jnp.dot(tri, v, preferred_element_type=F32)
    return (dot(c) + dot(b)) + dot(a)


def _logf_cumsum(pre, bias, pad, name):
    T = pre.shape[0]
    nb = T // BLOCK

    def body(p_ref, b_ref, c_ref, carry):
        i = pl.program_id(0)

        @pl.when(i == 0)
        def _():
            carry[...] = jnp.zeros_like(carry)

        x = p_ref[...] + b_ref[...]
        lf = jnp.minimum(x, 0.0) - jnp.log(1.0 + jnp.exp(-jnp.abs(x)))
        rows = i * BLOCK + lax.broadcasted_iota(jnp.int32, (BLOCK, LANES), 0)
        lf = jnp.where(rows >= pad, lf, 0.0)
        r = lax.broadcasted_iota(jnp.int32, (BLOCK, BLOCK), 0)
        c = lax.broadcasted_iota(jnp.int32, (BLOCK, BLOCK), 1)
        tri = jnp.where(c <= r, 1.0, 0.0).astype(BF16)
        c_ref[...] = _tri_matmul(tri, lf) + carry[...]
        carry[...] = c_ref[BLOCK - 1:BLOCK, :]

    return pl.pallas_call(
        body, name=name, grid=(nb,),
        in_specs=[pl.BlockSpec((BLOCK, LANES), lambda i: (i, 0)), pl.BlockSpec((1, LANES), lambda i: (0, 0))],
        out_specs=pl.BlockSpec((BLOCK, LANES), lambda i: (i, 0)),
        out_shape=jax.ShapeDtypeStruct((T, LANES), F32),
        scratch_shapes=[pltpu.VMEM((1, LANES), F32)], compiler_params=_cp(("arbitrary",)),
    )(pre, bias)


def _logf_bwd(pre, bias, dc, pad, name):
    T = pre.shape[0]
    nb = T // BLOCK

    def body(p_ref, b_ref, dc_ref, dp_ref, db_ref, carry, dlf_ref):
        i = pl.program_id(0)

        @pl.when(i == 0)
        def _():
            carry[...] = jnp.zeros_like(carry)

        r = lax.broadcasted_iota(jnp.int32, (BLOCK, BLOCK), 0)
        c = lax.broadcasted_iota(jnp.int32, (BLOCK, BLOCK), 1)
        tri = jnp.where(c >= r, 1.0, 0.0).astype(BF16)
        dlf_ref[...] = _tri_matmul(tri, dc_ref[...]) + carry[...]
        carry[...] = dlf_ref[0:1, :]
        dlf = dlf_ref[...]
        x = p_ref[...] + b_ref[...]
        rows = (nb - 1 - i) * BLOCK + lax.broadcasted_iota(jnp.int32, (BLOCK, LANES), 0)
        dpre = jnp.where(rows >= pad, dlf * jax.nn.sigmoid(-x), 0.0)
        dp_ref[...] = dpre
        partial = jnp.sum(dpre, axis=0, keepdims=True)

        @pl.when(i == 0)
        def _():
            db_ref[...] = partial

        @pl.when(i > 0)
        def _():
            db_ref[...] += partial

    rev = pl.BlockSpec((BLOCK, LANES), lambda i: (nb - 1 - i, 0))
    vec = pl.BlockSpec((1, LANES), lambda i: (0, 0))
    return pl.pallas_call(
        body, name=name, grid=(nb,), in_specs=[rev, vec, rev], out_specs=[rev, vec],
        out_shape=[jax.ShapeDtypeStruct((T, LANES), F32), jax.ShapeDtypeStruct((1, LANES), F32)],
        scratch_shapes=[pltpu.VMEM((1, LANES), F32), pltpu.VMEM((BLOCK, LANES), F32)],
        compiler_params=_cp(("arbitrary",)),
    )(pre, bias, dc)


def _loss_head(h, target, first, name):
    T, D = h.shape
    tr = BLOCK
    skip = first // tr

    def body(h_ref, t_ref, dh_ref, dhb_ref, loss_ref):
        i = pl.program_id(0)

        @pl.when(i == 0)
        def _():
            loss_ref[...] = jnp.zeros_like(loss_ref)

        @pl.when(i < skip)
        def _():
            dh_ref[...] = jnp.zeros_like(dh_ref)
            dhb_ref[...] = jnp.zeros_like(dhb_ref)

        @pl.when(i >= skip)
        def _():
            err = h_ref[...] - t_ref[...]
            dh_ref[...] = err * (1.0 / D)
            dhb_ref[...] = (err * (1.0 / D)).astype(BF16)
            loss_ref[...] += jnp.sum(err * err) * (0.5 / D)

    row = pl.BlockSpec((tr, D), lambda i: (i, 0))
    return pl.pallas_call(
        body, name=name, grid=(T // tr,),
        in_specs=[row, pl.BlockSpec((tr, D), lambda i: (jnp.maximum(i - skip, 0), 0))],
        out_specs=[row, row, pl.BlockSpec((8, LANES), lambda i: (0, 0))],
        out_shape=[jax.ShapeDtypeStruct((T, D), F32), jax.ShapeDtypeStruct((T, D), BF16),
                   jax.ShapeDtypeStruct((8, LANES), F32)],
        compiler_params=_cp(("arbitrary",)),
    )(h, target)


def _adamw(w, g, m, v, name):
    R, C = w.shape
    tr = _tile(R, max(8, TILE_BUDGET // (C * 4 * 7 * 3)), 8)
    bc1, bc2 = 1.0 - ADAM_B1 ** ADAM_STEP, 1.0 - ADAM_B2 ** ADAM_STEP

    def body(w_ref, g_ref, m_ref, v_ref, d_ref, mo_ref, vo_ref):
        g_ = g_ref[...]
        m_ = ADAM_B1 * m_ref[...] + (1.0 - ADAM_B1) * g_
        v_ = ADAM_B2 * v_ref[...] + (1.0 - ADAM_B2) * (g_ * g_)
        d_ref[...] = -ADAM_LR * ((m_ / bc1) / (jnp.sqrt(v_ / bc2) + ADAM_EPS) + ADAM_WD * w_ref[...])
        mo_ref[...] = m_
        vo_ref[...] = v_

    blk = pl.BlockSpec((tr, C), lambda i: (i, 0))
    sds = jax.ShapeDtypeStruct((R, C), F32)
    return pl.pallas_call(
        body, name=name, grid=(R // tr,), in_specs=[blk] * 4, out_specs=[blk] * 3, out_shape=[sds] * 3,
        compiler_params=_cp(("parallel",)),
    )(w, g, m, v)


def _pick_head(c_blk, h):
    lane = lax.broadcasted_iota(jnp.int32, c_blk.shape, 1)
    return jnp.sum(jnp.where(lane == h, c_blk, 0.0), axis=1, keepdims=True)


def _attn_fwd(q, k, v, c, ct, *, blk, pad, name):
    T, D = q.shape
    H, nk = D // HEAD_DIM, T // blk
    bq = blk
    nq = T // bq
    scale = 1.0 / math.sqrt(HEAD_DIM)

    def body(q_ref, k_ref, v_ref, c_ref, ct_ref, o_ref, lse_ref):
        h, i = pl.program_id(0), pl.program_id(1)
        qb = q_ref[...]
        cq = _pick_head(c_ref[...], h)
        last = (i * bq) // blk

        def step_fn(masked):
            def step(j, carry):
                m, l, acc = carry
                off = j * blk if isinstance(j, int) else pl.multiple_of(j * blk, blk)
                kb = k_ref[pl.ds(off, blk), :]
                vb = v_ref[pl.ds(off, blk), :]
                s = lax.dot_general(qb, kb, (((1,), (1,)), ((), ())), preferred_element_type=F32) * scale
                s = s + (cq - ct_ref[j])
                if masked:
                    qpos = i * bq + lax.broadcasted_iota(jnp.int32, (bq, blk), 0)
                    kpos = j * blk + lax.broadcasted_iota(jnp.int32, (bq, blk), 1)
                    s = jnp.where((kpos <= qpos) & (kpos >= pad), s, NEG)
                m_new = jnp.maximum(m, jnp.max(s, axis=1, keepdims=True))
                p = jnp.exp(s - m_new)
                alpha = jnp.exp(m - m_new)
                l = alpha * l + jnp.sum(p, axis=1, keepdims=True)
                acc = alpha * acc + jnp.dot(p.astype(BF16), vb, preferred_element_type=F32)
                return m_new, l, acc
            return step

        carry = (jnp.full((bq, 1), NEG, F32), jnp.zeros((bq, 1), F32), jnp.zeros((bq, HEAD_DIM), F32))
        carry = step_fn(True)(0, carry)
        carry = lax.fori_loop(1, last, step_fn(False), carry)
        m, l, acc = lax.cond(last > 0, lambda c: step_fn(True)(last, c), lambda c: c, carry)
        rowpos = i * bq + lax.broadcasted_iota(jnp.int32, (bq, 1), 0)
        o_ref[...] = jnp.where(rowpos >= pad, acc / l, 0.0)
        lse_ref[...] = jnp.broadcast_to(m + jnp.log(l), (bq, LANES))

    return pl.pallas_call(
        body, name=name, grid=(H, nq),
        in_specs=[pl.BlockSpec((bq, HEAD_DIM), lambda h, i: (i, h)),
                  pl.BlockSpec((T, HEAD_DIM), lambda h, i: (0, h)),
                  pl.BlockSpec((T, HEAD_DIM), lambda h, i: (0, h)),
                  pl.BlockSpec((bq, LANES), lambda h, i: (i, 0)),
                  pl.BlockSpec((None, nk, 1, blk), lambda h, i: (h, 0, 0, 0))],
        out_specs=[pl.BlockSpec((bq, HEAD_DIM), lambda h, i: (i, h)),
                   pl.BlockSpec((None, bq, LANES), lambda h, i: (h, i, 0))],
        out_shape=[jax.ShapeDtypeStruct((T, D), F32), jax.ShapeDtypeStruct((H, T, LANES), F32)],
        compiler_params=_cp(("parallel", "arbitrary")),
    )(q, k, v, c, ct)


def _attn_bwd(q, k, v, o, do, lse, c, ct, prev, *, blk, pad, name):
    T, D = q.shape
    H, nq = D // HEAD_DIM, T // blk
    scale = 1.0 / math.sqrt(HEAD_DIM)
    has_prev = prev is not None

    nt_dims = (((1,), (1,)), ((), ()))
    tn_dims = (((0,), (0,)), ((), ()))

    def body(*refs):
        q_ref, k_ref, v_ref, o_ref, do_ref, lse_ref, c_ref, ct_ref = refs[:8]
        pdk_ref, pdv_ref, pdc_ref = refs[8:11] if has_prev else (None, None, None)
        dq_ref, dk_ref, dv_ref, dct_ref = refs[-4:]
        h, j = pl.program_id(0), pl.program_id(1)

        @pl.when(j == 0)
        def _():
            dq_ref[...] = jnp.zeros_like(dq_ref)

        kb, vb = k_ref[...], v_ref[...]
        ck = ct_ref[...]

        def step_fn(masked):
            def step(i, carry):
                dk, dv, dck = carry
                off = pl.multiple_of(i * blk, blk)
                qb = q_ref[pl.ds(off, blk), :]
                dob = do_ref[pl.ds(off, blk), :]
                lse_i = lse_ref[pl.ds(off, blk), :][:, 0:1]
                cq = _pick_head(c_ref[pl.ds(off, blk), :], h)
                delta = jnp.sum(dob.astype(F32) * o_ref[pl.ds(off, blk), :], axis=1, keepdims=True)
                s = lax.dot_general(qb, kb, nt_dims, preferred_element_type=F32) * scale
                p = jnp.exp(s + (cq - ck) - lse_i)
                if masked:
                    qpos = i * blk + lax.broadcasted_iota(jnp.int32, (blk, blk), 0)
                    kpos = j * blk + lax.broadcasted_iota(jnp.int32, (blk, blk), 1)
                    p = jnp.where((kpos <= qpos) & (kpos >= pad), p, 0.0)
                dp = lax.dot_general(dob, vb, nt_dims, preferred_element_type=F32)
                ds = p * (dp - delta)
                pb, dsb = p.astype(BF16), ds.astype(BF16)
                dv = dv + lax.dot_general(pb, dob, tn_dims, preferred_element_type=F32)
                dk = dk + lax.dot_general(dsb, qb, tn_dims, preferred_element_type=F32)
                dck = dck - jnp.sum(ds, axis=0, keepdims=True)
                dq_ref[pl.ds(off, blk), :] += jnp.dot(dsb, kb, preferred_element_type=F32) * scale
                return dk, dv, dck
            return step

        carry = (jnp.zeros((blk, HEAD_DIM), F32), jnp.zeros((blk, HEAD_DIM), F32), jnp.zeros((1, blk), F32))
        carry = step_fn(True)(j, carry)
        below = lambda masked: (lambda c: lax.fori_loop(j + 1, nq, step_fn(masked), c))
        dk, dv, dck = lax.cond(j == 0, below(True), below(False), carry)
        dk = dk * scale
        if has_prev:
            dk, dv, dck = dk + pdk_ref[...], dv + pdv_ref[...], dck + pdc_ref[...]
        dk_ref[...] = dk
        dv_ref[...] = dv
        dct_ref[...] = dck

    col = pl.BlockSpec((T, HEAD_DIM), lambda h, j: (0, h))
    kblk = pl.BlockSpec((blk, HEAD_DIM), lambda h, j: (j, h))
    ctb = pl.BlockSpec((None, None, 1, blk), lambda h, j: (h, j, 0, 0))
    in_specs = [col, kblk, kblk, col, col,
                pl.BlockSpec((None, T, LANES), lambda h, j: (h, 0, 0)),
                pl.BlockSpec((T, LANES), lambda h, j: (0, 0)), ctb]
    operands = [q, k, v, o, do, lse, c, ct]
    if has_prev:
        in_specs += [kblk, kblk, ctb]
        operands += list(prev)
    return pl.pallas_call(
        body, name=name, grid=(H, nq), in_specs=in_specs, out_specs=[col, kblk, kblk, ctb],
        out_shape=[jax.ShapeDtypeStruct((T, D), F32), jax.ShapeDtypeStruct((T, D), F32),
                   jax.ShapeDtypeStruct((T, D), F32), jax.ShapeDtypeStruct((H, nq, 1, blk), F32)],
        compiler_params=_cp(("parallel", "arbitrary")),
    )(*operands)


HBM_SPEC = pl.BlockSpec(memory_space=pltpu.HBM)


def _place():
    x, y, c = lax.axis_index("x"), lax.axis_index("y"), lax.axis_index("c")
    chips = [(1 - x, y), (x, 1 - y), (1 - x, 1 - y)]
    return x, y, c, chips


def _remote(src, dst, send_sems, recv_sems, k, to):
    return pltpu.make_async_remote_copy(src_ref=src, dst_ref=dst, send_sem=send_sems.at[k],
                                        recv_sem=recv_sems.at[k], device_id=to, device_id_type=MESH)


def _all_gather(shards, specs, name):
    n = len(shards)

    def full_shape(s, axis):
        return (s.shape[0] * N_CHIPS, s.shape[1]) if axis == 0 else (s.shape[0], s.shape[1] * N_CHIPS)

    def body(*refs):
        srcs, outs = refs[:n], refs[n:2 * n]
        send_sems, recv_sems, local_sems = refs[2 * n:]
        x, y, c, chips = _place()
        sibling = (x, y, 1 - c)

        def region(t, chip, half):
            rs, cs = shards[t].shape
            q = 2 * chip[0] + chip[1]
            axis, split = specs[t]
            nrow = rs // 2 if half is not None else rs
            r0 = 0 if half is None else half * nrow
            if axis == 0:
                return outs[t].at[pl.ds(q * rs + r0, nrow), :]
            return outs[t].at[pl.ds(r0, nrow), pl.ds(pl.multiple_of(q * cs, cs), cs)]

        def piece(t, half):
            rs = shards[t].shape[0]
            if half is None:
                return srcs[t]
            return srcs[t].at[pl.ds(half * (rs // 2), rs // 2), :]

        local = [pltpu.make_async_copy(srcs[t], region(t, (x, y), None), local_sems.at[t]) for t in range(n)]
        for cp in local:
            cp.start()
        sends = []
        for t in range(n):
            half = c if specs[t][1] else None
            for j, chip in enumerate(chips):
                cp = _remote(piece(t, half), region(t, (x, y), half), send_sems, recv_sems, 6 * t + j, (*chip, c))
                cp.start()
                sends.append(cp)
        for t in range(n):
            half = c if specs[t][1] else None
            for j, chip in enumerate(chips):
                landed = region(t, chip, half)
                _remote(landed, landed, send_sems, recv_sems, 6 * t + j, (*chip, c)).wait_recv()
                if specs[t][1]:
                    cp = _remote(landed, landed, send_sems, recv_sems, 6 * t + 3 + j, sibling)
                    cp.start()
                    sends.append(cp)
        for t in range(n):
            if specs[t][1]:
                for j, chip in enumerate(chips):
                    got = region(t, chip, 1 - c)
                    _remote(got, got, send_sems, recv_sems, 6 * t + 3 + j, sibling).wait_recv()
        for cp in sends:
            cp.wait_send()
        for cp in local:
            cp.wait()

    return pl.pallas_call(
        body, name=name, in_specs=[HBM_SPEC] * n, out_specs=[HBM_SPEC] * n,
        out_shape=[jax.ShapeDtypeStruct(full_shape(s, specs[t][0]), s.dtype) for t, s in enumerate(shards)],
        scratch_shapes=[pltpu.SemaphoreType.DMA((6 * n,)), pltpu.SemaphoreType.DMA((6 * n,)),
                        pltpu.SemaphoreType.DMA((n,))],
        compiler_params=pltpu.CompilerParams(has_side_effects=True),
    )(*shards)


def _grad_view(g, axis):
    R, C = g.shape
    nq = N_CHIPS if axis == 0 else 1
    return g.reshape(nq, 2, R // (2 * nq), C)


def _add_half(view, got, c_idx, name):
    nq, _, Rh, C = view.shape
    tr = _tile(Rh, max(16, (2 * 1024 * 1024) // (C * 2)), 16)

    def body(c_ref, a_ref, b_ref, o_ref):
        o_ref[...] = (a_ref[...].astype(F32) + b_ref[...].astype(F32)).astype(BF16)

    grid_spec = pltpu.PrefetchScalarGridSpec(
        num_scalar_prefetch=1, grid=(nq, Rh // tr),
        in_specs=[pl.BlockSpec((None, None, tr, C), lambda q, i, c_ref: (q, c_ref[0], i, 0)),
                  pl.BlockSpec((None, tr, C), lambda q, i, c_ref: (q, i, 0))],
        out_specs=pl.BlockSpec((None, tr, C), lambda q, i, c_ref: (q, i, 0)))
    return pl.pallas_call(
        body, name=name, grid_spec=grid_spec, out_shape=jax.ShapeDtypeStruct((nq, Rh, C), BF16),
        compiler_params=_cp(("parallel", "parallel")),
    )(c_idx, view, got)


SEM_SPEC = pl.BlockSpec(memory_space=pltpu.SEMAPHORE)
ANY_SPEC = pl.BlockSpec(memory_space=pl.ANY)
TOKEN_SPEC = pl.BlockSpec(memory_space=pltpu.VMEM)
TOKEN = jax.ShapeDtypeStruct((8, LANES), F32)
SPLIT_COPY = pltpu.CompilerParams(has_side_effects=pltpu.SideEffectType.DATAFLOW_SIDE_EFFECTING)


def _region(ref, axis, chip, half):
    q = 2 * chip[0] + chip[1]
    if axis == 0:
        rs = ref.shape[0] // N_CHIPS
        return ref.at[pl.ds(q * rs + half * (rs // 2), rs // 2), :]
    rh, cs = ref.shape[0] // 2, ref.shape[1] // N_CHIPS
    return ref.at[pl.ds(half * rh, rh), pl.ds(pl.multiple_of(q * cs, cs), cs)]


def _gather_start(fulls, axes, after, name):
    n = len(fulls)

    def body(*refs):
        ins = refs[:n]
        token = refs[-1]
        send_sems, recv_sems = refs[n + 1], refs[n + 2]
        x, y, c, chips = _place()
        for t in range(n):
            mine = _region(ins[t], axes[t], (x, y), c)
            for j, chip in enumerate(chips):
                _remote(mine, mine, send_sems, recv_sems, 3 * t + j, (*chip, c)).start()
        token[...] = jnp.zeros_like(token)

    sems = pltpu.SemaphoreType.DMA((3 * n,))
    outs = pl.pallas_call(
        body, name=name, in_specs=[HBM_SPEC] * n + [ANY_SPEC],
        out_specs=[SEM_SPEC, SEM_SPEC] + [HBM_SPEC] * n + [TOKEN_SPEC],
        out_shape=[sems, sems] + [jax.ShapeDtypeStruct(f.shape, f.dtype) for f in fulls] + [TOKEN],
        input_output_aliases={t: 2 + t for t in range(n)}, compiler_params=SPLIT_COPY,
    )(*fulls, after)
    return outs[0], outs[1], list(outs[2:2 + n]), outs[-1]


def _gather_wait(send_sems, recv_sems, fulls, axes, after, name):
    n = len(fulls)

    def body(*refs):
        ins = refs[:n]
        send_sems, recv_sems = refs[n], refs[n + 1]
        x, y, c, chips = _place()
        for t in range(n):
            mine = _region(ins[t], axes[t], (x, y), c)
            for j, chip in enumerate(chips):
                _remote(mine, mine, send_sems, recv_sems, 3 * t + j, (*chip, c)).wait_send()
                theirs = _region(ins[t], axes[t], chip, c)
                _remote(theirs, theirs, send_sems, recv_sems, 3 * t + j, (*chip, c)).wait_recv()

    outs = pl.pallas_call(
        body, name=name, in_specs=[HBM_SPEC] * n + [SEM_SPEC, SEM_SPEC, ANY_SPEC], out_specs=[HBM_SPEC] * n,
        out_shape=[jax.ShapeDtypeStruct(f.shape, f.dtype) for f in fulls],
        input_output_aliases={t: t for t in range(n)}, compiler_params=SPLIT_COPY,
    )(*fulls, send_sems, recv_sems, after)
    return list(outs)


def _gather_forward(fulls, axes, name):
    n = len(fulls)

    def body(*refs):
        outs = refs[n:2 * n]
        send_sems, recv_sems = refs[2 * n], refs[2 * n + 1]
        x, y, c, chips = _place()
        sibling = (x, y, 1 - c)
        cps = []
        for t in range(n):
            for j, chip in enumerate(chips):
                landed = _region(outs[t], axes[t], chip, c)
                cps.append(_remote(landed, landed, send_sems, recv_sems, 3 * t + j, sibling))
        for cp in cps:
            cp.start()
        for t in range(n):
            for j, chip in enumerate(chips):
                got = _region(outs[t], axes[t], chip, 1 - c)
                _remote(got, got, send_sems, recv_sems, 3 * t + j, sibling).wait_recv()
        for cp in cps:
            cp.wait_send()

    outs = pl.pallas_call(
        body, name=name, in_specs=[HBM_SPEC] * n, out_specs=[HBM_SPEC] * n,
        out_shape=[jax.ShapeDtypeStruct(f.shape, f.dtype) for f in fulls],
        scratch_shapes=[pltpu.SemaphoreType.DMA((3 * n,)), pltpu.SemaphoreType.DMA((3 * n,))],
        input_output_aliases={t: t for t in range(n)},
        compiler_params=pltpu.CompilerParams(has_side_effects=True),
    )(*fulls)
    return list(outs)


def _shard_cols(s, axis):
    return s.shape[2] if axis == 0 else s.shape[2] // N_CHIPS


def _piece(ref, axis, chip):
    q = 2 * chip[0] + chip[1]
    if axis == 0:
        return ref.at[q]
    cs = ref.shape[2] // N_CHIPS
    return ref.at[0, :, pl.ds(pl.multiple_of(q * cs, cs), cs)]


def _scatter_start(sums, axes, after, name):
    n = len(sums)

    def body(*refs):
        ins = refs[:n]
        send_sems, recv_sems = refs[n + 1], refs[n + 2]
        lands = refs[2 * n + 3:3 * n + 3]
        token = refs[-1]
        x, y, c, chips = _place()
        for t in range(n):
            for j, chip in enumerate(chips):
                _remote(_piece(ins[t], axes[t], chip), lands[t].at[j], send_sems, recv_sems, 3 * t + j, (*chip, c)).start()
        token[...] = jnp.zeros_like(token)

    sems = pltpu.SemaphoreType.DMA((3 * n,))
    land_shapes = [jax.ShapeDtypeStruct((3, s.shape[1], _shard_cols(s, a)), s.dtype) for s, a in zip(sums, axes)]
    outs = pl.pallas_call(
        body, name=name, in_specs=[HBM_SPEC] * n + [ANY_SPEC],
        out_specs=[SEM_SPEC, SEM_SPEC] + [HBM_SPEC] * (2 * n) + [TOKEN_SPEC],
        out_shape=[sems, sems] + [jax.ShapeDtypeStruct(s.shape, s.dtype) for s in sums] + land_shapes + [TOKEN],
        input_output_aliases={t: 2 + t for t in range(n)}, compiler_params=SPLIT_COPY,
    )(*sums, after)
    return outs[0], outs[1], list(outs[2:2 + n]), list(outs[2 + n:2 + 2 * n]), outs[-1]


def _scatter_wait(send_sems, recv_sems, sums, lands, axes, after, name):
    n = len(sums)

    def body(*refs):
        ins, lnd = refs[:n], refs[n:2 * n]
        send_sems, recv_sems = refs[2 * n], refs[2 * n + 1]
        x, y, c, chips = _place()
        for t in range(n):
            for j, chip in enumerate(chips):
                cp = _remote(_piece(ins[t], axes[t], chip), lnd[t].at[j], send_sems, recv_sems, 3 * t + j, (*chip, c))
                cp.wait_send()
                cp.wait_recv()

    outs = pl.pallas_call(
        body, name=name, in_specs=[HBM_SPEC] * (2 * n) + [SEM_SPEC, SEM_SPEC, ANY_SPEC], out_specs=[HBM_SPEC] * (2 * n),
        out_shape=[jax.ShapeDtypeStruct(s.shape, s.dtype) for s in sums + lands],
        input_output_aliases={t: t for t in range(2 * n)}, compiler_params=SPLIT_COPY,
    )(*sums, *lands, send_sems, recv_sems, after)
    return list(outs[:n]), list(outs[n:])


def _sum_chips(own, axis, got, q_idx, c_idx, name):
    _, Rh, cc = got.shape
    tr = _tile(Rh, max(16, (1024 * 1024) // (cc * 2)), 16)

    def body(q_ref, c_ref, a_ref, b0, b1, b2, o_ref):
        f = lambda r: r[...].astype(F32)
        o_ref[...] = ((f(a_ref) + f(b0)) + f(b1)) + f(b2)

    if axis == 0:
        own_spec = pl.BlockSpec((None, tr, cc), lambda i, q, c: (q[0], i, 0))
    else:
        own_spec = pl.BlockSpec((None, tr, cc), lambda i, q, c: (0, i, q[0]))
    slot = lambda j: pl.BlockSpec((None, tr, cc), lambda i, q, c, j=j: (j, i, 0))
    grid_spec = pltpu.PrefetchScalarGridSpec(
        num_scalar_prefetch=2, grid=(Rh // tr,), in_specs=[own_spec, slot(0), slot(1), slot(2)],
        out_specs=pl.BlockSpec((None, tr, cc), lambda i, q, c: (c[0], i, 0)))
    return pl.pallas_call(
        body, name=name, grid_spec=grid_spec, out_shape=jax.ShapeDtypeStruct((2, Rh, cc), F32),
        compiler_params=_cp(("parallel",)),
    )(q_idx, c_idx, own, got, got, got)


def _join_halves(pairs, name):
    n = len(pairs)

    def body(*refs):
        outs = refs[n:2 * n]
        send_sems, recv_sems = refs[2 * n], refs[2 * n + 1]
        x, y, c, _ = _place()
        cps = [_remote(outs[t].at[c], outs[t].at[c], send_sems, recv_sems, t, (x, y, 1 - c)) for t in range(n)]
        for cp in cps:
            cp.start()
        for t in range(n):
            cps[t].wait_send()
            _remote(outs[t].at[1 - c], outs[t].at[1 - c], send_sems, recv_sems, t, (x, y, 1 - c)).wait_recv()

    outs = pl.pallas_call(
        body, name=name, in_specs=[HBM_SPEC] * n, out_specs=[HBM_SPEC] * n,
        out_shape=[jax.ShapeDtypeStruct(p.shape, p.dtype) for p in pairs],
        scratch_shapes=[pltpu.SemaphoreType.DMA((n,)), pltpu.SemaphoreType.DMA((n,))],
        input_output_aliases={t: t for t in range(n)},
        compiler_params=pltpu.CompilerParams(has_side_effects=True),
    )(*pairs)
    return list(outs)


def _swap_halves(views, name):
    n = len(views)

    def body(*refs):
        srcs, outs, send_sems, recv_sems = refs[:n], refs[n:2 * n], refs[2 * n], refs[2 * n + 1]
        x, y, c, _ = _place()
        cps = [_remote(srcs[t].at[:, 1 - c], outs[t], send_sems, recv_sems, t, (x, y, 1 - c)) for t in range(n)]
        for cp in cps:
            cp.start()
        for cp in cps:
            cp.wait()

    return pl.pallas_call(
        body, name=name, in_specs=[HBM_SPEC] * n, out_specs=[HBM_SPEC] * n,
        out_shape=[jax.ShapeDtypeStruct((v.shape[0],) + v.shape[2:], v.dtype) for v in views],
        scratch_shapes=[pltpu.SemaphoreType.DMA((n,)), pltpu.SemaphoreType.DMA((n,))],
        compiler_params=pltpu.CompilerParams(has_side_effects=True),
    )(*views)


def _swap_start(views, after, name):
    n = len(views)

    def body(*refs):
        ins = refs[:n]
        send_sems, recv_sems = refs[n + 1], refs[n + 2]
        lands = refs[2 * n + 3:3 * n + 3]
        token = refs[-1]
        x, y, c, _ = _place()
        for t in range(n):
            _remote(ins[t].at[:, 1 - c], lands[t], send_sems, recv_sems, t, (x, y, 1 - c)).start()
        token[...] = jnp.zeros_like(token)

    sems = pltpu.SemaphoreType.DMA((n,))
    land_shapes = [jax.ShapeDtypeStruct((v.shape[0],) + v.shape[2:], v.dtype) for v in views]
    outs = pl.pallas_call(
        body, name=name, in_specs=[HBM_SPEC] * n + [ANY_SPEC],
        out_specs=[SEM_SPEC, SEM_SPEC] + [HBM_SPEC] * (2 * n) + [TOKEN_SPEC],
        out_shape=[sems, sems] + [jax.ShapeDtypeStruct(v.shape, v.dtype) for v in views] + land_shapes + [TOKEN],
        input_output_aliases={t: 2 + t for t in range(n)}, compiler_params=SPLIT_COPY,
    )(*views, after)
    return outs[0], outs[1], list(outs[2:2 + n]), list(outs[2 + n:2 + 2 * n]), outs[-1]


def _swap_wait(send_sems, recv_sems, views, lands, after, name):
    n = len(views)

    def body(*refs):
        ins, lnd = refs[:n], refs[n:2 * n]
        send_sems, recv_sems = refs[2 * n], refs[2 * n + 1]
        x, y, c, _ = _place()
        for t in range(n):
            cp = _remote(ins[t].at[:, 1 - c], lnd[t], send_sems, recv_sems, t, (x, y, 1 - c))
            cp.wait_send()
            cp.wait_recv()

    outs = pl.pallas_call(
        body, name=name, in_specs=[HBM_SPEC] * (2 * n) + [SEM_SPEC, SEM_SPEC, ANY_SPEC], out_specs=[HBM_SPEC] * (2 * n),
        out_shape=[jax.ShapeDtypeStruct(a.shape, a.dtype) for a in views + lands],
        input_output_aliases={t: t for t in range(2 * n)}, compiler_params=SPLIT_COPY,
    )(*views, *lands, send_sems, recv_sems, after)
    return list(outs[:n]), list(outs[n:])


def _reduce_begin(views, got, axes, c_idx, after, tag):
    sums = [_add_half(v, p, c_idx, f"rs_add_{tag}_{t}") for t, (v, p) in enumerate(zip(views, got))]
    send_sems, recv_sems, sums, lands, token = _scatter_start(sums, axes, after, f"rs_chips_start_{tag}")
    return (send_sems, recv_sems, sums, lands, axes, tag), token


def _reduce_finish(pending, q_idx, c_idx, after):
    send_sems, recv_sems, sums, lands, axes, tag = pending
    sums, lands = _scatter_wait(send_sems, recv_sems, sums, lands, axes, after, f"rs_chips_wait_{tag}")
    pairs = [_sum_chips(s, a, r, q_idx, c_idx, f"rs_sum_{tag}_{t}")
             for t, (s, a, r) in enumerate(zip(sums, axes, lands))]
    joined = _join_halves(pairs, f"rs_join_{tag}")
    return [j.reshape(2 * j.shape[1], j.shape[2]) for j in joined]


def _all_reduce_small(buf, name):
    R, C = buf.shape

    def body(b_ref, o_ref, slots, send_sems, recv_sems):
        x, y, c, _ = _place()
        me = 4 * x + 2 * y + c
        cps = []
        for k in range(1, 8):
            to = (x ^ (k >> 2), y ^ ((k >> 1) & 1), c ^ (k & 1))
            cps.append(pltpu.make_async_remote_copy(
                src_ref=b_ref, dst_ref=slots.at[me], send_sem=send_sems.at[k - 1], recv_sem=recv_sems.at[me],
                device_id=to, device_id_type=MESH))
        for cp in cps:
            cp.start()
        slots[me] = b_ref[...]
        for k in range(1, 8):
            src = me ^ k
            pltpu.make_async_remote_copy(
                src_ref=b_ref, dst_ref=slots.at[src], send_sem=send_sems.at[k - 1], recv_sem=recv_sems.at[src],
                device_id=(x, y, c), device_id_type=MESH).wait_recv()
        for cp in cps:
            cp.wait_send()
        total = slots[0]
        for d in range(1, 8):
            total = total + slots[d]
        o_ref[...] = total

    vm = pl.BlockSpec(memory_space=pltpu.VMEM)
    return pl.pallas_call(
        body, name=name, in_specs=[vm], out_specs=vm, out_shape=jax.ShapeDtypeStruct((R, C), F32),
        scratch_shapes=[pltpu.VMEM((8, R, C), F32), pltpu.SemaphoreType.DMA((7,)), pltpu.SemaphoreType.DMA((8,))],
        compiler_params=pltpu.CompilerParams(has_side_effects=True, vmem_limit_bytes=VMEM_LIMIT),
    )(buf)


def kernel(x, meta, a_norm, a_w_in, a_conv, a_w_out, kv_norm, w_kv, k_norm, w_f, b_f, b_norm, b_w_q, b_q_norm, b_w_o, ffn_norm, ffn_w_gu, ffn_w_down, loss_target, m_meta, m_a_norm, m_a_w_in, m_a_conv, m_a_w_out, m_kv_norm, m_w_kv, m_k_norm, m_w_f, m_b_f, m_b_norm, m_b_w_q, m_b_q_norm, m_b_w_o, m_ffn_norm, m_ffn_w_gu, m_ffn_w_down, v_meta, v_a_norm, v_a_w_in, v_a_conv, v_a_w_out, v_kv_norm, v_w_kv, v_k_norm, v_w_f, v_b_f, v_b_norm, v_b_w_q, v_b_q_norm, v_b_w_o, v_ffn_norm, v_ffn_w_gu, v_ffn_w_down):
    SEQ, D = x.shape[1], x.shape[2]
    n_meta = meta.shape[0]
    pad = BLOCK - n_meta
    first = pad + n_meta
    T = first + SEQ
    H = D // HEAD_DIM
    Ds = D // N_CHIPS
    n_a, n_b, depth = a_w_in.shape[0], b_w_q.shape[0], ffn_norm.shape[0]
    blk = _tile(T, 384, BLOCK)
    cx, cy, cc = lax.axis_index("x"), lax.axis_index("y"), lax.axis_index("c")
    q_me = 2 * cx + cy
    c_idx = jnp.reshape(cc, (1,)).astype(jnp.int32)
    q_idx = jnp.reshape(q_me, (1,)).astype(jnp.int32)
    rows8 = lambda v: jnp.pad(v, ((0, -v.shape[0] % 8), (0, 0)))

    col_parts = [meta, a_norm, a_conv.reshape(3 * n_a, Ds)]
    col_pack = jnp.concatenate([rows8(p) for p in col_parts], axis=0)
    col_full, w_f_full = _all_gather([col_pack, w_f], [(1, False), (0, False)], "ag_small")
    col_offs = [sum(rows8(p).shape[0] for p in col_parts[:i]) for i in range(3)]
    meta_f = col_full[:n_meta]
    a_norm_f = col_full[col_offs[1]:col_offs[1] + n_a]
    a_conv_f = col_full[col_offs[2]:col_offs[2] + 3 * n_a].reshape(n_a, 3, D)
    w_fp = jnp.pad(w_f_full, ((0, 0), (0, LANES - H))).astype(BF16)
    b_fp = jnp.pad(b_f, (0, LANES - H)).reshape(1, LANES)

    stages = []
    for l in range(n_a):
        stages += [[(a_w_in, l, 1), (a_w_out, l, 0)], [(ffn_w_gu, l, 1), (ffn_w_down, l, 0)]]
    for j in range(n_b):
        stages += [[(b_w_q, j, 0), (b_w_o, j, 0)], [(ffn_w_gu, n_a + j, 1), (ffn_w_down, n_a + j, 0)]]
    stages[2 * n_a].append((w_kv[None], 0, 1))

    def gather_begin(k, after):
        axes = [ax for _, _, ax in stages[k]]
        fulls = [_cast_into_full(w, l, ax, q_idx, f"cast_{k}_{i}") for i, (w, l, ax) in enumerate(stages[k])]
        send_sems, recv_sems, fulls, token = _gather_start(fulls, axes, after, f"ag_start_{k}")
        return (send_sems, recv_sems, fulls, axes, k), token

    def gather_end(handle, after):
        send_sems, recv_sems, fulls, axes, k = handle
        fulls = _gather_wait(send_sems, recv_sems, fulls, axes, after, f"ag_wait_{k}")
        return _gather_forward(fulls, axes, f"ag_forward_{k}")

    handle, _ = gather_begin(0, col_full)
    arrived = [gather_end(handle, col_full)]
    gathering = []

    def enter_segment():
        w = arrived[-1]
        after, token = w[0], None
        while len(gathering) < 2 and len(arrived) + len(gathering) < len(stages):
            handle, token = gather_begin(len(arrived) + len(gathering), after)
            gathering.append(handle)
            after = token
        return w, token

    def leave_segment(h_out):
        if gathering:
            arrived.append(gather_end(gathering.pop(0), h_out))

    h = jnp.concatenate([jnp.zeros((pad, D), F32), meta_f, x[0]], axis=0)
    saved = []

    def ffn_fwd(h, layer):
        (w_gu, w_down), token = enter_segment()
        xn, xn_t = _rms_fwd(h, ffn_norm[layer:layer + 1], f"ffn_norm_{layer}", dep=token)
        z = _matmul(xn, w_gu, mode="nn", out_dtype=BF16, name=f"ffn_gu_{layer}", out_parts=2)
        act, act_t = _swiglu_fwd(z, f"swiglu_{layer}")
        out = _matmul(act, w_down, mode="nn", out_dtype=F32, name=f"ffn_down_{layer}", res=h)
        leave_segment(out)
        return out, (h, xn_t, z, act_t), (w_gu, w_down)

    wa, wb = [], []
    for l in range(n_a):
        (w_in, w_out), token = enter_segment()
        xn, xn_t = _rms_fwd(h, a_norm_f[l:l + 1], f"a_norm_{l}", dep=token)
        z = _matmul(xn, w_in, mode="nn", out_dtype=BF16, name=f"a_in_{l}", out_parts=3)
        y, y_t = _gate_fwd(z, a_conv_f[l], f"a_gate_{l}")
        h2 = _matmul(y, w_out, mode="nn", out_dtype=F32, name=f"a_out_{l}", res=h)
        leave_segment(h2)
        h3, ffn_saved, w_ffn = ffn_fwd(h2, l)
        saved.append((h, xn_t, z, y_t, ffn_saved))
        wa.append((w_in, w_out) + w_ffn)
        h = h3

    for j in range(n_b):
        layer = n_a + j
        w_mix, token = enter_segment()
        w_q, w_o = w_mix[:2]
        if j == 0:
            h_kv, w_kv_b = h, w_mix[2]
            xkv, xkv_t = _rms_fwd(h, kv_norm.reshape(1, D), "kv_norm", dep=token)
            token = None
            kvz = _matmul(xkv, w_kv_b, mode="nn", out_dtype=F32, name="kv_proj", out_parts=2)
            k_n = _headnorm_fwd(kvz, 0, k_norm.reshape(1, HEAD_DIM), "k_headnorm")
            v_b = _cast_part(kvz, 1, "v_cast")
            pre = _matmul(xkv, w_fp, mode="nn", out_dtype=F32, name="f_proj")
            c_cum = _logf_cumsum(pre, b_fp, pad, "logf_cumsum")
            c_t = c_cum[:, :H].T.reshape(H, T // blk, 1, blk)
        xn, xn_t = _rms_fwd(h, b_norm[j:j + 1], f"b_norm_{j}", dep=token)
        qz = _matmul(xn, w_q, mode="nn", out_dtype=F32, name=f"b_q_{j}")[None]
        q_n = _headnorm_fwd(qz, 0, b_q_norm[j:j + 1], f"q_headnorm_{j}")
        o, lse = _attn_fwd(q_n, k_n, v_b, c_cum, c_t, blk=blk, pad=pad, name=f"attn_fwd_{j}")
        h2 = _matmul(o, w_o, mode="nn", out_dtype=F32, name=f"b_o_{j}", res=h)
        leave_segment(h2)
        h3, ffn_saved, w_ffn = ffn_fwd(h2, layer)
        saved.append((h, xn_t, qz, q_n, o, lse, ffn_saved))
        wb.append((w_q, w_o) + w_ffn)
        h = h3

    dh, dhb, loss_blk = _loss_head(h, loss_target[0], first, "loss_head")
    loss = lax.psum(loss_blk[0, 0], ("x", "y", "c"))

    shards = {}
    swapping = []
    reducing = []

    def advance(done, after):
        token = None
        while len(reducing) >= 2:
            prev_names, pending = reducing.pop(0)
            got = _reduce_finish(pending, q_idx, c_idx, done)
            shards.update(zip(prev_names, got))
            after = got[0]
        if swapping:
            names, (send_sems, recv_sems, views, lands), axes, tag = swapping.pop()
            views, lands = _swap_wait(send_sems, recv_sems, views, lands, done, f"rs_swap_wait_{tag}")
            pending, token = _reduce_begin(views, lands, axes, c_idx, after, tag)
            reducing.append((names, pending))
            after = token
        return after, token

    def reduce_later(names, grads, axes, tag, done, now=False):
        after, _ = advance(done, c_idx)
        views = [_grad_view(g, a) for g, a in zip(grads, axes)]
        if now:
            got = _swap_halves(views, f"rs_swap_{tag}")
            pending, token = _reduce_begin(views, got, axes, c_idx, after, tag)
            reducing.append((names, pending))
            return token
        send_sems, recv_sems, views, lands, token = _swap_start(views, after, f"rs_swap_start_{tag}")
        swapping.append((names, (send_sems, recv_sems, views, lands), axes, tag))
        return token

    def ffn_bwd(dh, dhb, layer, w_gu, w_down, ffn_saved, dep):
        h_in, xn_t, z, act_t = ffn_saved
        da = _matmul(dhb, w_down, mode="nt", out_dtype=BF16, name=f"ffn_down_dx_{layer}", dep=dep)
        g_down = _matmul(act_t, dhb, mode="nn", out_dtype=BF16, name=f"ffn_down_dw_{layer}")
        dz = _swiglu_bwd(z, da, f"swiglu_bwd_{layer}")
        dxn = _matmul(dz, w_gu, mode="nt", out_dtype=F32, name=f"ffn_gu_dx_{layer}")
        g_gu = _matmul(xn_t, dz, mode="nn", out_dtype=BF16, name=f"ffn_gu_dw_{layer}")
        dh, dhb, dg = _rms_bwd(h_in, ffn_norm[layer:layer + 1], dxn, dh, f"ffn_norm_bwd_{layer}")
        token = reduce_later([("ffn_w_gu", layer), ("ffn_w_down", layer)], [g_gu, g_down], [1, 0], f"f{layer}", dh,
                             now=layer == 0)
        return dh, dhb, dg, token

    d_ffn_norm, d_b_norm, d_q_norm, d_a_norm, d_a_conv = {}, {}, {}, {}, {}
    kv_prev = None
    token = None
    for j in reversed(range(n_b)):
        layer = n_a + j
        w_q, w_o, w_gu, w_down = wb[j]
        h_in, xn_t, qz, q_n, o, lse, ffn_saved = saved[layer]
        dh, dhb, d_ffn_norm[layer], token = ffn_bwd(dh, dhb, layer, w_gu, w_down, ffn_saved, token)
        do = _matmul(dhb, w_o, mode="nt", out_dtype=BF16, name=f"b_o_dx_{j}", dep=token)
        g_o = _matmul(o.astype(BF16).T, dhb, mode="nn", out_dtype=BF16, name=f"b_o_dw_{j}")
        dq, dk, dv, dct = _attn_bwd(q_n, k_n, v_b, o, do, lse, c_cum, c_t, kv_prev, blk=blk, pad=pad,
                                    name=f"attn_bwd_{j}")
        kv_prev = (dk, dv, dct)
        dqz, d_q_norm[j] = _headnorm_bwd(qz, 0, b_q_norm[j:j + 1], dq, f"q_headnorm_bwd_{j}")
        dxn = _matmul(dqz, w_q, mode="nt", out_dtype=F32, name=f"b_q_dx_{j}")
        g_q = _matmul(xn_t, dqz, mode="nn", out_dtype=BF16, name=f"b_q_dw_{j}")
        dh, dhb, d_b_norm[j] = _rms_bwd(h_in, b_norm[j:j + 1], dxn, dh, f"b_norm_bwd_{j}")
        if j > 0:
            token = reduce_later([("b_w_q", j), ("b_w_o", j)], [g_q, g_o], [0, 0], f"b{j}", dh)

    dk, dv, dct = kv_prev
    dkz, d_k_norm = _headnorm_bwd(kvz, 0, k_norm.reshape(1, HEAD_DIM), dk, "k_headnorm_bwd")
    dvz = _cast_part(dv[None], 0, "dv_cast")
    dkv = jnp.stack([dkz, dvz])
    dc = jnp.pad(dct.reshape(H, T).T, ((0, 0), (0, LANES - H)))
    dpre, d_b_f = _logf_bwd(pre, b_fp, dc, pad, "logf_bwd")
    dxkv = _matmul(dkv, w_kv_b, mode="nt", out_dtype=F32, name="kv_proj_dx")
    dxkv = _matmul(dpre, w_fp, mode="nt", out_dtype=F32, name="f_proj_dx", res=dxkv)
    g_kv = _matmul(xkv_t, dkv, mode="nn", out_dtype=BF16, name="kv_proj_dw")
    d_w_f = _matmul(xkv_t, dpre, mode="nn", out_dtype=F32, name="f_proj_dw")
    dh, dhb, d_kv_norm = _rms_bwd(h_kv, kv_norm.reshape(1, D), dxkv, dh, "kv_norm_bwd")
    token = reduce_later([("b_w_q", 0), ("b_w_o", 0), ("w_kv", 0)], [g_q, g_o, g_kv], [0, 0, 1], "b0", dh)

    for l in reversed(range(n_a)):
        w_in, w_out, w_gu, w_down = wa[l]
        h_in, xn_t, z, y_t, ffn_saved = saved[l]
        dh, dhb, d_ffn_norm[l], token = ffn_bwd(dh, dhb, l, w_gu, w_down, ffn_saved, token)
        dy = _matmul(dhb, w_out, mode="nt", out_dtype=F32, name=f"a_out_dx_{l}", dep=token)
        g_out = _matmul(y_t, dhb, mode="nn", out_dtype=BF16, name=f"a_out_dw_{l}")
        dz, d_a_conv[l] = _gate_bwd(z, a_conv_f[l], dy, f"a_gate_bwd_{l}")
        dxn = _matmul(dz, w_in, mode="nt", out_dtype=F32, name=f"a_in_dx_{l}")
        g_in = _matmul(xn_t, dz, mode="nn", out_dtype=BF16, name=f"a_in_dw_{l}")
        dh, dhb, d_a_norm[l] = _rms_bwd(h_in, a_norm_f[l:l + 1], dxn, dh, f"a_norm_bwd_{l}")
        token = reduce_later([("a_w_in", l), ("a_w_out", l)], [g_in, g_out], [1, 0], f"a{l}", dh, now=l == 0)

    advance(dh, c_idx)
    last_names, last_pending = reducing.pop()
    for prev_names, pending in reducing:
        shards.update(zip(prev_names, _reduce_finish(pending, q_idx, c_idx, dh)))
    grad_x = dh[first:][None]

    widen = lambda v: jnp.pad(v, ((0, 0), (0, D - v.shape[1])))
    groups = [
        [dh[pad:first]],
        [d_a_norm[l] for l in range(n_a)],
        [d_a_conv[l] for l in range(n_a)],
        [d_kv_norm],
        [widen(d_k_norm)],
        [d_w_f[:, :H].T],
        [widen(d_b_f)],
        [d_b_norm[j] for j in range(n_b)],
        [widen(d_q_norm[j]) for j in range(n_b)],
        [d_ffn_norm[l] for l in range(depth)],
    ]
    pack = jnp.concatenate([rows8(p) for g in groups for p in g], axis=0)
    red = _all_reduce_small(pack, "ar_small")
    taken, off = [], 0
    for g in groups:
        r, rp = g[0].shape[0], rows8(g[0]).shape[0]
        taken.append(red[off:off + len(g) * rp].reshape(len(g), rp, D)[:, :r].reshape(len(g) * r, D))
        off += len(g) * rp
    take = lambda i: taken[i]
    mine = lambda a: lax.dynamic_slice_in_dim(a, q_me * Ds, Ds, axis=1)
    layers_of = lambda name, n: (lambda: jnp.stack([shards[(name, l)] for l in range(n)]))
    grad_of = {
        "meta": lambda: mine(take(0)),
        "a_norm": lambda: mine(take(1)),
        "a_w_in": layers_of("a_w_in", n_a),
        "a_conv": lambda: mine(take(2)).reshape(n_a, 3, Ds),
        "a_w_out": layers_of("a_w_out", n_a),
        "kv_norm": lambda: take(3).reshape(D),
        "w_kv": lambda: shards[("w_kv", 0)],
        "k_norm": lambda: take(4)[0, :HEAD_DIM],
        "w_f": lambda: mine(take(5)).T,
        "b_f": lambda: take(6)[0, :H],
        "b_norm": lambda: take(7),
        "b_w_q": layers_of("b_w_q", n_b),
        "b_q_norm": lambda: take(8)[:, :HEAD_DIM],
        "b_w_o": layers_of("b_w_o", n_b),
        "ffn_norm": lambda: take(9),
        "ffn_w_gu": layers_of("ffn_w_gu", depth),
        "ffn_w_down": layers_of("ffn_w_down", depth),
    }
    weights = dict(meta=meta, a_norm=a_norm, a_w_in=a_w_in, a_conv=a_conv, a_w_out=a_w_out, kv_norm=kv_norm, w_kv=w_kv,
                   k_norm=k_norm, w_f=w_f, b_f=b_f, b_norm=b_norm, b_w_q=b_w_q, b_q_norm=b_q_norm, b_w_o=b_w_o,
                   ffn_norm=ffn_norm, ffn_w_gu=ffn_w_gu, ffn_w_down=ffn_w_down)
    m_in = dict(meta=m_meta, a_norm=m_a_norm, a_w_in=m_a_w_in, a_conv=m_a_conv, a_w_out=m_a_w_out, kv_norm=m_kv_norm,
                w_kv=m_w_kv, k_norm=m_k_norm, w_f=m_w_f, b_f=m_b_f, b_norm=m_b_norm, b_w_q=m_b_w_q,
                b_q_norm=m_b_q_norm, b_w_o=m_b_w_o, ffn_norm=m_ffn_norm, ffn_w_gu=m_ffn_w_gu, ffn_w_down=m_ffn_w_down)
    v_in = dict(meta=v_meta, a_norm=v_a_norm, a_w_in=v_a_w_in, a_conv=v_a_conv, a_w_out=v_a_w_out, kv_norm=v_kv_norm,
                w_kv=v_w_kv, k_norm=v_k_norm, w_f=v_w_f, b_f=v_b_f, b_norm=v_b_norm, b_w_q=v_b_w_q,
                b_q_norm=v_b_q_norm, b_w_o=v_b_w_o, ffn_norm=v_ffn_norm, ffn_w_gu=v_ffn_w_gu, ffn_w_down=v_ffn_w_down)

    grads, deltas, new_m, new_v = {}, {}, {}, {}

    def update(name):
        w = weights[name]
        shape = w.shape
        two_d = (1, shape[0]) if w.ndim == 1 else (math.prod(shape[:-1]), shape[-1])
        r2 = lambda a: a.reshape(two_d)
        g = grad_of[name]()
        d_, m_, v_ = _adamw(r2(w), r2(g), r2(m_in[name]), r2(v_in[name]), f"adamw_{name}")
        deltas[name], new_m[name], new_v[name] = d_.reshape(shape), m_.reshape(shape), v_.reshape(shape)
        grads[name] = g.reshape(shape)

    late = {n for n, _ in last_names}
    for name in weights:
        if name not in late:
            update(name)
    shards.update(zip(last_names, _reduce_finish(last_pending, q_idx, c_idx, deltas["ffn_w_gu"])))
    for name in weights:
        if name in late:
            update(name)

    names = list(weights)
    return (loss, grad_x, *[grads[n] for n in names], *[deltas[n] for n in names],
            *[new_m[n] for n in names], *[new_v[n] for n in names])
```
